```python
import jax
import jax.numpy as jnp
from jax import lax
import numpy as np

D_MODEL = 1024
BATCH = 8
SEQ = 2048
DEPTH = 2

GRID_W = 64
HEAD_DIM = 64
NORM_EPS = 1e-6
RW_HEADS = 8
RW_DIM = RW_HEADS * HEAD_DIM
DECAY_RANK = 64
ICLR_RANK = 64
GATE_RANK = 128
RWKV_GN_EPS = 64e-5
RW_IN = 3 * RW_DIM + DECAY_RANK + ICLR_RANK + GATE_RANK
ATT_HEADS = 8
ATT_KV_HEADS = 2
ATT_DIM = ATT_HEADS * HEAD_DIM
ATT_KV_DIM = ATT_KV_HEADS * HEAD_DIM
ATT_IN = ATT_DIM + 2 * ATT_KV_DIM
Q_BLOCK = 128
ROPE_THETA = 10000.0
E_IN = RW_IN + ATT_IN
GLA_HEADS = 4
GLA_DK = D_MODEL // 2
GLA_DV = D_MODEL
GLA_DKH = GLA_DK // GLA_HEADS
GLA_DVH = GLA_DV // GLA_HEADS
GLA_GATE_RANK = 16
GLA_GATE_NORM = 16.0
GLA_CHUNK = 64
O_IN = 2 * GLA_DK + GLA_DV + GLA_GATE_RANK + GLA_DV
D_FF = 2816
N_EXPERTS = 8
TOP_K = 2
D_FF_EXPERT = 3584
MOE_BLOCK = 128
N_EVEN = (DEPTH + 1) // 2
N_ODD = DEPTH // 2

kernel_name = 'hybrid_rwkv7_gqa_gla_moe_encoder'


def rms_norm(x, g, eps=NORM_EPS):
    xf = x.astype(jnp.float32)
    y = xf * lax.rsqrt(jnp.mean(xf * xf, axis=-1, keepdims=True) + eps)
    return (y * g.astype(jnp.float32)).astype(x.dtype)


def swiglu(h, w_gate, w_up, w_down):
    return (jax.nn.silu(h @ w_gate) * (h @ w_up)) @ w_down


def centred_shift(x):
    prev = jnp.pad(x, ((0, 0), (1, 0), (0, 0)))[:, :-1]
    nxt = jnp.pad(x, ((0, 0), (0, 1), (0, 0)))[:, 1:]
    return 0.5 * (prev + nxt)


def rwkv7_scan(r, w, k, v, a, b, reverse):
    Bn, S, H, N = r.shape

    def step(state, inp):
        r_t, w_t, k_t, v_t, a_t, b_t = inp
        sa = jnp.einsum('bhvk,bhk->bhv', state, a_t)
        new = state * w_t[:, :, None, :] + sa[..., None] * b_t[:, :, None, :] + v_t[..., None] * k_t[:, :, None, :]
        read = state if reverse else new
        return new, jnp.einsum('bhvk,bhk->bhv', read, r_t)

    xs = tuple(jnp.moveaxis(t, 1, 0) for t in (r, w, k, v, a, b))
    init = jnp.zeros((Bn, H, N, N), jnp.float32)
    _, out = lax.scan(step, init, xs, reverse=reverse)
    return jnp.moveaxis(out, 0, 1)


def rwkv7_mixer(p, shift_mu, w0_f, w_up_f, w0_b, w_up_b, a0_f, a_up_f, a0_b, a_up_b, g_up, k_k, k_a, r_k, gn_w, gn_b):
    Bn, S, _ = p.shape
    p = p + (centred_shift(p) - p) * shift_mu
    o1 = 3 * RW_DIM
    o2 = o1 + DECAY_RANK
    o3 = o2 + ICLR_RANK
    r = p[..., :RW_DIM]
    k = p[..., RW_DIM:2 * RW_DIM]
    v = p[..., 2 * RW_DIM:o1]
    wd = jnp.tanh(p[..., o1:o2])
    ad = p[..., o2:o3]
    g = jax.nn.sigmoid(p[..., o3:]) @ g_up

    def heads(t):
        return t.reshape(Bn, S, RW_HEADS, HEAD_DIM).astype(jnp.float32)

    def per_head(c):
        return c.reshape(RW_HEADS, HEAD_DIM).astype(jnp.float32)

    rh, kh, vh = heads(r), heads(k), heads(v)
    kk = kh * per_head(k_k)
    kk = kk / jnp.maximum(jnp.sqrt(jnp.sum(kk * kk, axis=-1, keepdims=True)), 1e-12)

    def direction(w0, w_up, a0, a_up, reverse):
        z = (w0 + wd @ w_up).astype(jnp.float32)
        w = heads(jnp.exp(-jnp.exp(-jax.nn.softplus(-z) - 0.5)))
        a = heads(jax.nn.sigmoid((a0 + ad @ a_up).astype(jnp.float32)))
        kd = kh * (1.0 + (a - 1.0) * per_head(k_a))
        return rwkv7_scan(rh, w, kd, vh, -kk, kk * a, reverse), kd

    y_f, k_fwd = direction(w0_f, w_up_f, a0_f, a_up_f, False)
    y_b, _ = direction(w0_b, w_up_b, a0_b, a_up_b, True)
    y = y_f + y_b
    mean = jnp.mean(y, axis=-1, keepdims=True)
    var = jnp.mean(jnp.square(y - mean), axis=-1, keepdims=True)
    y = ((y - mean) * lax.rsqrt(var + RWKV_GN_EPS)).reshape(Bn, S, RW_DIM)
    y = y * gn_w.astype(jnp.float32) + gn_b.astype(jnp.float32)
    bonus = jnp.sum(rh * k_fwd * per_head(r_k), axis=-1, keepdims=True) * vh
    y = (y + bonus.reshape(Bn, S, RW_DIM)) * g.astype(jnp.float32)
    return y.astype(p.dtype)


def axial_rope_tables(S):
    rows = S // GRID_W
    row = jnp.repeat(jnp.arange(rows), GRID_W).astype(jnp.float32)
    col = jnp.tile(jnp.arange(GRID_W), rows).astype(jnp.float32)
    half = HEAD_DIM // 2
    freq = ROPE_THETA ** (-jnp.arange(0, half, 2, dtype=jnp.float32) / half)
    ang = jnp.concatenate([row[:, None] * freq, col[:, None] * freq], axis=-1)
    return jnp.cos(ang), jnp.sin(ang)


def apply_rope(x, cos, sin):
    xp = x.reshape(x.shape[:-1] + (HEAD_DIM // 2, 2))
    x0, x1 = xp[..., 0], xp[..., 1]
    c = cos[None, :, None, :]
    s = sin[None, :, None, :]
    return jnp.stack([x0 * c - x1 * s, x0 * s + x1 * c], axis=-1).reshape(x.shape)


def gqa_axial_attention(p, q_norm, k_norm):
    Bn, S, _ = p.shape
    G = ATT_HEADS // ATT_KV_HEADS
    q = p[..., :ATT_DIM].reshape(Bn, S, ATT_HEADS, HEAD_DIM)
    k = p[..., ATT_DIM:ATT_DIM + ATT_KV_DIM].reshape(Bn, S, ATT_KV_HEADS, HEAD_DIM)
    v = p[..., ATT_DIM + ATT_KV_DIM:].reshape(Bn, S, ATT_KV_HEADS, HEAD_DIM)
    cos, sin = axial_rope_tables(S)
    q = apply_rope(rms_norm(q, q_norm).astype(jnp.float32), cos, sin) * (HEAD_DIM ** -0.5)
    k = apply_rope(rms_norm(k, k_norm).astype(jnp.float32), cos, sin)
    nb = S // Q_BLOCK
    qb = q.reshape(Bn, nb, Q_BLOCK, ATT_KV_HEADS, G, HEAD_DIM).transpose(1, 0, 3, 4, 2, 5)

    def block(q_blk):
        s = jnp.einsum('bkgqd,bskd->bkgqs', q_blk, k)
        pr = jax.nn.softmax(s, axis=-1).astype(v.dtype)
        return jnp.einsum('bkgqs,bskd->bqkgd', pr, v)

    o = lax.map(block, qb)
    return o.transpose(1, 0, 2, 3, 4, 5).reshape(Bn, S, ATT_DIM).astype(p.dtype)


def gla_chunked(q, k, v, g, strict):
    Bn, H, S, dk = q.shape
    dv = v.shape[-1]
    C = GLA_CHUNK
    n = S // C
    q = q.reshape(Bn, H, n, C, dk)
    k = k.reshape(Bn, H, n, C, dk)
    g = g.reshape(Bn, H, n, C, dk)
    v = v.reshape(Bn, H, n, C, dv)
    b = jnp.cumsum(g, axis=3)
    b_mid = b[:, :, :, C // 2:C // 2 + 1]
    b_last = b[:, :, :, C - 1:C]
    att = jnp.einsum('bhnid,bhnjd->bhnij', q * jnp.exp(b - b_mid), k * jnp.exp(b_mid - b))
    mask = jnp.tril(jnp.ones((C, C), dtype=bool), k=-1 if strict else 0)
    att = jnp.where(mask, att, 0.0)
    o = jnp.einsum('bhnij,bhnjv->bhniv', att, v)
    kv = jnp.einsum('bhncd,bhncv->bhndv', k * jnp.exp(b_last - b), v)
    dec = jnp.exp(b_last[:, :, :, 0, :])

    def step(state, inp):
        d, kv_n = inp
        return state * d[..., None] + kv_n, state

    init = jnp.zeros((Bn, H, dk, dv), jnp.float32)
    _, states = lax.scan(step, init, (jnp.moveaxis(dec, 2, 0), jnp.moveaxis(kv, 2, 0)))
    states = jnp.moveaxis(states, 0, 2)
    o = o + jnp.einsum('bhncd,bhndv->bhncv', q * jnp.exp(b), states)
    return o.reshape(Bn, H, S, dv)


def gla_mixer(p, gate_up_f, gate_bias_f, gate_up_b, gate_bias_b, out_norm):
    Bn, S, _ = p.shape
    o1 = GLA_DK
    o2 = 2 * GLA_DK
    o3 = o2 + GLA_DV
    o4 = o3 + GLA_GATE_RANK
    gd = p[..., o3:o4]
    og = p[..., o4:]

    def heads(t, d):
        return t.reshape(Bn, S, GLA_HEADS, d).transpose(0, 2, 1, 3).astype(jnp.float32)

    qh = heads(p[..., :o1], GLA_DKH) * (GLA_DKH ** -0.5)
    kh = heads(p[..., o1:o2], GLA_DKH)
    vh = heads(p[..., o2:o3], GLA_DVH)

    def log_gate(up, bias):
        return heads(jax.nn.log_sigmoid((gd @ up + bias).astype(jnp.float32)) / GLA_GATE_NORM, GLA_DKH)

    def flip(t):
        return jnp.flip(t, axis=2)

    o_f = gla_chunked(qh, kh, vh, log_gate(gate_up_f, gate_bias_f), strict=False)
    o_b = flip(gla_chunked(flip(qh), flip(kh), flip(vh), flip(log_gate(gate_up_b, gate_bias_b)), strict=True))
    o = rms_norm(o_f + o_b, out_norm)
    o = o.transpose(0, 2, 1, 3).reshape(Bn, S, GLA_DV).astype(p.dtype)
    return o * jax.nn.silu(og)


def moe_swiglu(h, router, w_gate, w_up, w_down):
    Bn, S, D = h.shape
    T = Bn * S
    A = T * TOP_K
    hf = h.reshape(T, D)
    logits = (hf @ router).astype(jnp.float32)
    top_val, top_idx = lax.top_k(logits, TOP_K)
    gates = jax.nn.softmax(top_val, axis=-1)
    e_flat = top_idx.reshape(A)
    tok_flat = jnp.repeat(jnp.arange(T, dtype=jnp.int32), TOP_K)
    g_flat = gates.reshape(A)
    order = jnp.argsort(e_flat)
    e_sorted = e_flat[order]
    counts = jnp.zeros((N_EXPERTS,), jnp.int32).at[e_flat].add(1)
    padded = (counts + MOE_BLOCK - 1) // MOE_BLOCK * MOE_BLOCK
    start = jnp.cumsum(counts) - counts
    ends = jnp.cumsum(padded)
    pstart = ends - padded
    dest = pstart[e_sorted] + jnp.arange(A, dtype=jnp.int32) - start[e_sorted]
    P = (A + MOE_BLOCK - 1) // MOE_BLOCK * MOE_BLOCK + N_EXPERTS * MOE_BLOCK
    nb = P // MOE_BLOCK
    row_tok = jnp.zeros((P,), jnp.int32).at[dest].set(tok_flat[order])
    row_gate = jnp.zeros((P,), jnp.float32).at[dest].set(g_flat[order])
    blk_start = jnp.arange(nb, dtype=jnp.int32) * MOE_BLOCK
    blk_exp = jnp.minimum(jnp.searchsorted(ends, blk_start, side='right'), N_EXPERTS - 1)
    xb = hf[row_tok].reshape(nb, MOE_BLOCK, D)

    def expert_block(inp):
        x_blk, e = inp
        return (jax.nn.silu(x_blk @ w_gate[e]) * (x_blk @ w_up[e])) @ w_down[e]

    yb = lax.map(expert_block, (xb, blk_exp)).reshape(P, D)
    y = jnp.zeros((T, D), h.dtype).at[row_tok].add((yb * row_gate[:, None]).astype(h.dtype))
    return y.reshape(Bn, S, D)


def even_layer(x, norm1, w_in, shift_mu, w0_f, w_up_f, w0_b, w_up_b, a0_f, a_up_f, a0_b, a_up_b, g_up,
               k_k, k_a, r_k, gn_w, gn_b, q_norm, k_norm, w_out, norm2, ffn_gate, ffn_up, ffn_down):
    p = rms_norm(x, norm1) @ w_in
    y_a = rwkv7_mixer(p[..., :RW_IN], shift_mu, w0_f, w_up_f, w0_b, w_up_b, a0_f, a_up_f, a0_b, a_up_b,
                      g_up, k_k, k_a, r_k, gn_w, gn_b)
    y_b = gqa_axial_attention(p[..., RW_IN:], q_norm, k_norm)
    x = x + jnp.concatenate([y_a, y_b], axis=-1) @ w_out
    return x + swiglu(rms_norm(x, norm2), ffn_gate, ffn_up, ffn_down)


def odd_layer(x, norm1, w_in, gate_up_f, gate_bias_f, gate_up_b, gate_bias_b, out_norm, w_out,
              norm2, router, exp_gate, exp_up, exp_down):
    p = rms_norm(x, norm1) @ w_in
    x = x + gla_mixer(p, gate_up_f, gate_bias_f, gate_up_b, gate_bias_b, out_norm) @ w_out
    return x + moe_swiglu(rms_norm(x, norm2), router, exp_gate, exp_up, exp_down)


def setup_inputs(seed: int = 0) -> dict:
    key = jax.random.key(seed)
    ks = iter(jax.random.split(key, 48))
    D = D_MODEL
    NE, NO = N_EVEN, N_ODD

    def nrm(shape, scale):
        return scale * jax.random.normal(next(ks), shape, jnp.float32)

    def gain(shape):
        return 1.0 + nrm(shape, 0.02)

    def unif(shape, lo, hi):
        return jax.random.uniform(next(ks), shape, jnp.float32, lo, hi)

    return {
        'x': nrm((BATCH, SEQ, D), 1.0),
        'e_norm1': gain((NE, D)),
        'e_w_in': nrm((NE, D, E_IN), D ** -0.5),
        'e_shift_mu': unif((NE, RW_IN), 0.0, 1.0),
        'e_w0_f': unif((NE, RW_DIM), -6.0, -1.0),
        'e_w_up_f': nrm((NE, DECAY_RANK, RW_DIM), 0.1),
        'e_w0_b': unif((NE, RW_DIM), -6.0, -1.0),
        'e_w_up_b': nrm((NE, DECAY_RANK, RW_DIM), 0.1),
        'e_a0_f': nrm((NE, RW_DIM), 0.1),
        'e_a_up_f': nrm((NE, ICLR_RANK, RW_DIM), 0.1),
        'e_a0_b': nrm((NE, RW_DIM), 0.1),
        'e_a_up_b': nrm((NE, ICLR_RANK, RW_DIM), 0.1),
        'e_g_up': nrm((NE, GATE_RANK, RW_DIM), GATE_RANK ** -0.5),
        'e_k_k': 0.85 + nrm((NE, RW_DIM), 0.02),
        'e_k_a': gain((NE, RW_DIM)),
        'e_r_k': nrm((NE, RW_DIM), 0.1),
        'e_gn_w': gain((NE, RW_DIM)),
        'e_gn_b': nrm((NE, RW_DIM), 0.02),
        'e_q_norm': gain((NE, HEAD_DIM)),
        'e_k_norm': gain((NE, HEAD_DIM)),
        'e_w_out': nrm((NE, RW_DIM + ATT_DIM, D), (RW_DIM + ATT_DIM) ** -0.5),
        'e_norm2': gain((NE, D)),
        'e_ffn_gate': nrm((NE, D, D_FF), D ** -0.5),
        'e_ffn_up': nrm((NE, D, D_FF), D ** -0.5),
        'e_ffn_down': nrm((NE, D_FF, D), D_FF ** -0.5),
        'o_norm1': gain((NO, D)),
        'o_w_in': nrm((NO, D, O_IN), D ** -0.5),
        'o_gate_up_f': nrm((NO, GLA_GATE_RANK, GLA_DK), GLA_GATE_RANK ** -0.5),
        'o_gate_bias_f': nrm((NO, GLA_DK), 0.1),
        'o_gate_up_b': nrm((NO, GLA_GATE_RANK, GLA_DK), GLA_GATE_RANK ** -0.5),
        'o_gate_bias_b': nrm((NO, GLA_DK), 0.1),
        'o_out_norm': gain((NO, GLA_DVH)),
        'o_w_out': nrm((NO, GLA_DV, D), GLA_DV ** -0.5),
        'o_norm2': gain((NO, D)),
        'o_router': nrm((NO, D, N_EXPERTS), D ** -0.5),
        'o_exp_gate': nrm((NO, N_EXPERTS, D, D_FF_EXPERT), D ** -0.5),
        'o_exp_up': nrm((NO, N_EXPERTS, D, D_FF_EXPERT), D ** -0.5),
        'o_exp_down': nrm((NO, N_EXPERTS, D_FF_EXPERT, D), D_FF_EXPERT ** -0.5),
    }


def reference(x, e_norm1, e_w_in, e_shift_mu, e_w0_f, e_w_up_f, e_w0_b, e_w_up_b, e_a0_f, e_a_up_f,
              e_a0_b, e_a_up_b, e_g_up, e_k_k, e_k_a, e_r_k, e_gn_w, e_gn_b, e_q_norm, e_k_norm, e_w_out,
              e_norm2, e_ffn_gate, e_ffn_up, e_ffn_down, o_norm1, o_w_in, o_gate_up_f, o_gate_bias_f,
              o_gate_up_b, o_gate_bias_b, o_out_norm, o_w_out, o_norm2, o_router, o_exp_gate, o_exp_up,
              o_exp_down):
    for i in range(DEPTH):
        j = i // 2
        if i % 2 == 0:
            x = even_layer(x, e_norm1[j], e_w_in[j], e_shift_mu[j], e_w0_f[j], e_w_up_f[j], e_w0_b[j],
                           e_w_up_b[j], e_a0_f[j], e_a_up_f[j], e_a0_b[j], e_a_up_b[j], e_g_up[j], e_k_k[j],
                           e_k_a[j], e_r_k[j], e_gn_w[j], e_gn_b[j], e_q_norm[j], e_k_norm[j], e_w_out[j],
                           e_norm2[j], e_ffn_gate[j], e_ffn_up[j], e_ffn_down[j])
        else:
            x = odd_layer(x, o_norm1[j], o_w_in[j], o_gate_up_f[j], o_gate_bias_f[j], o_gate_up_b[j],
                          o_gate_bias_b[j], o_out_norm[j], o_w_out[j], o_norm2[j], o_router[j],
                          o_exp_gate[j], o_exp_up[j], o_exp_down[j])
    return x
```

```python
import functools

import jax
import jax.numpy as jnp
from jax import lax
from jax.experimental import pallas as pl
from jax.experimental.pallas import tpu as pltpu

F32 = jnp.float32
BF16 = jnp.bfloat16

D_MODEL = 1024
GRID_W = 64
HEAD_DIM = 64
NORM_EPS = 1e-6
RW_HEADS = 8
RW_DIM = 512
DECAY_RANK = 64
ICLR_RANK = 64
GATE_RANK = 128
RWKV_GN_EPS = 64e-5
RW_IN = 3 * RW_DIM + DECAY_RANK + ICLR_RANK + GATE_RANK
ATT_HEADS = 8
ATT_KV_HEADS = 2
ATT_DIM = 512
ATT_KV_DIM = 128
Q_BLOCK = 128
ROPE_THETA = 10000.0
GLA_HEADS = 4
GLA_DK = 512
GLA_DV = 1024
GLA_DKH = 128
GLA_DVH = 256
GLA_GATE_RANK = 16
GLA_GATE_NORM = 16.0
GLA_CHUNK = 64
N_EXPERTS = 8
TOP_K = 2
MOE_BLOCK = 128

VMEM_LIMIT = 48 * 1024 * 1024


def _cparams(sem):
    return pltpu.CompilerParams(dimension_semantics=sem, vmem_limit_bytes=VMEM_LIMIT)


def _norm_mm_kernel(x_ref, g_ref, w_ref, o_ref, xn_ref):
    @pl.when(pl.program_id(1) == 0)
    def _():
        x = x_ref[...]
        ms = jnp.mean(x * x, axis=-1, keepdims=True)
        xn_ref[...] = (x * lax.rsqrt(ms + NORM_EPS) * g_ref[...]).astype(BF16)

    o_ref[...] = jnp.dot(xn_ref[...], w_ref[...], preferred_element_type=F32).astype(o_ref.dtype)


def norm_matmul(x, g, w, *, tm, tn, out_dtype=F32):
    M, K = x.shape
    N = w.shape[1]
    return pl.pallas_call(
        _norm_mm_kernel,
        grid=(M // tm, N // tn),
        in_specs=[
            pl.BlockSpec((tm, K), lambda i, j: (i, 0)),
            pl.BlockSpec((1, K), lambda i, j: (0, 0)),
            pl.BlockSpec((K, tn), lambda i, j: (0, j)),
        ],
        out_specs=pl.BlockSpec((tm, tn), lambda i, j: (i, j)),
        out_shape=jax.ShapeDtypeStruct((M, N), out_dtype),
        scratch_shapes=[pltpu.VMEM((tm, K), BF16)],
        compiler_params=_cparams(("parallel", "arbitrary")),
        name="norm_matmul",
    )(x, g.reshape(1, K), w)


def _mm_res_kernel(y_ref, w_ref, r_ref, o_ref):
    o_ref[...] = r_ref[...] + jnp.dot(y_ref[...].astype(BF16), w_ref[...], preferred_element_type=F32)


def matmul_residual(y, w, r, *, tm, tn):
    M, K = y.shape
    N = w.shape[1]
    return pl.pallas_call(
        _mm_res_kernel,
        grid=(M // tm, N // tn),
        in_specs=[
            pl.BlockSpec((tm, K), lambda i, j: (i, 0)),
            pl.BlockSpec((K, tn), lambda i, j: (0, j)),
            pl.BlockSpec((tm, tn), lambda i, j: (i, j)),
        ],
        out_specs=pl.BlockSpec((tm, tn), lambda i, j: (i, j)),
        out_shape=jax.ShapeDtypeStruct((M, N), F32),
        compiler_params=_cparams(("parallel", "arbitrary")),
        name="matmul_residual",
    )(y, w, r)


def _ffn_kernel(x_ref, g_ref, wg_ref, wu_ref, wd_ref, o_ref, xn_ref, acc_ref):
    f = pl.program_id(1)

    @pl.when(f == 0)
    def _():
        x = x_ref[...]
        ms = jnp.mean(x * x, axis=-1, keepdims=True)
        xn_ref[...] = (x * lax.rsqrt(ms + NORM_EPS) * g_ref[...]).astype(BF16)

    xn = xn_ref[...]
    a = jnp.dot(xn, wg_ref[...], preferred_element_type=F32)
    b = jnp.dot(xn, wu_ref[...], preferred_element_type=F32)
    h = (a * jax.nn.sigmoid(a) * b).astype(BF16)
    part = jnp.dot(h, wd_ref[...], preferred_element_type=F32)

    @pl.when(f == 0)
    def _():
        acc_ref[...] = part

    @pl.when(f > 0)
    def _():
        acc_ref[...] += part

    @pl.when(f == pl.num_programs(1) - 1)
    def _():
        o_ref[...] = x_ref[...] + acc_ref[...]


def ffn_swiglu(x, g, wg, wu, wd, *, tm, tf):
    M, D = x.shape
    F = wg.shape[1]
    return pl.pallas_call(
        _ffn_kernel,
        grid=(M // tm, F // tf),
        in_specs=[
            pl.BlockSpec((tm, D), lambda i, f: (i, 0)),
            pl.BlockSpec((1, D), lambda i, f: (0, 0)),
            pl.BlockSpec((D, tf), lambda i, f: (0, f)),
            pl.BlockSpec((D, tf), lambda i, f: (0, f)),
            pl.BlockSpec((tf, D), lambda i, f: (f, 0)),
        ],
        out_specs=pl.BlockSpec((tm, D), lambda i, f: (i, 0)),
        out_shape=jax.ShapeDtypeStruct((M, D), F32),
        scratch_shapes=[pltpu.VMEM((tm, D), BF16), pltpu.VMEM((tm, D), F32)],
        compiler_params=_cparams(("parallel", "arbitrary")),
        name="ffn_swiglu",
    )(x, g.reshape(1, D), wg, wu, wd)


def _rms_norm(x, g, eps=NORM_EPS):
    xf = x.astype(F32)
    y = xf * lax.rsqrt(jnp.mean(xf * xf, axis=-1, keepdims=True) + eps)
    return (y * g.astype(F32)).astype(x.dtype)


def _centred_shift(x):
    prev = jnp.pad(x, ((0, 0), (1, 0), (0, 0)))[:, :-1]
    nxt = jnp.pad(x, ((0, 0), (0, 1), (0, 0)))[:, 1:]
    return 0.5 * (prev + nxt)


def _rwkv7_scan(r, w, k, v, a, b, reverse):
    Bn, S, H, N = r.shape

    def step(state, inp):
        r_t, w_t, k_t, v_t, a_t, b_t = inp
        sa = jnp.einsum('bhvk,bhk->bhv', state, a_t)
        new = state * w_t[:, :, None, :] + sa[..., None] * b_t[:, :, None, :] + v_t[..., None] * k_t[:, :, None, :]
        read = state if reverse else new
        return new, jnp.einsum('bhvk,bhk->bhv', read, r_t)

    xs = tuple(jnp.moveaxis(t, 1, 0) for t in (r, w, k, v, a, b))
    init = jnp.zeros((Bn, H, N, N), F32)
    _, out = lax.scan(step, init, xs, reverse=reverse)
    return jnp.moveaxis(out, 0, 1)


def _rwkv7_mixer(p, shift_mu, w0_f, w_up_f, w0_b, w_up_b, a0_f, a_up_f, a0_b, a_up_b, g_up, k_k, k_a, r_k, gn_w, gn_b):
    Bn, S, _ = p.shape
    p = p + (_centred_shift(p) - p) * shift_mu
    o1 = 3 * RW_DIM
    o2 = o1 + DECAY_RANK
    o3 = o2 + ICLR_RANK
    r = p[..., :RW_DIM]
    k = p[..., RW_DIM:2 * RW_DIM]
    v = p[..., 2 * RW_DIM:o1]
    wd = jnp.tanh(p[..., o1:o2])
    ad = p[..., o2:o3]
    g = jax.nn.sigmoid(p[..., o3:]) @ g_up

    def heads(t):
        return t.reshape(Bn, S, RW_HEADS, HEAD_DIM).astype(F32)

    def per_head(c):
        return c.reshape(RW_HEADS, HEAD_DIM).astype(F32)

    rh, kh, vh = heads(r), heads(k), heads(v)
    kk = kh * per_head(k_k)
    kk = kk / jnp.maximum(jnp.sqrt(jnp.sum(kk * kk, axis=-1, keepdims=True)), 1e-12)

    def direction(w0, w_up, a0, a_up, reverse):
        z = (w0 + wd @ w_up).astype(F32)
        w = heads(jnp.exp(-jnp.exp(-jax.nn.softplus(-z) - 0.5)))
        a = heads(jax.nn.sigmoid((a0 + ad @ a_up).astype(F32)))
        kd = kh * (1.0 + (a - 1.0) * per_head(k_a))
        return _rwkv7_scan(rh, w, kd, vh, -kk, kk * a, reverse), kd

    y_f, k_fwd = direction(w0_f, w_up_f, a0_f, a_up_f, False)
    y_b, _ = direction(w0_b, w_up_b, a0_b, a_up_b, True)
    y = y_f + y_b
    mean = jnp.mean(y, axis=-1, keepdims=True)
    var = jnp.mean(jnp.square(y - mean), axis=-1, keepdims=True)
    y = ((y - mean) * lax.rsqrt(var + RWKV_GN_EPS)).reshape(Bn, S, RW_DIM)
    y = y * gn_w.astype(F32) + gn_b.astype(F32)
    bonus = jnp.sum(rh * k_fwd * per_head(r_k), axis=-1, keepdims=True) * vh
    y = (y + bonus.reshape(Bn, S, RW_DIM)) * g.astype(F32)
    return y.astype(p.dtype)


def _axial_rope_tables(S):
    rows = S // GRID_W
    row = jnp.repeat(jnp.arange(rows), GRID_W).astype(F32)
    col = jnp.tile(jnp.arange(GRID_W), rows).astype(F32)
    half = HEAD_DIM // 2
    freq = ROPE_THETA ** (-jnp.arange(0, half, 2, dtype=F32) / half)
    ang = jnp.concatenate([row[:, None] * freq, col[:, None] * freq], axis=-1)
    return jnp.cos(ang), jnp.sin(ang)


def _apply_rope(x, cos, sin):
    xp = x.reshape(x.shape[:-1] + (HEAD_DIM // 2, 2))
    x0, x1 = xp[..., 0], xp[..., 1]
    c = cos[None, :, None, :]
    s = sin[None, :, None, :]
    return jnp.stack([x0 * c - x1 * s, x0 * s + x1 * c], axis=-1).reshape(x.shape)


def _gqa_axial_attention(p, q_norm, k_norm):
    Bn, S, _ = p.shape
    G = ATT_HEADS // ATT_KV_HEADS
    q = p[..., :ATT_DIM].reshape(Bn, S, ATT_HEADS, HEAD_DIM)
    k = p[..., ATT_DIM:ATT_DIM + ATT_KV_DIM].reshape(Bn, S, ATT_KV_HEADS, HEAD_DIM)
    v = p[..., ATT_DIM + ATT_KV_DIM:].reshape(Bn, S, ATT_KV_HEADS, HEAD_DIM)
    cos, sin = _axial_rope_tables(S)
    q = _apply_rope(_rms_norm(q, q_norm).astype(F32), cos, sin) * (HEAD_DIM ** -0.5)
    k = _apply_rope(_rms_norm(k, k_norm).astype(F32), cos, sin)
    nb = S // Q_BLOCK
    qb = q.reshape(Bn, nb, Q_BLOCK, ATT_KV_HEADS, G, HEAD_DIM).transpose(1, 0, 3, 4, 2, 5)

    def block(q_blk):
        s = jnp.einsum('bkgqd,bskd->bkgqs', q_blk, k)
        pr = jax.nn.softmax(s, axis=-1).astype(v.dtype)
        return jnp.einsum('bkgqs,bskd->bqkgd', pr, v)

    o = lax.map(block, qb)
    return o.transpose(1, 0, 2, 3, 4, 5).reshape(Bn, S, ATT_DIM).astype(p.dtype)


def _gla_chunked(q, k, v, g, strict):
    Bn, H, S, dk = q.shape
    dv = v.shape[-1]
    C = GLA_CHUNK
    n = S // C
    q = q.reshape(Bn, H, n, C, dk)
    k = k.reshape(Bn, H, n, C, dk)
    g = g.reshape(Bn, H, n, C, dk)
    v = v.reshape(Bn, H, n, C, dv)
    b = jnp.cumsum(g, axis=3)
    b_mid = b[:, :, :, C // 2:C // 2 + 1]
    b_last = b[:, :, :, C - 1:C]
    att = jnp.einsum('bhnid,bhnjd->bhnij', q * jnp.exp(b - b_mid), k * jnp.exp(b_mid - b))
    mask = jnp.tril(jnp.ones((C, C), dtype=bool), k=-1 if strict else 0)
    att = jnp.where(mask, att, 0.0)
    o = jnp.einsum('bhnij,bhnjv->bhniv', att, v)
    kv = jnp.einsum('bhncd,bhncv->bhndv', k * jnp.exp(b_last - b), v)
    dec = jnp.exp(b_last[:, :, :, 0, :])

    def step(state, inp):
        d, kv_n = inp
        return state * d[..., None] + kv_n, state

    init = jnp.zeros((Bn, H, dk, dv), F32)
    _, states = lax.scan(step, init, (jnp.moveaxis(dec, 2, 0), jnp.moveaxis(kv, 2, 0)))
    states = jnp.moveaxis(states, 0, 2)
    o = o + jnp.einsum('bhncd,bhndv->bhncv', q * jnp.exp(b), states)
    return o.reshape(Bn, H, S, dv)


def _gla_mixer(p, gate_up_f, gate_bias_f, gate_up_b, gate_bias_b, out_norm):
    Bn, S, _ = p.shape
    o1 = GLA_DK
    o2 = 2 * GLA_DK
    o3 = o2 + GLA_DV
    o4 = o3 + GLA_GATE_RANK
    gd = p[..., o3:o4]
    og = p[..., o4:]

    def heads(t, d):
        return t.reshape(Bn, S, GLA_HEADS, d).transpose(0, 2, 1, 3).astype(F32)

    qh = heads(p[..., :o1], GLA_DKH) * (GLA_DKH ** -0.5)
    kh = heads(p[..., o1:o2], GLA_DKH)
    vh = heads(p[..., o2:o3], GLA_DVH)

    def log_gate(up, bias):
        return heads(jax.nn.log_sigmoid((gd @ up + bias).astype(F32)) / GLA_GATE_NORM, GLA_DKH)

    def flip(t):
        return jnp.flip(t, axis=2)

    o_f = _gla_chunked(qh, kh, vh, log_gate(gate_up_f, gate_bias_f), strict=False)
    o_b = flip(_gla_chunked(flip(qh), flip(kh), flip(vh), flip(log_gate(gate_up_b, gate_bias_b)), strict=True))
    o = _rms_norm(o_f + o_b, out_norm)
    o = o.transpose(0, 2, 1, 3).reshape(Bn, S, GLA_DV).astype(p.dtype)
    return o * jax.nn.silu(og)


def _moe_swiglu(h, router, w_gate, w_up, w_down):
    Bn, S, D = h.shape
    T = Bn * S
    A = T * TOP_K
    hf = h.reshape(T, D)
    logits = (hf @ router).astype(F32)
    top_val, top_idx = lax.top_k(logits, TOP_K)
    gates = jax.nn.softmax(top_val, axis=-1)
    e_flat = top_idx.reshape(A)
    tok_flat = jnp.repeat(jnp.arange(T, dtype=jnp.int32), TOP_K)
    g_flat = gates.reshape(A)
    order = jnp.argsort(e_flat)
    e_sorted = e_flat[order]
    counts = jnp.zeros((N_EXPERTS,), jnp.int32).at[e_flat].add(1)
    padded = (counts + MOE_BLOCK - 1) // MOE_BLOCK * MOE_BLOCK
    start = jnp.cumsum(counts) - counts
    ends = jnp.cumsum(padded)
    pstart = ends - padded
    dest = pstart[e_sorted] + jnp.arange(A, dtype=jnp.int32) - start[e_sorted]
    P = (A + MOE_BLOCK - 1) // MOE_BLOCK * MOE_BLOCK + N_EXPERTS * MOE_BLOCK
    nb = P // MOE_BLOCK
    row_tok = jnp.zeros((P,), jnp.int32).at[dest].set(tok_flat[order])
    row_gate = jnp.zeros((P,), F32).at[dest].set(g_flat[order])
    blk_start = jnp.arange(nb, dtype=jnp.int32) * MOE_BLOCK
    blk_exp = jnp.minimum(jnp.searchsorted(ends, blk_start, side='right'), N_EXPERTS - 1)
    xb = hf[row_tok].reshape(nb, MOE_BLOCK, D)

    def expert_block(inp):
        x_blk, e = inp
        return (jax.nn.silu(x_blk @ w_gate[e]) * (x_blk @ w_up[e])) @ w_down[e]

    yb = lax.map(expert_block, (xb, blk_exp)).reshape(P, D)
    y = jnp.zeros((T, D), h.dtype).at[row_tok].add((yb * row_gate[:, None]).astype(h.dtype))
    return y.reshape(Bn, S, D)


def _even_layer(x, norm1, w_in, shift_mu, w0_f, w_up_f, w0_b, w_up_b, a0_f, a_up_f, a0_b, a_up_b, g_up,
                k_k, k_a, r_k, gn_w, gn_b, q_norm, k_norm, w_out, norm2, ffn_gate, ffn_up, ffn_down):
    Bn, S, D = x.shape
    T = Bn * S
    xf = x.reshape(T, D)
    p = norm_matmul(xf, norm1, w_in.astype(BF16), tm=512, tn=512).reshape(Bn, S, -1)
    y_a = _rwkv7_mixer(p[..., :RW_IN], shift_mu, w0_f, w_up_f, w0_b, w_up_b, a0_f, a_up_f, a0_b, a_up_b,
                       g_up, k_k, k_a, r_k, gn_w, gn_b)
    y_b = _gqa_axial_attention(p[..., RW_IN:], q_norm, k_norm)
    y = jnp.concatenate([y_a, y_b], axis=-1).reshape(T, -1)
    xf = matmul_residual(y, w_out.astype(BF16), xf, tm=512, tn=512)
    xf = ffn_swiglu(xf, norm2, ffn_gate.astype(BF16), ffn_up.astype(BF16), ffn_down.astype(BF16), tm=512, tf=1408)
    return xf.reshape(Bn, S, D)


def _odd_layer(x, norm1, w_in, gate_up_f, gate_bias_f, gate_up_b, gate_bias_b, out_norm, w_out,
               norm2, router, exp_gate, exp_up, exp_down):
    Bn, S, D = x.shape
    T = Bn * S
    xf = x.reshape(T, D)
    o3 = 2 * GLA_DK + GLA_DV
    o4 = o3 + GLA_GATE_RANK
    w_main = jnp.concatenate([w_in[:, :o3], w_in[:, o4:]], axis=1).astype(BF16)
    p_main = norm_matmul(xf, norm1, w_main, tm=512, tn=512)
    gd = _rms_norm(xf, norm1) @ w_in[:, o3:o4]
    p = jnp.concatenate([p_main[:, :o3], gd, p_main[:, o3:]], axis=1).reshape(Bn, S, -1)
    y = _gla_mixer(p, gate_up_f, gate_bias_f, gate_up_b, gate_bias_b, out_norm).reshape(T, -1)
    xf = matmul_residual(y, w_out.astype(BF16), xf, tm=512, tn=512)
    x = xf.reshape(Bn, S, D)
    return x + _moe_swiglu(_rms_norm(x, norm2), router, exp_gate, exp_up, exp_down)


def kernel(x, e_norm1, e_w_in, e_shift_mu, e_w0_f, e_w_up_f, e_w0_b, e_w_up_b, e_a0_f, e_a_up_f, e_a0_b, e_a_up_b, e_g_up, e_k_k, e_k_a, e_r_k, e_gn_w, e_gn_b, e_q_norm, e_k_norm, e_w_out, e_norm2, e_ffn_gate, e_ffn_up, e_ffn_down, o_norm1, o_w_in, o_gate_up_f, o_gate_bias_f, o_gate_up_b, o_gate_bias_b, o_out_norm, o_w_out, o_norm2, o_router, o_exp_gate, o_exp_up, o_exp_down):
    x = _even_layer(x, e_norm1[0], e_w_in[0], e_shift_mu[0], e_w0_f[0], e_w_up_f[0], e_w0_b[0],
                    e_w_up_b[0], e_a0_f[0], e_a_up_f[0], e_a0_b[0], e_a_up_b[0], e_g_up[0], e_k_k[0],
                    e_k_a[0], e_r_k[0], e_gn_w[0], e_gn_b[0], e_q_norm[0], e_k_norm[0], e_w_out[0],
                    e_norm2[0], e_ffn_gate[0], e_ffn_up[0], e_ffn_down[0])
    x = _odd_layer(x, o_norm1[0], o_w_in[0], o_gate_up_f[0], o_gate_bias_f[0], o_gate_up_b[0],
                   o_gate_bias_b[0], o_out_norm[0], o_w_out[0], o_norm2[0], o_router[0],
                   o_exp_gate[0], o_exp_up[0], o_exp_down[0])
    return x
```

```python
import jax
import jax.numpy as jnp
from jax import lax
from jax.experimental import pallas as pl
from jax.experimental.pallas import tpu as pltpu

F32 = jnp.float32
BF16 = jnp.bfloat16
HI = lax.Precision.HIGHEST

D_MODEL = 1024
GRID_W = 64
HEAD_DIM = 64
NORM_EPS = 1e-6
RW_HEADS = 8
RW_DIM = 512
DECAY_RANK = 64
ICLR_RANK = 64
GATE_RANK = 128
RWKV_GN_EPS = 64e-5
RW_IN = 3 * RW_DIM + DECAY_RANK + ICLR_RANK + GATE_RANK
RW_CHUNK = 64
ATT_HEADS = 8
ATT_KV_HEADS = 2
ATT_DIM = 512
ATT_KV_DIM = 128
ATT_IN = ATT_DIM + 2 * ATT_KV_DIM
ROPE_THETA = 10000.0
GLA_HEADS = 4
GLA_DK = 512
GLA_DV = 1024
GLA_DKH = 128
GLA_DVH = 256
GLA_GATE_RANK = 16
GLA_GATE_NORM = 16.0
GLA_CHUNK = 64
N_EXPERTS = 8
TOP_K = 2
LANES = 128

VMEM_LIMIT = 48 * 1024 * 1024


def _cparams(sem):
    return pltpu.CompilerParams(dimension_semantics=sem, vmem_limit_bytes=VMEM_LIMIT)


def _iota2(shape, dim):
    return lax.broadcasted_iota(jnp.int32, shape, dim)


def _dot(a, b, precision=HI):
    return jnp.dot(a, b, preferred_element_type=F32, precision=precision)


def _bdot(a, b):
    return jnp.dot(a.astype(BF16), b.astype(BF16), preferred_element_type=F32)


def _bdot_nt(a, b):
    return lax.dot_general(a.astype(BF16), b.astype(BF16), (((1,), (1,)), ((), ())), preferred_element_type=F32)


def _bdot_tn(a, b):
    return lax.dot_general(a.astype(BF16), b.astype(BF16), (((0,), (0,)), ((), ())), preferred_element_type=F32)


def _norm_mm_kernel(x_ref, g_ref, w_ref, o_ref, xn_ref):
    @pl.when(pl.program_id(1) == 0)
    def _():
        x = x_ref[...]
        ms = jnp.mean(x * x, axis=-1, keepdims=True)
        xn_ref[...] = (x * lax.rsqrt(ms + NORM_EPS) * g_ref[...]).astype(xn_ref.dtype)

    precision = HI if xn_ref.dtype == F32 else None
    o_ref[...] = jnp.dot(xn_ref[...], w_ref[...], preferred_element_type=F32, precision=precision).astype(o_ref.dtype)


def norm_matmul(x, g, w, *, tm, tn, out_dtype=F32):
    M, K = x.shape
    N = w.shape[1]
    return pl.pallas_call(
        _norm_mm_kernel,
        grid=(M // tm, N // tn),
        in_specs=[
            pl.BlockSpec((tm, K), lambda i, j: (i, 0)),
            pl.BlockSpec((1, K), lambda i, j: (0, 0)),
            pl.BlockSpec((K, tn), lambda i, j: (0, j)),
        ],
        out_specs=pl.BlockSpec((tm, tn), lambda i, j: (i, j)),
        out_shape=jax.ShapeDtypeStruct((M, N), out_dtype),
        scratch_shapes=[pltpu.VMEM((tm, K), w.dtype)],
        compiler_params=_cparams(("parallel", "arbitrary")),
        name="norm_matmul",
    )(x, g.reshape(1, K), w)


def _mm_res_kernel(*refs):
    n = (len(refs) - 2) // 2
    r_ref, o_ref = refs[2 * n], refs[2 * n + 1]
    acc = r_ref[...]
    for y_ref, w_ref in zip(refs[:n], refs[n:2 * n]):
        acc = acc + jnp.dot(y_ref[...].astype(BF16), w_ref[...], preferred_element_type=F32)
    o_ref[...] = acc


def matmul_residual(ys, ws, r, *, tm, tn):
    M, N = r.shape
    y_specs = [pl.BlockSpec((tm, y.shape[1]), lambda i, j: (i, 0)) for y in ys]
    w_specs = [pl.BlockSpec((w.shape[0], tn), lambda i, j: (0, j)) for w in ws]
    return pl.pallas_call(
        _mm_res_kernel,
        grid=(M // tm, N // tn),
        in_specs=y_specs + w_specs + [pl.BlockSpec((tm, tn), lambda i, j: (i, j))],
        out_specs=pl.BlockSpec((tm, tn), lambda i, j: (i, j)),
        out_shape=jax.ShapeDtypeStruct((M, N), F32),
        compiler_params=_cparams(("parallel", "arbitrary")),
        name="matmul_residual",
    )(*ys, *ws, r)


def _ffn_kernel(x_ref, g_ref, wg_ref, wu_ref, wd_ref, o_ref, xn_ref, acc_ref):
    f = pl.program_id(1)

    @pl.when(f == 0)
    def _():
        x = x_ref[...]
        ms = jnp.mean(x * x, axis=-1, keepdims=True)
        xn_ref[...] = (x * lax.rsqrt(ms + NORM_EPS) * g_ref[...]).astype(BF16)

    xn = xn_ref[...]
    a = jnp.dot(xn, wg_ref[...], preferred_element_type=F32)
    b = jnp.dot(xn, wu_ref[...], preferred_element_type=F32)
    h = (a * jax.nn.sigmoid(a) * b).astype(BF16)
    part = jnp.dot(h, wd_ref[...], preferred_element_type=F32)

    @pl.when(f == 0)
    def _():
        acc_ref[...] = part

    @pl.when(f > 0)
    def _():
        acc_ref[...] += part

    @pl.when(f == pl.num_programs(1) - 1)
    def _():
        o_ref[...] = x_ref[...] + acc_ref[...]


def ffn_swiglu(x, g, wg, wu, wd, *, tm, tf):
    M, D = x.shape
    F = wg.shape[1]
    return pl.pallas_call(
        _ffn_kernel,
        grid=(M // tm, F // tf),
        in_specs=[
            pl.BlockSpec((tm, D), lambda i, f: (i, 0)),
            pl.BlockSpec((1, D), lambda i, f: (0, 0)),
            pl.BlockSpec((D, tf), lambda i, f: (0, f)),
            pl.BlockSpec((D, tf), lambda i, f: (0, f)),
            pl.BlockSpec((tf, D), lambda i, f: (f, 0)),
        ],
        out_specs=pl.BlockSpec((tm, D), lambda i, f: (i, 0)),
        out_shape=jax.ShapeDtypeStruct((M, D), F32),
        scratch_shapes=[pltpu.VMEM((tm, D), BF16), pltpu.VMEM((tm, D), F32)],
        compiler_params=_cparams(("parallel", "arbitrary")),
        name="ffn_swiglu",
    )(x, g.reshape(1, D), wg, wu, wd)


def _rwkv_prep_kernel(cur_ref, prev_ref, next_ref, mu_ref, w0f_ref, wupf_ref, w0b_ref, wupb_ref,
                      a0f_ref, aupf_ref, a0b_ref, aupb_ref, gup_ref, kk_ref, hsum_ref,
                      r_ref, k_ref, v_ref, kkn_ref, g_ref, lwf_ref, lf_ref, asf_ref, lwb_ref, lb_ref, asb_ref):
    i = pl.program_id(1)
    nt = pl.num_programs(1)
    x = cur_ref[0]
    Tt = x.shape[0]
    row = _iota2(x.shape, 0)
    prev_row = jnp.where(i > 0, prev_ref[0, 7:8, :], 0.0)
    next_row = jnp.where(i < nt - 1, next_ref[0, 0:1, :], 0.0)
    prev = jnp.where(row == 0, prev_row, pltpu.roll(x, 1, 0))
    nxt = jnp.where(row == Tt - 1, next_row, pltpu.roll(x, Tt - 1, 0))
    pm = x + (0.5 * (prev + nxt) - x) * mu_ref[...]
    o1 = 3 * RW_DIM
    o2 = o1 + DECAY_RANK
    o3 = o2 + ICLR_RANK
    r_ref[0] = pm[:, :RW_DIM]
    k = pm[:, RW_DIM:2 * RW_DIM]
    k_ref[0] = k
    v_ref[0] = pm[:, 2 * RW_DIM:o1]
    wd = jnp.tanh(pm[:, o1:o2])
    ad = pm[:, o2:o3]
    gin = jax.nn.sigmoid(pm[:, o3:])
    g_ref[0] = _dot(gin, gup_ref[...])
    kk = k * kk_ref[...]
    ss = _dot(kk * kk, hsum_ref[...])
    kkn_ref[0] = kk / jnp.maximum(jnp.sqrt(ss), 1e-12)

    ci = _iota2((RW_CHUNK, RW_CHUNK), 0)
    cj = _iota2((RW_CHUNK, RW_CHUNK), 1)
    tril = (cj <= ci).astype(F32)
    triu = (cj >= ci).astype(F32)

    def direction(w0_ref, wup_ref, a0_ref, aup_ref, tri, lw_ref, l_ref, as_ref):
        z = w0_ref[...] + _dot(wd, wup_ref[...])
        sp = jnp.maximum(-z, 0.0) + jnp.log(1.0 + jnp.exp(-jnp.abs(z)))
        lw = -jnp.exp(-sp - 0.5)
        lw_ref[0] = lw
        as_ref[0] = jax.nn.sigmoid(a0_ref[...] + _dot(ad, aup_ref[...]))
        for c in range(Tt // RW_CHUNK):
            sl = slice(c * RW_CHUNK, (c + 1) * RW_CHUNK)
            l_ref[0, sl, :] = _dot(tri, lw[sl, :])

    direction(w0f_ref, wupf_ref, a0f_ref, aupf_ref, tril, lwf_ref, lf_ref, asf_ref)
    direction(w0b_ref, wupb_ref, a0b_ref, aupb_ref, triu, lwb_ref, lb_ref, asb_ref)


def rwkv_prep(p, shift_mu, w0_f, w_up_f, w0_b, w_up_b, a0_f, a_up_f, a0_b, a_up_b, g_up, k_k, *, tt=256):
    B, S, _ = p.shape
    nt = S // tt
    hsum = (jnp.arange(RW_DIM)[:, None] // HEAD_DIM == jnp.arange(RW_DIM)[None, :] // HEAD_DIM).astype(F32)
    row = lambda a: a.reshape(1, -1)
    full = lambda a: pl.BlockSpec(a.shape, lambda b, i: (0,) * a.ndim)
    params = [row(shift_mu), row(w0_f), w_up_f, row(w0_b), w_up_b, row(a0_f), a_up_f, row(a0_b), a_up_b, g_up,
              row(k_k), hsum]
    tb = tt // 8
    in_specs = [
        pl.BlockSpec((1, tt, RW_IN), lambda b, i: (b, i, 0)),
        pl.BlockSpec((1, 8, RW_IN), lambda b, i: (b, jnp.maximum(i * tb - 1, 0), 0)),
        pl.BlockSpec((1, 8, RW_IN), lambda b, i: (b, jnp.minimum((i + 1) * tb, S // 8 - 1), 0)),
    ] + [full(a) for a in params]
    out_spec = pl.BlockSpec((1, tt, RW_DIM), lambda b, i: (b, i, 0))
    out_sds = jax.ShapeDtypeStruct((B, S, RW_DIM), F32)
    return pl.pallas_call(
        _rwkv_prep_kernel,
        grid=(B, nt),
        in_specs=in_specs,
        out_specs=[out_spec] * 11,
        out_shape=[out_sds] * 11,
        compiler_params=_cparams(("parallel", "arbitrary")),
        name="rwkv_prep",
    )(p, p, p, *params)


def _chunk_terms(chains):
    C = chains[0]["r"].shape[0]
    n = len(chains)
    R = range(n)
    pre = []
    for ch in chains:
        r, k, v, kk, lw, L, asig, ka, rev = (ch[x] for x in ("r", "k", "v", "kk", "lw", "L", "asig", "ka", "rev"))
        a = -kk
        b = kk * asig
        kd = k * (1.0 + (asig - 1.0) * ka)
        Lp = L - lw
        Lr = Lp if rev else L
        Lend = L[0:1, :] if rev else L[C - 1:C, :]
        Lmid = L[C // 2:C // 2 + 1, :]
        einv = jnp.exp(Lmid - L)
        eend = jnp.exp(Lend - L)
        pre.append(dict(
            ar=jnp.concatenate([a * jnp.exp(Lp - Lmid), r * jnp.exp(Lr - Lmid)], axis=0).astype(BF16),
            bt=(b * einv).astype(BF16), kt=(kd * einv).astype(BF16),
            a0=(a * jnp.exp(Lp)).astype(BF16), r0=r * jnp.exp(Lr),
            bh=(b * eend).astype(BF16), kh=(kd * eend).astype(BF16), vb=v.astype(BF16),
            dec=jnp.exp(Lend)))
    mk = [ch["masks"] for ch in chains]
    Ab = [_bdot_nt(pre[i]["ar"], pre[i]["bt"]) for i in R]
    Ak = [_bdot_nt(pre[i]["ar"], pre[i]["kt"]) for i in R]
    Aab = [jnp.where(mk[i][0], Ab[i][:C], 0.0) for i in R]
    Arb = [jnp.where(mk[i][1], Ab[i][C:], 0.0).astype(BF16) for i in R]
    AakArk = [jnp.concatenate([jnp.where(mk[i][0], Ak[i][:C], 0.0), jnp.where(mk[i][1], Ak[i][C:], 0.0)],
                              axis=0).astype(BF16) for i in R]
    AV = [jnp.dot(AakArk[i], pre[i]["vb"], preferred_element_type=F32) for i in R]
    Xb = [jnp.where(mk[i][2], Aab[i], 0.0).astype(BF16) for i in R]
    X2b = [jnp.dot(Xb[i], Xb[i], preferred_element_type=F32).astype(BF16) for i in R]
    T = [mk[i][6] + Xb[i].astype(F32) for i in R]
    T = [T[i] + _bdot(T[i], X2b[i]) for i in R]
    X4b = [jnp.dot(X2b[i], X2b[i], preferred_element_type=F32).astype(BF16) for i in R]
    T = [T[i] + _bdot(T[i], X4b[i]) for i in R]
    for lvl in (3, 4, 5):
        Tb = [T[i].astype(BF16) for i in R]
        ET = [jnp.dot(jnp.where(mk[i][lvl], Aab[i], 0.0).astype(BF16), Tb[i], preferred_element_type=F32).astype(BF16)
              for i in R]
        T = [T[i] + jnp.dot(Tb[i], ET[i], preferred_element_type=F32) for i in R]
    Tb = [T[i].astype(BF16) for i in R]
    A0p = [jnp.dot(Tb[i], pre[i]["a0"], preferred_element_type=F32).astype(BF16) for i in R]
    Uv = [jnp.dot(Tb[i], AV[i][:C].astype(BF16), preferred_element_type=F32).astype(BF16) for i in R]
    Rpp = [pre[i]["r0"] + jnp.dot(Arb[i], A0p[i], preferred_element_type=F32) for i in R]
    Yv = [jnp.dot(Arb[i], Uv[i], preferred_element_type=F32) + AV[i][C:] for i in R]
    P = [_bdot_tn(pre[i]["bh"], A0p[i]) + mk[i][6] * pre[i]["dec"] for i in R]
    Q = [_bdot_tn(pre[i]["bh"], Uv[i]) + _bdot_tn(pre[i]["kh"], pre[i]["vb"]) for i in R]
    return [(Rpp[i], Yv[i], P[i], Q[i]) for i in R]


def _make_masks(C, rev):
    ii = _iota2((C, C), 0)
    jj = _iota2((C, C), 1)
    strict = (jj > ii) if rev else (jj < ii)
    rmask = strict if rev else (jj <= ii)
    blk = lambda s: (ii // s) == (jj // s)
    m8 = blk(8)
    e16 = blk(16) & jnp.logical_not(blk(8))
    e32 = blk(32) & jnp.logical_not(blk(16))
    e64 = jnp.logical_not(blk(32))
    eye = (ii == jj).astype(F32)
    return strict, rmask, m8, e16, e32, e64, eye


RW_GROUP = 8
RW_UNROLL = 4


def _rwkv_scan_kernel(r_ref, k_ref, v_ref, kk_ref, g_ref, lwf_ref, lf_ref, asf_ref, lwb_ref, lb_ref, asb_ref,
                      ka_ref, rk_ref, gnw_ref, gnb_ref, havg_ref, o_ref,
                      yf_ref, yb_ref, h_ref, rpp_ref, yv_ref, p_ref, q_ref):
    S = r_ref.shape[1]
    C = RW_CHUNK
    N = HEAD_DIM
    n = S // C
    G = RW_GROUP
    masks_f = _make_masks(C, False)
    masks_b = _make_masks(C, True)
    h_ref[...] = jnp.zeros_like(h_ref)
    dirs = ((False, lwf_ref, lf_ref, asf_ref, yf_ref, masks_f), (True, lwb_ref, lb_ref, asb_ref, yb_ref, masks_b))

    def chunk_index(rev, gi, j):
        c = gi * G + j
        return (n - 1 - c) if rev else c

    def group(gi, carry):
        def phase_a(it, carry):
            chains, slots = [], []
            for u in range(RW_UNROLL):
                j = it * RW_UNROLL + u
                for di, (rev, lw_ref, l_ref, as_ref, y_ref, masks) in enumerate(dirs):
                    rows = pl.ds(pl.multiple_of(chunk_index(rev, gi, j) * C, C), C)
                    for hh in range(2):
                        ls = slice(hh * N, (hh + 1) * N)
                        chains.append(dict(
                            r=r_ref[0, rows, ls], k=k_ref[0, rows, ls], v=v_ref[0, rows, ls], kk=kk_ref[0, rows, ls],
                            lw=lw_ref[0, rows, ls], L=l_ref[0, rows, ls], asig=as_ref[0, rows, ls], ka=ka_ref[:, ls],
                            rev=rev, masks=masks))
                        slots.append((di * 2 + hh, j))
            for (s, j), (Rpp, Yv, P, Q) in zip(slots, _chunk_terms(chains)):
                rpp_ref[s, j] = Rpp.astype(BF16)
                yv_ref[s, j] = Yv
                p_ref[s, j] = P.astype(BF16)
                q_ref[s, j] = Q
            return carry

        lax.fori_loop(0, G // RW_UNROLL, phase_a, 0)

        def phase_b(j, carry):
            for di, (rev, lw_ref, l_ref, as_ref, y_ref, masks) in enumerate(dirs):
                rows = pl.ds(pl.multiple_of(chunk_index(rev, gi, j) * C, C), C)
                ys = []
                for hh in range(2):
                    s = di * 2 + hh
                    Hb = h_ref[s].astype(BF16)
                    ys.append(jnp.dot(rpp_ref[s, j], Hb, preferred_element_type=F32) + yv_ref[s, j])
                    h_ref[s] = jnp.dot(p_ref[s, j], Hb, preferred_element_type=F32) + q_ref[s, j]
                y_ref[rows, :] = jnp.concatenate(ys, axis=1)
            return carry

        lax.fori_loop(0, G, phase_b, 0)
        return carry

    lax.fori_loop(0, n // G, group, 0)

    RT = 256

    def fin(i, carry):
        rows = pl.ds(pl.multiple_of(i * RT, RT), RT)
        y = yf_ref[rows, :] + yb_ref[rows, :]
        mean = _dot(y, havg_ref[...])
        d = y - mean
        var = _dot(d * d, havg_ref[...])
        yn = d * lax.rsqrt(var + RWKV_GN_EPS) * gnw_ref[...] + gnb_ref[...]
        r = r_ref[0, rows, :]
        kf = k_ref[0, rows, :] * (1.0 + (asf_ref[0, rows, :] - 1.0) * ka_ref[...])
        bonus = _dot(r * kf * rk_ref[...], havg_ref[...]) * float(N) * v_ref[0, rows, :]
        o_ref[0, rows, :] = ((yn + bonus) * g_ref[0, rows, :]).astype(o_ref.dtype)
        return carry

    lax.fori_loop(0, S // RT, fin, 0)


def rwkv_scan(r, k, v, kk, g, lw_f, l_f, as_f, lw_b, l_b, as_b, k_a, r_k, gn_w, gn_b, out_dtype=F32):
    B, S, _ = r.shape
    W = 2 * HEAD_DIM
    N = HEAD_DIM
    havg = (jnp.arange(W)[:, None] // HEAD_DIM == jnp.arange(W)[None, :] // HEAD_DIM).astype(F32) / HEAD_DIM
    seq = pl.BlockSpec((1, S, W), lambda b, h: (b, 0, h))
    par = pl.BlockSpec((1, W), lambda b, h: (0, h))
    row = lambda a: a.reshape(1, -1)
    G = RW_GROUP
    return pl.pallas_call(
        _rwkv_scan_kernel,
        grid=(B, RW_DIM // W),
        in_specs=[seq] * 11 + [par] * 4 + [pl.BlockSpec((W, W), lambda b, h: (0, 0))],
        out_specs=seq,
        out_shape=jax.ShapeDtypeStruct((B, S, RW_DIM), out_dtype),
        scratch_shapes=[pltpu.VMEM((S, W), F32), pltpu.VMEM((S, W), F32), pltpu.VMEM((4, N, N), F32),
                        pltpu.VMEM((4, G, RW_CHUNK, N), BF16), pltpu.VMEM((4, G, RW_CHUNK, N), F32),
                        pltpu.VMEM((4, G, N, N), BF16), pltpu.VMEM((4, G, N, N), F32)],
        compiler_params=_cparams(("parallel", "parallel")),
        name="rwkv_scan",
    )(r, k, v, kk, g, lw_f, l_f, as_f, lw_b, l_b, as_b, row(k_a), row(r_k), row(gn_w), row(gn_b), havg)


def rwkv7_mixer(p, shift_mu, w0_f, w_up_f, w0_b, w_up_b, a0_f, a_up_f, a0_b, a_up_b, g_up, k_k, k_a, r_k, gn_w,
                gn_b, out_dtype=F32, tt=256):
    outs = rwkv_prep(p, shift_mu, w0_f, w_up_f, w0_b, w_up_b, a0_f, a_up_f, a0_b, a_up_b, g_up, k_k, tt=tt)
    return rwkv_scan(*outs, k_a, r_k, gn_w, gn_b, out_dtype=out_dtype)


def _head_norm_rope(x, gain, cos, sin_signed, havg):
    ms = jnp.dot(x * x, havg, preferred_element_type=F32, precision=HI)
    xn = x * lax.rsqrt(ms + NORM_EPS) * gain
    W = x.shape[1]
    even = (_iota2(x.shape, 1) % 2) == 0
    partner = jnp.where(even, pltpu.roll(xn, W - 1, 1), pltpu.roll(xn, 1, 1))
    return xn * cos + partner * sin_signed


def _attn_kernel(q_ref, k_ref, v_ref, cosq_ref, sinq_ref, cosk_ref, sink_ref, qg_ref, kg_ref, hq_ref, hk_ref,
                 o_ref, ks_ref, vs_ref):
    i = pl.program_id(1)
    G = ATT_HEADS // ATT_KV_HEADS
    D = HEAD_DIM

    @pl.when(i == 0)
    def _():
        kr = _head_norm_rope(k_ref[0], kg_ref[...], cosk_ref[...], sink_ref[...], hk_ref[...])
        for kv in range(ATT_KV_HEADS):
            ks_ref[kv] = kr[:, kv * D:(kv + 1) * D].astype(BF16)
            vs_ref[kv] = v_ref[0, :, kv * D:(kv + 1) * D].astype(BF16)

    q = _head_norm_rope(q_ref[0], qg_ref[...], cosq_ref[...], sinq_ref[...], hq_ref[...]) * (D ** -0.5)
    qb = q.astype(BF16)
    outs = []
    for h in range(ATT_HEADS):
        kv = h // G
        s = lax.dot_general(qb[:, h * D:(h + 1) * D], ks_ref[kv], (((1,), (1,)), ((), ())),
                            preferred_element_type=F32)
        m = jnp.max(s, axis=-1, keepdims=True)
        p = jnp.exp(s - m)
        l = jnp.sum(p, axis=-1, keepdims=True)
        o = jnp.dot(p.astype(BF16), vs_ref[kv], preferred_element_type=F32)
        outs.append(o / l)
    o_ref[0] = jnp.concatenate(outs, axis=1).astype(o_ref.dtype)


def _rope_tables(S):
    rows = S // GRID_W
    row = jnp.repeat(jnp.arange(rows), GRID_W).astype(F32)
    col = jnp.tile(jnp.arange(GRID_W), rows).astype(F32)
    half = HEAD_DIM // 2
    freq = ROPE_THETA ** (-jnp.arange(0, half, 2, dtype=F32) / half)
    ang = jnp.concatenate([row[:, None] * freq, col[:, None] * freq], axis=-1)
    cos = jnp.repeat(jnp.cos(ang), 2, axis=-1)
    sin = jnp.repeat(jnp.sin(ang), 2, axis=-1)
    sign = jnp.where(jnp.arange(HEAD_DIM) % 2 == 0, -1.0, 1.0).astype(F32)
    return cos, sin * sign


def gqa_attention(p, q_norm, k_norm, *, tq=256, out_dtype=F32):
    B, S, _ = p.shape
    cos, sin = _rope_tables(S)
    tile = lambda t, n: jnp.tile(t, (1, n))
    hq = (jnp.arange(ATT_DIM)[:, None] // HEAD_DIM == jnp.arange(ATT_DIM)[None, :] // HEAD_DIM).astype(F32) / HEAD_DIM
    hk = hq[:ATT_KV_DIM, :ATT_KV_DIM]
    qg = jnp.tile(q_norm, ATT_HEADS).reshape(1, ATT_DIM)
    kg = jnp.tile(k_norm, ATT_KV_HEADS).reshape(1, ATT_KV_DIM)
    nq = ATT_DIM // ATT_KV_DIM
    const = lambda a: pl.BlockSpec(a.shape, lambda b, i: (0, 0))
    return pl.pallas_call(
        _attn_kernel,
        grid=(B, S // tq),
        in_specs=[
            pl.BlockSpec((1, tq, ATT_DIM), lambda b, i: (b, i, 0)),
            pl.BlockSpec((1, S, ATT_KV_DIM), lambda b, i: (b, 0, nq)),
            pl.BlockSpec((1, S, ATT_KV_DIM), lambda b, i: (b, 0, nq + 1)),
            pl.BlockSpec((tq, ATT_DIM), lambda b, i: (i, 0)),
            pl.BlockSpec((tq, ATT_DIM), lambda b, i: (i, 0)),
            pl.BlockSpec((S, ATT_KV_DIM), lambda b, i: (0, 0)),
            pl.BlockSpec((S, ATT_KV_DIM), lambda b, i: (0, 0)),
            const(qg), const(kg), const(hq), const(hk),
        ],
        out_specs=pl.BlockSpec((1, tq, ATT_DIM), lambda b, i: (b, i, 0)),
        out_shape=jax.ShapeDtypeStruct((B, S, ATT_DIM), out_dtype),
        scratch_shapes=[pltpu.VMEM((ATT_KV_HEADS, S, HEAD_DIM), BF16), pltpu.VMEM((ATT_KV_HEADS, S, HEAD_DIM), BF16)],
        compiler_params=_cparams(("parallel", "arbitrary")),
        name="gqa_attention",
    )(p, p, p, tile(cos, ATT_HEADS), tile(sin, ATT_HEADS), tile(cos, ATT_KV_HEADS), tile(sin, ATT_KV_HEADS),
      qg, kg, hq, hk)


GLA_UNROLL = 4


def _gla_kernel(q_ref, k_ref, v_ref, gd_ref, og_ref, upf_ref, bf_ref, upb_ref, bb_ref, on_ref, o_ref,
                yf_ref, yb_ref, kv_ref, qe_ref, dec_ref):
    S = q_ref.shape[1]
    C = GLA_CHUNK
    n = S // C
    ii = _iota2((C, C), 0)
    jj = _iota2((C, C), 1)
    dirs = (
        (False, (jj <= ii).astype(F32), jj <= ii, upf_ref, bf_ref, yf_ref),
        (True, (jj >= ii).astype(F32), jj > ii, upb_ref, bb_ref, yb_ref),
    )
    scale = GLA_DKH ** -0.5

    def phase_a(it, carry):
        items = []
        for u in range(GLA_UNROLL):
            c = it * GLA_UNROLL + u
            rows = pl.ds(pl.multiple_of(c * C, C), C)
            q = q_ref[0, rows, :] * scale
            k = k_ref[0, rows, :]
            vb = v_ref[0, rows, :].astype(BF16)
            gd = gd_ref[0, rows, :]
            for di, (rev, tri, mask, up_ref, b_ref, y_ref) in enumerate(dirs):
                z = jnp.dot(gd, up_ref[...], preferred_element_type=F32, precision=HI) + b_ref[...]
                g = (jnp.minimum(z, 0.0) - jnp.log(1.0 + jnp.exp(-jnp.abs(z)))) * (1.0 / GLA_GATE_NORM)
                b = jnp.dot(tri, g, preferred_element_type=F32, precision=HI)
                b_mid = b[C // 2:C // 2 + 1, :]
                b_last = b[0:1, :] if rev else b[C - 1:C, :]
                items.append(dict(
                    c=c, di=di, rows=rows, mask=mask, y_ref=y_ref, vb=vb,
                    qm=(q * jnp.exp(b - b_mid)).astype(BF16), km=(k * jnp.exp(b_mid - b)).astype(BF16),
                    ke=(k * jnp.exp(b_last - b)).astype(BF16), qe=(q * jnp.exp(b)).astype(BF16),
                    dec=jnp.exp(b_last)))
        att = [lax.dot_general(x["qm"], x["km"], (((1,), (1,)), ((), ())), preferred_element_type=F32) for x in items]
        att = [jnp.where(x["mask"], a, 0.0).astype(BF16) for x, a in zip(items, att)]
        for x, a in zip(items, att):
            x["y_ref"][x["rows"], :] = jnp.dot(a, x["vb"], preferred_element_type=F32)
            kv_ref[x["di"], x["c"]] = lax.dot_general(x["vb"], x["ke"], (((0,), (0,)), ((), ())),
                                                      preferred_element_type=F32)
            qe_ref[x["di"], x["c"]] = x["qe"]
            dec_ref[x["di"], x["c"]] = x["dec"]
        return carry

    lax.fori_loop(0, n // GLA_UNROLL, phase_a, 0)

    def phase_b(j, states):
        new = []
        for di, (rev, tri, mask, up_ref, b_ref, y_ref) in enumerate(dirs):
            c = (n - 1 - j) if rev else j
            rows = pl.ds(pl.multiple_of(c * C, C), C)
            st = states[di]
            y_ref[rows, :] += lax.dot_general(qe_ref[di, c], st.astype(BF16), (((1,), (1,)), ((), ())),
                                              preferred_element_type=F32)
            new.append(st * dec_ref[di, c] + kv_ref[di, c])
        return tuple(new)

    z0 = jnp.zeros((GLA_DVH, GLA_DKH), F32)
    lax.fori_loop(0, n, phase_b, (z0, z0))

    RT = 256

    def fin(i, carry):
        rows = pl.ds(pl.multiple_of(i * RT, RT), RT)
        o = yf_ref[rows, :] + yb_ref[rows, :]
        ms = jnp.mean(o * o, axis=-1, keepdims=True)
        on = o * lax.rsqrt(ms + NORM_EPS) * on_ref[...]
        og = og_ref[0, rows, :]
        o_ref[0, rows, :] = (on * (og * jax.nn.sigmoid(og))).astype(o_ref.dtype)
        return carry

    lax.fori_loop(0, S // RT, fin, 0)


def gla_mixer(pm, gd, gate_up_f, gate_bias_f, gate_up_b, gate_bias_b, out_norm, out_dtype=F32):
    B, S, _ = pm.shape
    n = S // GLA_CHUNK
    H = GLA_HEADS
    kb = GLA_DK // GLA_DKH
    vb0 = 2 * GLA_DK // GLA_DVH
    ob0 = vb0 + GLA_DV // GLA_DVH
    return pl.pallas_call(
        _gla_kernel,
        grid=(B, H),
        in_specs=[
            pl.BlockSpec((1, S, GLA_DKH), lambda b, h: (b, 0, h)),
            pl.BlockSpec((1, S, GLA_DKH), lambda b, h: (b, 0, kb + h)),
            pl.BlockSpec((1, S, GLA_DVH), lambda b, h: (b, 0, vb0 + h)),
            pl.BlockSpec((1, S, GLA_GATE_RANK), lambda b, h: (b, 0, 0)),
            pl.BlockSpec((1, S, GLA_DVH), lambda b, h: (b, 0, ob0 + h)),
            pl.BlockSpec((GLA_GATE_RANK, GLA_DKH), lambda b, h: (0, h)),
            pl.BlockSpec((1, GLA_DKH), lambda b, h: (0, h)),
            pl.BlockSpec((GLA_GATE_RANK, GLA_DKH), lambda b, h: (0, h)),
            pl.BlockSpec((1, GLA_DKH), lambda b, h: (0, h)),
            pl.BlockSpec((1, GLA_DVH), lambda b, h: (0, 0)),
        ],
        out_specs=pl.BlockSpec((1, S, GLA_DVH), lambda b, h: (b, 0, h)),
        out_shape=jax.ShapeDtypeStruct((B, S, GLA_DV), out_dtype),
        scratch_shapes=[pltpu.VMEM((S, GLA_DVH), F32), pltpu.VMEM((S, GLA_DVH), F32),
                        pltpu.VMEM((2, n, GLA_DVH, GLA_DKH), F32), pltpu.VMEM((2, n, GLA_CHUNK, GLA_DKH), BF16),
                        pltpu.VMEM((2, n, 1, GLA_DKH), F32)],
        compiler_params=_cparams(("parallel", "parallel")),
        name="gla_mixer",
    )(pm, pm, pm, gd, pm, gate_up_f, gate_bias_f.reshape(1, -1), gate_up_b, gate_bias_b.reshape(1, -1),
      out_norm.reshape(1, -1))


def _router_kernel(x_ref, g_ref, wr_ref, hn_ref, idx_ref, gate_ref):
    x = x_ref[...]
    ms = jnp.mean(x * x, axis=-1, keepdims=True)
    hn = x * lax.rsqrt(ms + NORM_EPS) * g_ref[...]
    hn_ref[...] = hn
    logits = jnp.dot(hn, wr_ref[...], preferred_element_type=F32, precision=HI)
    lane = _iota2(logits.shape, 1)
    neg = jnp.float32(-jnp.inf)
    logits = jnp.where(lane < N_EXPERTS, logits, neg)
    m1 = jnp.max(logits, axis=-1, keepdims=True)
    i1 = jnp.min(jnp.where(logits == m1, lane, LANES), axis=-1, keepdims=True)
    rest = jnp.where(lane == i1, neg, logits)
    m2 = jnp.max(rest, axis=-1, keepdims=True)
    i2 = jnp.min(jnp.where(rest == m2, lane, LANES), axis=-1, keepdims=True)
    e2 = jnp.exp(m2 - m1)
    g1 = 1.0 / (1.0 + e2)
    g2 = e2 / (1.0 + e2)
    idx_ref[...] = jnp.where(lane == 0, i1, jnp.where(lane == 1, i2, 0))
    gate_ref[...] = jnp.where(lane == 0, g1, jnp.where(lane == 1, g2, 0.0))


def moe_router(x, g, router, *, tm=512):
    T, D = x.shape
    wr = jnp.zeros((D, LANES), F32).at[:, :N_EXPERTS].set(router)
    return pl.pallas_call(
        _router_kernel,
        grid=(T // tm,),
        in_specs=[
            pl.BlockSpec((tm, D), lambda i: (i, 0)),
            pl.BlockSpec((1, D), lambda i: (0, 0)),
            pl.BlockSpec((D, LANES), lambda i: (0, 0)),
        ],
        out_specs=[
            pl.BlockSpec((tm, D), lambda i: (i, 0)),
            pl.BlockSpec((tm, LANES), lambda i: (i, 0)),
            pl.BlockSpec((tm, LANES), lambda i: (i, 0)),
        ],
        out_shape=[jax.ShapeDtypeStruct((T, D), F32), jax.ShapeDtypeStruct((T, LANES), jnp.int32),
                   jax.ShapeDtypeStruct((T, LANES), F32)],
        compiler_params=_cparams(("parallel",)),
        name="moe_router",
    )(x, g.reshape(1, D), wr)


def _expert_kernel(te_ref, tv_ref, tok_ref, hn_hbm, wg_ref, wu_ref, wd_ref, o_ref, xg_ref, xb_ref, acc_ref, sem):
    i = pl.program_id(0)
    f = pl.program_id(1)
    nf = pl.num_programs(1)
    tm = xg_ref.shape[0]
    valid = tv_ref[i] > 0

    @pl.when(jnp.logical_and(valid, f == 0))
    def _():
        def issue(r, c):
            tok = tok_ref[i * tm + r]
            pltpu.make_async_copy(hn_hbm.at[pl.ds(tok, 1), :], xg_ref.at[pl.ds(r, 1), :], sem).start()
            return c

        lax.fori_loop(0, tm, issue, 0)

        def drain(r, c):
            pltpu.make_async_copy(hn_hbm.at[pl.ds(0, 1), :], xg_ref.at[pl.ds(r, 1), :], sem).wait()
            return c

        lax.fori_loop(0, tm, drain, 0)
        xb_ref[...] = xg_ref[...].astype(BF16)

    @pl.when(valid)
    def _():
        xb = xb_ref[...]
        a = jnp.dot(xb, wg_ref[0], preferred_element_type=F32)
        b = jnp.dot(xb, wu_ref[0], preferred_element_type=F32)
        h = (a * jax.nn.sigmoid(a) * b).astype(BF16)
        part = jnp.dot(h, wd_ref[0], preferred_element_type=F32)

        @pl.when(f == 0)
        def _():
            acc_ref[...] = part

        @pl.when(f > 0)
        def _():
            acc_ref[...] += part

    @pl.when(f == nf - 1)
    def _():
        o_ref[...] = jnp.where(valid, acc_ref[...], 0.0)


def moe_experts(hn, tile_expert, tile_valid, row_tok, wg, wu, wd, *, tm, tf):
    T, D = hn.shape
    F = wg.shape[2]
    P = row_tok.shape[0]
    nt = P // tm
    nf = F // tf

    def w_in_map(i, f, te, tv, tok):
        return (te[i], 0, jnp.where(tv[i] > 0, f, nf - 1))

    def w_out_map(i, f, te, tv, tok):
        return (te[i], jnp.where(tv[i] > 0, f, nf - 1), 0)

    grid_spec = pltpu.PrefetchScalarGridSpec(
        num_scalar_prefetch=3,
        grid=(nt, nf),
        in_specs=[
            pl.BlockSpec(memory_space=pl.ANY),
            pl.BlockSpec((1, D, tf), w_in_map),
            pl.BlockSpec((1, D, tf), w_in_map),
            pl.BlockSpec((1, tf, D), w_out_map),
        ],
        out_specs=pl.BlockSpec((tm, D), lambda i, f, te, tv, tok: (i, 0)),
        scratch_shapes=[pltpu.VMEM((tm, D), F32), pltpu.VMEM((tm, D), BF16), pltpu.VMEM((tm, D), F32),
                        pltpu.SemaphoreType.DMA(())],
    )
    return pl.pallas_call(
        _expert_kernel,
        grid_spec=grid_spec,
        out_shape=jax.ShapeDtypeStruct((P, D), F32),
        compiler_params=_cparams(("arbitrary", "arbitrary")),
        name="moe_experts",
    )(tile_expert, tile_valid, row_tok, hn, wg, wu, wd)


def _combine_kernel(dest_ref, x_ref, gate_ref, yb_hbm, o_ref, b0_ref, b1_ref, sem):
    i = pl.program_id(0)
    tc = x_ref.shape[0]

    def issue(r, c):
        t = (i * tc + r) * TOP_K
        pltpu.make_async_copy(yb_hbm.at[pl.ds(dest_ref[t], 1), :], b0_ref.at[pl.ds(r, 1), :], sem).start()
        pltpu.make_async_copy(yb_hbm.at[pl.ds(dest_ref[t + 1], 1), :], b1_ref.at[pl.ds(r, 1), :], sem).start()
        return c

    lax.fori_loop(0, tc, issue, 0)

    def drain(r, c):
        pltpu.make_async_copy(yb_hbm.at[pl.ds(0, 1), :], b0_ref.at[pl.ds(r, 1), :], sem).wait()
        pltpu.make_async_copy(yb_hbm.at[pl.ds(0, 1), :], b1_ref.at[pl.ds(r, 1), :], sem).wait()
        return c

    lax.fori_loop(0, tc, drain, 0)
    g = gate_ref[...]
    o_ref[...] = x_ref[...] + g[:, 0:1] * b0_ref[...] + g[:, 1:2] * b1_ref[...]


def moe_combine(x, gates, yb, dest, *, tc=256):
    T, D = x.shape
    grid_spec = pltpu.PrefetchScalarGridSpec(
        num_scalar_prefetch=1,
        grid=(T // tc,),
        in_specs=[
            pl.BlockSpec((tc, D), lambda i, d: (i, 0)),
            pl.BlockSpec((tc, LANES), lambda i, d: (i, 0)),
            pl.BlockSpec(memory_space=pl.ANY),
        ],
        out_specs=pl.BlockSpec((tc, D), lambda i, d: (i, 0)),
        scratch_shapes=[pltpu.VMEM((tc, D), F32), pltpu.VMEM((tc, D), F32), pltpu.SemaphoreType.DMA(())],
    )
    return pl.pallas_call(
        _combine_kernel,
        grid_spec=grid_spec,
        out_shape=jax.ShapeDtypeStruct((T, D), F32),
        compiler_params=_cparams(("arbitrary",)),
        name="moe_combine",
    )(dest, x, gates, yb)


def moe_dispatch_plan(idx, *, tm):
    T = idx.shape[0]
    A = T * TOP_K
    e_flat = idx[:, :TOP_K].reshape(A)
    onehot = (e_flat[:, None] == jnp.arange(N_EXPERTS, dtype=jnp.int32)[None, :]).astype(jnp.int32)
    rank = jnp.sum((jnp.cumsum(onehot, axis=0) - onehot) * onehot, axis=1)
    counts = jnp.sum(onehot, axis=0)
    padded = (counts + tm - 1) // tm * tm
    ends = jnp.cumsum(padded)
    pstart = ends - padded
    dest = pstart[e_flat] + rank
    P = (A // tm + N_EXPERTS) * tm
    nt = P // tm
    tok_flat = jnp.arange(A, dtype=jnp.int32) // TOP_K
    row_tok = jnp.zeros((P,), jnp.int32).at[dest].set(tok_flat)
    tile_start = jnp.arange(nt, dtype=jnp.int32) * tm
    tile_expert = jnp.minimum(jnp.searchsorted(ends, tile_start, side='right'), N_EXPERTS - 1).astype(jnp.int32)
    tile_valid = (tile_start < ends[-1]).astype(jnp.int32)
    last_valid = jnp.maximum(jnp.sum(tile_valid) - 1, 0)
    tile_expert = jnp.where(tile_valid > 0, tile_expert, tile_expert[last_valid])
    return dest.astype(jnp.int32), row_tok, tile_expert, tile_valid


def moe_layer(x, norm2, router, wg, wu, wd, *, tm=512, tf=896, tr=512, tc=256):
    hn, idx, gates = moe_router(x, norm2, router, tm=tr)
    dest, row_tok, tile_expert, tile_valid = moe_dispatch_plan(idx, tm=tm)
    yb = moe_experts(hn, tile_expert, tile_valid, row_tok, wg, wu, wd, tm=tm, tf=tf)
    return moe_combine(x, gates, yb, dest, tc=tc)


def _even_layer(x, norm1, w_in, shift_mu, w0_f, w_up_f, w0_b, w_up_b, a0_f, a_up_f, a0_b, a_up_b, g_up,
                k_k, k_a, r_k, gn_w, gn_b, q_norm, k_norm, w_out, norm2, ffn_gate, ffn_up, ffn_down):
    Bn, S, D = x.shape
    T = Bn * S
    xf = x.reshape(T, D)
    w_in = w_in.astype(BF16)
    p_rw = norm_matmul(xf, norm1, w_in[:, :RW_IN], tm=512, tn=RW_IN // 2).reshape(Bn, S, RW_IN)
    p_att = norm_matmul(xf, norm1, w_in[:, RW_IN:], tm=512, tn=ATT_IN).reshape(Bn, S, ATT_IN)
    y_a = rwkv7_mixer(p_rw, shift_mu, w0_f, w_up_f, w0_b, w_up_b, a0_f, a_up_f, a0_b, a_up_b,
                      g_up, k_k, k_a, r_k, gn_w, gn_b, out_dtype=BF16)
    y_b = gqa_attention(p_att, q_norm, k_norm, out_dtype=BF16)
    w_out = w_out.astype(BF16)
    xf = matmul_residual([y_a.reshape(T, RW_DIM), y_b.reshape(T, ATT_DIM)], [w_out[:RW_DIM], w_out[RW_DIM:]], xf,
                         tm=512, tn=512)
    return ffn_swiglu(xf, norm2, ffn_gate.astype(BF16), ffn_up.astype(BF16), ffn_down.astype(BF16), tm=512, tf=1408)


def _odd_layer(xf, Bn, S, norm1, w_in, gate_up_f, gate_bias_f, gate_up_b, gate_bias_b, out_norm, w_out,
               norm2, router, exp_gate, exp_up, exp_down):
    T, D = xf.shape
    o3 = 2 * GLA_DK + GLA_DV
    o4 = o3 + GLA_GATE_RANK
    w_main = jnp.concatenate([w_in[:, :o3], w_in[:, o4:]], axis=1).astype(BF16)
    w_gd = jnp.zeros((D, LANES), F32).at[:, :GLA_GATE_RANK].set(w_in[:, o3:o4])
    p_main = norm_matmul(xf, norm1, w_main, tm=512, tn=512).reshape(Bn, S, -1)
    gd = norm_matmul(xf, norm1, w_gd, tm=512, tn=LANES)[:, :GLA_GATE_RANK].reshape(Bn, S, GLA_GATE_RANK)
    o = gla_mixer(p_main, gd, gate_up_f, gate_bias_f, gate_up_b, gate_bias_b, out_norm, out_dtype=BF16)
    xf = matmul_residual([o.reshape(T, GLA_DV)], [w_out.astype(BF16)], xf, tm=512, tn=512)
    return moe_layer(xf, norm2, router, exp_gate.astype(BF16), exp_up.astype(BF16), exp_down.astype(BF16))


def kernel(x, e_norm1, e_w_in, e_shift_mu, e_w0_f, e_w_up_f, e_w0_b, e_w_up_b, e_a0_f, e_a_up_f, e_a0_b, e_a_up_b, e_g_up, e_k_k, e_k_a, e_r_k, e_gn_w, e_gn_b, e_q_norm, e_k_norm, e_w_out, e_norm2, e_ffn_gate, e_ffn_up, e_ffn_down, o_norm1, o_w_in, o_gate_up_f, o_gate_bias_f, o_gate_up_b, o_gate_bias_b, o_out_norm, o_w_out, o_norm2, o_router, o_exp_gate, o_exp_up, o_exp_down):
    Bn, S, D = x.shape
    xf = _even_layer(x, e_norm1[0], e_w_in[0], e_shift_mu[0], e_w0_f[0], e_w_up_f[0], e_w0_b[0],
                     e_w_up_b[0], e_a0_f[0], e_a_up_f[0], e_a0_b[0], e_a_up_b[0], e_g_up[0], e_k_k[0],
                     e_k_a[0], e_r_k[0], e_gn_w[0], e_gn_b[0], e_q_norm[0], e_k_norm[0], e_w_out[0],
                     e_norm2[0], e_ffn_gate[0], e_ffn_up[0], e_ffn_down[0])
    xf = _odd_layer(xf, Bn, S, o_norm1[0], o_w_in[0], o_gate_up_f[0], o_gate_bias_f[0], o_gate_up_b[0],
                    o_gate_bias_b[0], o_out_norm[0], o_w_out[0], o_norm2[0], o_router[0],
                    o_exp_gate[0], o_exp_up[0], o_exp_down[0])
    return xf.reshape(Bn, S, D)
```

```python
import functools

import jax
import jax.numpy as jnp
from jax import lax
from jax.experimental import pallas as pl
from jax.experimental.pallas import tpu as pltpu

F32 = jnp.float32
BF16 = jnp.bfloat16
HI = lax.Precision.HIGHEST

D_MODEL = 1024
GRID_W = 64
HEAD_DIM = 64
NORM_EPS = 1e-6
RW_HEADS = 8
RW_DIM = 512
DECAY_RANK = 64
ICLR_RANK = 64
GATE_RANK = 128
RWKV_GN_EPS = 64e-5
RW_IN = 3 * RW_DIM + DECAY_RANK + ICLR_RANK + GATE_RANK
RW_CHUNK = 64
ATT_HEADS = 8
ATT_KV_HEADS = 2
ATT_DIM = 512
ATT_KV_DIM = 128
ATT_IN = ATT_DIM + 2 * ATT_KV_DIM
ROPE_THETA = 10000.0
GLA_HEADS = 4
GLA_DK = 512
GLA_DV = 1024
GLA_DKH = 128
GLA_DVH = 256
GLA_GATE_RANK = 16
GLA_GATE_NORM = 16.0
GLA_CHUNK = 64
N_EXPERTS = 8
TOP_K = 2
LANES = 128

VMEM_LIMIT = 48 * 1024 * 1024


def _cparams(sem):
    return pltpu.CompilerParams(dimension_semantics=sem, vmem_limit_bytes=VMEM_LIMIT)


def _iota2(shape, dim):
    return lax.broadcasted_iota(jnp.int32, shape, dim)


def _split(x, parts):
    out = []
    for _ in range(parts):
        t = x.astype(BF16)
        out.append(t)
        x = x - t.astype(F32)
    return out


def _dot_sel(x, m, parts=2):
    mb = m.astype(BF16)
    acc = None
    for t in _split(x, parts):
        d = jnp.dot(t, mb, preferred_element_type=F32)
        acc = d if acc is None else acc + d
    return acc


def _sel_dot(m, x, parts=3):
    mb = m.astype(BF16)
    acc = None
    for t in _split(x, parts):
        d = jnp.dot(mb, t, preferred_element_type=F32)
        acc = d if acc is None else acc + d
    return acc


def _dot3(a, b):
    ah, al = _split(a, 2)
    bh, bl = _split(b, 2)
    d = lambda u, v: jnp.dot(u, v, preferred_element_type=F32)
    return d(ah, bh) + (d(ah, bl) + d(al, bh))


def _bdot(a, b):
    return jnp.dot(a.astype(BF16), b.astype(BF16), preferred_element_type=F32)


def _bdot_nt(a, b):
    return lax.dot_general(a.astype(BF16), b.astype(BF16), (((1,), (1,)), ((), ())), preferred_element_type=F32)


def _bdot_tn(a, b):
    return lax.dot_general(a.astype(BF16), b.astype(BF16), (((0,), (0,)), ((), ())), preferred_element_type=F32)


def _norm_mm_kernel(x_ref, g_ref, w_ref, o_ref, xn_ref):
    @pl.when(pl.program_id(1) == 0)
    def _():
        x = x_ref[...]
        ms = jnp.mean(x * x, axis=-1, keepdims=True)
        xn_ref[...] = (x * lax.rsqrt(ms + NORM_EPS) * g_ref[...]).astype(xn_ref.dtype)

    precision = HI if xn_ref.dtype == F32 else None
    o_ref[...] = jnp.dot(xn_ref[...], w_ref[...], preferred_element_type=F32, precision=precision).astype(o_ref.dtype)


def norm_matmul(x, g, w, *, tm, tn, out_dtype=F32):
    M, K = x.shape
    N = w.shape[1]
    return pl.pallas_call(
        _norm_mm_kernel,
        grid=(M // tm, N // tn),
        in_specs=[
            pl.BlockSpec((tm, K), lambda i, j: (i, 0)),
            pl.BlockSpec((1, K), lambda i, j: (0, 0)),
            pl.BlockSpec((K, tn), lambda i, j: (0, j)),
        ],
        out_specs=pl.BlockSpec((tm, tn), lambda i, j: (i, j)),
        out_shape=jax.ShapeDtypeStruct((M, N), out_dtype),
        scratch_shapes=[pltpu.VMEM((tm, K), w.dtype)],
        compiler_params=_cparams(("parallel", "arbitrary")),
        name="norm_matmul",
    )(x, g.reshape(1, K), w)


def _mm_res_kernel(*refs):
    n = (len(refs) - 2) // 2
    r_ref, o_ref = refs[2 * n], refs[2 * n + 1]
    acc = r_ref[...]
    for y_ref, w_ref in zip(refs[:n], refs[n:2 * n]):
        acc = acc + jnp.dot(y_ref[...].astype(BF16), w_ref[...], preferred_element_type=F32)
    o_ref[...] = acc


def matmul_residual(ys, ws, r, *, tm, tn):
    M, N = r.shape
    y_specs = [pl.BlockSpec((tm, y.shape[1]), lambda i, j: (i, 0)) for y in ys]
    w_specs = [pl.BlockSpec((w.shape[0], tn), lambda i, j: (0, j)) for w in ws]
    return pl.pallas_call(
        _mm_res_kernel,
        grid=(M // tm, N // tn),
        in_specs=y_specs + w_specs + [pl.BlockSpec((tm, tn), lambda i, j: (i, j))],
        out_specs=pl.BlockSpec((tm, tn), lambda i, j: (i, j)),
        out_shape=jax.ShapeDtypeStruct((M, N), F32),
        compiler_params=_cparams(("parallel", "arbitrary")),
        name="matmul_residual",
    )(*ys, *ws, r)


def _ffn_kernel(x_ref, g_ref, wg_ref, wu_ref, wd_ref, o_ref, xn_ref, acc_ref):
    f = pl.program_id(1)

    @pl.when(f == 0)
    def _():
        x = x_ref[...]
        ms = jnp.mean(x * x, axis=-1, keepdims=True)
        xn_ref[...] = (x * lax.rsqrt(ms + NORM_EPS) * g_ref[...]).astype(BF16)

    xn = xn_ref[...]
    a = jnp.dot(xn, wg_ref[...], preferred_element_type=F32)
    b = jnp.dot(xn, wu_ref[...], preferred_element_type=F32)
    h = (a * jax.nn.sigmoid(a) * b).astype(BF16)
    part = jnp.dot(h, wd_ref[...], preferred_element_type=F32)

    @pl.when(f == 0)
    def _():
        acc_ref[...] = part

    @pl.when(f > 0)
    def _():
        acc_ref[...] += part

    @pl.when(f == pl.num_programs(1) - 1)
    def _():
        o_ref[...] = x_ref[...] + acc_ref[...]


def ffn_swiglu(x, g, wg, wu, wd, *, tm, tf):
    M, D = x.shape
    F = wg.shape[1]
    return pl.pallas_call(
        _ffn_kernel,
        grid=(M // tm, F // tf),
        in_specs=[
            pl.BlockSpec((tm, D), lambda i, f: (i, 0)),
            pl.BlockSpec((1, D), lambda i, f: (0, 0)),
            pl.BlockSpec((D, tf), lambda i, f: (0, f)),
            pl.BlockSpec((D, tf), lambda i, f: (0, f)),
            pl.BlockSpec((tf, D), lambda i, f: (f, 0)),
        ],
        out_specs=pl.BlockSpec((tm, D), lambda i, f: (i, 0)),
        out_shape=jax.ShapeDtypeStruct((M, D), F32),
        scratch_shapes=[pltpu.VMEM((tm, D), BF16), pltpu.VMEM((tm, D), F32)],
        compiler_params=_cparams(("parallel", "arbitrary")),
        name="ffn_swiglu",
    )(x, g.reshape(1, D), wg, wu, wd)


def _rwkv_prep_kernel(cur_ref, prev_ref, next_ref, mu_ref, w0f_ref, wupf_ref, w0b_ref, wupb_ref,
                      a0f_ref, aupf_ref, a0b_ref, aupb_ref, gup_ref, kk_ref, hsum_ref,
                      r_ref, k_ref, v_ref, kkn_ref, g_ref, lwf_ref, lf_ref, asf_ref, lwb_ref, lb_ref, asb_ref):
    i = pl.program_id(1)
    nt = pl.num_programs(1)
    x = cur_ref[0]
    Tt = x.shape[0]
    row = _iota2(x.shape, 0)
    prev_row = jnp.where(i > 0, prev_ref[0, 7:8, :], 0.0)
    next_row = jnp.where(i < nt - 1, next_ref[0, 0:1, :], 0.0)
    prev = jnp.where(row == 0, prev_row, pltpu.roll(x, 1, 0))
    nxt = jnp.where(row == Tt - 1, next_row, pltpu.roll(x, Tt - 1, 0))
    pm = x + (0.5 * (prev + nxt) - x) * mu_ref[...]
    o1 = 3 * RW_DIM
    o2 = o1 + DECAY_RANK
    o3 = o2 + ICLR_RANK
    r_ref[0] = pm[:, :RW_DIM]
    k = pm[:, RW_DIM:2 * RW_DIM]
    k_ref[0] = k
    v_ref[0] = pm[:, 2 * RW_DIM:o1]
    wd = jnp.tanh(pm[:, o1:o2])
    ad = pm[:, o2:o3]
    gin = jax.nn.sigmoid(pm[:, o3:])
    g_ref[0] = jnp.dot(gin.astype(BF16), gup_ref[...].astype(BF16), preferred_element_type=F32)
    kk = k * kk_ref[...]
    ss = _dot_sel(kk * kk, hsum_ref[...])
    kkn_ref[0] = kk / jnp.maximum(jnp.sqrt(ss), 1e-12)

    ci = _iota2((RW_CHUNK, RW_CHUNK), 0)
    cj = _iota2((RW_CHUNK, RW_CHUNK), 1)
    tril = (cj <= ci).astype(F32)
    triu = (cj >= ci).astype(F32)

    def direction(w0_ref, wup_ref, a0_ref, aup_ref, tri, lw_ref, l_ref, as_ref):
        z = w0_ref[...] + _dot3(wd, wup_ref[...])
        sp = jnp.maximum(-z, 0.0) + jnp.log(1.0 + jnp.exp(-jnp.abs(z)))
        lw = -jnp.exp(-sp - 0.5)
        lw_ref[0] = lw
        as_ref[0] = jax.nn.sigmoid(a0_ref[...] + _dot3(ad, aup_ref[...]))
        for c in range(Tt // RW_CHUNK):
            sl = slice(c * RW_CHUNK, (c + 1) * RW_CHUNK)
            l_ref[0, sl, :] = _sel_dot(tri, lw[sl, :])

    direction(w0f_ref, wupf_ref, a0f_ref, aupf_ref, tril, lwf_ref, lf_ref, asf_ref)
    direction(w0b_ref, wupb_ref, a0b_ref, aupb_ref, triu, lwb_ref, lb_ref, asb_ref)


def rwkv_prep(p, shift_mu, w0_f, w_up_f, w0_b, w_up_b, a0_f, a_up_f, a0_b, a_up_b, g_up, k_k, *, tt=256):
    B, S, _ = p.shape
    nt = S // tt
    hsum = (jnp.arange(RW_DIM)[:, None] // HEAD_DIM == jnp.arange(RW_DIM)[None, :] // HEAD_DIM).astype(F32)
    row = lambda a: a.reshape(1, -1)
    full = lambda a: pl.BlockSpec(a.shape, lambda b, i: (0,) * a.ndim)
    params = [row(shift_mu), row(w0_f), w_up_f, row(w0_b), w_up_b, row(a0_f), a_up_f, row(a0_b), a_up_b, g_up,
              row(k_k), hsum]
    tb = tt // 8
    in_specs = [
        pl.BlockSpec((1, tt, RW_IN), lambda b, i: (b, i, 0)),
        pl.BlockSpec((1, 8, RW_IN), lambda b, i: (b, jnp.maximum(i * tb - 1, 0), 0)),
        pl.BlockSpec((1, 8, RW_IN), lambda b, i: (b, jnp.minimum((i + 1) * tb, S // 8 - 1), 0)),
    ] + [full(a) for a in params]
    out_spec = pl.BlockSpec((1, tt, RW_DIM), lambda b, i: (b, i, 0))
    out_sds = jax.ShapeDtypeStruct((B, S, RW_DIM), F32)
    return pl.pallas_call(
        _rwkv_prep_kernel,
        grid=(B, nt),
        in_specs=in_specs,
        out_specs=[out_spec] * 11,
        out_shape=[out_sds] * 11,
        compiler_params=_cparams(("parallel", "arbitrary")),
        name="rwkv_prep",
    )(p, p, p, *params)


def _chunk_terms(chains, out):
    C = chains[0]["r"].shape[0]
    n = len(chains)
    R = range(n)
    pre = []
    for ch in chains:
        r, k, v, kk, lw, L, asig, ka, rev = (ch[x] for x in ("r", "k", "v", "kk", "lw", "L", "asig", "ka", "rev"))
        a = -kk
        b = kk * asig
        kd = k * (1.0 + (asig - 1.0) * ka)
        Lp = L - lw
        Lr = Lp if rev else L
        Lend = L[0:1, :] if rev else L[C - 1:C, :]
        Lmid = L[C // 2:C // 2 + 1, :]
        einv = jnp.exp(Lmid - L)
        eend = jnp.exp(Lend - L)
        pre.append(dict(
            ar=jnp.concatenate([a * jnp.exp(Lp - Lmid), r * jnp.exp(Lr - Lmid)], axis=0).astype(BF16),
            bt=(b * einv).astype(BF16), kt=(kd * einv).astype(BF16),
            a0=(a * jnp.exp(Lp)).astype(BF16), r0=r * jnp.exp(Lr),
            bh=(b * eend).astype(BF16), kh=(kd * eend).astype(BF16), vb=v.astype(BF16),
            dec=jnp.exp(Lend)))
    mk = [ch["masks"] for ch in chains]
    dot = lambda u, v: jnp.dot(u, v, preferred_element_type=F32)
    Ab = [_bdot_nt(pre[i]["ar"], pre[i]["bt"]) for i in R]
    Ak = [_bdot_nt(pre[i]["ar"], pre[i]["kt"]) for i in R]
    yield
    Aab = [jnp.where(mk[i][0], Ab[i][:C], 0.0) for i in R]
    Arb = [jnp.where(mk[i][1], Ab[i][C:], 0.0).astype(BF16) for i in R]
    AakArk = [jnp.concatenate([jnp.where(mk[i][0], Ak[i][:C], 0.0), jnp.where(mk[i][1], Ak[i][C:], 0.0)],
                              axis=0).astype(BF16) for i in R]
    AV = [dot(AakArk[i], pre[i]["vb"]) for i in R]
    Xb = [jnp.where(mk[i][2], Aab[i], 0.0).astype(BF16) for i in R]
    X2b = [dot(Xb[i], Xb[i]).astype(BF16) for i in R]
    yield
    T = [mk[i][6] + Xb[i].astype(F32) for i in R]
    T = [T[i] + _bdot(T[i], X2b[i]) for i in R]
    X4b = [dot(X2b[i], X2b[i]).astype(BF16) for i in R]
    yield
    T = [T[i] + _bdot(T[i], X4b[i]) for i in R]
    yield
    for lvl in (3, 4, 5):
        Tb = [T[i].astype(BF16) for i in R]
        ET = [dot(jnp.where(mk[i][lvl], Aab[i], 0.0).astype(BF16), Tb[i]).astype(BF16) for i in R]
        yield
        T = [T[i] + dot(Tb[i], ET[i]) for i in R]
        yield
    Tb = [T[i].astype(BF16) for i in R]
    A0p = [dot(Tb[i], pre[i]["a0"]).astype(BF16) for i in R]
    Uv = [dot(Tb[i], AV[i][:C].astype(BF16)).astype(BF16) for i in R]
    yield
    Rpp = [pre[i]["r0"] + dot(Arb[i], A0p[i]) for i in R]
    Yv = [dot(Arb[i], Uv[i]) + AV[i][C:] for i in R]
    P = [_bdot_tn(pre[i]["bh"], A0p[i]) + mk[i][6] * pre[i]["dec"] for i in R]
    Q = [_bdot_tn(pre[i]["bh"], Uv[i]) + _bdot_tn(pre[i]["kh"], pre[i]["vb"]) for i in R]
    out.extend((Rpp[i], Yv[i], P[i], Q[i]) for i in R)


def _make_masks(C, rev):
    ii = _iota2((C, C), 0)
    jj = _iota2((C, C), 1)
    strict = (jj > ii) if rev else (jj < ii)
    rmask = strict if rev else (jj <= ii)
    blk = lambda s: (ii // s) == (jj // s)
    m8 = blk(8)
    e16 = blk(16) & jnp.logical_not(blk(8))
    e32 = blk(32) & jnp.logical_not(blk(16))
    e64 = jnp.logical_not(blk(32))
    eye = (ii == jj).astype(F32)
    return strict, rmask, m8, e16, e32, e64, eye


RW_UNROLL = 4


def _rwkv_scan_kernel(r_ref, k_ref, v_ref, kk_ref, g_ref, lwf_ref, lf_ref, asf_ref, lwb_ref, lb_ref, asb_ref,
                      ka_ref, rk_ref, gnw_ref, gnb_ref, havg_ref, o_ref,
                      yf_ref, yb_ref, h_ref, rpp_ref, yv_ref, p_ref, q_ref):
    S = r_ref.shape[1]
    C = RW_CHUNK
    N = HEAD_DIM
    n = S // C
    U = RW_UNROLL
    nset = n // U
    masks_f = _make_masks(C, False)
    masks_b = _make_masks(C, True)
    h_ref[...] = jnp.zeros_like(h_ref)
    dirs = ((False, lwf_ref, lf_ref, asf_ref, yf_ref, masks_f), (True, lwb_ref, lb_ref, asb_ref, yb_ref, masks_b))

    def chunk_rows(rev, st, u):
        c = st * U + u
        c = (n - 1 - c) if rev else c
        return pl.ds(pl.multiple_of(c * C, C), C)

    def state_step(st, u):
        slot = st % 2
        for di, (rev, lw_ref, l_ref, as_ref, y_ref, masks) in enumerate(dirs):
            ys = []
            for hh in range(2):
                s = di * 2 + hh
                Hb = h_ref[s].astype(BF16)
                ys.append(jnp.dot(rpp_ref[slot, s, u], Hb, preferred_element_type=F32) + yv_ref[slot, s, u])
                h_ref[s] = jnp.dot(p_ref[slot, s, u], Hb, preferred_element_type=F32) + q_ref[slot, s, u]
            y_ref[chunk_rows(rev, st, u), :] = jnp.concatenate(ys, axis=1)

    def chunk_set(st, state_of):
        chains, slots = [], []
        for u in range(U):
            for di, (rev, lw_ref, l_ref, as_ref, y_ref, masks) in enumerate(dirs):
                rows = chunk_rows(rev, st, u)
                for hh in range(2):
                    ls = slice(hh * N, (hh + 1) * N)
                    chains.append(dict(
                        r=r_ref[0, rows, ls], k=k_ref[0, rows, ls], v=v_ref[0, rows, ls], kk=kk_ref[0, rows, ls],
                        lw=lw_ref[0, rows, ls], L=l_ref[0, rows, ls], asig=as_ref[0, rows, ls], ka=ka_ref[:, ls],
                        rev=rev, masks=masks))
                    slots.append((di * 2 + hh, u))
        res = []
        pending = list(range(U)) if state_of is not None else []
        for stage, _ in enumerate(_chunk_terms(chains, res)):
            if pending and stage % 2 == 1:
                state_step(state_of, pending.pop(0))
        for u in pending:
            state_step(state_of, u)
        slot = st % 2
        for (s, u), (Rpp, Yv, P, Q) in zip(slots, res):
            rpp_ref[slot, s, u] = Rpp.astype(BF16)
            yv_ref[slot, s, u] = Yv
            p_ref[slot, s, u] = P.astype(BF16)
            q_ref[slot, s, u] = Q

    chunk_set(0, None)

    def body(st, carry):
        chunk_set(st, st - 1)
        return carry

    lax.fori_loop(1, nset, body, 0)
    for u in range(U):
        state_step(nset - 1, u)

    RT = 256

    def fin(i, carry):
        rows = pl.ds(pl.multiple_of(i * RT, RT), RT)
        y = yf_ref[rows, :] + yb_ref[rows, :]
        mean = _dot_sel(y, havg_ref[...])
        d = y - mean
        var = _dot_sel(d * d, havg_ref[...])
        yn = d * lax.rsqrt(var + RWKV_GN_EPS) * gnw_ref[...] + gnb_ref[...]
        r = r_ref[0, rows, :]
        kf = k_ref[0, rows, :] * (1.0 + (asf_ref[0, rows, :] - 1.0) * ka_ref[...])
        bonus = _dot_sel(r * kf * rk_ref[...], havg_ref[...]) * float(N) * v_ref[0, rows, :]
        o_ref[0, rows, :] = ((yn + bonus) * g_ref[0, rows, :]).astype(o_ref.dtype)
        return carry

    lax.fori_loop(0, S // RT, fin, 0)


def rwkv_scan(r, k, v, kk, g, lw_f, l_f, as_f, lw_b, l_b, as_b, k_a, r_k, gn_w, gn_b, out_dtype=F32):
    B, S, _ = r.shape
    W = 2 * HEAD_DIM
    N = HEAD_DIM
    havg = (jnp.arange(W)[:, None] // HEAD_DIM == jnp.arange(W)[None, :] // HEAD_DIM).astype(F32) / HEAD_DIM
    seq = pl.BlockSpec((1, S, W), lambda b, h: (b, 0, h))
    par = pl.BlockSpec((1, W), lambda b, h: (0, h))
    row = lambda a: a.reshape(1, -1)
    U = RW_UNROLL
    return pl.pallas_call(
        _rwkv_scan_kernel,
        grid=(B, RW_DIM // W),
        in_specs=[seq] * 11 + [par] * 4 + [pl.BlockSpec((W, W), lambda b, h: (0, 0))],
        out_specs=seq,
        out_shape=jax.ShapeDtypeStruct((B, S, RW_DIM), out_dtype),
        scratch_shapes=[pltpu.VMEM((S, W), F32), pltpu.VMEM((S, W), F32), pltpu.VMEM((4, N, N), F32),
                        pltpu.VMEM((2, 4, U, RW_CHUNK, N), BF16), pltpu.VMEM((2, 4, U, RW_CHUNK, N), F32),
                        pltpu.VMEM((2, 4, U, N, N), BF16), pltpu.VMEM((2, 4, U, N, N), F32)],
        compiler_params=_cparams(("parallel", "parallel")),
        name="rwkv_scan",
    )(r, k, v, kk, g, lw_f, l_f, as_f, lw_b, l_b, as_b, row(k_a), row(r_k), row(gn_w), row(gn_b), havg)


def rwkv7_mixer(p, shift_mu, w0_f, w_up_f, w0_b, w_up_b, a0_f, a_up_f, a0_b, a_up_b, g_up, k_k, k_a, r_k, gn_w,
                gn_b, out_dtype=F32, tt=256):
    outs = rwkv_prep(p, shift_mu, w0_f, w_up_f, w0_b, w_up_b, a0_f, a_up_f, a0_b, a_up_b, g_up, k_k, tt=tt)
    return rwkv_scan(*outs, k_a, r_k, gn_w, gn_b, out_dtype=out_dtype)


def _head_norm_rope(x, gain, cos, sin_signed, havg):
    ms = _dot_sel(x * x, havg)
    xn = x * lax.rsqrt(ms + NORM_EPS) * gain
    W = x.shape[1]
    even = (_iota2(x.shape, 1) % 2) == 0
    partner = jnp.where(even, pltpu.roll(xn, W - 1, 1), pltpu.roll(xn, 1, 1))
    return xn * cos + partner * sin_signed


def _attn_kernel(q_ref, k_ref, v_ref, cosq_ref, sinq_ref, cosk_ref, sink_ref, qg_ref, kg_ref, hq_ref, hk_ref,
                 o_ref, ks_ref, vs_ref):
    i = pl.program_id(1)
    G = ATT_HEADS // ATT_KV_HEADS
    D = HEAD_DIM

    @pl.when(i == 0)
    def _():
        kr = _head_norm_rope(k_ref[0], kg_ref[...], cosk_ref[...], sink_ref[...], hk_ref[...])
        for kv in range(ATT_KV_HEADS):
            ks_ref[kv] = kr[:, kv * D:(kv + 1) * D].astype(BF16)
            vs_ref[kv] = v_ref[0, :, kv * D:(kv + 1) * D].astype(BF16)

    q = _head_norm_rope(q_ref[0], qg_ref[...], cosq_ref[...], sinq_ref[...], hq_ref[...]) * (D ** -0.5)
    qb = q.astype(BF16)
    outs = []
    for h in range(ATT_HEADS):
        kv = h // G
        s = lax.dot_general(qb[:, h * D:(h + 1) * D], ks_ref[kv], (((1,), (1,)), ((), ())),
                            preferred_element_type=F32)
        m = jnp.max(s, axis=-1, keepdims=True)
        p = jnp.exp(s - m)
        l = jnp.sum(p, axis=-1, keepdims=True)
        o = jnp.dot(p.astype(BF16), vs_ref[kv], preferred_element_type=F32)
        outs.append(o / l)
    o_ref[0] = jnp.concatenate(outs, axis=1).astype(o_ref.dtype)


def _rope_tables(S):
    rows = S // GRID_W
    row = jnp.repeat(jnp.arange(rows), GRID_W).astype(F32)
    col = jnp.tile(jnp.arange(GRID_W), rows).astype(F32)
    half = HEAD_DIM // 2
    freq = ROPE_THETA ** (-jnp.arange(0, half, 2, dtype=F32) / half)
    ang = jnp.concatenate([row[:, None] * freq, col[:, None] * freq], axis=-1)
    cos = jnp.repeat(jnp.cos(ang), 2, axis=-1)
    sin = jnp.repeat(jnp.sin(ang), 2, axis=-1)
    sign = jnp.where(jnp.arange(HEAD_DIM) % 2 == 0, -1.0, 1.0).astype(F32)
    return cos, sin * sign


def gqa_attention(p, q_norm, k_norm, *, tq=256, out_dtype=F32):
    B, S, _ = p.shape
    cos, sin = _rope_tables(S)
    tile = lambda t, n: jnp.tile(t, (1, n))
    hq = (jnp.arange(ATT_DIM)[:, None] // HEAD_DIM == jnp.arange(ATT_DIM)[None, :] // HEAD_DIM).astype(F32) / HEAD_DIM
    hk = hq[:ATT_KV_DIM, :ATT_KV_DIM]
    qg = jnp.tile(q_norm, ATT_HEADS).reshape(1, ATT_DIM)
    kg = jnp.tile(k_norm, ATT_KV_HEADS).reshape(1, ATT_KV_DIM)
    nq = ATT_DIM // ATT_KV_DIM
    const = lambda a: pl.BlockSpec(a.shape, lambda b, i: (0, 0))
    return pl.pallas_call(
        _attn_kernel,
        grid=(B, S // tq),
        in_specs=[
            pl.BlockSpec((1, tq, ATT_DIM), lambda b, i: (b, i, 0)),
            pl.BlockSpec((1, S, ATT_KV_DIM), lambda b, i: (b, 0, nq)),
            pl.BlockSpec((1, S, ATT_KV_DIM), lambda b, i: (b, 0, nq + 1)),
            pl.BlockSpec((tq, ATT_DIM), lambda b, i: (i, 0)),
            pl.BlockSpec((tq, ATT_DIM), lambda b, i: (i, 0)),
            pl.BlockSpec((S, ATT_KV_DIM), lambda b, i: (0, 0)),
            pl.BlockSpec((S, ATT_KV_DIM), lambda b, i: (0, 0)),
            const(qg), const(kg), const(hq), const(hk),
        ],
        out_specs=pl.BlockSpec((1, tq, ATT_DIM), lambda b, i: (b, i, 0)),
        out_shape=jax.ShapeDtypeStruct((B, S, ATT_DIM), out_dtype),
        scratch_shapes=[pltpu.VMEM((ATT_KV_HEADS, S, HEAD_DIM), BF16), pltpu.VMEM((ATT_KV_HEADS, S, HEAD_DIM), BF16)],
        compiler_params=_cparams(("parallel", "arbitrary")),
        name="gqa_attention",
    )(p, p, p, tile(cos, ATT_HEADS), tile(sin, ATT_HEADS), tile(cos, ATT_KV_HEADS), tile(sin, ATT_KV_HEADS),
      qg, kg, hq, hk)


GLA_UNROLL = 4


def _gla_kernel(q_ref, k_ref, v_ref, gd_ref, og_ref, upf_ref, bf_ref, upb_ref, bb_ref, on_ref, o_ref,
                yf_ref, yb_ref, st_ref, kv_ref, qe_ref, dec_ref):
    S = q_ref.shape[1]
    C = GLA_CHUNK
    n = S // C
    U = GLA_UNROLL
    nset = n // U
    ii = _iota2((C, C), 0)
    jj = _iota2((C, C), 1)
    dirs = (
        (False, (jj <= ii).astype(F32), jj <= ii, upf_ref, bf_ref, yf_ref),
        (True, (jj >= ii).astype(F32), jj > ii, upb_ref, bb_ref, yb_ref),
    )
    scale = GLA_DKH ** -0.5
    st_ref[...] = jnp.zeros_like(st_ref)

    def chunk_rows(rev, st, u):
        c = st * U + u
        c = (n - 1 - c) if rev else c
        return pl.ds(pl.multiple_of(c * C, C), C)

    def state_step(st, u):
        slot = st % 2
        for di, (rev, tri, mask, up_ref, b_ref, y_ref) in enumerate(dirs):
            state = st_ref[di]
            rows = chunk_rows(rev, st, u)
            y_ref[rows, :] += lax.dot_general(qe_ref[slot, di, u], state.astype(BF16), (((1,), (1,)), ((), ())),
                                              preferred_element_type=F32)
            st_ref[di] = state * dec_ref[slot, di, u] + kv_ref[slot, di, u]

    def chunk_set(st, state_of):
        pending = list(range(U)) if state_of is not None else []
        items = []
        for u in range(U):
            for di, (rev, tri, mask, up_ref, b_ref, y_ref) in enumerate(dirs):
                rows = chunk_rows(rev, st, u)
                z = _dot3(gd_ref[0, rows, :], up_ref[...]) + b_ref[...]
                g = (jnp.minimum(z, 0.0) - jnp.log(1.0 + jnp.exp(-jnp.abs(z)))) * (1.0 / GLA_GATE_NORM)
                items.append(dict(u=u, di=di, rows=rows, rev=rev, mask=mask, y_ref=y_ref, g=g, tri=tri))
        for x in items:
            x["b"] = _sel_dot(x["tri"], x["g"])
        if pending:
            state_step(state_of, pending.pop(0))
        for x in items:
            b = x["b"]
            q = q_ref[0, x["rows"], :] * scale
            k = k_ref[0, x["rows"], :]
            b_mid = b[C // 2:C // 2 + 1, :]
            b_last = b[0:1, :] if x["rev"] else b[C - 1:C, :]
            x["vb"] = v_ref[0, x["rows"], :].astype(BF16)
            x["qm"] = (q * jnp.exp(b - b_mid)).astype(BF16)
            x["km"] = (k * jnp.exp(b_mid - b)).astype(BF16)
            x["ke"] = (k * jnp.exp(b_last - b)).astype(BF16)
            x["qe"] = (q * jnp.exp(b)).astype(BF16)
            x["dec"] = jnp.exp(b_last)
        att = [lax.dot_general(x["qm"], x["km"], (((1,), (1,)), ((), ())), preferred_element_type=F32) for x in items]
        if pending:
            state_step(state_of, pending.pop(0))
        att = [jnp.where(x["mask"], a, 0.0).astype(BF16) for x, a in zip(items, att)]
        slot = st % 2
        for idx, (x, a) in enumerate(zip(items, att)):
            x["y_ref"][x["rows"], :] = jnp.dot(a, x["vb"], preferred_element_type=F32)
            kv_ref[slot, x["di"], x["u"]] = lax.dot_general(x["vb"], x["ke"], (((0,), (0,)), ((), ())),
                                                            preferred_element_type=F32)
            qe_ref[slot, x["di"], x["u"]] = x["qe"]
            dec_ref[slot, x["di"], x["u"]] = x["dec"]
            if pending and idx % 3 == 2:
                state_step(state_of, pending.pop(0))
        for u in pending:
            state_step(state_of, u)

    chunk_set(0, None)

    def body(st, carry):
        chunk_set(st, st - 1)
        return carry

    lax.fori_loop(1, nset, body, 0)
    for u in range(U):
        state_step(nset - 1, u)

    RT = 256

    def fin(i, carry):
        rows = pl.ds(pl.multiple_of(i * RT, RT), RT)
        o = yf_ref[rows, :] + yb_ref[rows, :]
        ms = jnp.mean(o * o, axis=-1, keepdims=True)
        on = o * lax.rsqrt(ms + NORM_EPS) * on_ref[...]
        og = og_ref[0, rows, :]
        o_ref[0, rows, :] = (on * (og * jax.nn.sigmoid(og))).astype(o_ref.dtype)
        return carry

    lax.fori_loop(0, S // RT, fin, 0)


def gla_mixer(pm, gd, gate_up_f, gate_bias_f, gate_up_b, gate_bias_b, out_norm, out_dtype=F32):
    B, S, _ = pm.shape
    U = GLA_UNROLL
    H = GLA_HEADS
    kb = GLA_DK // GLA_DKH
    vb0 = 2 * GLA_DK // GLA_DVH
    ob0 = vb0 + GLA_DV // GLA_DVH
    return pl.pallas_call(
        _gla_kernel,
        grid=(B, H),
        in_specs=[
            pl.BlockSpec((1, S, GLA_DKH), lambda b, h: (b, 0, h)),
            pl.BlockSpec((1, S, GLA_DKH), lambda b, h: (b, 0, kb + h)),
            pl.BlockSpec((1, S, GLA_DVH), lambda b, h: (b, 0, vb0 + h)),
            pl.BlockSpec((1, S, GLA_GATE_RANK), lambda b, h: (b, 0, 0)),
            pl.BlockSpec((1, S, GLA_DVH), lambda b, h: (b, 0, ob0 + h)),
            pl.BlockSpec((GLA_GATE_RANK, GLA_DKH), lambda b, h: (0, h)),
            pl.BlockSpec((1, GLA_DKH), lambda b, h: (0, h)),
            pl.BlockSpec((GLA_GATE_RANK, GLA_DKH), lambda b, h: (0, h)),
            pl.BlockSpec((1, GLA_DKH), lambda b, h: (0, h)),
            pl.BlockSpec((1, GLA_DVH), lambda b, h: (0, 0)),
        ],
        out_specs=pl.BlockSpec((1, S, GLA_DVH), lambda b, h: (b, 0, h)),
        out_shape=jax.ShapeDtypeStruct((B, S, GLA_DV), out_dtype),
        scratch_shapes=[pltpu.VMEM((S, GLA_DVH), F32), pltpu.VMEM((S, GLA_DVH), F32),
                        pltpu.VMEM((2, GLA_DVH, GLA_DKH), F32),
                        pltpu.VMEM((2, 2, U, GLA_DVH, GLA_DKH), F32), pltpu.VMEM((2, 2, U, GLA_CHUNK, GLA_DKH), BF16),
                        pltpu.VMEM((2, 2, U, 1, GLA_DKH), F32)],
        compiler_params=_cparams(("parallel", "parallel")),
        name="gla_mixer",
    )(pm, pm, pm, gd, pm, gate_up_f, gate_bias_f.reshape(1, -1), gate_up_b, gate_bias_b.reshape(1, -1),
      out_norm.reshape(1, -1))


def _router_kernel(x_ref, g_ref, wr_ref, hn_ref, idx_ref, gate_ref):
    x = x_ref[...]
    ms = jnp.mean(x * x, axis=-1, keepdims=True)
    hn = x * lax.rsqrt(ms + NORM_EPS) * g_ref[...]
    hn_ref[...] = hn
    logits = _dot3(hn, wr_ref[...])
    lane = _iota2(logits.shape, 1)
    neg = jnp.float32(-jnp.inf)
    logits = jnp.where(lane < N_EXPERTS, logits, neg)
    m1 = jnp.max(logits, axis=-1, keepdims=True)
    i1 = jnp.min(jnp.where(logits == m1, lane, LANES), axis=-1, keepdims=True)
    rest = jnp.where(lane == i1, neg, logits)
    m2 = jnp.max(rest, axis=-1, keepdims=True)
    i2 = jnp.min(jnp.where(rest == m2, lane, LANES), axis=-1, keepdims=True)
    e2 = jnp.exp(m2 - m1)
    g1 = 1.0 / (1.0 + e2)
    g2 = e2 / (1.0 + e2)
    idx_ref[...] = jnp.where(lane == 0, i1, jnp.where(lane == 1, i2, 0))
    gate_ref[...] = jnp.where(lane == 0, g1, jnp.where(lane == 1, g2, 0.0))


def moe_router(x, g, router, *, tm=512):
    T, D = x.shape
    wr = jnp.zeros((D, LANES), F32).at[:, :N_EXPERTS].set(router)
    return pl.pallas_call(
        _router_kernel,
        grid=(T // tm,),
        in_specs=[
            pl.BlockSpec((tm, D), lambda i: (i, 0)),
            pl.BlockSpec((1, D), lambda i: (0, 0)),
            pl.BlockSpec((D, LANES), lambda i: (0, 0)),
        ],
        out_specs=[
            pl.BlockSpec((tm, D), lambda i: (i, 0)),
            pl.BlockSpec((tm, LANES), lambda i: (i, 0)),
            pl.BlockSpec((tm, LANES), lambda i: (i, 0)),
        ],
        out_shape=[jax.ShapeDtypeStruct((T, D), F32), jax.ShapeDtypeStruct((T, LANES), jnp.int32),
                   jax.ShapeDtypeStruct((T, LANES), F32)],
        compiler_params=_cparams(("parallel",)),
        name="moe_router",
    )(x, g.reshape(1, D), wr)


def _expert_kernel(te_ref, tv_ref, tok_ref, hn_hbm, wg_ref, wu_ref, wd_ref, o_ref, xg_ref, xb_ref, acc_ref, sem, *, nf):
    i = pl.program_id(0)
    f = pl.program_id(1)
    tm = xb_ref.shape[0]
    valid = tv_ref[i] > 0
    prev_valid = tv_ref[jnp.maximum(i - 1, 0)] > 0
    slot = i % 2

    def issue_row(tile, dst_slot, r):
        tok = tok_ref[tile * tm + r]
        pltpu.make_async_copy(hn_hbm.at[pl.ds(tok, 1), :], xg_ref.at[dst_slot, pl.ds(r, 1), :], sem.at[dst_slot]).start()

    @pl.when(jnp.logical_and(valid, jnp.logical_and(i == 0, f == 0)))
    def _():
        def body(r, c):
            issue_row(0, 0, r)
            return c

        lax.fori_loop(0, tm, body, 0, unroll=8)

    @pl.when(jnp.logical_and(f == 0, jnp.logical_or(valid, jnp.logical_and(i > 0, prev_valid))))
    def _():
        pltpu.make_async_copy(hn_hbm.at[pl.ds(0, tm), :], xg_ref.at[slot], sem.at[slot]).wait()

    @pl.when(jnp.logical_and(valid, f == 0))
    def _():
        xb_ref[...] = xg_ref[slot].astype(BF16)

    @pl.when(valid)
    def _():
        per = tm // nf
        row0 = f * per
        xb = xb_ref[...]
        a = jnp.dot(xb, wg_ref[0], preferred_element_type=F32)
        for j in range(per // 2):
            issue_row(i + 1, 1 - slot, row0 + j)
        b = jnp.dot(xb, wu_ref[0], preferred_element_type=F32)
        for j in range(per // 2, per):
            issue_row(i + 1, 1 - slot, row0 + j)
        h = (a * jax.nn.sigmoid(a) * b).astype(BF16)
        part = jnp.dot(h, wd_ref[0], preferred_element_type=F32)

        @pl.when(f == 0)
        def _():
            acc_ref[...] = part

        @pl.when(f > 0)
        def _():
            acc_ref[...] += part

    @pl.when(f == nf - 1)
    def _():
        o_ref[...] = jnp.where(valid, acc_ref[...], 0.0)


def moe_experts(hn, tile_expert, tile_valid, row_tok, wg, wu, wd, *, tm, tf):
    T, D = hn.shape
    F = wg.shape[2]
    P = row_tok.shape[0] - tm
    nt = P // tm + 1
    nf = F // tf

    def w_in_map(i, f, te, tv, tok):
        return (te[i], 0, jnp.where(tv[i] > 0, f, nf - 1))

    def w_out_map(i, f, te, tv, tok):
        return (te[i], jnp.where(tv[i] > 0, f, nf - 1), 0)

    grid_spec = pltpu.PrefetchScalarGridSpec(
        num_scalar_prefetch=3,
        grid=(nt, nf),
        in_specs=[
            pl.BlockSpec(memory_space=pl.ANY),
            pl.BlockSpec((1, D, tf), w_in_map),
            pl.BlockSpec((1, D, tf), w_in_map),
            pl.BlockSpec((1, tf, D), w_out_map),
        ],
        out_specs=pl.BlockSpec((tm, D), lambda i, f, te, tv, tok: (i, 0)),
        scratch_shapes=[pltpu.VMEM((2, tm, D), F32), pltpu.VMEM((tm, D), BF16), pltpu.VMEM((tm, D), F32),
                        pltpu.SemaphoreType.DMA((2,))],
    )
    return pl.pallas_call(
        functools.partial(_expert_kernel, nf=nf),
        grid_spec=grid_spec,
        out_shape=jax.ShapeDtypeStruct((P + tm, D), F32),
        compiler_params=_cparams(("arbitrary", "arbitrary")),
        name="moe_experts",
    )(tile_expert, tile_valid, row_tok, hn, wg, wu, wd)


def _combine_kernel(dest_ref, x_ref, gate_ref, yb_hbm, o_ref, buf_ref, sem):
    i = pl.program_id(0)
    n = pl.num_programs(0)
    tc = x_ref.shape[0]
    slot = i % 2

    def issue(step, dst_slot):
        base = step * tc * TOP_K

        def body(r, c):
            t = base + r * TOP_K
            for k in range(TOP_K):
                pltpu.make_async_copy(yb_hbm.at[pl.ds(dest_ref[t + k], 1), :], buf_ref.at[dst_slot, k, pl.ds(r, 1), :],
                                      sem.at[dst_slot]).start()
            return c

        lax.fori_loop(0, tc, body, 0, unroll=4)

    @pl.when(i == 0)
    def _():
        issue(0, 0)

    for k in range(TOP_K):
        pltpu.make_async_copy(yb_hbm.at[pl.ds(0, tc), :], buf_ref.at[slot, k], sem.at[slot]).wait()

    @pl.when(i + 1 < n)
    def _():
        issue(i + 1, 1 - slot)

    g = gate_ref[...]
    o_ref[...] = x_ref[...] + g[:, 0:1] * buf_ref[slot, 0] + g[:, 1:2] * buf_ref[slot, 1]


def moe_combine(x, gates, yb, dest, *, tc=256):
    T, D = x.shape
    grid_spec = pltpu.PrefetchScalarGridSpec(
        num_scalar_prefetch=1,
        grid=(T // tc,),
        in_specs=[
            pl.BlockSpec((tc, D), lambda i, d: (i, 0)),
            pl.BlockSpec((tc, LANES), lambda i, d: (i, 0)),
            pl.BlockSpec(memory_space=pl.ANY),
        ],
        out_specs=pl.BlockSpec((tc, D), lambda i, d: (i, 0)),
        scratch_shapes=[pltpu.VMEM((2, TOP_K, tc, D), F32), pltpu.SemaphoreType.DMA((2,))],
    )
    return pl.pallas_call(
        _combine_kernel,
        grid_spec=grid_spec,
        out_shape=jax.ShapeDtypeStruct((T, D), F32),
        compiler_params=_cparams(("arbitrary",)),
        name="moe_combine",
    )(dest, x, gates, yb)


def moe_dispatch_plan(idx, *, tm):
    T = idx.shape[0]
    A = T * TOP_K
    e_flat = idx[:, :TOP_K].reshape(A)
    onehot = (e_flat[:, None] == jnp.arange(N_EXPERTS, dtype=jnp.int32)[None, :]).astype(jnp.int32)
    rank = jnp.sum((jnp.cumsum(onehot, axis=0) - onehot) * onehot, axis=1)
    counts = jnp.sum(onehot, axis=0)
    padded = (counts + tm - 1) // tm * tm
    ends = jnp.cumsum(padded)
    pstart = ends - padded
    dest = pstart[e_flat] + rank
    P = (A // tm + N_EXPERTS + 1) * tm
    nt = P // tm
    tok_flat = jnp.arange(A, dtype=jnp.int32) // TOP_K
    row_tok = jnp.zeros((P,), jnp.int32).at[dest].set(tok_flat)
    tile_start = jnp.arange(nt, dtype=jnp.int32) * tm
    tile_expert = jnp.minimum(jnp.sum((tile_start[:, None] >= ends[None, :]).astype(jnp.int32), axis=1), N_EXPERTS - 1)
    tile_valid = (tile_start < ends[-1]).astype(jnp.int32)
    last_valid = jnp.maximum(jnp.sum(tile_valid) - 1, 0)
    tile_expert = jnp.where(tile_valid > 0, tile_expert, tile_expert[last_valid])
    return dest.astype(jnp.int32), row_tok, tile_expert, tile_valid


def moe_layer(x, norm2, router, wg, wu, wd, *, tm=512, tf=1792, tr=512, tc=256):
    hn, idx, gates = moe_router(x, norm2, router, tm=tr)
    dest, row_tok, tile_expert, tile_valid = moe_dispatch_plan(idx, tm=tm)
    yb = moe_experts(hn, tile_expert, tile_valid, row_tok, wg, wu, wd, tm=tm, tf=tf)
    return moe_combine(x, gates, yb, dest, tc=tc)


def _even_layer(x, norm1, w_in, shift_mu, w0_f, w_up_f, w0_b, w_up_b, a0_f, a_up_f, a0_b, a_up_b, g_up,
                k_k, k_a, r_k, gn_w, gn_b, q_norm, k_norm, w_out, norm2, ffn_gate, ffn_up, ffn_down):
    Bn, S, D = x.shape
    T = Bn * S
    xf = x.reshape(T, D)
    w_in = w_in.astype(BF16)
    p_rw = norm_matmul(xf, norm1, w_in[:, :RW_IN], tm=1024, tn=RW_IN // 2).reshape(Bn, S, RW_IN)
    p_att = norm_matmul(xf, norm1, w_in[:, RW_IN:], tm=1024, tn=ATT_IN).reshape(Bn, S, ATT_IN)
    y_a = rwkv7_mixer(p_rw, shift_mu, w0_f, w_up_f, w0_b, w_up_b, a0_f, a_up_f, a0_b, a_up_b,
                      g_up, k_k, k_a, r_k, gn_w, gn_b, out_dtype=BF16)
    y_b = gqa_attention(p_att, q_norm, k_norm, out_dtype=BF16)
    w_out = w_out.astype(BF16)
    xf = matmul_residual([y_a.reshape(T, RW_DIM), y_b.reshape(T, ATT_DIM)], [w_out[:RW_DIM], w_out[RW_DIM:]], xf,
                         tm=1024, tn=1024)
    return ffn_swiglu(xf, norm2, ffn_gate.astype(BF16), ffn_up.astype(BF16), ffn_down.astype(BF16), tm=512, tf=1408)


def _odd_layer(xf, Bn, S, norm1, w_in, gate_up_f, gate_bias_f, gate_up_b, gate_bias_b, out_norm, w_out,
               norm2, router, exp_gate, exp_up, exp_down):
    T, D = xf.shape
    o3 = 2 * GLA_DK + GLA_DV
    o4 = o3 + GLA_GATE_RANK
    w_main = jnp.concatenate([w_in[:, :o3], w_in[:, o4:]], axis=1).astype(BF16)
    w_gd = jnp.zeros((D, LANES), F32).at[:, :GLA_GATE_RANK].set(w_in[:, o3:o4])
    p_main = norm_matmul(xf, norm1, w_main, tm=1024, tn=1024).reshape(Bn, S, -1)
    gd = norm_matmul(xf, norm1, w_gd, tm=1024, tn=LANES)[:, :GLA_GATE_RANK].reshape(Bn, S, GLA_GATE_RANK)
    o = gla_mixer(p_main, gd, gate_up_f, gate_bias_f, gate_up_b, gate_bias_b, out_norm, out_dtype=BF16)
    xf = matmul_residual([o.reshape(T, GLA_DV)], [w_out.astype(BF16)], xf, tm=1024, tn=1024)
    return moe_layer(xf, norm2, router, exp_gate.astype(BF16), exp_up.astype(BF16), exp_down.astype(BF16))


def kernel(x, e_norm1, e_w_in, e_shift_mu, e_w0_f, e_w_up_f, e_w0_b, e_w_up_b, e_a0_f, e_a_up_f, e_a0_b, e_a_up_b, e_g_up, e_k_k, e_k_a, e_r_k, e_gn_w, e_gn_b, e_q_norm, e_k_norm, e_w_out, e_norm2, e_ffn_gate, e_ffn_up, e_ffn_down, o_norm1, o_w_in, o_gate_up_f, o_gate_bias_f, o_gate_up_b, o_gate_bias_b, o_out_norm, o_w_out, o_norm2, o_router, o_exp_gate, o_exp_up, o_exp_down):
    Bn, S, D = x.shape
    xf = _even_layer(x, e_norm1[0], e_w_in[0], e_shift_mu[0], e_w0_f[0], e_w_up_f[0], e_w0_b[0],
                     e_w_up_b[0], e_a0_f[0], e_a_up_f[0], e_a0_b[0], e_a_up_b[0], e_g_up[0], e_k_k[0],
                     e_k_a[0], e_r_k[0], e_gn_w[0], e_gn_b[0], e_q_norm[0], e_k_norm[0], e_w_out[0],
                     e_norm2[0], e_ffn_gate[0], e_ffn_up[0], e_ffn_down[0])
    xf = _odd_layer(xf, Bn, S, o_norm1[0], o_w_in[0], o_gate_up_f[0], o_gate_bias_f[0], o_gate_up_b[0],
                    o_gate_bias_b[0], o_out_norm[0], o_w_out[0], o_norm2[0], o_router[0],
                    o_exp_gate[0], o_exp_up[0], o_exp_down[0])
    return xf.reshape(Bn, S, D)
```

```python
import functools

import jax
import jax.numpy as jnp
from jax import lax
from jax.experimental import pallas as pl
from jax.experimental.pallas import tpu as pltpu

F32 = jnp.float32
BF16 = jnp.bfloat16
HI = lax.Precision.HIGHEST

D_MODEL = 1024
GRID_W = 64
HEAD_DIM = 64
NORM_EPS = 1e-6
RW_HEADS = 8
RW_DIM = 512
DECAY_RANK = 64
ICLR_RANK = 64
GATE_RANK = 128
RWKV_GN_EPS = 64e-5
RW_IN = 3 * RW_DIM + DECAY_RANK + ICLR_RANK + GATE_RANK
RW_CHUNK = 64
ATT_HEADS = 8
ATT_KV_HEADS = 2
ATT_DIM = 512
ATT_KV_DIM = 128
ATT_IN = ATT_DIM + 2 * ATT_KV_DIM
ROPE_THETA = 10000.0
ATT_VT_ROWS = HEAD_DIM + 16
LOG2_E = 1.4426950408889634
GLA_HEADS = 4
GLA_DK = 512
GLA_DV = 1024
GLA_DKH = 128
GLA_DVH = 256
GLA_GATE_RANK = 16
GLA_GATE_NORM = 16.0
GLA_CHUNK = 64
N_EXPERTS = 8
TOP_K = 2
LANES = 128

VMEM_LIMIT = 48 * 1024 * 1024
VMEM_LIMIT_BIG = 56 * 1024 * 1024


def _cparams(sem, vmem=VMEM_LIMIT):
    return pltpu.CompilerParams(dimension_semantics=sem, vmem_limit_bytes=vmem)


def _iota2(shape, dim):
    return lax.broadcasted_iota(jnp.int32, shape, dim)


def _split(x, parts):
    out = []
    for _ in range(parts):
        t = x.astype(BF16)
        out.append(t)
        x = x - t.astype(F32)
    return out


def _dot_sel(x, m, parts=2):
    mb = m.astype(BF16)
    acc = None
    for t in _split(x, parts):
        d = jnp.dot(t, mb, preferred_element_type=F32)
        acc = d if acc is None else acc + d
    return acc


def _sel_dot(m, x, parts=3):
    mb = m.astype(BF16)
    acc = None
    for t in _split(x, parts):
        d = jnp.dot(mb, t, preferred_element_type=F32)
        acc = d if acc is None else acc + d
    return acc


def _dot3(a, b):
    ah, al = _split(a, 2)
    bh, bl = _split(b, 2)
    d = lambda u, v: jnp.dot(u, v, preferred_element_type=F32)
    return d(ah, bh) + (d(ah, bl) + d(al, bh))


def _bdot(a, b):
    return jnp.dot(a.astype(BF16), b.astype(BF16), preferred_element_type=F32)


def _bdot_nt(a, b):
    return lax.dot_general(a.astype(BF16), b.astype(BF16), (((1,), (1,)), ((), ())), preferred_element_type=F32)


def _bdot_tn(a, b):
    return lax.dot_general(a.astype(BF16), b.astype(BF16), (((0,), (0,)), ((), ())), preferred_element_type=F32)


def _norm_mm_kernel(x_ref, g_ref, w_ref, o_ref, xn_ref):
    @pl.when(pl.program_id(1) == 0)
    def _():
        x = x_ref[...]
        ms = jnp.mean(x * x, axis=-1, keepdims=True)
        xn_ref[...] = (x * lax.rsqrt(ms + NORM_EPS) * g_ref[...]).astype(xn_ref.dtype)

    if xn_ref.dtype == F32:
        o_ref[...] = _dot3(xn_ref[...], w_ref[...]).astype(o_ref.dtype)
    else:
        o_ref[...] = jnp.dot(xn_ref[...], w_ref[...], preferred_element_type=F32).astype(o_ref.dtype)


def norm_matmul(x, g, w, *, tm, tn, out_dtype=F32):
    M, K = x.shape
    N = w.shape[1]
    return pl.pallas_call(
        _norm_mm_kernel,
        grid=(M // tm, N // tn),
        in_specs=[
            pl.BlockSpec((tm, K), lambda i, j: (i, 0)),
            pl.BlockSpec((1, K), lambda i, j: (0, 0)),
            pl.BlockSpec((K, tn), lambda i, j: (0, j)),
        ],
        out_specs=pl.BlockSpec((tm, tn), lambda i, j: (i, j)),
        out_shape=jax.ShapeDtypeStruct((M, N), out_dtype),
        scratch_shapes=[pltpu.VMEM((tm, K), w.dtype)],
        compiler_params=_cparams(("parallel", "arbitrary")),
        name="norm_matmul",
    )(x, g.reshape(1, K), w)


def _mm_res_kernel(*refs):
    n = (len(refs) - 2) // 2
    r_ref, o_ref = refs[2 * n], refs[2 * n + 1]
    acc = r_ref[...]
    for y_ref, w_ref in zip(refs[:n], refs[n:2 * n]):
        acc = acc + jnp.dot(y_ref[...].astype(BF16), w_ref[...], preferred_element_type=F32)
    o_ref[...] = acc


def matmul_residual(ys, ws, r, *, tm, tn):
    M, N = r.shape
    y_specs = [pl.BlockSpec((tm, y.shape[1]), lambda i, j: (i, 0)) for y in ys]
    w_specs = [pl.BlockSpec((w.shape[0], tn), lambda i, j: (0, j)) for w in ws]
    return pl.pallas_call(
        _mm_res_kernel,
        grid=(M // tm, N // tn),
        in_specs=y_specs + w_specs + [pl.BlockSpec((tm, tn), lambda i, j: (i, j))],
        out_specs=pl.BlockSpec((tm, tn), lambda i, j: (i, j)),
        out_shape=jax.ShapeDtypeStruct((M, N), F32),
        compiler_params=_cparams(("parallel", "arbitrary")),
        name="matmul_residual",
    )(*ys, *ws, r)


def _ffn_kernel(*refs, n_cast):
    x_ref, g_ref, wg_ref, wu_ref, wd_ref = refs[:5]
    cast_in = refs[5:5 + n_cast]
    o_ref = refs[5 + n_cast]
    cast_out = refs[6 + n_cast:6 + 2 * n_cast]
    xn_ref, acc_ref = refs[6 + 2 * n_cast:]
    f = pl.program_id(1)

    @pl.when(f == 0)
    def _():
        x = x_ref[...]
        ms = jnp.mean(x * x, axis=-1, keepdims=True)
        xn_ref[...] = (x * lax.rsqrt(ms + NORM_EPS) * g_ref[...]).astype(BF16)

    for src, dst in zip(cast_in, cast_out):
        dst[...] = src[...].astype(BF16)

    xn = xn_ref[...]
    a = jnp.dot(xn, wg_ref[...], preferred_element_type=F32)
    b = jnp.dot(xn, wu_ref[...], preferred_element_type=F32)
    h = (a * jax.nn.sigmoid(a) * b).astype(BF16)
    part = jnp.dot(h, wd_ref[...], preferred_element_type=F32)

    @pl.when(f == 0)
    def _():
        acc_ref[...] = part

    @pl.when(f > 0)
    def _():
        acc_ref[...] += part

    @pl.when(f == pl.num_programs(1) - 1)
    def _():
        o_ref[...] = x_ref[...] + acc_ref[...]


def ffn_swiglu(x, g, wg, wu, wd, *, tm, tf, cast=()):
    M, D = x.shape
    F = wg.shape[1]
    nf = F // tf
    steps = (M // tm) * nf
    cast_specs, cast_shapes = [], []
    for w in cast:
        E, R, C = w.shape
        per = steps // E
        cast_specs.append(pl.BlockSpec((1, R // per, C), lambda i, f, per=per: ((i * nf + f) // per, (i * nf + f) % per, 0)))
        cast_shapes.append(jax.ShapeDtypeStruct(w.shape, BF16))
    outs = pl.pallas_call(
        functools.partial(_ffn_kernel, n_cast=len(cast)),
        grid=(M // tm, nf),
        in_specs=[
            pl.BlockSpec((tm, D), lambda i, f: (i, 0)),
            pl.BlockSpec((1, D), lambda i, f: (0, 0)),
            pl.BlockSpec((D, tf), lambda i, f: (0, f)),
            pl.BlockSpec((D, tf), lambda i, f: (0, f)),
            pl.BlockSpec((tf, D), lambda i, f: (f, 0)),
        ] + cast_specs,
        out_specs=[pl.BlockSpec((tm, D), lambda i, f: (i, 0))] + cast_specs,
        out_shape=[jax.ShapeDtypeStruct((M, D), F32)] + cast_shapes,
        scratch_shapes=[pltpu.VMEM((tm, D), BF16), pltpu.VMEM((tm, D), F32)],
        compiler_params=_cparams(("arbitrary", "arbitrary"), vmem=VMEM_LIMIT_BIG if cast else VMEM_LIMIT),
        name="ffn_swiglu",
    )(x, g.reshape(1, D), wg, wu, wd, *cast)
    return outs[0], tuple(outs[1:])


def _rwkv_prep_kernel(cur_ref, prev_ref, next_ref, mu_ref, w0f_ref, wupf_ref, w0b_ref, wupb_ref,
                      a0f_ref, aupf_ref, a0b_ref, aupb_ref, gup_ref, kk_ref, hsum_ref,
                      r_ref, k_ref, v_ref, kkn_ref, g_ref, lwf_ref, lf_ref, asf_ref, lwb_ref, lb_ref, asb_ref):
    i = pl.program_id(1)
    nt = pl.num_programs(1)
    x = cur_ref[0]
    Tt = x.shape[0]
    row = _iota2(x.shape, 0)
    prev_row = jnp.where(i > 0, prev_ref[0, 7:8, :], 0.0)
    next_row = jnp.where(i < nt - 1, next_ref[0, 0:1, :], 0.0)
    prev = jnp.where(row == 0, prev_row, pltpu.roll(x, 1, 0))
    nxt = jnp.where(row == Tt - 1, next_row, pltpu.roll(x, Tt - 1, 0))
    pm = x + (0.5 * (prev + nxt) - x) * mu_ref[...]
    o1 = 3 * RW_DIM
    o2 = o1 + DECAY_RANK
    o3 = o2 + ICLR_RANK
    r_ref[0] = pm[:, :RW_DIM]
    k = pm[:, RW_DIM:2 * RW_DIM]
    k_ref[0] = k
    v_ref[0] = pm[:, 2 * RW_DIM:o1]
    wd = jnp.tanh(pm[:, o1:o2])
    ad = pm[:, o2:o3]
    gin = jax.nn.sigmoid(pm[:, o3:])
    g_ref[0] = jnp.dot(gin.astype(BF16), gup_ref[...].astype(BF16), preferred_element_type=F32)
    kk = k * kk_ref[...]
    ss = _dot_sel(kk * kk, hsum_ref[...])
    kkn_ref[0] = kk / jnp.maximum(jnp.sqrt(ss), 1e-12)

    ci = _iota2((RW_CHUNK, RW_CHUNK), 0)
    cj = _iota2((RW_CHUNK, RW_CHUNK), 1)
    tril = (cj <= ci).astype(F32)
    triu = (cj >= ci).astype(F32)

    def direction(w0_ref, wup_ref, a0_ref, aup_ref, tri, lw_ref, l_ref, as_ref):
        z = w0_ref[...] + _dot3(wd, wup_ref[...])
        sp = jnp.maximum(-z, 0.0) + jnp.log(1.0 + jnp.exp(-jnp.abs(z)))
        lw = -jnp.exp(-sp - 0.5)
        lw_ref[0] = lw
        as_ref[0] = jax.nn.sigmoid(a0_ref[...] + _dot3(ad, aup_ref[...]))
        for c in range(Tt // RW_CHUNK):
            sl = slice(c * RW_CHUNK, (c + 1) * RW_CHUNK)
            l_ref[0, sl, :] = _sel_dot(tri, lw[sl, :])

    direction(w0f_ref, wupf_ref, a0f_ref, aupf_ref, tril, lwf_ref, lf_ref, asf_ref)
    direction(w0b_ref, wupb_ref, a0b_ref, aupb_ref, triu, lwb_ref, lb_ref, asb_ref)


def rwkv_prep(p, shift_mu, w0_f, w_up_f, w0_b, w_up_b, a0_f, a_up_f, a0_b, a_up_b, g_up, k_k, *, tt=256):
    B, S, _ = p.shape
    nt = S // tt
    hsum = (jnp.arange(RW_DIM)[:, None] // HEAD_DIM == jnp.arange(RW_DIM)[None, :] // HEAD_DIM).astype(F32)
    row = lambda a: a.reshape(1, -1)
    full = lambda a: pl.BlockSpec(a.shape, lambda b, i: (0,) * a.ndim)
    params = [row(shift_mu), row(w0_f), w_up_f, row(w0_b), w_up_b, row(a0_f), a_up_f, row(a0_b), a_up_b, g_up,
              row(k_k), hsum]
    tb = tt // 8
    in_specs = [
        pl.BlockSpec((1, tt, RW_IN), lambda b, i: (b, i, 0)),
        pl.BlockSpec((1, 8, RW_IN), lambda b, i: (b, jnp.maximum(i * tb - 1, 0), 0)),
        pl.BlockSpec((1, 8, RW_IN), lambda b, i: (b, jnp.minimum((i + 1) * tb, S // 8 - 1), 0)),
    ] + [full(a) for a in params]
    out_spec = pl.BlockSpec((1, tt, RW_DIM), lambda b, i: (b, i, 0))
    out_sds = jax.ShapeDtypeStruct((B, S, RW_DIM), F32)
    return pl.pallas_call(
        _rwkv_prep_kernel,
        grid=(B, nt),
        in_specs=in_specs,
        out_specs=[out_spec] * 11,
        out_shape=[out_sds] * 11,
        compiler_params=_cparams(("parallel", "arbitrary")),
        name="rwkv_prep",
    )(p, p, p, *params)


def _bd(y, bd_mask):
    return jnp.where(bd_mask, jnp.concatenate([y, y], axis=0), jnp.zeros((), y.dtype))


def _chunk_terms(chains, out, bd_mask):
    C = chains[0]["r"].shape[0]
    n = len(chains)
    R = range(n)
    dot = lambda u, v: jnp.dot(u, v, preferred_element_type=F32)
    bd = lambda y: _bd(y, bd_mask)
    pre = []
    for ch in chains:
        r, k, v, kk, lw, L, asig, ka, rev = (ch[x] for x in ("r", "k", "v", "kk", "lw", "L", "asig", "ka", "rev"))
        a = -kk
        b = kk * asig
        kd = k * (1.0 + (asig - 1.0) * ka)
        Lp = L - lw
        Lr = Lp if rev else L
        Lend = L[0:1, :] if rev else L[C - 1:C, :]
        Lmid = L[C // 2:C // 2 + 1, :]
        einv = jnp.exp(Lmid - L)
        eend = jnp.exp(Lend - L)
        pre.append(dict(
            ar=jnp.concatenate([a * jnp.exp(Lp - Lmid), r * jnp.exp(Lr - Lmid)], axis=0).astype(BF16),
            bt=bd((b * einv).astype(BF16)), kt=bd((kd * einv).astype(BF16)),
            a0=bd((a * jnp.exp(Lp)).astype(BF16)), r0=r * jnp.exp(Lr),
            bh=(b * eend).astype(BF16), kh=(kd * eend).astype(BF16), vb=v.astype(BF16),
            dec=jnp.exp(Lend)))
    mk = [ch["masks"] for ch in chains]
    Ab = [_bdot_nt(pre[i]["ar"], pre[i]["bt"]) for i in R]
    Ak = [_bdot_nt(pre[i]["ar"], pre[i]["kt"]) for i in R]
    yield
    Aab = [jnp.where(mk[i][0], Ab[i][:C], 0.0) for i in R]
    Arb = [jnp.where(mk[i][1], Ab[i][C:], 0.0).astype(BF16) for i in R]
    AakArk = [jnp.concatenate([jnp.where(mk[i][0], Ak[i][:C], 0.0), jnp.where(mk[i][1], Ak[i][C:], 0.0)],
                              axis=0).astype(BF16) for i in R]
    AV = [dot(AakArk[i], bd(pre[i]["vb"])) for i in R]
    Xb = [jnp.where(mk[i][2], Aab[i], 0.0).astype(BF16) for i in R]
    X2b = [dot(Xb[i], bd(Xb[i])).astype(BF16) for i in R]
    yield
    T = [mk[i][6] + Xb[i].astype(F32) for i in R]
    T = [T[i] + dot(T[i].astype(BF16), bd(X2b[i])) for i in R]
    X4b = [dot(X2b[i], bd(X2b[i])).astype(BF16) for i in R]
    yield
    T = [T[i] + dot(T[i].astype(BF16), bd(X4b[i])) for i in R]
    yield
    for lvl in (3, 4, 5):
        Tb = [T[i].astype(BF16) for i in R]
        ET = [dot(jnp.where(mk[i][lvl], Aab[i], 0.0).astype(BF16), bd(Tb[i])).astype(BF16) for i in R]
        yield
        T = [T[i] + dot(Tb[i], bd(ET[i])) for i in R]
        yield
    Tb = [T[i].astype(BF16) for i in R]
    A0p = [dot(Tb[i], pre[i]["a0"]).astype(BF16) for i in R]
    Uv = [dot(Tb[i], bd(AV[i][:C].astype(BF16))).astype(BF16) for i in R]
    yield
    Rpp = [pre[i]["r0"] + dot(Arb[i], bd(A0p[i])) for i in R]
    Yv = [dot(Arb[i], bd(Uv[i])) + AV[i][C:] for i in R]
    eye2 = (_iota2(bd_mask.shape, 0) == _iota2(bd_mask.shape, 1)).astype(F32)
    P = [jnp.where(bd_mask, _bdot_tn(pre[i]["bh"], A0p[i]), 0.0) + eye2 * pre[i]["dec"] for i in R]
    Q = [jnp.where(bd_mask, _bdot_tn(pre[i]["bh"], Uv[i]) + _bdot_tn(pre[i]["kh"], pre[i]["vb"]), 0.0) for i in R]
    out.extend((Rpp[i], Yv[i], P[i], Q[i]) for i in R)


def _make_masks(C, N, rev):
    ii = _iota2((C, 2 * N), 0)
    jj = _iota2((C, 2 * N), 1) % N
    strict = (jj > ii) if rev else (jj < ii)
    rmask = strict if rev else (jj <= ii)
    blk = lambda s: (ii // s) == (jj // s)
    m8 = blk(8)
    e16 = blk(16) & jnp.logical_not(blk(8))
    e32 = blk(32) & jnp.logical_not(blk(16))
    e64 = jnp.logical_not(blk(32))
    eye = (ii == jj).astype(F32)
    return strict, rmask, m8, e16, e32, e64, eye


RW_UNROLL = 8


def _rwkv_scan_kernel(r_ref, k_ref, v_ref, kk_ref, g_ref, lwf_ref, lf_ref, asf_ref, lwb_ref, lb_ref, asb_ref,
                      ka_ref, rk_ref, gnw_ref, gnb_ref, havg_ref, o_ref,
                      yf_ref, yb_ref, h_ref, rpp_ref, yv_ref, p_ref, q_ref):
    S = r_ref.shape[1]
    C = RW_CHUNK
    N = HEAD_DIM
    W = 2 * N
    n = S // C
    U = RW_UNROLL
    nset = n // U
    masks_f = _make_masks(C, N, False)
    masks_b = _make_masks(C, N, True)
    bd_mask = (_iota2((W, W), 0) // N) == (_iota2((W, W), 1) // N)
    h_ref[...] = jnp.zeros_like(h_ref)
    dirs = ((False, lwf_ref, lf_ref, asf_ref, yf_ref, masks_f), (True, lwb_ref, lb_ref, asb_ref, yb_ref, masks_b))

    def chunk_rows(rev, st, u):
        c = st * U + u
        c = (n - 1 - c) if rev else c
        return pl.ds(pl.multiple_of(c * C, C), C)

    def state_step(st, u):
        slot = st % 2
        for di, (rev, lw_ref, l_ref, as_ref, y_ref, masks) in enumerate(dirs):
            Hb = h_ref[di].astype(BF16)
            y_ref[chunk_rows(rev, st, u), :] = (jnp.dot(rpp_ref[slot, di, u], Hb, preferred_element_type=F32)
                                                + yv_ref[slot, di, u])
            h_ref[di] = jnp.dot(p_ref[slot, di, u], Hb, preferred_element_type=F32) + q_ref[slot, di, u]

    def chunk_set(st, state_of):
        chains, slots = [], []
        for u in range(U):
            for di, (rev, lw_ref, l_ref, as_ref, y_ref, masks) in enumerate(dirs):
                rows = chunk_rows(rev, st, u)
                chains.append(dict(
                    r=r_ref[0, rows, :], k=k_ref[0, rows, :], v=v_ref[0, rows, :], kk=kk_ref[0, rows, :],
                    lw=lw_ref[0, rows, :], L=l_ref[0, rows, :], asig=as_ref[0, rows, :], ka=ka_ref[...],
                    rev=rev, masks=masks))
                slots.append((di, u))
        res = []
        pending = list(range(U)) if state_of is not None else []
        for stage, _ in enumerate(_chunk_terms(chains, res, bd_mask)):
            if pending and stage % 2 == 1:
                state_step(state_of, pending.pop(0))
        for u in pending:
            state_step(state_of, u)
        slot = st % 2
        for (di, u), (Rpp, Yv, P, Q) in zip(slots, res):
            rpp_ref[slot, di, u] = Rpp.astype(BF16)
            yv_ref[slot, di, u] = Yv
            p_ref[slot, di, u] = P.astype(BF16)
            q_ref[slot, di, u] = Q

    chunk_set(0, None)

    def body(st, carry):
        chunk_set(st, st - 1)
        return carry

    lax.fori_loop(1, nset, body, 0)
    for u in range(U):
        state_step(nset - 1, u)

    RT = 256

    def fin(i, carry):
        rows = pl.ds(pl.multiple_of(i * RT, RT), RT)
        y = yf_ref[rows, :] + yb_ref[rows, :]
        mean = _dot_sel(y, havg_ref[...])
        d = y - mean
        var = _dot_sel(d * d, havg_ref[...])
        yn = d * lax.rsqrt(var + RWKV_GN_EPS) * gnw_ref[...] + gnb_ref[...]
        r = r_ref[0, rows, :]
        kf = k_ref[0, rows, :] * (1.0 + (asf_ref[0, rows, :] - 1.0) * ka_ref[...])
        bonus = _dot_sel(r * kf * rk_ref[...], havg_ref[...]) * float(N) * v_ref[0, rows, :]
        o_ref[0, rows, :] = ((yn + bonus) * g_ref[0, rows, :]).astype(o_ref.dtype)
        return carry

    lax.fori_loop(0, S // RT, fin, 0)


def rwkv_scan(r, k, v, kk, g, lw_f, l_f, as_f, lw_b, l_b, as_b, k_a, r_k, gn_w, gn_b, out_dtype=F32):
    B, S, _ = r.shape
    W = 2 * HEAD_DIM
    N = HEAD_DIM
    havg = (jnp.arange(W)[:, None] // HEAD_DIM == jnp.arange(W)[None, :] // HEAD_DIM).astype(F32) / HEAD_DIM
    seq = pl.BlockSpec((1, S, W), lambda b, h: (b, 0, h))
    par = pl.BlockSpec((1, W), lambda b, h: (0, h))
    row = lambda a: a.reshape(1, -1)
    U = RW_UNROLL
    return pl.pallas_call(
        _rwkv_scan_kernel,
        grid=(B, RW_DIM // W),
        in_specs=[seq] * 11 + [par] * 4 + [pl.BlockSpec((W, W), lambda b, h: (0, 0))],
        out_specs=seq,
        out_shape=jax.ShapeDtypeStruct((B, S, RW_DIM), out_dtype),
        scratch_shapes=[pltpu.VMEM((S, W), F32), pltpu.VMEM((S, W), F32), pltpu.VMEM((2, W, W), F32),
                        pltpu.VMEM((2, 2, U, RW_CHUNK, W), BF16), pltpu.VMEM((2, 2, U, RW_CHUNK, W), F32),
                        pltpu.VMEM((2, 2, U, W, W), BF16), pltpu.VMEM((2, 2, U, W, W), F32)],
        compiler_params=_cparams(("parallel", "parallel")),
        name="rwkv_scan",
    )(r, k, v, kk, g, lw_f, l_f, as_f, lw_b, l_b, as_b, row(k_a), row(r_k), row(gn_w), row(gn_b), havg)


def rwkv7_mixer(p, shift_mu, w0_f, w_up_f, w0_b, w_up_b, a0_f, a_up_f, a0_b, a_up_b, g_up, k_k, k_a, r_k, gn_w,
                gn_b, out_dtype=F32, tt=256):
    outs = rwkv_prep(p, shift_mu, w0_f, w_up_f, w0_b, w_up_b, a0_f, a_up_f, a0_b, a_up_b, g_up, k_k, tt=tt)
    return rwkv_scan(*outs, k_a, r_k, gn_w, gn_b, out_dtype=out_dtype)


def _head_norm_rope(x, gain, cos, sin_signed, havg):
    ms = _dot_sel(x * x, havg)
    xn = x * lax.rsqrt(ms + NORM_EPS) * gain
    W = x.shape[1]
    even = (_iota2(x.shape, 1) % 2) == 0
    partner = jnp.where(even, pltpu.roll(xn, W - 1, 1), pltpu.roll(xn, 1, 1))
    return xn * cos + partner * sin_signed


def _attn_kernel(q_ref, k_ref, v_ref, cosq_ref, sinq_ref, cosk_ref, sink_ref, qg_ref, kg_ref, hq_ref, hk_ref,
                 o_ref, ks_ref, vt_ref):
    i = pl.program_id(1)
    G = ATT_HEADS // ATT_KV_HEADS
    D = HEAD_DIM

    @pl.when(i == 0)
    def _():
        kr = _head_norm_rope(k_ref[0], kg_ref[...], cosk_ref[...], sink_ref[...], hk_ref[...])
        vt = jnp.transpose(v_ref[0])
        ones = jnp.ones((ATT_VT_ROWS - D, vt.shape[1]), BF16)
        for kv in range(ATT_KV_HEADS):
            ks_ref[kv] = kr[:, kv * D:(kv + 1) * D].astype(BF16)
            vt_ref[kv] = jnp.concatenate([vt[kv * D:(kv + 1) * D, :].astype(BF16), ones], axis=0)

    q = _head_norm_rope(q_ref[0], qg_ref[...], cosq_ref[...], sinq_ref[...], hq_ref[...]) * (D ** -0.5 * LOG2_E)
    qb = q.astype(BF16)
    outs = []
    for kv in range(ATT_KV_HEADS):
        hs = range(kv * G, (kv + 1) * G)
        st = [lax.dot_general(ks_ref[kv], qb[:, h * D:(h + 1) * D], (((1,), (1,)), ((), ())),
                              preferred_element_type=F32) for h in hs]
        m = [jnp.max(x, axis=0, keepdims=True) for x in st]
        p = [jnp.exp2(x - mm).astype(BF16) for x, mm in zip(st, m)]
        ot = [jnp.dot(vt_ref[kv], x, preferred_element_type=F32) for x in p]
        outs.extend(o[:D] / o[D:D + 1] for o in ot)
    pairs = [jnp.transpose(jnp.concatenate(outs[2 * j:2 * j + 2], axis=0)) for j in range(ATT_HEADS // 2)]
    o_ref[0] = jnp.concatenate(pairs, axis=1).astype(o_ref.dtype)


def _rope_tables(S):
    rows = S // GRID_W
    row = jnp.repeat(jnp.arange(rows), GRID_W).astype(F32)
    col = jnp.tile(jnp.arange(GRID_W), rows).astype(F32)
    half = HEAD_DIM // 2
    freq = ROPE_THETA ** (-jnp.arange(0, half, 2, dtype=F32) / half)
    ang = jnp.concatenate([row[:, None] * freq, col[:, None] * freq], axis=-1)
    cos = jnp.repeat(jnp.cos(ang), 2, axis=-1)
    sin = jnp.repeat(jnp.sin(ang), 2, axis=-1)
    sign = jnp.where(jnp.arange(HEAD_DIM) % 2 == 0, -1.0, 1.0).astype(F32)
    return cos, sin * sign


def gqa_attention(p, q_norm, k_norm, *, tq=512, out_dtype=F32):
    B, S, _ = p.shape
    cos, sin = _rope_tables(S)
    tile = lambda t, n: jnp.tile(t, (1, n))
    hq = (jnp.arange(ATT_DIM)[:, None] // HEAD_DIM == jnp.arange(ATT_DIM)[None, :] // HEAD_DIM).astype(F32) / HEAD_DIM
    hk = hq[:ATT_KV_DIM, :ATT_KV_DIM]
    qg = jnp.tile(q_norm, ATT_HEADS).reshape(1, ATT_DIM)
    kg = jnp.tile(k_norm, ATT_KV_HEADS).reshape(1, ATT_KV_DIM)
    nq = ATT_DIM // ATT_KV_DIM
    const = lambda a: pl.BlockSpec(a.shape, lambda b, i: (0, 0))
    return pl.pallas_call(
        _attn_kernel,
        grid=(B, S // tq),
        in_specs=[
            pl.BlockSpec((1, tq, ATT_DIM), lambda b, i: (b, i, 0)),
            pl.BlockSpec((1, S, ATT_KV_DIM), lambda b, i: (b, 0, nq)),
            pl.BlockSpec((1, S, ATT_KV_DIM), lambda b, i: (b, 0, nq + 1)),
            pl.BlockSpec((tq, ATT_DIM), lambda b, i: (i, 0)),
            pl.BlockSpec((tq, ATT_DIM), lambda b, i: (i, 0)),
            pl.BlockSpec((S, ATT_KV_DIM), lambda b, i: (0, 0)),
            pl.BlockSpec((S, ATT_KV_DIM), lambda b, i: (0, 0)),
            const(qg), const(kg), const(hq), const(hk),
        ],
        out_specs=pl.BlockSpec((1, tq, ATT_DIM), lambda b, i: (b, i, 0)),
        out_shape=jax.ShapeDtypeStruct((B, S, ATT_DIM), out_dtype),
        scratch_shapes=[pltpu.VMEM((ATT_KV_HEADS, S, HEAD_DIM), BF16), pltpu.VMEM((ATT_KV_HEADS, ATT_VT_ROWS, S), BF16)],
        compiler_params=_cparams(("parallel", "arbitrary")),
        name="gqa_attention",
    )(p, p, p, tile(cos, ATT_HEADS), tile(sin, ATT_HEADS), tile(cos, ATT_KV_HEADS), tile(sin, ATT_KV_HEADS),
      qg, kg, hq, hk)


GLA_UNROLL = 8


def _gla_kernel(q_ref, k_ref, v_ref, gd_ref, og_ref, upf_ref, bf_ref, upb_ref, bb_ref, on_ref, o_ref,
                yf_ref, yb_ref, st_ref, kv_ref, qe_ref, dec_ref):
    S = q_ref.shape[1]
    C = GLA_CHUNK
    n = S // C
    U = GLA_UNROLL
    nset = n // U
    ii = _iota2((C, C), 0)
    jj = _iota2((C, C), 1)
    dirs = (
        (False, (jj <= ii).astype(F32), jj <= ii, upf_ref, bf_ref, yf_ref),
        (True, (jj >= ii).astype(F32), jj > ii, upb_ref, bb_ref, yb_ref),
    )
    scale = GLA_DKH ** -0.5
    st_ref[...] = jnp.zeros_like(st_ref)

    def chunk_rows(rev, st, u):
        c = st * U + u
        c = (n - 1 - c) if rev else c
        return pl.ds(pl.multiple_of(c * C, C), C)

    def state_step(st, u):
        slot = st % 2
        for di, (rev, tri, mask, up_ref, b_ref, y_ref) in enumerate(dirs):
            state = st_ref[di]
            rows = chunk_rows(rev, st, u)
            y_ref[rows, :] += lax.dot_general(qe_ref[slot, di, u], state.astype(BF16), (((1,), (1,)), ((), ())),
                                              preferred_element_type=F32)
            st_ref[di] = state * dec_ref[slot, di, u] + kv_ref[slot, di, u]

    def chunk_set(st, state_of):
        pending = list(range(U)) if state_of is not None else []
        items = []
        for u in range(U):
            for di, (rev, tri, mask, up_ref, b_ref, y_ref) in enumerate(dirs):
                rows = chunk_rows(rev, st, u)
                z = _dot3(gd_ref[0, rows, :], up_ref[...]) + b_ref[...]
                g = (jnp.minimum(z, 0.0) - jnp.log(1.0 + jnp.exp(-jnp.abs(z)))) * (1.0 / GLA_GATE_NORM)
                items.append(dict(u=u, di=di, rows=rows, rev=rev, mask=mask, y_ref=y_ref, g=g, tri=tri))
        for x in items:
            x["b"] = _sel_dot(x["tri"], x["g"])
        if pending:
            state_step(state_of, pending.pop(0))
        for x in items:
            b = x["b"]
            q = q_ref[0, x["rows"], :] * scale
            k = k_ref[0, x["rows"], :]
            b_mid = b[C // 2:C // 2 + 1, :]
            b_last = b[0:1, :] if x["rev"] else b[C - 1:C, :]
            x["vb"] = v_ref[0, x["rows"], :].astype(BF16)
            x["qm"] = (q * jnp.exp(b - b_mid)).astype(BF16)
            x["km"] = (k * jnp.exp(b_mid - b)).astype(BF16)
            x["ke"] = (k * jnp.exp(b_last - b)).astype(BF16)
            x["qe"] = (q * jnp.exp(b)).astype(BF16)
            x["dec"] = jnp.exp(b_last)
        att = [lax.dot_general(x["qm"], x["km"], (((1,), (1,)), ((), ())), preferred_element_type=F32) for x in items]
        if pending:
            state_step(state_of, pending.pop(0))
        att = [jnp.where(x["mask"], a, 0.0).astype(BF16) for x, a in zip(items, att)]
        slot = st % 2
        for idx, (x, a) in enumerate(zip(items, att)):
            x["y_ref"][x["rows"], :] = jnp.dot(a, x["vb"], preferred_element_type=F32)
            kv_ref[slot, x["di"], x["u"]] = lax.dot_general(x["vb"], x["ke"], (((0,), (0,)), ((), ())),
                                                            preferred_element_type=F32)
            qe_ref[slot, x["di"], x["u"]] = x["qe"]
            dec_ref[slot, x["di"], x["u"]] = x["dec"]
            if pending and idx % 3 == 2:
                state_step(state_of, pending.pop(0))
        for u in pending:
            state_step(state_of, u)

    chunk_set(0, None)

    def body(st, carry):
        chunk_set(st, st - 1)
        return carry

    lax.fori_loop(1, nset, body, 0)
    for u in range(U):
        state_step(nset - 1, u)

    RT = 256

    def fin(i, carry):
        rows = pl.ds(pl.multiple_of(i * RT, RT), RT)
        o = yf_ref[rows, :] + yb_ref[rows, :]
        ms = jnp.mean(o * o, axis=-1, keepdims=True)
        on = o * lax.rsqrt(ms + NORM_EPS) * on_ref[...]
        og = og_ref[0, rows, :]
        o_ref[0, rows, :] = (on * (og * jax.nn.sigmoid(og))).astype(o_ref.dtype)
        return carry

    lax.fori_loop(0, S // RT, fin, 0)


def gla_mixer(pm, gd, gate_up_f, gate_bias_f, gate_up_b, gate_bias_b, out_norm, out_dtype=F32):
    B, S, _ = pm.shape
    U = GLA_UNROLL
    H = GLA_HEADS
    kb = GLA_DK // GLA_DKH
    vb0 = 2 * GLA_DK // GLA_DVH
    ob0 = vb0 + GLA_DV // GLA_DVH
    return pl.pallas_call(
        _gla_kernel,
        grid=(B, H),
        in_specs=[
            pl.BlockSpec((1, S, GLA_DKH), lambda b, h: (b, 0, h)),
            pl.BlockSpec((1, S, GLA_DKH), lambda b, h: (b, 0, kb + h)),
            pl.BlockSpec((1, S, GLA_DVH), lambda b, h: (b, 0, vb0 + h)),
            pl.BlockSpec((1, S, GLA_GATE_RANK), lambda b, h: (b, 0, 0)),
            pl.BlockSpec((1, S, GLA_DVH), lambda b, h: (b, 0, ob0 + h)),
            pl.BlockSpec((GLA_GATE_RANK, GLA_DKH), lambda b, h: (0, h)),
            pl.BlockSpec((1, GLA_DKH), lambda b, h: (0, h)),
            pl.BlockSpec((GLA_GATE_RANK, GLA_DKH), lambda b, h: (0, h)),
            pl.BlockSpec((1, GLA_DKH), lambda b, h: (0, h)),
            pl.BlockSpec((1, GLA_DVH), lambda b, h: (0, 0)),
        ],
        out_specs=pl.BlockSpec((1, S, GLA_DVH), lambda b, h: (b, 0, h)),
        out_shape=jax.ShapeDtypeStruct((B, S, GLA_DV), out_dtype),
        scratch_shapes=[pltpu.VMEM((S, GLA_DVH), F32), pltpu.VMEM((S, GLA_DVH), F32),
                        pltpu.VMEM((2, GLA_DVH, GLA_DKH), F32),
                        pltpu.VMEM((2, 2, U, GLA_DVH, GLA_DKH), F32), pltpu.VMEM((2, 2, U, GLA_CHUNK, GLA_DKH), BF16),
                        pltpu.VMEM((2, 2, U, 1, GLA_DKH), F32)],
        compiler_params=_cparams(("parallel", "parallel")),
        name="gla_mixer",
    )(pm, pm, pm, gd, pm, gate_up_f, gate_bias_f.reshape(1, -1), gate_up_b, gate_bias_b.reshape(1, -1),
      out_norm.reshape(1, -1))


def _router_kernel(x_ref, g_ref, wr_ref, hn_ref, idx_ref, gate_ref):
    x = x_ref[...]
    ms = jnp.mean(x * x, axis=-1, keepdims=True)
    hn = x * lax.rsqrt(ms + NORM_EPS) * g_ref[...]
    hn_ref[...] = hn
    logits = _dot3(hn, wr_ref[...])
    lane = _iota2(logits.shape, 1)
    neg = jnp.float32(-jnp.inf)
    logits = jnp.where(lane < N_EXPERTS, logits, neg)
    m1 = jnp.max(logits, axis=-1, keepdims=True)
    i1 = jnp.min(jnp.where(logits == m1, lane, LANES), axis=-1, keepdims=True)
    rest = jnp.where(lane == i1, neg, logits)
    m2 = jnp.max(rest, axis=-1, keepdims=True)
    i2 = jnp.min(jnp.where(rest == m2, lane, LANES), axis=-1, keepdims=True)
    e2 = jnp.exp(m2 - m1)
    g1 = 1.0 / (1.0 + e2)
    g2 = e2 / (1.0 + e2)
    idx_ref[...] = jnp.where(lane == 0, i1, jnp.where(lane == 1, i2, 0))
    gate_ref[...] = jnp.where(lane == 0, g1, jnp.where(lane == 1, g2, 0.0))


def moe_router(x, g, router, *, tm=512):
    T, D = x.shape
    wr = jnp.zeros((D, LANES), F32).at[:, :N_EXPERTS].set(router)
    return pl.pallas_call(
        _router_kernel,
        grid=(T // tm,),
        in_specs=[
            pl.BlockSpec((tm, D), lambda i: (i, 0)),
            pl.BlockSpec((1, D), lambda i: (0, 0)),
            pl.BlockSpec((D, LANES), lambda i: (0, 0)),
        ],
        out_specs=[
            pl.BlockSpec((tm, D), lambda i: (i, 0)),
            pl.BlockSpec((tm, LANES), lambda i: (i, 0)),
            pl.BlockSpec((tm, LANES), lambda i: (i, 0)),
        ],
        out_shape=[jax.ShapeDtypeStruct((T, D), F32), jax.ShapeDtypeStruct((T, LANES), jnp.int32),
                   jax.ShapeDtypeStruct((T, LANES), F32)],
        compiler_params=_cparams(("parallel",)),
        name="moe_router",
    )(x, g.reshape(1, D), wr)


def _expert_kernel(te_ref, tv_ref, tok_ref, hn_hbm, wg_ref, wu_ref, wd_ref, o_ref, xg_ref, xb_ref, acc_ref, sem, *, nf):
    i = pl.program_id(0)
    f = pl.program_id(1)
    tm = xb_ref.shape[0]
    valid = tv_ref[i] > 0
    prev_valid = tv_ref[jnp.maximum(i - 1, 0)] > 0
    slot = i % 2

    def issue_row(tile, dst_slot, r):
        tok = tok_ref[tile * tm + r]
        pltpu.make_async_copy(hn_hbm.at[pl.ds(tok, 1), :], xg_ref.at[dst_slot, pl.ds(r, 1), :], sem.at[dst_slot]).start()

    @pl.when(jnp.logical_and(valid, jnp.logical_and(i == 0, f == 0)))
    def _():
        def body(r, c):
            issue_row(0, 0, r)
            return c

        lax.fori_loop(0, tm, body, 0, unroll=8)

    @pl.when(jnp.logical_and(f == 0, jnp.logical_or(valid, jnp.logical_and(i > 0, prev_valid))))
    def _():
        pltpu.make_async_copy(hn_hbm.at[pl.ds(0, tm), :], xg_ref.at[slot], sem.at[slot]).wait()

    @pl.when(jnp.logical_and(valid, f == 0))
    def _():
        xb_ref[...] = xg_ref[slot].astype(BF16)

    @pl.when(valid)
    def _():
        per = tm // nf
        row0 = f * per
        xb = xb_ref[...]
        a = jnp.dot(xb, wg_ref[0], preferred_element_type=F32)
        for j in range(per // 2):
            issue_row(i + 1, 1 - slot, row0 + j)
        b = jnp.dot(xb, wu_ref[0], preferred_element_type=F32)
        for j in range(per // 2, per):
            issue_row(i + 1, 1 - slot, row0 + j)
        h = (a * jax.nn.sigmoid(a) * b).astype(BF16)
        part = jnp.dot(h, wd_ref[0], preferred_element_type=F32)

        @pl.when(f == 0)
        def _():
            acc_ref[...] = part

        @pl.when(f > 0)
        def _():
            acc_ref[...] += part

    @pl.when(f == nf - 1)
    def _():
        o_ref[...] = jnp.where(valid, acc_ref[...], 0.0)


def moe_experts(hn, tile_expert, tile_valid, row_tok, wg, wu, wd, *, tm, tf):
    T, D = hn.shape
    F = wg.shape[2]
    P = row_tok.shape[0] - tm
    nt = P // tm + 1
    nf = F // tf

    def w_in_map(i, f, te, tv, tok):
        return (te[i], 0, jnp.where(tv[i] > 0, f, nf - 1))

    def w_out_map(i, f, te, tv, tok):
        return (te[i], jnp.where(tv[i] > 0, f, nf - 1), 0)

    grid_spec = pltpu.PrefetchScalarGridSpec(
        num_scalar_prefetch=3,
        grid=(nt, nf),
        in_specs=[
            pl.BlockSpec(memory_space=pl.ANY),
            pl.BlockSpec((1, D, tf), w_in_map),
            pl.BlockSpec((1, D, tf), w_in_map),
            pl.BlockSpec((1, tf, D), w_out_map),
        ],
        out_specs=pl.BlockSpec((tm, D), lambda i, f, te, tv, tok: (i, 0)),
        scratch_shapes=[pltpu.VMEM((2, tm, D), F32), pltpu.VMEM((tm, D), BF16), pltpu.VMEM((tm, D), F32),
                        pltpu.SemaphoreType.DMA((2,))],
    )
    return pl.pallas_call(
        functools.partial(_expert_kernel, nf=nf),
        grid_spec=grid_spec,
        out_shape=jax.ShapeDtypeStruct((P + tm, D), F32),
        compiler_params=_cparams(("arbitrary", "arbitrary")),
        name="moe_experts",
    )(tile_expert, tile_valid, row_tok, hn, wg, wu, wd)


def _combine_kernel(dest_ref, x_ref, gate_ref, yb_hbm, o_ref, buf_ref, sem):
    i = pl.program_id(0)
    n = pl.num_programs(0)
    tc = x_ref.shape[0]
    slot = i % 2

    def issue(step, dst_slot):
        base = step * tc * TOP_K

        def body(r, c):
            t = base + r * TOP_K
            for k in range(TOP_K):
                pltpu.make_async_copy(yb_hbm.at[pl.ds(dest_ref[t + k], 1), :], buf_ref.at[dst_slot, k, pl.ds(r, 1), :],
                                      sem.at[dst_slot]).start()
            return c

        lax.fori_loop(0, tc, body, 0, unroll=4)

    @pl.when(i == 0)
    def _():
        issue(0, 0)

    for k in range(TOP_K):
        pltpu.make_async_copy(yb_hbm.at[pl.ds(0, tc), :], buf_ref.at[slot, k], sem.at[slot]).wait()

    @pl.when(i + 1 < n)
    def _():
        issue(i + 1, 1 - slot)

    g = gate_ref[...]
    o_ref[...] = x_ref[...] + g[:, 0:1] * buf_ref[slot, 0] + g[:, 1:2] * buf_ref[slot, 1]


def moe_combine(x, gates, yb, dest, *, tc=256):
    T, D = x.shape
    grid_spec = pltpu.PrefetchScalarGridSpec(
        num_scalar_prefetch=1,
        grid=(T // tc,),
        in_specs=[
            pl.BlockSpec((tc, D), lambda i, d: (i, 0)),
            pl.BlockSpec((tc, LANES), lambda i, d: (i, 0)),
            pl.BlockSpec(memory_space=pl.ANY),
        ],
        out_specs=pl.BlockSpec((tc, D), lambda i, d: (i, 0)),
        scratch_shapes=[pltpu.VMEM((2, TOP_K, tc, D), F32), pltpu.SemaphoreType.DMA((2,))],
    )
    return pl.pallas_call(
        _combine_kernel,
        grid_spec=grid_spec,
        out_shape=jax.ShapeDtypeStruct((T, D), F32),
        compiler_params=_cparams(("arbitrary",)),
        name="moe_combine",
    )(dest, x, gates, yb)


def moe_dispatch_plan(idx, *, tm):
    T = idx.shape[0]
    A = T * TOP_K
    e_flat = idx[:, :TOP_K].reshape(A)
    onehot = (e_flat[:, None] == jnp.arange(N_EXPERTS, dtype=jnp.int32)[None, :]).astype(jnp.int32)
    rank = jnp.sum((jnp.cumsum(onehot, axis=0) - onehot) * onehot, axis=1)
    counts = jnp.sum(onehot, axis=0)
    padded = (counts + tm - 1) // tm * tm
    ends = jnp.cumsum(padded)
    pstart = ends - padded
    dest = pstart[e_flat] + rank
    P = (A // tm + N_EXPERTS + 1) * tm
    nt = P // tm
    tok_flat = jnp.arange(A, dtype=jnp.int32) // TOP_K
    row_tok = jnp.zeros((P,), jnp.int32).at[dest].set(tok_flat)
    tile_start = jnp.arange(nt, dtype=jnp.int32) * tm
    tile_expert = jnp.minimum(jnp.sum((tile_start[:, None] >= ends[None, :]).astype(jnp.int32), axis=1), N_EXPERTS - 1)
    tile_valid = (tile_start < ends[-1]).astype(jnp.int32)
    last_valid = jnp.maximum(jnp.sum(tile_valid) - 1, 0)
    tile_expert = jnp.where(tile_valid > 0, tile_expert, tile_expert[last_valid])
    return dest.astype(jnp.int32), row_tok, tile_expert, tile_valid


def moe_layer(x, norm2, router, wg, wu, wd, *, tm=512, tf=1792, tr=512, tc=256):
    hn, idx, gates = moe_router(x, norm2, router, tm=tr)
    dest, row_tok, tile_expert, tile_valid = moe_dispatch_plan(idx, tm=tm)
    yb = moe_experts(hn, tile_expert, tile_valid, row_tok, wg, wu, wd, tm=tm, tf=tf)
    return moe_combine(x, gates, yb, dest, tc=tc)


def _even_layer(x, norm1, w_in, shift_mu, w0_f, w_up_f, w0_b, w_up_b, a0_f, a_up_f, a0_b, a_up_b, g_up,
                k_k, k_a, r_k, gn_w, gn_b, q_norm, k_norm, w_out, norm2, ffn_gate, ffn_up, ffn_down, cast):
    Bn, S, D = x.shape
    T = Bn * S
    xf = x.reshape(T, D)
    w_in = w_in.astype(BF16)
    p_rw = norm_matmul(xf, norm1, w_in[:, :RW_IN], tm=1024, tn=RW_IN // 2).reshape(Bn, S, RW_IN)
    p_att = norm_matmul(xf, norm1, w_in[:, RW_IN:], tm=1024, tn=ATT_IN).reshape(Bn, S, ATT_IN)
    y_a = rwkv7_mixer(p_rw, shift_mu, w0_f, w_up_f, w0_b, w_up_b, a0_f, a_up_f, a0_b, a_up_b,
                      g_up, k_k, k_a, r_k, gn_w, gn_b, out_dtype=BF16)
    y_b = gqa_attention(p_att, q_norm, k_norm, out_dtype=BF16)
    w_out = w_out.astype(BF16)
    xf = matmul_residual([y_a.reshape(T, RW_DIM), y_b.reshape(T, ATT_DIM)], [w_out[:RW_DIM], w_out[RW_DIM:]], xf,
                         tm=1024, tn=1024)
    return ffn_swiglu(xf, norm2, ffn_gate.astype(BF16), ffn_up.astype(BF16), ffn_down.astype(BF16), tm=512, tf=1408,
                      cast=cast)


def _odd_layer(xf, Bn, S, norm1, w_in, gate_up_f, gate_bias_f, gate_up_b, gate_bias_b, out_norm, w_out,
               norm2, router, exp_gate, exp_up, exp_down):
    T, D = xf.shape
    o3 = 2 * GLA_DK + GLA_DV
    o4 = o3 + GLA_GATE_RANK
    w_main = jnp.concatenate([w_in[:, :o3], w_in[:, o4:]], axis=1).astype(BF16)
    w_gd = jnp.zeros((D, LANES), F32).at[:, :GLA_GATE_RANK].set(w_in[:, o3:o4])
    p_main = norm_matmul(xf, norm1, w_main, tm=1024, tn=1024).reshape(Bn, S, -1)
    gd = norm_matmul(xf, norm1, w_gd, tm=1024, tn=LANES)[:, :GLA_GATE_RANK].reshape(Bn, S, GLA_GATE_RANK)
    o = gla_mixer(p_main, gd, gate_up_f, gate_bias_f, gate_up_b, gate_bias_b, out_norm, out_dtype=BF16)
    xf = matmul_residual([o.reshape(T, GLA_DV)], [w_out.astype(BF16)], xf, tm=1024, tn=1024)
    return moe_layer(xf, norm2, router, exp_gate, exp_up, exp_down)


def kernel(x, e_norm1, e_w_in, e_shift_mu, e_w0_f, e_w_up_f, e_w0_b, e_w_up_b, e_a0_f, e_a_up_f, e_a0_b, e_a_up_b, e_g_up, e_k_k, e_k_a, e_r_k, e_gn_w, e_gn_b, e_q_norm, e_k_norm, e_w_out, e_norm2, e_ffn_gate, e_ffn_up, e_ffn_down, o_norm1, o_w_in, o_gate_up_f, o_gate_bias_f, o_gate_up_b, o_gate_bias_b, o_out_norm, o_w_out, o_norm2, o_router, o_exp_gate, o_exp_up, o_exp_down):
    Bn, S, D = x.shape
    xf, (exp_gate, exp_up, exp_down) = _even_layer(
        x, e_norm1[0], e_w_in[0], e_shift_mu[0], e_w0_f[0], e_w_up_f[0], e_w0_b[0],
        e_w_up_b[0], e_a0_f[0], e_a_up_f[0], e_a0_b[0], e_a_up_b[0], e_g_up[0], e_k_k[0],
        e_k_a[0], e_r_k[0], e_gn_w[0], e_gn_b[0], e_q_norm[0], e_k_norm[0], e_w_out[0],
        e_norm2[0], e_ffn_gate[0], e_ffn_up[0], e_ffn_down[0], (o_exp_gate[0], o_exp_up[0], o_exp_down[0]))
    xf = _odd_layer(xf, Bn, S, o_norm1[0], o_w_in[0], o_gate_up_f[0], o_gate_bias_f[0], o_gate_up_b[0],
                    o_gate_bias_b[0], o_out_norm[0], o_w_out[0], o_norm2[0], o_router[0],
                    exp_gate, exp_up, exp_down)
    return xf.reshape(Bn, S, D)
```

```python
import functools

import jax
import jax.numpy as jnp
from jax import lax
from jax.experimental import pallas as pl
from jax.experimental.pallas import tpu as pltpu

F32 = jnp.float32
BF16 = jnp.bfloat16
HI = lax.Precision.HIGHEST

D_MODEL = 1024
GRID_W = 64
HEAD_DIM = 64
NORM_EPS = 1e-6
RW_HEADS = 8
RW_DIM = 512
DECAY_RANK = 64
ICLR_RANK = 64
GATE_RANK = 128
RWKV_GN_EPS = 64e-5
RW_IN = 3 * RW_DIM + DECAY_RANK + ICLR_RANK + GATE_RANK
RW_CHUNK = 64
ATT_HEADS = 8
ATT_KV_HEADS = 2
ATT_DIM = 512
ATT_KV_DIM = 128
ATT_IN = ATT_DIM + 2 * ATT_KV_DIM
ROPE_THETA = 10000.0
ATT_VT_ROWS = HEAD_DIM + 16
LOG2_E = 1.4426950408889634
GLA_HEADS = 4
GLA_DK = 512
GLA_DV = 1024
GLA_DKH = 128
GLA_DVH = 256
GLA_GATE_RANK = 16
GLA_GATE_NORM = 16.0
GLA_CHUNK = 64
N_EXPERTS = 8
TOP_K = 2
LANES = 128

VMEM_LIMIT = 48 * 1024 * 1024
VMEM_LIMIT_BIG = 56 * 1024 * 1024


def _cparams(sem, vmem=VMEM_LIMIT):
    return pltpu.CompilerParams(dimension_semantics=sem, vmem_limit_bytes=vmem)


def _iota2(shape, dim):
    return lax.broadcasted_iota(jnp.int32, shape, dim)


def _split(x, parts):
    out = []
    for _ in range(parts):
        t = x.astype(BF16)
        out.append(t)
        x = x - t.astype(F32)
    return out


def _dot_sel(x, m, parts=2):
    mb = m.astype(BF16)
    acc = None
    for t in _split(x, parts):
        d = jnp.dot(t, mb, preferred_element_type=F32)
        acc = d if acc is None else acc + d
    return acc


def _sel_dot(m, x, parts=3):
    mb = m.astype(BF16)
    acc = None
    for t in _split(x, parts):
        d = jnp.dot(mb, t, preferred_element_type=F32)
        acc = d if acc is None else acc + d
    return acc


def _dot3(a, b):
    ah, al = _split(a, 2)
    bh, bl = _split(b, 2)
    d = lambda u, v: jnp.dot(u, v, preferred_element_type=F32)
    return d(ah, bh) + (d(ah, bl) + d(al, bh))


def _bdot(a, b):
    return jnp.dot(a.astype(BF16), b.astype(BF16), preferred_element_type=F32)


def _bdot_nt(a, b):
    return lax.dot_general(a.astype(BF16), b.astype(BF16), (((1,), (1,)), ((), ())), preferred_element_type=F32)


def _bdot_tn(a, b):
    return lax.dot_general(a.astype(BF16), b.astype(BF16), (((0,), (0,)), ((), ())), preferred_element_type=F32)


def _norm_mm_kernel(x_ref, g_ref, w_ref, o_ref, xn_ref):
    @pl.when(pl.program_id(1) == 0)
    def _():
        x = x_ref[...]
        ms = jnp.mean(x * x, axis=-1, keepdims=True)
        xn_ref[...] = (x * lax.rsqrt(ms + NORM_EPS) * g_ref[...]).astype(xn_ref.dtype)

    if xn_ref.dtype == F32:
        o_ref[...] = _dot3(xn_ref[...], w_ref[...]).astype(o_ref.dtype)
    else:
        o_ref[...] = jnp.dot(xn_ref[...], w_ref[...], preferred_element_type=F32).astype(o_ref.dtype)


def norm_matmul(x, g, w, *, tm, tn, out_dtype=F32):
    M, K = x.shape
    N = w.shape[1]
    return pl.pallas_call(
        _norm_mm_kernel,
        grid=(M // tm, N // tn),
        in_specs=[
            pl.BlockSpec((tm, K), lambda i, j: (i, 0)),
            pl.BlockSpec((1, K), lambda i, j: (0, 0)),
            pl.BlockSpec((K, tn), lambda i, j: (0, j)),
        ],
        out_specs=pl.BlockSpec((tm, tn), lambda i, j: (i, j)),
        out_shape=jax.ShapeDtypeStruct((M, N), out_dtype),
        scratch_shapes=[pltpu.VMEM((tm, K), w.dtype)],
        compiler_params=_cparams(("parallel", "arbitrary")),
        name="norm_matmul",
    )(x, g.reshape(1, K), w)


def _mm_res_kernel(*refs):
    n = (len(refs) - 2) // 2
    r_ref, o_ref = refs[2 * n], refs[2 * n + 1]
    acc = r_ref[...]
    for y_ref, w_ref in zip(refs[:n], refs[n:2 * n]):
        acc = acc + jnp.dot(y_ref[...].astype(BF16), w_ref[...], preferred_element_type=F32)
    o_ref[...] = acc


def matmul_residual(ys, ws, r, *, tm, tn):
    M, N = r.shape
    y_specs = [pl.BlockSpec((tm, y.shape[1]), lambda i, j: (i, 0)) for y in ys]
    w_specs = [pl.BlockSpec((w.shape[0], tn), lambda i, j: (0, j)) for w in ws]
    return pl.pallas_call(
        _mm_res_kernel,
        grid=(M // tm, N // tn),
        in_specs=y_specs + w_specs + [pl.BlockSpec((tm, tn), lambda i, j: (i, j))],
        out_specs=pl.BlockSpec((tm, tn), lambda i, j: (i, j)),
        out_shape=jax.ShapeDtypeStruct((M, N), F32),
        compiler_params=_cparams(("parallel", "arbitrary")),
        name="matmul_residual",
    )(*ys, *ws, r)


def _ffn_kernel(*refs, n_cast):
    x_ref, g_ref, wg_ref, wu_ref, wd_ref = refs[:5]
    cast_in = refs[5:5 + n_cast]
    o_ref = refs[5 + n_cast]
    cast_out = refs[6 + n_cast:6 + 2 * n_cast]
    xn_ref, acc_ref = refs[6 + 2 * n_cast:]
    f = pl.program_id(1)

    @pl.when(f == 0)
    def _():
        x = x_ref[...]
        ms = jnp.mean(x * x, axis=-1, keepdims=True)
        xn_ref[...] = (x * lax.rsqrt(ms + NORM_EPS) * g_ref[...]).astype(BF16)

    for src, dst in zip(cast_in, cast_out):
        dst[...] = src[...].astype(BF16)

    xn = xn_ref[...]
    a = jnp.dot(xn, wg_ref[...], preferred_element_type=F32)
    b = jnp.dot(xn, wu_ref[...], preferred_element_type=F32)
    h = (a * jax.nn.sigmoid(a) * b).astype(BF16)
    part = jnp.dot(h, wd_ref[...], preferred_element_type=F32)

    @pl.when(f == 0)
    def _():
        acc_ref[...] = part

    @pl.when(f > 0)
    def _():
        acc_ref[...] += part

    @pl.when(f == pl.num_programs(1) - 1)
    def _():
        o_ref[...] = x_ref[...] + acc_ref[...]


def ffn_swiglu(x, g, wg, wu, wd, *, tm, tf, cast=()):
    M, D = x.shape
    F = wg.shape[1]
    nf = F // tf
    steps = (M // tm) * nf
    cast_specs, cast_shapes = [], []
    for w in cast:
        E, R, C = w.shape
        per = steps // E
        cast_specs.append(pl.BlockSpec((1, R // per, C), lambda i, f, per=per: ((i * nf + f) // per, (i * nf + f) % per, 0)))
        cast_shapes.append(jax.ShapeDtypeStruct(w.shape, BF16))
    outs = pl.pallas_call(
        functools.partial(_ffn_kernel, n_cast=len(cast)),
        grid=(M // tm, nf),
        in_specs=[
            pl.BlockSpec((tm, D), lambda i, f: (i, 0)),
            pl.BlockSpec((1, D), lambda i, f: (0, 0)),
            pl.BlockSpec((D, tf), lambda i, f: (0, f)),
            pl.BlockSpec((D, tf), lambda i, f: (0, f)),
            pl.BlockSpec((tf, D), lambda i, f: (f, 0)),
        ] + cast_specs,
        out_specs=[pl.BlockSpec((tm, D), lambda i, f: (i, 0))] + cast_specs,
        out_shape=[jax.ShapeDtypeStruct((M, D), F32)] + cast_shapes,
        scratch_shapes=[pltpu.VMEM((tm, D), BF16), pltpu.VMEM((tm, D), F32)],
        compiler_params=_cparams(("arbitrary", "arbitrary"), vmem=VMEM_LIMIT_BIG if cast else VMEM_LIMIT),
        name="ffn_swiglu",
    )(x, g.reshape(1, D), wg, wu, wd, *cast)
    return outs[0], tuple(outs[1:])


def _rwkv_prep_kernel(cur_ref, prev_ref, next_ref, mu_ref, w0f_ref, wupf_ref, w0b_ref, wupb_ref,
                      a0f_ref, aupf_ref, a0b_ref, aupb_ref, gup_ref, kk_ref, hsum_ref,
                      r_ref, k_ref, v_ref, kkn_ref, g_ref, lwf_ref, lf_ref, asf_ref, lwb_ref, lb_ref, asb_ref):
    i = pl.program_id(1)
    nt = pl.num_programs(1)
    x = cur_ref[0]
    Tt = x.shape[0]
    row = _iota2(x.shape, 0)
    prev_row = jnp.where(i > 0, prev_ref[0, 7:8, :], 0.0)
    next_row = jnp.where(i < nt - 1, next_ref[0, 0:1, :], 0.0)
    prev = jnp.where(row == 0, prev_row, pltpu.roll(x, 1, 0))
    nxt = jnp.where(row == Tt - 1, next_row, pltpu.roll(x, Tt - 1, 0))
    pm = x + (0.5 * (prev + nxt) - x) * mu_ref[...]
    o1 = 3 * RW_DIM
    o2 = o1 + DECAY_RANK
    o3 = o2 + ICLR_RANK
    r_ref[0] = pm[:, :RW_DIM]
    k = pm[:, RW_DIM:2 * RW_DIM]
    k_ref[0] = k
    v_ref[0] = pm[:, 2 * RW_DIM:o1]
    wd = jnp.tanh(pm[:, o1:o2])
    ad = pm[:, o2:o3]
    gin = jax.nn.sigmoid(pm[:, o3:])
    g_ref[0] = jnp.dot(gin.astype(BF16), gup_ref[...].astype(BF16), preferred_element_type=F32)
    kk = k * kk_ref[...]
    ss = _dot_sel(kk * kk, hsum_ref[...])
    kkn_ref[0] = kk / jnp.maximum(jnp.sqrt(ss), 1e-12)

    ci = _iota2((RW_CHUNK, RW_CHUNK), 0)
    cj = _iota2((RW_CHUNK, RW_CHUNK), 1)
    tril = (cj <= ci).astype(F32)
    triu = (cj >= ci).astype(F32)

    def direction(w0_ref, wup_ref, a0_ref, aup_ref, tri, lw_ref, l_ref, as_ref):
        z = w0_ref[...] + _dot3(wd, wup_ref[...])
        sp = jnp.maximum(-z, 0.0) + jnp.log(1.0 + jnp.exp(-jnp.abs(z)))
        lw = -jnp.exp(-sp - 0.5)
        lw_ref[0] = lw
        as_ref[0] = jax.nn.sigmoid(a0_ref[...] + _dot3(ad, aup_ref[...]))
        for c in range(Tt // RW_CHUNK):
            sl = slice(c * RW_CHUNK, (c + 1) * RW_CHUNK)
            l_ref[0, sl, :] = _sel_dot(tri, lw[sl, :])

    direction(w0f_ref, wupf_ref, a0f_ref, aupf_ref, tril, lwf_ref, lf_ref, asf_ref)
    direction(w0b_ref, wupb_ref, a0b_ref, aupb_ref, triu, lwb_ref, lb_ref, asb_ref)


def rwkv_prep(p, shift_mu, w0_f, w_up_f, w0_b, w_up_b, a0_f, a_up_f, a0_b, a_up_b, g_up, k_k, *, tt=256):
    B, S, _ = p.shape
    nt = S // tt
    hsum = (jnp.arange(RW_DIM)[:, None] // HEAD_DIM == jnp.arange(RW_DIM)[None, :] // HEAD_DIM).astype(F32)
    row = lambda a: a.reshape(1, -1)
    full = lambda a: pl.BlockSpec(a.shape, lambda b, i: (0,) * a.ndim)
    params = [row(shift_mu), row(w0_f), w_up_f, row(w0_b), w_up_b, row(a0_f), a_up_f, row(a0_b), a_up_b, g_up,
              row(k_k), hsum]
    tb = tt // 8
    in_specs = [
        pl.BlockSpec((1, tt, RW_IN), lambda b, i: (b, i, 0)),
        pl.BlockSpec((1, 8, RW_IN), lambda b, i: (b, jnp.maximum(i * tb - 1, 0), 0)),
        pl.BlockSpec((1, 8, RW_IN), lambda b, i: (b, jnp.minimum((i + 1) * tb, S // 8 - 1), 0)),
    ] + [full(a) for a in params]
    out_spec = pl.BlockSpec((1, tt, RW_DIM), lambda b, i: (b, i, 0))
    out_sds = jax.ShapeDtypeStruct((B, S, RW_DIM), F32)
    return pl.pallas_call(
        _rwkv_prep_kernel,
        grid=(B, nt),
        in_specs=in_specs,
        out_specs=[out_spec] * 11,
        out_shape=[out_sds] * 11,
        compiler_params=_cparams(("parallel", "arbitrary")),
        name="rwkv_prep",
    )(p, p, p, *params)


def _bd(y, bd_mask):
    return jnp.where(bd_mask, jnp.concatenate([y, y], axis=0), jnp.zeros((), y.dtype))


def _chunk_terms(chains, out, bd_mask):
    C = chains[0]["r"].shape[0]
    n = len(chains)
    R = range(n)
    dot = lambda u, v: jnp.dot(u, v, preferred_element_type=F32)
    bd = lambda y: _bd(y, bd_mask)
    pre = []
    for ch in chains:
        r, k, v, kk, lw, L, asig, ka, rev = (ch[x] for x in ("r", "k", "v", "kk", "lw", "L", "asig", "ka", "rev"))
        a = -kk
        b = kk * asig
        kd = k * (1.0 + (asig - 1.0) * ka)
        Lp = L - lw
        Lr = Lp if rev else L
        Lend = L[0:1, :] if rev else L[C - 1:C, :]
        Lmid = L[C // 2:C // 2 + 1, :]
        einv = jnp.exp(Lmid - L)
        eend = jnp.exp(Lend - L)
        pre.append(dict(
            ar=jnp.concatenate([a * jnp.exp(Lp - Lmid), r * jnp.exp(Lr - Lmid)], axis=0).astype(BF16),
            bt=bd((b * einv).astype(BF16)), kt=bd((kd * einv).astype(BF16)),
            a0=bd((a * jnp.exp(Lp)).astype(BF16)), r0=r * jnp.exp(Lr),
            bh=(b * eend).astype(BF16), kh=(kd * eend).astype(BF16), vb=v.astype(BF16),
            dec=jnp.exp(Lend)))
    mk = [ch["masks"] for ch in chains]
    Ab = [_bdot_nt(pre[i]["ar"], pre[i]["bt"]) for i in R]
    Ak = [_bdot_nt(pre[i]["ar"], pre[i]["kt"]) for i in R]
    yield
    Aab = [jnp.where(mk[i][0], Ab[i][:C], 0.0) for i in R]
    Arb = [jnp.where(mk[i][1], Ab[i][C:], 0.0).astype(BF16) for i in R]
    AakArk = [jnp.concatenate([jnp.where(mk[i][0], Ak[i][:C], 0.0), jnp.where(mk[i][1], Ak[i][C:], 0.0)],
                              axis=0).astype(BF16) for i in R]
    AV = [dot(AakArk[i], bd(pre[i]["vb"])) for i in R]
    Xb = [jnp.where(mk[i][2], Aab[i], 0.0).astype(BF16) for i in R]
    X2b = [dot(Xb[i], bd(Xb[i])).astype(BF16) for i in R]
    yield
    T = [mk[i][6] + Xb[i].astype(F32) for i in R]
    T = [T[i] + dot(T[i].astype(BF16), bd(X2b[i])) for i in R]
    X4b = [dot(X2b[i], bd(X2b[i])).astype(BF16) for i in R]
    yield
    T = [T[i] + dot(T[i].astype(BF16), bd(X4b[i])) for i in R]
    yield
    for lvl in (3, 4, 5):
        Tb = [T[i].astype(BF16) for i in R]
        ET = [dot(jnp.where(mk[i][lvl], Aab[i], 0.0).astype(BF16), bd(Tb[i])).astype(BF16) for i in R]
        yield
        T = [T[i] + dot(Tb[i], bd(ET[i])) for i in R]
        yield
    Tb = [T[i].astype(BF16) for i in R]
    A0p = [dot(Tb[i], pre[i]["a0"]).astype(BF16) for i in R]
    Uv = [dot(Tb[i], bd(AV[i][:C].astype(BF16))).astype(BF16) for i in R]
    yield
    Rpp = [pre[i]["r0"] + dot(Arb[i], bd(A0p[i])) for i in R]
    Yv = [dot(Arb[i], bd(Uv[i])) + AV[i][C:] for i in R]
    eye2 = (_iota2(bd_mask.shape, 0) == _iota2(bd_mask.shape, 1)).astype(F32)
    P = [jnp.where(bd_mask, _bdot_tn(pre[i]["bh"], A0p[i]), 0.0) + eye2 * pre[i]["dec"] for i in R]
    Q = [jnp.where(bd_mask, _bdot_tn(pre[i]["bh"], Uv[i]) + _bdot_tn(pre[i]["kh"], pre[i]["vb"]), 0.0) for i in R]
    out.extend((Rpp[i], Yv[i], P[i], Q[i]) for i in R)


def _make_masks(C, N, rev):
    ii = _iota2((C, 2 * N), 0)
    jj = _iota2((C, 2 * N), 1) % N
    strict = (jj > ii) if rev else (jj < ii)
    rmask = strict if rev else (jj <= ii)
    blk = lambda s: (ii // s) == (jj // s)
    m8 = blk(8)
    e16 = blk(16) & jnp.logical_not(blk(8))
    e32 = blk(32) & jnp.logical_not(blk(16))
    e64 = jnp.logical_not(blk(32))
    eye = (ii == jj).astype(F32)
    return strict, rmask, m8, e16, e32, e64, eye


RW_UNROLL = 8
RW_GROUPS = 4
RW_SKEW = 1


def _rwkv_scan_kernel(r_ref, k_ref, v_ref, kk_ref, g_ref, lwf_ref, lf_ref, asf_ref, lwb_ref, lb_ref, asb_ref,
                      ka_ref, rk_ref, gnw_ref, gnb_ref, havg_ref, o_ref,
                      yf_ref, yb_ref, h_ref, rpp_ref, yv_ref, p_ref, q_ref):
    S = r_ref.shape[1]
    C = RW_CHUNK
    N = HEAD_DIM
    W = 2 * N
    n = S // C
    U = RW_UNROLL
    nset = n // U
    masks_f = _make_masks(C, N, False)
    masks_b = _make_masks(C, N, True)
    bd_mask = (_iota2((W, W), 0) // N) == (_iota2((W, W), 1) // N)
    h_ref[...] = jnp.zeros_like(h_ref)
    dirs = ((False, lwf_ref, lf_ref, asf_ref, yf_ref, masks_f), (True, lwb_ref, lb_ref, asb_ref, yb_ref, masks_b))

    def chunk_rows(rev, st, u):
        c = st * U + u
        c = (n - 1 - c) if rev else c
        return pl.ds(pl.multiple_of(c * C, C), C)

    def state_step(st, u):
        slot = st % 2
        for di, (rev, lw_ref, l_ref, as_ref, y_ref, masks) in enumerate(dirs):
            Hb = h_ref[di].astype(BF16)
            y_ref[chunk_rows(rev, st, u), :] = (jnp.dot(rpp_ref[slot, di, u], Hb, preferred_element_type=F32)
                                                + yv_ref[slot, di, u])
            h_ref[di] = jnp.dot(p_ref[slot, di, u], Hb, preferred_element_type=F32) + q_ref[slot, di, u]

    def chunk_set(st, state_of):
        chains, slots = [], []
        for u in range(U):
            for di, (rev, lw_ref, l_ref, as_ref, y_ref, masks) in enumerate(dirs):
                rows = chunk_rows(rev, st, u)
                chains.append(dict(
                    r=r_ref[0, rows, :], k=k_ref[0, rows, :], v=v_ref[0, rows, :], kk=kk_ref[0, rows, :],
                    lw=lw_ref[0, rows, :], L=l_ref[0, rows, :], asig=as_ref[0, rows, :], ka=ka_ref[...],
                    rev=rev, masks=masks))
                slots.append((di, u))
        groups = [chains[g::RW_GROUPS] for g in range(RW_GROUPS)]
        gres = [[] for _ in groups]
        gens = [_chunk_terms(grp, out, bd_mask) for grp, out in zip(groups, gres)]
        alive = [True] * RW_GROUPS
        pending = list(range(U)) if state_of is not None else []
        t = 0
        while any(alive):
            for g in range(RW_GROUPS):
                if alive[g] and t >= g * RW_SKEW:
                    try:
                        next(gens[g])
                    except StopIteration:
                        alive[g] = False
            if pending and t % 2 == 1:
                state_step(state_of, pending.pop(0))
            t += 1
        for u in pending:
            state_step(state_of, u)
        res = [None] * len(chains)
        for g in range(RW_GROUPS):
            res[g::RW_GROUPS] = gres[g]
        slot = st % 2
        for (di, u), (Rpp, Yv, P, Q) in zip(slots, res):
            rpp_ref[slot, di, u] = Rpp.astype(BF16)
            yv_ref[slot, di, u] = Yv
            p_ref[slot, di, u] = P.astype(BF16)
            q_ref[slot, di, u] = Q

    chunk_set(0, None)

    def body(st, carry):
        chunk_set(st, st - 1)
        return carry

    lax.fori_loop(1, nset, body, 0)
    for u in range(U):
        state_step(nset - 1, u)

    RT = 256

    def fin(i, carry):
        rows = pl.ds(pl.multiple_of(i * RT, RT), RT)
        y = yf_ref[rows, :] + yb_ref[rows, :]
        mean = _dot_sel(y, havg_ref[...])
        d = y - mean
        var = _dot_sel(d * d, havg_ref[...])
        yn = d * lax.rsqrt(var + RWKV_GN_EPS) * gnw_ref[...] + gnb_ref[...]
        r = r_ref[0, rows, :]
        kf = k_ref[0, rows, :] * (1.0 + (asf_ref[0, rows, :] - 1.0) * ka_ref[...])
        bonus = _dot_sel(r * kf * rk_ref[...], havg_ref[...]) * float(N) * v_ref[0, rows, :]
        o_ref[0, rows, :] = ((yn + bonus) * g_ref[0, rows, :]).astype(o_ref.dtype)
        return carry

    lax.fori_loop(0, S // RT, fin, 0)


def rwkv_scan(r, k, v, kk, g, lw_f, l_f, as_f, lw_b, l_b, as_b, k_a, r_k, gn_w, gn_b, out_dtype=F32):
    B, S, _ = r.shape
    W = 2 * HEAD_DIM
    N = HEAD_DIM
    havg = (jnp.arange(W)[:, None] // HEAD_DIM == jnp.arange(W)[None, :] // HEAD_DIM).astype(F32) / HEAD_DIM
    seq = pl.BlockSpec((1, S, W), lambda b, h: (b, 0, h))
    par = pl.BlockSpec((1, W), lambda b, h: (0, h))
    row = lambda a: a.reshape(1, -1)
    U = RW_UNROLL
    return pl.pallas_call(
        _rwkv_scan_kernel,
        grid=(B, RW_DIM // W),
        in_specs=[seq] * 11 + [par] * 4 + [pl.BlockSpec((W, W), lambda b, h: (0, 0))],
        out_specs=seq,
        out_shape=jax.ShapeDtypeStruct((B, S, RW_DIM), out_dtype),
        scratch_shapes=[pltpu.VMEM((S, W), F32), pltpu.VMEM((S, W), F32), pltpu.VMEM((2, W, W), F32),
                        pltpu.VMEM((2, 2, U, RW_CHUNK, W), BF16), pltpu.VMEM((2, 2, U, RW_CHUNK, W), F32),
                        pltpu.VMEM((2, 2, U, W, W), BF16), pltpu.VMEM((2, 2, U, W, W), F32)],
        compiler_params=_cparams(("parallel", "parallel")),
        name="rwkv_scan",
    )(r, k, v, kk, g, lw_f, l_f, as_f, lw_b, l_b, as_b, row(k_a), row(r_k), row(gn_w), row(gn_b), havg)


def rwkv7_mixer(p, shift_mu, w0_f, w_up_f, w0_b, w_up_b, a0_f, a_up_f, a0_b, a_up_b, g_up, k_k, k_a, r_k, gn_w,
                gn_b, out_dtype=F32, tt=256):
    outs = rwkv_prep(p, shift_mu, w0_f, w_up_f, w0_b, w_up_b, a0_f, a_up_f, a0_b, a_up_b, g_up, k_k, tt=tt)
    return rwkv_scan(*outs, k_a, r_k, gn_w, gn_b, out_dtype=out_dtype)


def _head_norm_rope(x, gain, cos, sin_signed, havg):
    ms = _dot_sel(x * x, havg)
    xn = x * lax.rsqrt(ms + NORM_EPS) * gain
    W = x.shape[1]
    even = (_iota2(x.shape, 1) % 2) == 0
    partner = jnp.where(even, pltpu.roll(xn, W - 1, 1), pltpu.roll(xn, 1, 1))
    return xn * cos + partner * sin_signed


def _attn_kernel(q_ref, k_ref, v_ref, cosq_ref, sinq_ref, cosk_ref, sink_ref, qg_ref, kg_ref, hq_ref, hk_ref,
                 o_ref, ks_ref, vt_ref):
    i = pl.program_id(1)
    G = ATT_HEADS // ATT_KV_HEADS
    D = HEAD_DIM

    @pl.when(i == 0)
    def _():
        kr = _head_norm_rope(k_ref[0], kg_ref[...], cosk_ref[...], sink_ref[...], hk_ref[...])
        vt = jnp.transpose(v_ref[0])
        ones = jnp.ones((ATT_VT_ROWS - D, vt.shape[1]), BF16)
        for kv in range(ATT_KV_HEADS):
            ks_ref[kv] = kr[:, kv * D:(kv + 1) * D].astype(BF16)
            vt_ref[kv] = jnp.concatenate([vt[kv * D:(kv + 1) * D, :].astype(BF16), ones], axis=0)

    q = _head_norm_rope(q_ref[0], qg_ref[...], cosq_ref[...], sinq_ref[...], hq_ref[...]) * (D ** -0.5 * LOG2_E)
    qb = q.astype(BF16)
    outs = []
    for kv in range(ATT_KV_HEADS):
        hs = range(kv * G, (kv + 1) * G)
        st = [lax.dot_general(ks_ref[kv], qb[:, h * D:(h + 1) * D], (((1,), (1,)), ((), ())),
                              preferred_element_type=F32) for h in hs]
        m = [jnp.max(x, axis=0, keepdims=True) for x in st]
        p = [jnp.exp2(x - mm).astype(BF16) for x, mm in zip(st, m)]
        ot = [jnp.dot(vt_ref[kv], x, preferred_element_type=F32) for x in p]
        outs.extend(o[:D] / o[D:D + 1] for o in ot)
    pairs = [jnp.transpose(jnp.concatenate(outs[2 * j:2 * j + 2], axis=0)) for j in range(ATT_HEADS // 2)]
    o_ref[0] = jnp.concatenate(pairs, axis=1).astype(o_ref.dtype)


def _rope_tables(S):
    rows = S // GRID_W
    row = jnp.repeat(jnp.arange(rows), GRID_W).astype(F32)
    col = jnp.tile(jnp.arange(GRID_W), rows).astype(F32)
    half = HEAD_DIM // 2
    freq = ROPE_THETA ** (-jnp.arange(0, half, 2, dtype=F32) / half)
    ang = jnp.concatenate([row[:, None] * freq, col[:, None] * freq], axis=-1)
    cos = jnp.repeat(jnp.cos(ang), 2, axis=-1)
    sin = jnp.repeat(jnp.sin(ang), 2, axis=-1)
    sign = jnp.where(jnp.arange(HEAD_DIM) % 2 == 0, -1.0, 1.0).astype(F32)
    return cos, sin * sign


def gqa_attention(p, q_norm, k_norm, *, tq=512, out_dtype=F32):
    B, S, _ = p.shape
    cos, sin = _rope_tables(S)
    tile = lambda t, n: jnp.tile(t, (1, n))
    hq = (jnp.arange(ATT_DIM)[:, None] // HEAD_DIM == jnp.arange(ATT_DIM)[None, :] // HEAD_DIM).astype(F32) / HEAD_DIM
    hk = hq[:ATT_KV_DIM, :ATT_KV_DIM]
    qg = jnp.tile(q_norm, ATT_HEADS).reshape(1, ATT_DIM)
    kg = jnp.tile(k_norm, ATT_KV_HEADS).reshape(1, ATT_KV_DIM)
    nq = ATT_DIM // ATT_KV_DIM
    const = lambda a: pl.BlockSpec(a.shape, lambda b, i: (0, 0))
    return pl.pallas_call(
        _attn_kernel,
        grid=(B, S // tq),
        in_specs=[
            pl.BlockSpec((1, tq, ATT_DIM), lambda b, i: (b, i, 0)),
            pl.BlockSpec((1, S, ATT_KV_DIM), lambda b, i: (b, 0, nq)),
            pl.BlockSpec((1, S, ATT_KV_DIM), lambda b, i: (b, 0, nq + 1)),
            pl.BlockSpec((tq, ATT_DIM), lambda b, i: (i, 0)),
            pl.BlockSpec((tq, ATT_DIM), lambda b, i: (i, 0)),
            pl.BlockSpec((S, ATT_KV_DIM), lambda b, i: (0, 0)),
            pl.BlockSpec((S, ATT_KV_DIM), lambda b, i: (0, 0)),
            const(qg), const(kg), const(hq), const(hk),
        ],
        out_specs=pl.BlockSpec((1, tq, ATT_DIM), lambda b, i: (b, i, 0)),
        out_shape=jax.ShapeDtypeStruct((B, S, ATT_DIM), out_dtype),
        scratch_shapes=[pltpu.VMEM((ATT_KV_HEADS, S, HEAD_DIM), BF16), pltpu.VMEM((ATT_KV_HEADS, ATT_VT_ROWS, S), BF16)],
        compiler_params=_cparams(("parallel", "arbitrary")),
        name="gqa_attention",
    )(p, p, p, tile(cos, ATT_HEADS), tile(sin, ATT_HEADS), tile(cos, ATT_KV_HEADS), tile(sin, ATT_KV_HEADS),
      qg, kg, hq, hk)


GLA_UNROLL = 8


def _gla_kernel(q_ref, k_ref, v_ref, gd_ref, og_ref, upf_ref, bf_ref, upb_ref, bb_ref, on_ref, o_ref,
                yf_ref, yb_ref, st_ref, kv_ref, qe_ref, dec_ref):
    S = q_ref.shape[1]
    C = GLA_CHUNK
    n = S // C
    U = GLA_UNROLL
    nset = n // U
    ii = _iota2((C, C), 0)
    jj = _iota2((C, C), 1)
    dirs = (
        (False, (jj <= ii).astype(F32), jj <= ii, upf_ref, bf_ref, yf_ref),
        (True, (jj >= ii).astype(F32), jj > ii, upb_ref, bb_ref, yb_ref),
    )
    scale = GLA_DKH ** -0.5
    st_ref[...] = jnp.zeros_like(st_ref)

    def chunk_rows(rev, st, u):
        c = st * U + u
        c = (n - 1 - c) if rev else c
        return pl.ds(pl.multiple_of(c * C, C), C)

    def state_step(st, u):
        slot = st % 2
        for di, (rev, tri, mask, up_ref, b_ref, y_ref) in enumerate(dirs):
            state = st_ref[di]
            rows = chunk_rows(rev, st, u)
            y_ref[rows, :] += lax.dot_general(qe_ref[slot, di, u], state.astype(BF16), (((1,), (1,)), ((), ())),
                                              preferred_element_type=F32)
            st_ref[di] = state * dec_ref[slot, di, u] + kv_ref[slot, di, u]

    def chunk_set(st, state_of):
        pending = list(range(U)) if state_of is not None else []
        items = []
        for u in range(U):
            for di, (rev, tri, mask, up_ref, b_ref, y_ref) in enumerate(dirs):
                rows = chunk_rows(rev, st, u)
                z = _dot3(gd_ref[0, rows, :], up_ref[...]) + b_ref[...]
                g = (jnp.minimum(z, 0.0) - jnp.log(1.0 + jnp.exp(-jnp.abs(z)))) * (1.0 / GLA_GATE_NORM)
                items.append(dict(u=u, di=di, rows=rows, rev=rev, mask=mask, y_ref=y_ref, g=g, tri=tri))
        for x in items:
            x["b"] = _sel_dot(x["tri"], x["g"])
        if pending:
            state_step(state_of, pending.pop(0))
        for x in items:
            b = x["b"]
            q = q_ref[0, x["rows"], :] * scale
            k = k_ref[0, x["rows"], :]
            b_mid = b[C // 2:C // 2 + 1, :]
            b_last = b[0:1, :] if x["rev"] else b[C - 1:C, :]
            x["vb"] = v_ref[0, x["rows"], :].astype(BF16)
            x["qm"] = (q * jnp.exp(b - b_mid)).astype(BF16)
            x["km"] = (k * jnp.exp(b_mid - b)).astype(BF16)
            x["ke"] = (k * jnp.exp(b_last - b)).astype(BF16)
            x["qe"] = (q * jnp.exp(b)).astype(BF16)
            x["dec"] = jnp.exp(b_last)
        att = [lax.dot_general(x["qm"], x["km"], (((1,), (1,)), ((), ())), preferred_element_type=F32) for x in items]
        if pending:
            state_step(state_of, pending.pop(0))
        att = [jnp.where(x["mask"], a, 0.0).astype(BF16) for x, a in zip(items, att)]
        slot = st % 2
        for idx, (x, a) in enumerate(zip(items, att)):
            x["y_ref"][x["rows"], :] = jnp.dot(a, x["vb"], preferred_element_type=F32)
            kv_ref[slot, x["di"], x["u"]] = lax.dot_general(x["vb"], x["ke"], (((0,), (0,)), ((), ())),
                                                            preferred_element_type=F32)
            qe_ref[slot, x["di"], x["u"]] = x["qe"]
            dec_ref[slot, x["di"], x["u"]] = x["dec"]
            if pending and idx % 3 == 2:
                state_step(state_of, pending.pop(0))
        for u in pending:
            state_step(state_of, u)

    chunk_set(0, None)

    def body(st, carry):
        chunk_set(st, st - 1)
        return carry

    lax.fori_loop(1, nset, body, 0)
    for u in range(U):
        state_step(nset - 1, u)

    RT = 256

    def fin(i, carry):
        rows = pl.ds(pl.multiple_of(i * RT, RT), RT)
        o = yf_ref[rows, :] + yb_ref[rows, :]
        ms = jnp.mean(o * o, axis=-1, keepdims=True)
        on = o * lax.rsqrt(ms + NORM_EPS) * on_ref[...]
        og = og_ref[0, rows, :]
        o_ref[0, rows, :] = (on * (og * jax.nn.sigmoid(og))).astype(o_ref.dtype)
        return carry

    lax.fori_loop(0, S // RT, fin, 0)


def gla_mixer(pm, gd, gate_up_f, gate_bias_f, gate_up_b, gate_bias_b, out_norm, out_dtype=F32):
    B, S, _ = pm.shape
    U = GLA_UNROLL
    H = GLA_HEADS
    kb = GLA_DK // GLA_DKH
    vb0 = 2 * GLA_DK // GLA_DVH
    ob0 = vb0 + GLA_DV // GLA_DVH
    return pl.pallas_call(
        _gla_kernel,
        grid=(B, H),
        in_specs=[
            pl.BlockSpec((1, S, GLA_DKH), lambda b, h: (b, 0, h)),
            pl.BlockSpec((1, S, GLA_DKH), lambda b, h: (b, 0, kb + h)),
            pl.BlockSpec((1, S, GLA_DVH), lambda b, h: (b, 0, vb0 + h)),
            pl.BlockSpec((1, S, GLA_GATE_RANK), lambda b, h: (b, 0, 0)),
            pl.BlockSpec((1, S, GLA_DVH), lambda b, h: (b, 0, ob0 + h)),
            pl.BlockSpec((GLA_GATE_RANK, GLA_DKH), lambda b, h: (0, h)),
            pl.BlockSpec((1, GLA_DKH), lambda b, h: (0, h)),
            pl.BlockSpec((GLA_GATE_RANK, GLA_DKH), lambda b, h: (0, h)),
            pl.BlockSpec((1, GLA_DKH), lambda b, h: (0, h)),
            pl.BlockSpec((1, GLA_DVH), lambda b, h: (0, 0)),
        ],
        out_specs=pl.BlockSpec((1, S, GLA_DVH), lambda b, h: (b, 0, h)),
        out_shape=jax.ShapeDtypeStruct((B, S, GLA_DV), out_dtype),
        scratch_shapes=[pltpu.VMEM((S, GLA_DVH), F32), pltpu.VMEM((S, GLA_DVH), F32),
                        pltpu.VMEM((2, GLA_DVH, GLA_DKH), F32),
                        pltpu.VMEM((2, 2, U, GLA_DVH, GLA_DKH), F32), pltpu.VMEM((2, 2, U, GLA_CHUNK, GLA_DKH), BF16),
                        pltpu.VMEM((2, 2, U, 1, GLA_DKH), F32)],
        compiler_params=_cparams(("parallel", "parallel")),
        name="gla_mixer",
    )(pm, pm, pm, gd, pm, gate_up_f, gate_bias_f.reshape(1, -1), gate_up_b, gate_bias_b.reshape(1, -1),
      out_norm.reshape(1, -1))


def _router_kernel(x_ref, g_ref, wr_ref, hn_ref, idx_ref, gate_ref):
    x = x_ref[...]
    ms = jnp.mean(x * x, axis=-1, keepdims=True)
    hn = x * lax.rsqrt(ms + NORM_EPS) * g_ref[...]
    hn_ref[...] = hn
    logits = _dot3(hn, wr_ref[...])
    lane = _iota2(logits.shape, 1)
    neg = jnp.float32(-jnp.inf)
    logits = jnp.where(lane < N_EXPERTS, logits, neg)
    m1 = jnp.max(logits, axis=-1, keepdims=True)
    i1 = jnp.min(jnp.where(logits == m1, lane, LANES), axis=-1, keepdims=True)
    rest = jnp.where(lane == i1, neg, logits)
    m2 = jnp.max(rest, axis=-1, keepdims=True)
    i2 = jnp.min(jnp.where(rest == m2, lane, LANES), axis=-1, keepdims=True)
    e2 = jnp.exp(m2 - m1)
    g1 = 1.0 / (1.0 + e2)
    g2 = e2 / (1.0 + e2)
    idx_ref[...] = jnp.where(lane == 0, i1, jnp.where(lane == 1, i2, 0))
    gate_ref[...] = jnp.where(lane == 0, g1, jnp.where(lane == 1, g2, 0.0))


def moe_router(x, g, router, *, tm=512):
    T, D = x.shape
    wr = jnp.zeros((D, LANES), F32).at[:, :N_EXPERTS].set(router)
    return pl.pallas_call(
        _router_kernel,
        grid=(T // tm,),
        in_specs=[
            pl.BlockSpec((tm, D), lambda i: (i, 0)),
            pl.BlockSpec((1, D), lambda i: (0, 0)),
            pl.BlockSpec((D, LANES), lambda i: (0, 0)),
        ],
        out_specs=[
            pl.BlockSpec((tm, D), lambda i: (i, 0)),
            pl.BlockSpec((tm, LANES), lambda i: (i, 0)),
            pl.BlockSpec((tm, LANES), lambda i: (i, 0)),
        ],
        out_shape=[jax.ShapeDtypeStruct((T, D), F32), jax.ShapeDtypeStruct((T, LANES), jnp.int32),
                   jax.ShapeDtypeStruct((T, LANES), F32)],
        compiler_params=_cparams(("parallel",)),
        name="moe_router",
    )(x, g.reshape(1, D), wr)


def _expert_kernel(te_ref, tv_ref, tok_ref, hn_hbm, wg_ref, wu_ref, wd_ref, o_ref, xg_ref, xb_ref, acc_ref, sem, *, nf):
    i = pl.program_id(0)
    f = pl.program_id(1)
    tm = xb_ref.shape[0]
    valid = tv_ref[i] > 0
    prev_valid = tv_ref[jnp.maximum(i - 1, 0)] > 0
    slot = i % 2

    def issue_row(tile, dst_slot, r):
        tok = tok_ref[tile * tm + r]
        pltpu.make_async_copy(hn_hbm.at[pl.ds(tok, 1), :], xg_ref.at[dst_slot, pl.ds(r, 1), :], sem.at[dst_slot]).start()

    @pl.when(jnp.logical_and(valid, jnp.logical_and(i == 0, f == 0)))
    def _():
        def body(r, c):
            issue_row(0, 0, r)
            return c

        lax.fori_loop(0, tm, body, 0, unroll=8)

    @pl.when(jnp.logical_and(f == 0, jnp.logical_or(valid, jnp.logical_and(i > 0, prev_valid))))
    def _():
        pltpu.make_async_copy(hn_hbm.at[pl.ds(0, tm), :], xg_ref.at[slot], sem.at[slot]).wait()

    @pl.when(jnp.logical_and(valid, f == 0))
    def _():
        xb_ref[...] = xg_ref[slot].astype(BF16)

    @pl.when(valid)
    def _():
        per = tm // nf
        row0 = f * per
        xb = xb_ref[...]
        a = jnp.dot(xb, wg_ref[0], preferred_element_type=F32)
        for j in range(per // 2):
            issue_row(i + 1, 1 - slot, row0 + j)
        b = jnp.dot(xb, wu_ref[0], preferred_element_type=F32)
        for j in range(per // 2, per):
            issue_row(i + 1, 1 - slot, row0 + j)
        h = (a * jax.nn.sigmoid(a) * b).astype(BF16)
        part = jnp.dot(h, wd_ref[0], preferred_element_type=F32)

        @pl.when(f == 0)
        def _():
            acc_ref[...] = part

        @pl.when(f > 0)
        def _():
            acc_ref[...] += part

    @pl.when(f == nf - 1)
    def _():
        o_ref[...] = jnp.where(valid, acc_ref[...], 0.0)


def moe_experts(hn, tile_expert, tile_valid, row_tok, wg, wu, wd, *, tm, tf):
    T, D = hn.shape
    F = wg.shape[2]
    P = row_tok.shape[0] - tm
    nt = P // tm + 1
    nf = F // tf

    def w_in_map(i, f, te, tv, tok):
        return (te[i], 0, jnp.where(tv[i] > 0, f, nf - 1))

    def w_out_map(i, f, te, tv, tok):
        return (te[i], jnp.where(tv[i] > 0, f, nf - 1), 0)

    grid_spec = pltpu.PrefetchScalarGridSpec(
        num_scalar_prefetch=3,
        grid=(nt, nf),
        in_specs=[
            pl.BlockSpec(memory_space=pl.ANY),
            pl.BlockSpec((1, D, tf), w_in_map),
            pl.BlockSpec((1, D, tf), w_in_map),
            pl.BlockSpec((1, tf, D), w_out_map),
        ],
        out_specs=pl.BlockSpec((tm, D), lambda i, f, te, tv, tok: (i, 0)),
        scratch_shapes=[pltpu.VMEM((2, tm, D), F32), pltpu.VMEM((tm, D), BF16), pltpu.VMEM((tm, D), F32),
                        pltpu.SemaphoreType.DMA((2,))],
    )
    return pl.pallas_call(
        functools.partial(_expert_kernel, nf=nf),
        grid_spec=grid_spec,
        out_shape=jax.ShapeDtypeStruct((P + tm, D), F32),
        compiler_params=_cparams(("arbitrary", "arbitrary")),
        name="moe_experts",
    )(tile_expert, tile_valid, row_tok, hn, wg, wu, wd)


def _combine_kernel(dest_ref, x_ref, gate_ref, yb_hbm, o_ref, buf_ref, sem):
    i = pl.program_id(0)
    n = pl.num_programs(0)
    tc = x_ref.shape[0]
    slot = i % 2

    def issue(step, dst_slot):
        base = step * tc * TOP_K

        for r in range(tc):
            for k in range(TOP_K):
                pltpu.make_async_copy(yb_hbm.at[pl.ds(dest_ref[base + r * TOP_K + k], 1), :],
                                      buf_ref.at[dst_slot, k, pl.ds(r, 1), :], sem.at[dst_slot]).start()

    @pl.when(i == 0)
    def _():
        issue(0, 0)

    for k in range(TOP_K):
        pltpu.make_async_copy(yb_hbm.at[pl.ds(0, tc), :], buf_ref.at[slot, k], sem.at[slot]).wait()

    @pl.when(i + 1 < n)
    def _():
        issue(i + 1, 1 - slot)

    g = gate_ref[...]
    o_ref[...] = x_ref[...] + g[:, 0:1] * buf_ref[slot, 0] + g[:, 1:2] * buf_ref[slot, 1]


def moe_combine(x, gates, yb, dest, *, tc=256):
    T, D = x.shape
    grid_spec = pltpu.PrefetchScalarGridSpec(
        num_scalar_prefetch=1,
        grid=(T // tc,),
        in_specs=[
            pl.BlockSpec((tc, D), lambda i, d: (i, 0)),
            pl.BlockSpec((tc, LANES), lambda i, d: (i, 0)),
            pl.BlockSpec(memory_space=pl.ANY),
        ],
        out_specs=pl.BlockSpec((tc, D), lambda i, d: (i, 0)),
        scratch_shapes=[pltpu.VMEM((2, TOP_K, tc, D), F32), pltpu.SemaphoreType.DMA((2,))],
    )
    return pl.pallas_call(
        _combine_kernel,
        grid_spec=grid_spec,
        out_shape=jax.ShapeDtypeStruct((T, D), F32),
        compiler_params=_cparams(("arbitrary",)),
        name="moe_combine",
    )(dest, x, gates, yb)


def moe_dispatch_plan(idx, *, tm):
    T = idx.shape[0]
    A = T * TOP_K
    e_flat = idx[:, :TOP_K].reshape(A)
    onehot = (e_flat[:, None] == jnp.arange(N_EXPERTS, dtype=jnp.int32)[None, :]).astype(jnp.int32)
    rank = jnp.sum((jnp.cumsum(onehot, axis=0) - onehot) * onehot, axis=1)
    counts = jnp.sum(onehot, axis=0)
    padded = (counts + tm - 1) // tm * tm
    ends = jnp.cumsum(padded)
    pstart = ends - padded
    dest = pstart[e_flat] + rank
    P = (A // tm + N_EXPERTS + 1) * tm
    nt = P // tm
    tok_flat = jnp.arange(A, dtype=jnp.int32) // TOP_K
    row_tok = jnp.zeros((P,), jnp.int32).at[dest].set(tok_flat)
    tile_start = jnp.arange(nt, dtype=jnp.int32) * tm
    tile_expert = jnp.minimum(jnp.sum((tile_start[:, None] >= ends[None, :]).astype(jnp.int32), axis=1), N_EXPERTS - 1)
    tile_valid = (tile_start < ends[-1]).astype(jnp.int32)
    last_valid = jnp.maximum(jnp.sum(tile_valid) - 1, 0)
    tile_expert = jnp.where(tile_valid > 0, tile_expert, tile_expert[last_valid])
    return dest.astype(jnp.int32), row_tok, tile_expert, tile_valid


def moe_layer(x, norm2, router, wg, wu, wd, *, tm=512, tf=1792, tr=512, tc=256):
    hn, idx, gates = moe_router(x, norm2, router, tm=tr)
    dest, row_tok, tile_expert, tile_valid = moe_dispatch_plan(idx, tm=tm)
    yb = moe_experts(hn, tile_expert, tile_valid, row_tok, wg, wu, wd, tm=tm, tf=tf)
    return moe_combine(x, gates, yb, dest, tc=tc)


def _even_layer(x, norm1, w_in, shift_mu, w0_f, w_up_f, w0_b, w_up_b, a0_f, a_up_f, a0_b, a_up_b, g_up,
                k_k, k_a, r_k, gn_w, gn_b, q_norm, k_norm, w_out, norm2, ffn_gate, ffn_up, ffn_down, cast):
    Bn, S, D = x.shape
    T = Bn * S
    xf = x.reshape(T, D)
    w_in = w_in.astype(BF16)
    p_rw = norm_matmul(xf, norm1, w_in[:, :RW_IN], tm=1024, tn=RW_IN // 2).reshape(Bn, S, RW_IN)
    p_att = norm_matmul(xf, norm1, w_in[:, RW_IN:], tm=1024, tn=ATT_IN).reshape(Bn, S, ATT_IN)
    y_a = rwkv7_mixer(p_rw, shift_mu, w0_f, w_up_f, w0_b, w_up_b, a0_f, a_up_f, a0_b, a_up_b,
                      g_up, k_k, k_a, r_k, gn_w, gn_b, out_dtype=BF16)
    y_b = gqa_attention(p_att, q_norm, k_norm, out_dtype=BF16)
    w_out = w_out.astype(BF16)
    xf = matmul_residual([y_a.reshape(T, RW_DIM), y_b.reshape(T, ATT_DIM)], [w_out[:RW_DIM], w_out[RW_DIM:]], xf,
                         tm=1024, tn=1024)
    return ffn_swiglu(xf, norm2, ffn_gate.astype(BF16), ffn_up.astype(BF16), ffn_down.astype(BF16), tm=512, tf=1408,
                      cast=cast)


def _odd_layer(xf, Bn, S, norm1, w_in, gate_up_f, gate_bias_f, gate_up_b, gate_bias_b, out_norm, w_out,
               norm2, router, exp_gate, exp_up, exp_down):
    T, D = xf.shape
    o3 = 2 * GLA_DK + GLA_DV
    o4 = o3 + GLA_GATE_RANK
    w_main = jnp.concatenate([w_in[:, :o3], w_in[:, o4:]], axis=1).astype(BF16)
    w_gd = jnp.zeros((D, LANES), F32).at[:, :GLA_GATE_RANK].set(w_in[:, o3:o4])
    p_main = norm_matmul(xf, norm1, w_main, tm=1024, tn=1024).reshape(Bn, S, -1)
    gd = norm_matmul(xf, norm1, w_gd, tm=1024, tn=LANES)[:, :GLA_GATE_RANK].reshape(Bn, S, GLA_GATE_RANK)
    o = gla_mixer(p_main, gd, gate_up_f, gate_bias_f, gate_up_b, gate_bias_b, out_norm, out_dtype=BF16)
    xf = matmul_residual([o.reshape(T, GLA_DV)], [w_out.astype(BF16)], xf, tm=1024, tn=1024)
    return moe_layer(xf, norm2, router, exp_gate, exp_up, exp_down)


def kernel(x, e_norm1, e_w_in, e_shift_mu, e_w0_f, e_w_up_f, e_w0_b, e_w_up_b, e_a0_f, e_a_up_f, e_a0_b, e_a_up_b, e_g_up, e_k_k, e_k_a, e_r_k, e_gn_w, e_gn_b, e_q_norm, e_k_norm, e_w_out, e_norm2, e_ffn_gate, e_ffn_up, e_ffn_down, o_norm1, o_w_in, o_gate_up_f, o_gate_bias_f, o_gate_up_b, o_gate_bias_b, o_out_norm, o_w_out, o_norm2, o_router, o_exp_gate, o_exp_up, o_exp_down):
    Bn, S, D = x.shape
    xf, (exp_gate, exp_up, exp_down) = _even_layer(
        x, e_norm1[0], e_w_in[0], e_shift_mu[0], e_w0_f[0], e_w_up_f[0], e_w0_b[0],
        e_w_up_b[0], e_a0_f[0], e_a_up_f[0], e_a0_b[0], e_a_up_b[0], e_g_up[0], e_k_k[0],
        e_k_a[0], e_r_k[0], e_gn_w[0], e_gn_b[0], e_q_norm[0], e_k_norm[0], e_w_out[0],
        e_norm2[0], e_ffn_gate[0], e_ffn_up[0], e_ffn_down[0], (o_exp_gate[0], o_exp_up[0], o_exp_down[0]))
    xf = _odd_layer(xf, Bn, S, o_norm1[0], o_w_in[0], o_gate_up_f[0], o_gate_bias_f[0], o_gate_up_b[0],
                    o_gate_bias_b[0], o_out_norm[0], o_w_out[0], o_norm2[0], o_router[0],
                    exp_gate, exp_up, exp_down)
    return xf.reshape(Bn, S, D)
```

```python
import functools

import jax
import jax.numpy as jnp
from jax import lax
from jax.experimental import pallas as pl
from jax.experimental.pallas import tpu as pltpu

F32 = jnp.float32
BF16 = jnp.bfloat16
HI = lax.Precision.HIGHEST

D_MODEL = 1024
GRID_W = 64
HEAD_DIM = 64
NORM_EPS = 1e-6
RW_HEADS = 8
RW_DIM = 512
DECAY_RANK = 64
ICLR_RANK = 64
GATE_RANK = 128
RWKV_GN_EPS = 64e-5
RW_IN = 3 * RW_DIM + DECAY_RANK + ICLR_RANK + GATE_RANK
RW_CHUNK = 64
ATT_HEADS = 8
ATT_KV_HEADS = 2
ATT_DIM = 512
ATT_KV_DIM = 128
ATT_IN = ATT_DIM + 2 * ATT_KV_DIM
ROPE_THETA = 10000.0
ATT_VT_ROWS = HEAD_DIM + 16
LOG2_E = 1.4426950408889634
GLA_HEADS = 4
GLA_DK = 512
GLA_DV = 1024
GLA_DKH = 128
GLA_DVH = 256
GLA_GATE_RANK = 16
GLA_GATE_NORM = 16.0
GLA_CHUNK = 64
N_EXPERTS = 8
TOP_K = 2
LANES = 128

VMEM_LIMIT = 48 * 1024 * 1024
VMEM_LIMIT_BIG = 56 * 1024 * 1024


def _cparams(sem, vmem=VMEM_LIMIT):
    return pltpu.CompilerParams(dimension_semantics=sem, vmem_limit_bytes=vmem)


def _iota2(shape, dim):
    return lax.broadcasted_iota(jnp.int32, shape, dim)


def _split(x, parts):
    out = []
    for _ in range(parts):
        t = x.astype(BF16)
        out.append(t)
        x = x - t.astype(F32)
    return out


def _dot_sel(x, m, parts=2):
    mb = m.astype(BF16)
    acc = None
    for t in _split(x, parts):
        d = jnp.dot(t, mb, preferred_element_type=F32)
        acc = d if acc is None else acc + d
    return acc


def _sel_dot(m, x, parts=3):
    mb = m.astype(BF16)
    acc = None
    for t in _split(x, parts):
        d = jnp.dot(mb, t, preferred_element_type=F32)
        acc = d if acc is None else acc + d
    return acc


def _dot3(a, b):
    ah, al = _split(a, 2)
    bh, bl = _split(b, 2)
    d = lambda u, v: jnp.dot(u, v, preferred_element_type=F32)
    return d(ah, bh) + (d(ah, bl) + d(al, bh))


def _bdot(a, b):
    return jnp.dot(a.astype(BF16), b.astype(BF16), preferred_element_type=F32)


def _bdot_nt(a, b):
    return lax.dot_general(a.astype(BF16), b.astype(BF16), (((1,), (1,)), ((), ())), preferred_element_type=F32)


def _bdot_tn(a, b):
    return lax.dot_general(a.astype(BF16), b.astype(BF16), (((0,), (0,)), ((), ())), preferred_element_type=F32)


def _norm_mm_kernel(x_ref, g_ref, w_ref, o_ref, xn_ref):
    @pl.when(pl.program_id(1) == 0)
    def _():
        x = x_ref[...]
        ms = jnp.mean(x * x, axis=-1, keepdims=True)
        xn_ref[...] = (x * lax.rsqrt(ms + NORM_EPS) * g_ref[...]).astype(xn_ref.dtype)

    if xn_ref.dtype == F32:
        o_ref[...] = _dot3(xn_ref[...], w_ref[...]).astype(o_ref.dtype)
    else:
        o_ref[...] = jnp.dot(xn_ref[...], w_ref[...], preferred_element_type=F32).astype(o_ref.dtype)


def norm_matmul(x, g, w, *, tm, tn, out_dtype=F32):
    M, K = x.shape
    N = w.shape[1]
    return pl.pallas_call(
        _norm_mm_kernel,
        grid=(M // tm, N // tn),
        in_specs=[
            pl.BlockSpec((tm, K), lambda i, j: (i, 0)),
            pl.BlockSpec((1, K), lambda i, j: (0, 0)),
            pl.BlockSpec((K, tn), lambda i, j: (0, j)),
        ],
        out_specs=pl.BlockSpec((tm, tn), lambda i, j: (i, j)),
        out_shape=jax.ShapeDtypeStruct((M, N), out_dtype),
        scratch_shapes=[pltpu.VMEM((tm, K), w.dtype)],
        compiler_params=_cparams(("parallel", "arbitrary")),
        name="norm_matmul",
    )(x, g.reshape(1, K), w)


def _mm_res_kernel(*refs):
    n = (len(refs) - 2) // 2
    r_ref, o_ref = refs[2 * n], refs[2 * n + 1]
    acc = r_ref[...]
    for y_ref, w_ref in zip(refs[:n], refs[n:2 * n]):
        acc = acc + jnp.dot(y_ref[...].astype(BF16), w_ref[...], preferred_element_type=F32)
    o_ref[...] = acc


def matmul_residual(ys, ws, r, *, tm, tn):
    M, N = r.shape
    y_specs = [pl.BlockSpec((tm, y.shape[1]), lambda i, j: (i, 0)) for y in ys]
    w_specs = [pl.BlockSpec((w.shape[0], tn), lambda i, j: (0, j)) for w in ws]
    return pl.pallas_call(
        _mm_res_kernel,
        grid=(M // tm, N // tn),
        in_specs=y_specs + w_specs + [pl.BlockSpec((tm, tn), lambda i, j: (i, j))],
        out_specs=pl.BlockSpec((tm, tn), lambda i, j: (i, j)),
        out_shape=jax.ShapeDtypeStruct((M, N), F32),
        compiler_params=_cparams(("parallel", "arbitrary")),
        name="matmul_residual",
    )(*ys, *ws, r)


def _ffn_kernel(*refs, n_cast):
    x_ref, g_ref, wg_ref, wu_ref, wd_ref = refs[:5]
    cast_in = refs[5:5 + n_cast]
    o_ref = refs[5 + n_cast]
    cast_out = refs[6 + n_cast:6 + 2 * n_cast]
    xn_ref, acc_ref = refs[6 + 2 * n_cast:]
    f = pl.program_id(1)

    @pl.when(f == 0)
    def _():
        x = x_ref[...]
        ms = jnp.mean(x * x, axis=-1, keepdims=True)
        xn_ref[...] = (x * lax.rsqrt(ms + NORM_EPS) * g_ref[...]).astype(BF16)

    for src, dst in zip(cast_in, cast_out):
        dst[...] = src[...].astype(BF16)

    xn = xn_ref[...]
    a = jnp.dot(xn, wg_ref[...], preferred_element_type=F32)
    b = jnp.dot(xn, wu_ref[...], preferred_element_type=F32)
    h = (a * jax.nn.sigmoid(a) * b).astype(BF16)
    part = jnp.dot(h, wd_ref[...], preferred_element_type=F32)

    @pl.when(f == 0)
    def _():
        acc_ref[...] = part

    @pl.when(f > 0)
    def _():
        acc_ref[...] += part

    @pl.when(f == pl.num_programs(1) - 1)
    def _():
        o_ref[...] = x_ref[...] + acc_ref[...]


def ffn_swiglu(x, g, wg, wu, wd, *, tm, tf, cast=()):
    M, D = x.shape
    F = wg.shape[1]
    nf = F // tf
    steps = (M // tm) * nf
    cast_specs, cast_shapes = [], []
    for w in cast:
        E, R, C = w.shape
        per = steps // E
        cast_specs.append(pl.BlockSpec((1, R // per, C), lambda i, f, per=per: ((i * nf + f) // per, (i * nf + f) % per, 0)))
        cast_shapes.append(jax.ShapeDtypeStruct(w.shape, BF16))
    outs = pl.pallas_call(
        functools.partial(_ffn_kernel, n_cast=len(cast)),
        grid=(M // tm, nf),
        in_specs=[
            pl.BlockSpec((tm, D), lambda i, f: (i, 0)),
            pl.BlockSpec((1, D), lambda i, f: (0, 0)),
            pl.BlockSpec((D, tf), lambda i, f: (0, f)),
            pl.BlockSpec((D, tf), lambda i, f: (0, f)),
            pl.BlockSpec((tf, D), lambda i, f: (f, 0)),
        ] + cast_specs,
        out_specs=[pl.BlockSpec((tm, D), lambda i, f: (i, 0))] + cast_specs,
        out_shape=[jax.ShapeDtypeStruct((M, D), F32)] + cast_shapes,
        scratch_shapes=[pltpu.VMEM((tm, D), BF16), pltpu.VMEM((tm, D), F32)],
        compiler_params=_cparams(("arbitrary", "arbitrary"), vmem=VMEM_LIMIT_BIG if cast else VMEM_LIMIT),
        name="ffn_swiglu",
    )(x, g.reshape(1, D), wg, wu, wd, *cast)
    return outs[0], tuple(outs[1:])


def _rwkv_prep_kernel(cur_ref, prev_ref, next_ref, n1_ref, win_ref, mu_ref, w0f_ref, wupf_ref, w0b_ref, wupb_ref,
                      a0f_ref, aupf_ref, a0b_ref, aupb_ref, gup_ref, kk_ref, hsum_ref,
                      r_ref, k_ref, v_ref, kkn_ref, g_ref, lwf_ref, lf_ref, asf_ref, lwb_ref, lb_ref, asb_ref):
    i = pl.program_id(1)
    nt = pl.num_programs(1)

    def project(xv):
        ms = jnp.mean(xv * xv, axis=-1, keepdims=True)
        xn = (xv * lax.rsqrt(ms + NORM_EPS) * n1_ref[...]).astype(BF16)
        return jnp.dot(xn, win_ref[...], preferred_element_type=F32)

    Tt = cur_ref.shape[1]
    pall = project(jnp.concatenate([cur_ref[0], prev_ref[0], next_ref[0]], axis=0))
    x = pall[:Tt]
    halo = pall[Tt:]
    row = _iota2(x.shape, 0)
    prev_row = jnp.where(i > 0, halo[7:8, :], 0.0)
    next_row = jnp.where(i < nt - 1, halo[8:9, :], 0.0)
    prev = jnp.where(row == 0, prev_row, pltpu.roll(x, 1, 0))
    nxt = jnp.where(row == Tt - 1, next_row, pltpu.roll(x, Tt - 1, 0))
    pm = x + (0.5 * (prev + nxt) - x) * mu_ref[...]
    o1 = 3 * RW_DIM
    o2 = o1 + DECAY_RANK
    o3 = o2 + ICLR_RANK
    r_ref[0] = pm[:, :RW_DIM]
    k = pm[:, RW_DIM:2 * RW_DIM]
    k_ref[0] = k
    v_ref[0] = pm[:, 2 * RW_DIM:o1]
    wd = jnp.tanh(pm[:, o1:o2])
    ad = pm[:, o2:o3]
    gin = jax.nn.sigmoid(pm[:, o3:])
    g_ref[0] = jnp.dot(gin.astype(BF16), gup_ref[...].astype(BF16), preferred_element_type=F32)
    kk = k * kk_ref[...]
    ss = _dot_sel(kk * kk, hsum_ref[...])
    kkn_ref[0] = kk / jnp.maximum(jnp.sqrt(ss), 1e-12)

    ci = _iota2((RW_CHUNK, RW_CHUNK), 0)
    cj = _iota2((RW_CHUNK, RW_CHUNK), 1)
    tril = (cj <= ci).astype(F32)
    triu = (cj >= ci).astype(F32)

    def direction(w0_ref, wup_ref, a0_ref, aup_ref, tri, lw_ref, l_ref, as_ref):
        z = w0_ref[...] + _dot3(wd, wup_ref[...])
        sp = jnp.maximum(-z, 0.0) + jnp.log(1.0 + jnp.exp(-jnp.abs(z)))
        lw = -jnp.exp(-sp - 0.5)
        lw_ref[0] = lw
        as_ref[0] = jax.nn.sigmoid(a0_ref[...] + _dot3(ad, aup_ref[...]))
        for c in range(Tt // RW_CHUNK):
            sl = slice(c * RW_CHUNK, (c + 1) * RW_CHUNK)
            l_ref[0, sl, :] = _sel_dot(tri, lw[sl, :])

    direction(w0f_ref, wupf_ref, a0f_ref, aupf_ref, tril, lwf_ref, lf_ref, asf_ref)
    direction(w0b_ref, wupb_ref, a0b_ref, aupb_ref, triu, lwb_ref, lb_ref, asb_ref)


def rwkv_prep(x, norm1, w_in, shift_mu, w0_f, w_up_f, w0_b, w_up_b, a0_f, a_up_f, a0_b, a_up_b, g_up, k_k, *, tt=256):
    B, S, D = x.shape
    nt = S // tt
    hsum = (jnp.arange(RW_DIM)[:, None] // HEAD_DIM == jnp.arange(RW_DIM)[None, :] // HEAD_DIM).astype(F32)
    row = lambda a: a.reshape(1, -1)
    full = lambda a: pl.BlockSpec(a.shape, lambda b, i: (0,) * a.ndim)
    params = [row(norm1), w_in, row(shift_mu), row(w0_f), w_up_f, row(w0_b), w_up_b, row(a0_f), a_up_f, row(a0_b), a_up_b,
              g_up, row(k_k), hsum]
    tb = tt // 8
    in_specs = [
        pl.BlockSpec((1, tt, D), lambda b, i: (b, i, 0)),
        pl.BlockSpec((1, 8, D), lambda b, i: (b, jnp.maximum(i * tb - 1, 0), 0)),
        pl.BlockSpec((1, 8, D), lambda b, i: (b, jnp.minimum((i + 1) * tb, S // 8 - 1), 0)),
    ] + [full(a) for a in params]
    out_spec = pl.BlockSpec((1, tt, RW_DIM), lambda b, i: (b, i, 0))
    out_sds = jax.ShapeDtypeStruct((B, S, RW_DIM), F32)
    return pl.pallas_call(
        _rwkv_prep_kernel,
        grid=(B, nt),
        in_specs=in_specs,
        out_specs=[out_spec] * 11,
        out_shape=[out_sds] * 11,
        compiler_params=_cparams(("parallel", "arbitrary")),
        name="rwkv_prep",
    )(x, x, x, *params)


def _bd(y, bd_mask):
    return jnp.where(bd_mask, jnp.concatenate([y, y], axis=0), jnp.zeros((), y.dtype))


def _chunk_terms(chains, out, bd_mask):
    C = chains[0]["r"].shape[0]
    n = len(chains)
    R = range(n)
    dot = lambda u, v: jnp.dot(u, v, preferred_element_type=F32)
    bd = lambda y: _bd(y, bd_mask)
    pre = []
    for ch in chains:
        r, k, v, kk, lw, L, asig, ka, rev = (ch[x] for x in ("r", "k", "v", "kk", "lw", "L", "asig", "ka", "rev"))
        a = -kk
        b = kk * asig
        kd = k * (1.0 + (asig - 1.0) * ka)
        Lp = L - lw
        Lr = Lp if rev else L
        Lend = L[0:1, :] if rev else L[C - 1:C, :]
        Lmid = L[C // 2:C // 2 + 1, :]
        einv = jnp.exp(Lmid - L)
        eend = jnp.exp(Lend - L)
        pre.append(dict(
            ar=jnp.concatenate([a * jnp.exp(Lp - Lmid), r * jnp.exp(Lr - Lmid)], axis=0).astype(BF16),
            bt=bd((b * einv).astype(BF16)), kt=bd((kd * einv).astype(BF16)),
            a0=bd((a * jnp.exp(Lp)).astype(BF16)), r0=r * jnp.exp(Lr),
            bh=(b * eend).astype(BF16), kh=(kd * eend).astype(BF16), vb=v.astype(BF16),
            dec=jnp.exp(Lend)))
    mk = [ch["masks"] for ch in chains]
    Ab = [_bdot_nt(pre[i]["ar"], pre[i]["bt"]) for i in R]
    Ak = [_bdot_nt(pre[i]["ar"], pre[i]["kt"]) for i in R]
    yield
    Aab = [jnp.where(mk[i][0], Ab[i][:C], 0.0) for i in R]
    Arb = [jnp.where(mk[i][1], Ab[i][C:], 0.0).astype(BF16) for i in R]
    AakArk = [jnp.concatenate([jnp.where(mk[i][0], Ak[i][:C], 0.0), jnp.where(mk[i][1], Ak[i][C:], 0.0)],
                              axis=0).astype(BF16) for i in R]
    AV = [dot(AakArk[i], bd(pre[i]["vb"])) for i in R]
    Xb = [jnp.where(mk[i][2], Aab[i], 0.0).astype(BF16) for i in R]
    X2b = [dot(Xb[i], bd(Xb[i])).astype(BF16) for i in R]
    yield
    T = [mk[i][6] + Xb[i].astype(F32) for i in R]
    T = [T[i] + dot(T[i].astype(BF16), bd(X2b[i])) for i in R]
    X4b = [dot(X2b[i], bd(X2b[i])).astype(BF16) for i in R]
    yield
    T = [T[i] + dot(T[i].astype(BF16), bd(X4b[i])) for i in R]
    yield
    for lvl in (3, 4, 5):
        Tb = [T[i].astype(BF16) for i in R]
        ET = [dot(jnp.where(mk[i][lvl], Aab[i], 0.0).astype(BF16), bd(Tb[i])).astype(BF16) for i in R]
        yield
        T = [T[i] + dot(Tb[i], bd(ET[i])) for i in R]
        yield
    Tb = [T[i].astype(BF16) for i in R]
    A0p = [dot(Tb[i], pre[i]["a0"]).astype(BF16) for i in R]
    Uv = [dot(Tb[i], bd(AV[i][:C].astype(BF16))).astype(BF16) for i in R]
    yield
    Rpp = [pre[i]["r0"] + dot(Arb[i], bd(A0p[i])) for i in R]
    Yv = [dot(Arb[i], bd(Uv[i])) + AV[i][C:] for i in R]
    eye2 = (_iota2(bd_mask.shape, 0) == _iota2(bd_mask.shape, 1)).astype(F32)
    P = [jnp.where(bd_mask, _bdot_tn(pre[i]["bh"], A0p[i]), 0.0) + eye2 * pre[i]["dec"] for i in R]
    Q = [jnp.where(bd_mask, _bdot_tn(pre[i]["bh"], Uv[i]) + _bdot_tn(pre[i]["kh"], pre[i]["vb"]), 0.0) for i in R]
    out.extend((Rpp[i], Yv[i], P[i], Q[i]) for i in R)


def _make_masks(C, N, rev):
    ii = _iota2((C, 2 * N), 0)
    jj = _iota2((C, 2 * N), 1) % N
    strict = (jj > ii) if rev else (jj < ii)
    rmask = strict if rev else (jj <= ii)
    blk = lambda s: (ii // s) == (jj // s)
    m8 = blk(8)
    e16 = blk(16) & jnp.logical_not(blk(8))
    e32 = blk(32) & jnp.logical_not(blk(16))
    e64 = jnp.logical_not(blk(32))
    eye = (ii == jj).astype(F32)
    return strict, rmask, m8, e16, e32, e64, eye


RW_UNROLL = 8


def _rwkv_scan_kernel(r_ref, k_ref, v_ref, kk_ref, g_ref, lwf_ref, lf_ref, asf_ref, lwb_ref, lb_ref, asb_ref,
                      ka_ref, rk_ref, gnw_ref, gnb_ref, havg_ref, o_ref,
                      yf_ref, yb_ref, h_ref, rpp_ref, yv_ref, p_ref, q_ref):
    S = r_ref.shape[1]
    C = RW_CHUNK
    N = HEAD_DIM
    W = 2 * N
    n = S // C
    U = RW_UNROLL
    nset = n // U
    masks_f = _make_masks(C, N, False)
    masks_b = _make_masks(C, N, True)
    bd_mask = (_iota2((W, W), 0) // N) == (_iota2((W, W), 1) // N)
    h_ref[...] = jnp.zeros_like(h_ref)
    dirs = ((False, lwf_ref, lf_ref, asf_ref, yf_ref, masks_f), (True, lwb_ref, lb_ref, asb_ref, yb_ref, masks_b))

    def chunk_rows(rev, st, u):
        c = st * U + u
        c = (n - 1 - c) if rev else c
        return pl.ds(pl.multiple_of(c * C, C), C)

    def state_step(st, u):
        slot = st % 2
        for di, (rev, lw_ref, l_ref, as_ref, y_ref, masks) in enumerate(dirs):
            Hb = h_ref[di].astype(BF16)
            y_ref[chunk_rows(rev, st, u), :] = (jnp.dot(rpp_ref[slot, di, u], Hb, preferred_element_type=F32)
                                                + yv_ref[slot, di, u])
            h_ref[di] = jnp.dot(p_ref[slot, di, u], Hb, preferred_element_type=F32) + q_ref[slot, di, u]

    def chunk_set(st, state_of):
        chains, slots = [], []
        for u in range(U):
            for di, (rev, lw_ref, l_ref, as_ref, y_ref, masks) in enumerate(dirs):
                rows = chunk_rows(rev, st, u)
                chains.append(dict(
                    r=r_ref[0, rows, :], k=k_ref[0, rows, :], v=v_ref[0, rows, :], kk=kk_ref[0, rows, :],
                    lw=lw_ref[0, rows, :], L=l_ref[0, rows, :], asig=as_ref[0, rows, :], ka=ka_ref[...],
                    rev=rev, masks=masks))
                slots.append((di, u))
        res = []
        pending = list(range(U)) if state_of is not None else []
        for stage, _ in enumerate(_chunk_terms(chains, res, bd_mask)):
            if pending and stage % 2 == 1:
                state_step(state_of, pending.pop(0))
        for u in pending:
            state_step(state_of, u)
        slot = st % 2
        for (di, u), (Rpp, Yv, P, Q) in zip(slots, res):
            rpp_ref[slot, di, u] = Rpp.astype(BF16)
            yv_ref[slot, di, u] = Yv
            p_ref[slot, di, u] = P.astype(BF16)
            q_ref[slot, di, u] = Q

    chunk_set(0, None)

    def body(st, carry):
        chunk_set(st, st - 1)
        return carry

    lax.fori_loop(1, nset, body, 0)
    for u in range(U):
        state_step(nset - 1, u)

    RT = 256

    def fin(i, carry):
        rows = pl.ds(pl.multiple_of(i * RT, RT), RT)
        y = yf_ref[rows, :] + yb_ref[rows, :]
        mean = _dot_sel(y, havg_ref[...])
        d = y - mean
        var = _dot_sel(d * d, havg_ref[...])
        yn = d * lax.rsqrt(var + RWKV_GN_EPS) * gnw_ref[...] + gnb_ref[...]
        r = r_ref[0, rows, :]
        kf = k_ref[0, rows, :] * (1.0 + (asf_ref[0, rows, :] - 1.0) * ka_ref[...])
        bonus = _dot_sel(r * kf * rk_ref[...], havg_ref[...]) * float(N) * v_ref[0, rows, :]
        o_ref[0, rows, :] = ((yn + bonus) * g_ref[0, rows, :]).astype(o_ref.dtype)
        return carry

    lax.fori_loop(0, S // RT, fin, 0)


def rwkv_scan(r, k, v, kk, g, lw_f, l_f, as_f, lw_b, l_b, as_b, k_a, r_k, gn_w, gn_b, out_dtype=F32):
    B, S, _ = r.shape
    W = 2 * HEAD_DIM
    N = HEAD_DIM
    havg = (jnp.arange(W)[:, None] // HEAD_DIM == jnp.arange(W)[None, :] // HEAD_DIM).astype(F32) / HEAD_DIM
    seq = pl.BlockSpec((1, S, W), lambda b, h: (b, 0, h))
    par = pl.BlockSpec((1, W), lambda b, h: (0, h))
    row = lambda a: a.reshape(1, -1)
    U = RW_UNROLL
    return pl.pallas_call(
        _rwkv_scan_kernel,
        grid=(B, RW_DIM // W),
        in_specs=[seq] * 11 + [par] * 4 + [pl.BlockSpec((W, W), lambda b, h: (0, 0))],
        out_specs=seq,
        out_shape=jax.ShapeDtypeStruct((B, S, RW_DIM), out_dtype),
        scratch_shapes=[pltpu.VMEM((S, W), F32), pltpu.VMEM((S, W), F32), pltpu.VMEM((2, W, W), F32),
                        pltpu.VMEM((2, 2, U, RW_CHUNK, W), BF16), pltpu.VMEM((2, 2, U, RW_CHUNK, W), F32),
                        pltpu.VMEM((2, 2, U, W, W), BF16), pltpu.VMEM((2, 2, U, W, W), F32)],
        compiler_params=_cparams(("parallel", "parallel")),
        name="rwkv_scan",
    )(r, k, v, kk, g, lw_f, l_f, as_f, lw_b, l_b, as_b, row(k_a), row(r_k), row(gn_w), row(gn_b), havg)


def rwkv7_mixer(x, norm1, w_in, shift_mu, w0_f, w_up_f, w0_b, w_up_b, a0_f, a_up_f, a0_b, a_up_b, g_up, k_k, k_a,
                r_k, gn_w, gn_b, out_dtype=F32, tt=512):
    outs = rwkv_prep(x, norm1, w_in, shift_mu, w0_f, w_up_f, w0_b, w_up_b, a0_f, a_up_f, a0_b, a_up_b, g_up, k_k, tt=tt)
    return rwkv_scan(*outs, k_a, r_k, gn_w, gn_b, out_dtype=out_dtype)


def _head_norm_rope(x, gain, cos, sin_signed, havg):
    ms = _dot_sel(x * x, havg)
    xn = x * lax.rsqrt(ms + NORM_EPS) * gain
    W = x.shape[1]
    even = (_iota2(x.shape, 1) % 2) == 0
    partner = jnp.where(even, pltpu.roll(xn, W - 1, 1), pltpu.roll(xn, 1, 1))
    return xn * cos + partner * sin_signed


def _attn_kernel(q_ref, k_ref, v_ref, cosq_ref, sinq_ref, cosk_ref, sink_ref, qg_ref, kg_ref, hq_ref, hk_ref,
                 o_ref, ks_ref, vt_ref):
    i = pl.program_id(1)
    G = ATT_HEADS // ATT_KV_HEADS
    D = HEAD_DIM

    @pl.when(i == 0)
    def _():
        kr = _head_norm_rope(k_ref[0], kg_ref[...], cosk_ref[...], sink_ref[...], hk_ref[...])
        vt = jnp.transpose(v_ref[0])
        ones = jnp.ones((ATT_VT_ROWS - D, vt.shape[1]), BF16)
        for kv in range(ATT_KV_HEADS):
            ks_ref[kv] = kr[:, kv * D:(kv + 1) * D].astype(BF16)
            vt_ref[kv] = jnp.concatenate([vt[kv * D:(kv + 1) * D, :].astype(BF16), ones], axis=0)

    q = _head_norm_rope(q_ref[0], qg_ref[...], cosq_ref[...], sinq_ref[...], hq_ref[...]) * (D ** -0.5 * LOG2_E)
    qb = q.astype(BF16)
    outs = []
    for kv in range(ATT_KV_HEADS):
        hs = range(kv * G, (kv + 1) * G)
        st = [lax.dot_general(ks_ref[kv], qb[:, h * D:(h + 1) * D], (((1,), (1,)), ((), ())),
                              preferred_element_type=F32) for h in hs]
        m = [jnp.max(x, axis=0, keepdims=True) for x in st]
        p = [jnp.exp2(x - mm).astype(BF16) for x, mm in zip(st, m)]
        ot = [jnp.dot(vt_ref[kv], x, preferred_element_type=F32) for x in p]
        outs.extend(o[:D] / o[D:D + 1] for o in ot)
    pairs = [jnp.transpose(jnp.concatenate(outs[2 * j:2 * j + 2], axis=0)) for j in range(ATT_HEADS // 2)]
    o_ref[0] = jnp.concatenate(pairs, axis=1).astype(o_ref.dtype)


def _rope_tables(S):
    rows = S // GRID_W
    row = jnp.repeat(jnp.arange(rows), GRID_W).astype(F32)
    col = jnp.tile(jnp.arange(GRID_W), rows).astype(F32)
    half = HEAD_DIM // 2
    freq = ROPE_THETA ** (-jnp.arange(0, half, 2, dtype=F32) / half)
    ang = jnp.concatenate([row[:, None] * freq, col[:, None] * freq], axis=-1)
    cos = jnp.repeat(jnp.cos(ang), 2, axis=-1)
    sin = jnp.repeat(jnp.sin(ang), 2, axis=-1)
    sign = jnp.where(jnp.arange(HEAD_DIM) % 2 == 0, -1.0, 1.0).astype(F32)
    return cos, sin * sign


def gqa_attention(p, q_norm, k_norm, *, tq=512, out_dtype=F32):
    B, S, _ = p.shape
    cos, sin = _rope_tables(S)
    tile = lambda t, n: jnp.tile(t, (1, n))
    hq = (jnp.arange(ATT_DIM)[:, None] // HEAD_DIM == jnp.arange(ATT_DIM)[None, :] // HEAD_DIM).astype(F32) / HEAD_DIM
    hk = hq[:ATT_KV_DIM, :ATT_KV_DIM]
    qg = jnp.tile(q_norm, ATT_HEADS).reshape(1, ATT_DIM)
    kg = jnp.tile(k_norm, ATT_KV_HEADS).reshape(1, ATT_KV_DIM)
    nq = ATT_DIM // ATT_KV_DIM
    const = lambda a: pl.BlockSpec(a.shape, lambda b, i: (0, 0))
    return pl.pallas_call(
        _attn_kernel,
        grid=(B, S // tq),
        in_specs=[
            pl.BlockSpec((1, tq, ATT_DIM), lambda b, i: (b, i, 0)),
            pl.BlockSpec((1, S, ATT_KV_DIM), lambda b, i: (b, 0, nq)),
            pl.BlockSpec((1, S, ATT_KV_DIM), lambda b, i: (b, 0, nq + 1)),
            pl.BlockSpec((tq, ATT_DIM), lambda b, i: (i, 0)),
            pl.BlockSpec((tq, ATT_DIM), lambda b, i: (i, 0)),
            pl.BlockSpec((S, ATT_KV_DIM), lambda b, i: (0, 0)),
            pl.BlockSpec((S, ATT_KV_DIM), lambda b, i: (0, 0)),
            const(qg), const(kg), const(hq), const(hk),
        ],
        out_specs=pl.BlockSpec((1, tq, ATT_DIM), lambda b, i: (b, i, 0)),
        out_shape=jax.ShapeDtypeStruct((B, S, ATT_DIM), out_dtype),
        scratch_shapes=[pltpu.VMEM((ATT_KV_HEADS, S, HEAD_DIM), BF16), pltpu.VMEM((ATT_KV_HEADS, ATT_VT_ROWS, S), BF16)],
        compiler_params=_cparams(("parallel", "arbitrary")),
        name="gqa_attention",
    )(p, p, p, tile(cos, ATT_HEADS), tile(sin, ATT_HEADS), tile(cos, ATT_KV_HEADS), tile(sin, ATT_KV_HEADS),
      qg, kg, hq, hk)


GLA_UNROLL = 8


def _gla_kernel(q_ref, k_ref, v_ref, gd_ref, og_ref, upf_ref, bf_ref, upb_ref, bb_ref, on_ref, o_ref,
                yf_ref, yb_ref, st_ref, kv_ref, qe_ref, dec_ref):
    S = q_ref.shape[1]
    C = GLA_CHUNK
    n = S // C
    U = GLA_UNROLL
    nset = n // U
    ii = _iota2((C, C), 0)
    jj = _iota2((C, C), 1)
    dirs = (
        (False, (jj <= ii).astype(F32), jj <= ii, upf_ref, bf_ref, yf_ref),
        (True, (jj >= ii).astype(F32), jj > ii, upb_ref, bb_ref, yb_ref),
    )
    scale = GLA_DKH ** -0.5
    st_ref[...] = jnp.zeros_like(st_ref)

    def chunk_rows(rev, st, u):
        c = st * U + u
        c = (n - 1 - c) if rev else c
        return pl.ds(pl.multiple_of(c * C, C), C)

    def state_step(st, u):
        slot = st % 2
        for di, (rev, tri, mask, up_ref, b_ref, y_ref) in enumerate(dirs):
            state = st_ref[di]
            rows = chunk_rows(rev, st, u)
            y_ref[rows, :] += lax.dot_general(qe_ref[slot, di, u], state.astype(BF16), (((1,), (1,)), ((), ())),
                                              preferred_element_type=F32)
            st_ref[di] = state * dec_ref[slot, di, u] + kv_ref[slot, di, u]

    def chunk_set(st, state_of):
        pending = list(range(U)) if state_of is not None else []
        items = []
        for u in range(U):
            for di, (rev, tri, mask, up_ref, b_ref, y_ref) in enumerate(dirs):
                rows = chunk_rows(rev, st, u)
                z = _dot3(gd_ref[0, rows, :], up_ref[...]) + b_ref[...]
                g = (jnp.minimum(z, 0.0) - jnp.log(1.0 + jnp.exp(-jnp.abs(z)))) * (1.0 / GLA_GATE_NORM)
                items.append(dict(u=u, di=di, rows=rows, rev=rev, mask=mask, y_ref=y_ref, g=g, tri=tri))
        for x in items:
            x["b"] = _sel_dot(x["tri"], x["g"])
        if pending:
            state_step(state_of, pending.pop(0))
        for x in items:
            b = x["b"]
            q = q_ref[0, x["rows"], :] * scale
            k = k_ref[0, x["rows"], :]
            b_mid = b[C // 2:C // 2 + 1, :]
            b_last = b[0:1, :] if x["rev"] else b[C - 1:C, :]
            x["vb"] = v_ref[0, x["rows"], :].astype(BF16)
            x["qm"] = (q * jnp.exp(b - b_mid)).astype(BF16)
            x["km"] = (k * jnp.exp(b_mid - b)).astype(BF16)
            x["ke"] = (k * jnp.exp(b_last - b)).astype(BF16)
            x["qe"] = (q * jnp.exp(b)).astype(BF16)
            x["dec"] = jnp.exp(b_last)
        att = [lax.dot_general(x["qm"], x["km"], (((1,), (1,)), ((), ())), preferred_element_type=F32) for x in items]
        if pending:
            state_step(state_of, pending.pop(0))
        att = [jnp.where(x["mask"], a, 0.0).astype(BF16) for x, a in zip(items, att)]
        slot = st % 2
        for idx, (x, a) in enumerate(zip(items, att)):
            x["y_ref"][x["rows"], :] = jnp.dot(a, x["vb"], preferred_element_type=F32)
            kv_ref[slot, x["di"], x["u"]] = lax.dot_general(x["vb"], x["ke"], (((0,), (0,)), ((), ())),
                                                            preferred_element_type=F32)
            qe_ref[slot, x["di"], x["u"]] = x["qe"]
            dec_ref[slot, x["di"], x["u"]] = x["dec"]
            if pending and idx % 3 == 2:
                state_step(state_of, pending.pop(0))
        for u in pending:
            state_step(state_of, u)

    chunk_set(0, None)

    def body(st, carry):
        chunk_set(st, st - 1)
        return carry

    lax.fori_loop(1, nset, body, 0)
    for u in range(U):
        state_step(nset - 1, u)

    RT = 256

    def fin(i, carry):
        rows = pl.ds(pl.multiple_of(i * RT, RT), RT)
        o = yf_ref[rows, :] + yb_ref[rows, :]
        ms = jnp.mean(o * o, axis=-1, keepdims=True)
        on = o * lax.rsqrt(ms + NORM_EPS) * on_ref[...]
        og = og_ref[0, rows, :]
        o_ref[0, rows, :] = (on * (og * jax.nn.sigmoid(og))).astype(o_ref.dtype)
        return carry

    lax.fori_loop(0, S // RT, fin, 0)


def gla_mixer(pm, gd, gate_up_f, gate_bias_f, gate_up_b, gate_bias_b, out_norm, out_dtype=F32):
    B, S, _ = pm.shape
    U = GLA_UNROLL
    H = GLA_HEADS
    kb = GLA_DK // GLA_DKH
    vb0 = 2 * GLA_DK // GLA_DVH
    ob0 = vb0 + GLA_DV // GLA_DVH
    return pl.pallas_call(
        _gla_kernel,
        grid=(B, H),
        in_specs=[
            pl.BlockSpec((1, S, GLA_DKH), lambda b, h: (b, 0, h)),
            pl.BlockSpec((1, S, GLA_DKH), lambda b, h: (b, 0, kb + h)),
            pl.BlockSpec((1, S, GLA_DVH), lambda b, h: (b, 0, vb0 + h)),
            pl.BlockSpec((1, S, GLA_GATE_RANK), lambda b, h: (b, 0, 0)),
            pl.BlockSpec((1, S, GLA_DVH), lambda b, h: (b, 0, ob0 + h)),
            pl.BlockSpec((GLA_GATE_RANK, GLA_DKH), lambda b, h: (0, h)),
            pl.BlockSpec((1, GLA_DKH), lambda b, h: (0, h)),
            pl.BlockSpec((GLA_GATE_RANK, GLA_DKH), lambda b, h: (0, h)),
            pl.BlockSpec((1, GLA_DKH), lambda b, h: (0, h)),
            pl.BlockSpec((1, GLA_DVH), lambda b, h: (0, 0)),
        ],
        out_specs=pl.BlockSpec((1, S, GLA_DVH), lambda b, h: (b, 0, h)),
        out_shape=jax.ShapeDtypeStruct((B, S, GLA_DV), out_dtype),
        scratch_shapes=[pltpu.VMEM((S, GLA_DVH), F32), pltpu.VMEM((S, GLA_DVH), F32),
                        pltpu.VMEM((2, GLA_DVH, GLA_DKH), F32),
                        pltpu.VMEM((2, 2, U, GLA_DVH, GLA_DKH), F32), pltpu.VMEM((2, 2, U, GLA_CHUNK, GLA_DKH), BF16),
                        pltpu.VMEM((2, 2, U, 1, GLA_DKH), F32)],
        compiler_params=_cparams(("parallel", "parallel")),
        name="gla_mixer",
    )(pm, pm, pm, gd, pm, gate_up_f, gate_bias_f.reshape(1, -1), gate_up_b, gate_bias_b.reshape(1, -1),
      out_norm.reshape(1, -1))


def _router_kernel(x_ref, g_ref, wr_ref, hn_ref, idx_ref, gate_ref):
    x = x_ref[...]
    ms = jnp.mean(x * x, axis=-1, keepdims=True)
    hn = x * lax.rsqrt(ms + NORM_EPS) * g_ref[...]
    hn_ref[...] = hn
    logits = _dot3(hn, wr_ref[...])
    lane = _iota2(logits.shape, 1)
    neg = jnp.float32(-jnp.inf)
    logits = jnp.where(lane < N_EXPERTS, logits, neg)
    m1 = jnp.max(logits, axis=-1, keepdims=True)
    i1 = jnp.min(jnp.where(logits == m1, lane, LANES), axis=-1, keepdims=True)
    rest = jnp.where(lane == i1, neg, logits)
    m2 = jnp.max(rest, axis=-1, keepdims=True)
    i2 = jnp.min(jnp.where(rest == m2, lane, LANES), axis=-1, keepdims=True)
    e2 = jnp.exp(m2 - m1)
    g1 = 1.0 / (1.0 + e2)
    g2 = e2 / (1.0 + e2)
    idx_ref[...] = jnp.where(lane == 0, i1, jnp.where(lane == 1, i2, 0))
    gate_ref[...] = jnp.where(lane == 0, g1, jnp.where(lane == 1, g2, 0.0))


def moe_router(x, g, router, *, tm=512):
    T, D = x.shape
    wr = jnp.zeros((D, LANES), F32).at[:, :N_EXPERTS].set(router)
    return pl.pallas_call(
        _router_kernel,
        grid=(T // tm,),
        in_specs=[
            pl.BlockSpec((tm, D), lambda i: (i, 0)),
            pl.BlockSpec((1, D), lambda i: (0, 0)),
            pl.BlockSpec((D, LANES), lambda i: (0, 0)),
        ],
        out_specs=[
            pl.BlockSpec((tm, D), lambda i: (i, 0)),
            pl.BlockSpec((tm, LANES), lambda i: (i, 0)),
            pl.BlockSpec((tm, LANES), lambda i: (i, 0)),
        ],
        out_shape=[jax.ShapeDtypeStruct((T, D), F32), jax.ShapeDtypeStruct((T, LANES), jnp.int32),
                   jax.ShapeDtypeStruct((T, LANES), F32)],
        compiler_params=_cparams(("parallel",)),
        name="moe_router",
    )(x, g.reshape(1, D), wr)


def _expert_kernel(te_ref, tv_ref, tok_ref, hn_hbm, wg_ref, wu_ref, wd_ref, o_ref, xg_ref, xb_ref, acc_ref, sem, *, nf):
    i = pl.program_id(0)
    f = pl.program_id(1)
    tm = xb_ref.shape[0]
    valid = tv_ref[i] > 0
    prev_valid = tv_ref[jnp.maximum(i - 1, 0)] > 0
    slot = i % 2

    def issue_row(tile, dst_slot, r):
        tok = tok_ref[tile * tm + r]
        pltpu.make_async_copy(hn_hbm.at[pl.ds(tok, 1), :], xg_ref.at[dst_slot, pl.ds(r, 1), :], sem.at[dst_slot]).start()

    @pl.when(jnp.logical_and(valid, jnp.logical_and(i == 0, f == 0)))
    def _():
        def body(r, c):
            issue_row(0, 0, r)
            return c

        lax.fori_loop(0, tm, body, 0, unroll=8)

    @pl.when(jnp.logical_and(f == 0, jnp.logical_or(valid, jnp.logical_and(i > 0, prev_valid))))
    def _():
        pltpu.make_async_copy(hn_hbm.at[pl.ds(0, tm), :], xg_ref.at[slot], sem.at[slot]).wait()

    @pl.when(jnp.logical_and(valid, f == 0))
    def _():
        xb_ref[...] = xg_ref[slot].astype(BF16)

    @pl.when(valid)
    def _():
        per = tm // nf
        row0 = f * per
        xb = xb_ref[...]
        a = jnp.dot(xb, wg_ref[0], preferred_element_type=F32)
        for j in range(per // 2):
            issue_row(i + 1, 1 - slot, row0 + j)
        b = jnp.dot(xb, wu_ref[0], preferred_element_type=F32)
        for j in range(per // 2, per):
            issue_row(i + 1, 1 - slot, row0 + j)
        h = (a * jax.nn.sigmoid(a) * b).astype(BF16)
        part = jnp.dot(h, wd_ref[0], preferred_element_type=F32)

        @pl.when(f == 0)
        def _():
            acc_ref[...] = part

        @pl.when(f > 0)
        def _():
            acc_ref[...] += part

    @pl.when(f == nf - 1)
    def _():
        o_ref[...] = jnp.where(valid, acc_ref[...], 0.0)


def moe_experts(hn, tile_expert, tile_valid, row_tok, wg, wu, wd, *, tm, tf):
    T, D = hn.shape
    F = wg.shape[2]
    P = row_tok.shape[0] - tm
    nt = P // tm + 1
    nf = F // tf

    def w_in_map(i, f, te, tv, tok):
        return (te[i], 0, jnp.where(tv[i] > 0, f, nf - 1))

    def w_out_map(i, f, te, tv, tok):
        return (te[i], jnp.where(tv[i] > 0, f, nf - 1), 0)

    grid_spec = pltpu.PrefetchScalarGridSpec(
        num_scalar_prefetch=3,
        grid=(nt, nf),
        in_specs=[
            pl.BlockSpec(memory_space=pl.ANY),
            pl.BlockSpec((1, D, tf), w_in_map),
            pl.BlockSpec((1, D, tf), w_in_map),
            pl.BlockSpec((1, tf, D), w_out_map),
        ],
        out_specs=pl.BlockSpec((tm, D), lambda i, f, te, tv, tok: (i, 0)),
        scratch_shapes=[pltpu.VMEM((2, tm, D), F32), pltpu.VMEM((tm, D), BF16), pltpu.VMEM((tm, D), F32),
                        pltpu.SemaphoreType.DMA((2,))],
    )
    return pl.pallas_call(
        functools.partial(_expert_kernel, nf=nf),
        grid_spec=grid_spec,
        out_shape=jax.ShapeDtypeStruct((P + tm, D), F32),
        compiler_params=_cparams(("arbitrary", "arbitrary")),
        name="moe_experts",
    )(tile_expert, tile_valid, row_tok, hn, wg, wu, wd)


def _combine_kernel(dest_ref, x_ref, gate_ref, yb_hbm, o_ref, buf_ref, sem):
    i = pl.program_id(0)
    n = pl.num_programs(0)
    tc = x_ref.shape[0]
    slot = i % 2

    def issue(step, dst_slot):
        base = step * tc * TOP_K

        for r in range(tc):
            for k in range(TOP_K):
                pltpu.make_async_copy(yb_hbm.at[pl.ds(dest_ref[base + r * TOP_K + k], 1), :],
                                      buf_ref.at[dst_slot, k, pl.ds(r, 1), :], sem.at[dst_slot]).start()

    @pl.when(i == 0)
    def _():
        issue(0, 0)

    for k in range(TOP_K):
        pltpu.make_async_copy(yb_hbm.at[pl.ds(0, tc), :], buf_ref.at[slot, k], sem.at[slot]).wait()

    @pl.when(i + 1 < n)
    def _():
        issue(i + 1, 1 - slot)

    g = gate_ref[...]
    o_ref[...] = x_ref[...] + g[:, 0:1] * buf_ref[slot, 0] + g[:, 1:2] * buf_ref[slot, 1]


def moe_combine(x, gates, yb, dest, *, tc=256):
    T, D = x.shape
    grid_spec = pltpu.PrefetchScalarGridSpec(
        num_scalar_prefetch=1,
        grid=(T // tc,),
        in_specs=[
            pl.BlockSpec((tc, D), lambda i, d: (i, 0)),
            pl.BlockSpec((tc, LANES), lambda i, d: (i, 0)),
            pl.BlockSpec(memory_space=pl.ANY),
        ],
        out_specs=pl.BlockSpec((tc, D), lambda i, d: (i, 0)),
        scratch_shapes=[pltpu.VMEM((2, TOP_K, tc, D), F32), pltpu.SemaphoreType.DMA((2,))],
    )
    return pl.pallas_call(
        _combine_kernel,
        grid_spec=grid_spec,
        out_shape=jax.ShapeDtypeStruct((T, D), F32),
        compiler_params=_cparams(("arbitrary",)),
        name="moe_combine",
    )(dest, x, gates, yb)


def moe_dispatch_plan(idx, *, tm):
    T = idx.shape[0]
    A = T * TOP_K
    e_flat = idx[:, :TOP_K].reshape(A)
    onehot = (e_flat[:, None] == jnp.arange(N_EXPERTS, dtype=jnp.int32)[None, :]).astype(jnp.int32)
    rank = jnp.sum((jnp.cumsum(onehot, axis=0) - onehot) * onehot, axis=1)
    counts = jnp.sum(onehot, axis=0)
    padded = (counts + tm - 1) // tm * tm
    ends = jnp.cumsum(padded)
    pstart = ends - padded
    dest = pstart[e_flat] + rank
    P = (A // tm + N_EXPERTS + 1) * tm
    nt = P // tm
    tok_flat = jnp.arange(A, dtype=jnp.int32) // TOP_K
    row_tok = jnp.zeros((P,), jnp.int32).at[dest].set(tok_flat)
    tile_start = jnp.arange(nt, dtype=jnp.int32) * tm
    tile_expert = jnp.minimum(jnp.sum((tile_start[:, None] >= ends[None, :]).astype(jnp.int32), axis=1), N_EXPERTS - 1)
    tile_valid = (tile_start < ends[-1]).astype(jnp.int32)
    last_valid = jnp.maximum(jnp.sum(tile_valid) - 1, 0)
    tile_expert = jnp.where(tile_valid > 0, tile_expert, tile_expert[last_valid])
    return dest.astype(jnp.int32), row_tok, tile_expert, tile_valid


def moe_layer(x, norm2, router, wg, wu, wd, *, tm=512, tf=1792, tr=512, tc=256):
    hn, idx, gates = moe_router(x, norm2, router, tm=tr)
    dest, row_tok, tile_expert, tile_valid = moe_dispatch_plan(idx, tm=tm)
    yb = moe_experts(hn, tile_expert, tile_valid, row_tok, wg, wu, wd, tm=tm, tf=tf)
    return moe_combine(x, gates, yb, dest, tc=tc)


def _even_layer(x, norm1, w_in, shift_mu, w0_f, w_up_f, w0_b, w_up_b, a0_f, a_up_f, a0_b, a_up_b, g_up,
                k_k, k_a, r_k, gn_w, gn_b, q_norm, k_norm, w_out, norm2, ffn_gate, ffn_up, ffn_down, cast):
    Bn, S, D = x.shape
    T = Bn * S
    xf = x.reshape(T, D)
    w_in = w_in.astype(BF16)
    p_att = norm_matmul(xf, norm1, w_in[:, RW_IN:], tm=1024, tn=ATT_IN).reshape(Bn, S, ATT_IN)
    y_a = rwkv7_mixer(x, norm1, w_in[:, :RW_IN], shift_mu, w0_f, w_up_f, w0_b, w_up_b, a0_f, a_up_f, a0_b, a_up_b,
                      g_up, k_k, k_a, r_k, gn_w, gn_b, out_dtype=BF16)
    y_b = gqa_attention(p_att, q_norm, k_norm, out_dtype=BF16)
    w_out = w_out.astype(BF16)
    xf = matmul_residual([y_a.reshape(T, RW_DIM), y_b.reshape(T, ATT_DIM)], [w_out[:RW_DIM], w_out[RW_DIM:]], xf,
                         tm=1024, tn=1024)
    return ffn_swiglu(xf, norm2, ffn_gate.astype(BF16), ffn_up.astype(BF16), ffn_down.astype(BF16), tm=512, tf=1408,
                      cast=cast)


def _odd_layer(xf, Bn, S, norm1, w_in, gate_up_f, gate_bias_f, gate_up_b, gate_bias_b, out_norm, w_out,
               norm2, router, exp_gate, exp_up, exp_down):
    T, D = xf.shape
    o3 = 2 * GLA_DK + GLA_DV
    o4 = o3 + GLA_GATE_RANK
    w_main = jnp.concatenate([w_in[:, :o3], w_in[:, o4:]], axis=1).astype(BF16)
    w_gd = jnp.zeros((D, LANES), F32).at[:, :GLA_GATE_RANK].set(w_in[:, o3:o4])
    p_main = norm_matmul(xf, norm1, w_main, tm=1024, tn=1024).reshape(Bn, S, -1)
    gd = norm_matmul(xf, norm1, w_gd, tm=1024, tn=LANES)[:, :GLA_GATE_RANK].reshape(Bn, S, GLA_GATE_RANK)
    o = gla_mixer(p_main, gd, gate_up_f, gate_bias_f, gate_up_b, gate_bias_b, out_norm, out_dtype=BF16)
    xf = matmul_residual([o.reshape(T, GLA_DV)], [w_out.astype(BF16)], xf, tm=1024, tn=1024)
    return moe_layer(xf, norm2, router, exp_gate, exp_up, exp_down)


def kernel(x, e_norm1, e_w_in, e_shift_mu, e_w0_f, e_w_up_f, e_w0_b, e_w_up_b, e_a0_f, e_a_up_f, e_a0_b, e_a_up_b, e_g_up, e_k_k, e_k_a, e_r_k, e_gn_w, e_gn_b, e_q_norm, e_k_norm, e_w_out, e_norm2, e_ffn_gate, e_ffn_up, e_ffn_down, o_norm1, o_w_in, o_gate_up_f, o_gate_bias_f, o_gate_up_b, o_gate_bias_b, o_out_norm, o_w_out, o_norm2, o_router, o_exp_gate, o_exp_up, o_exp_down):
    Bn, S, D = x.shape
    xf, (exp_gate, exp_up, exp_down) = _even_layer(
        x, e_norm1[0], e_w_in[0], e_shift_mu[0], e_w0_f[0], e_w_up_f[0], e_w0_b[0],
        e_w_up_b[0], e_a0_f[0], e_a_up_f[0], e_a0_b[0], e_a_up_b[0], e_g_up[0], e_k_k[0],
        e_k_a[0], e_r_k[0], e_gn_w[0], e_gn_b[0], e_q_norm[0], e_k_norm[0], e_w_out[0],
        e_norm2[0], e_ffn_gate[0], e_ffn_up[0], e_ffn_down[0], (o_exp_gate[0], o_exp_up[0], o_exp_down[0]))
    xf = _odd_layer(xf, Bn, S, o_norm1[0], o_w_in[0], o_gate_up_f[0], o_gate_bias_f[0], o_gate_up_b[0],
                    o_gate_bias_b[0], o_out_norm[0], o_w_out[0], o_norm2[0], o_router[0],
                    exp_gate, exp_up, exp_down)
    return xf.reshape(Bn, S, D)
```

```python
import functools

import jax
import jax.numpy as jnp
from jax import lax
from jax.experimental import pallas as pl
from jax.experimental.pallas import tpu as pltpu

F32 = jnp.float32
BF16 = jnp.bfloat16
HI = lax.Precision.HIGHEST

D_MODEL = 1024
GRID_W = 64
HEAD_DIM = 64
NORM_EPS = 1e-6
RW_HEADS = 8
RW_DIM = 512
DECAY_RANK = 64
ICLR_RANK = 64
GATE_RANK = 128
RWKV_GN_EPS = 64e-5
RW_IN = 3 * RW_DIM + DECAY_RANK + ICLR_RANK + GATE_RANK
RW_CHUNK = 64
ATT_HEADS = 8
ATT_KV_HEADS = 2
ATT_DIM = 512
ATT_KV_DIM = 128
ATT_IN = ATT_DIM + 2 * ATT_KV_DIM
ROPE_THETA = 10000.0
ATT_VT_ROWS = HEAD_DIM + 16
LOG2_E = 1.4426950408889634
GLA_HEADS = 4
GLA_DK = 512
GLA_DV = 1024
GLA_DKH = 128
GLA_DVH = 256
GLA_GATE_RANK = 16
GLA_GATE_NORM = 16.0
GLA_CHUNK = 64
N_EXPERTS = 8
TOP_K = 2
LANES = 128

VMEM_LIMIT = 48 * 1024 * 1024
VMEM_LIMIT_BIG = 56 * 1024 * 1024


def _cparams(sem, vmem=VMEM_LIMIT):
    return pltpu.CompilerParams(dimension_semantics=sem, vmem_limit_bytes=vmem)


def _iota2(shape, dim):
    return lax.broadcasted_iota(jnp.int32, shape, dim)


def _split(x, parts):
    out = []
    for _ in range(parts):
        t = x.astype(BF16)
        out.append(t)
        x = x - t.astype(F32)
    return out


def _dot_sel(x, m, parts=2):
    mb = m.astype(BF16)
    acc = None
    for t in _split(x, parts):
        d = jnp.dot(t, mb, preferred_element_type=F32)
        acc = d if acc is None else acc + d
    return acc


def _sel_dot(m, x, parts=3):
    mb = m.astype(BF16)
    acc = None
    for t in _split(x, parts):
        d = jnp.dot(mb, t, preferred_element_type=F32)
        acc = d if acc is None else acc + d
    return acc


def _dot3(a, b):
    ah, al = _split(a, 2)
    bh, bl = _split(b, 2)
    d = lambda u, v: jnp.dot(u, v, preferred_element_type=F32)
    return d(ah, bh) + (d(ah, bl) + d(al, bh))


def _bdot(a, b):
    return jnp.dot(a.astype(BF16), b.astype(BF16), preferred_element_type=F32)


def _bdot_nt(a, b):
    return lax.dot_general(a.astype(BF16), b.astype(BF16), (((1,), (1,)), ((), ())), preferred_element_type=F32)


def _bdot_tn(a, b):
    return lax.dot_general(a.astype(BF16), b.astype(BF16), (((0,), (0,)), ((), ())), preferred_element_type=F32)


def _norm_mm_kernel(x_ref, g_ref, w_ref, o_ref, xn_ref):
    @pl.when(pl.program_id(1) == 0)
    def _():
        x = x_ref[...]
        ms = jnp.mean(x * x, axis=-1, keepdims=True)
        xn_ref[...] = (x * lax.rsqrt(ms + NORM_EPS) * g_ref[...]).astype(xn_ref.dtype)

    if xn_ref.dtype == F32:
        o_ref[...] = _dot3(xn_ref[...], w_ref[...]).astype(o_ref.dtype)
    else:
        o_ref[...] = jnp.dot(xn_ref[...], w_ref[...], preferred_element_type=F32).astype(o_ref.dtype)


def norm_matmul(x, g, w, *, tm, tn, out_dtype=F32):
    M, K = x.shape
    N = w.shape[1]
    return pl.pallas_call(
        _norm_mm_kernel,
        grid=(M // tm, N // tn),
        in_specs=[
            pl.BlockSpec((tm, K), lambda i, j: (i, 0)),
            pl.BlockSpec((1, K), lambda i, j: (0, 0)),
            pl.BlockSpec((K, tn), lambda i, j: (0, j)),
        ],
        out_specs=pl.BlockSpec((tm, tn), lambda i, j: (i, j)),
        out_shape=jax.ShapeDtypeStruct((M, N), out_dtype),
        scratch_shapes=[pltpu.VMEM((tm, K), w.dtype)],
        compiler_params=_cparams(("parallel", "arbitrary")),
        name="norm_matmul",
    )(x, g.reshape(1, K), w)


def _mm_res_kernel(*refs):
    n = (len(refs) - 2) // 2
    r_ref, o_ref = refs[2 * n], refs[2 * n + 1]
    acc = r_ref[...]
    for y_ref, w_ref in zip(refs[:n], refs[n:2 * n]):
        acc = acc + jnp.dot(y_ref[...].astype(BF16), w_ref[...], preferred_element_type=F32)
    o_ref[...] = acc


def matmul_residual(ys, ws, r, *, tm, tn):
    M, N = r.shape
    y_specs = [pl.BlockSpec((tm, y.shape[1]), lambda i, j: (i, 0)) for y in ys]
    w_specs = [pl.BlockSpec((w.shape[0], tn), lambda i, j: (0, j)) for w in ws]
    return pl.pallas_call(
        _mm_res_kernel,
        grid=(M // tm, N // tn),
        in_specs=y_specs + w_specs + [pl.BlockSpec((tm, tn), lambda i, j: (i, j))],
        out_specs=pl.BlockSpec((tm, tn), lambda i, j: (i, j)),
        out_shape=jax.ShapeDtypeStruct((M, N), F32),
        compiler_params=_cparams(("parallel", "arbitrary")),
        name="matmul_residual",
    )(*ys, *ws, r)


def _ffn_kernel(*refs, n_cast):
    x_ref, g_ref, wg_ref, wu_ref, wd_ref = refs[:5]
    cast_in = refs[5:5 + n_cast]
    o_ref = refs[5 + n_cast]
    cast_out = refs[6 + n_cast:6 + 2 * n_cast]
    xn_ref, acc_ref = refs[6 + 2 * n_cast:]
    f = pl.program_id(1)

    @pl.when(f == 0)
    def _():
        x = x_ref[...]
        ms = jnp.mean(x * x, axis=-1, keepdims=True)
        xn_ref[...] = (x * lax.rsqrt(ms + NORM_EPS) * g_ref[...]).astype(BF16)

    for src, dst in zip(cast_in, cast_out):
        dst[...] = src[...].astype(BF16)

    xn = xn_ref[...]
    a = jnp.dot(xn, wg_ref[...], preferred_element_type=F32)
    b = jnp.dot(xn, wu_ref[...], preferred_element_type=F32)
    h = (a * jax.nn.sigmoid(a) * b).astype(BF16)
    part = jnp.dot(h, wd_ref[...], preferred_element_type=F32)

    @pl.when(f == 0)
    def _():
        acc_ref[...] = part

    @pl.when(f > 0)
    def _():
        acc_ref[...] += part

    @pl.when(f == pl.num_programs(1) - 1)
    def _():
        o_ref[...] = x_ref[...] + acc_ref[...]


def ffn_swiglu(x, g, wg, wu, wd, *, tm, tf, cast=()):
    M, D = x.shape
    F = wg.shape[1]
    nf = F // tf
    steps = (M // tm) * nf
    cast_specs, cast_shapes = [], []
    for w in cast:
        E, R, C = w.shape
        per = steps // E
        cast_specs.append(pl.BlockSpec((1, R // per, C), lambda i, f, per=per: ((i * nf + f) // per, (i * nf + f) % per, 0)))
        cast_shapes.append(jax.ShapeDtypeStruct(w.shape, BF16))
    outs = pl.pallas_call(
        functools.partial(_ffn_kernel, n_cast=len(cast)),
        grid=(M // tm, nf),
        in_specs=[
            pl.BlockSpec((tm, D), lambda i, f: (i, 0)),
            pl.BlockSpec((1, D), lambda i, f: (0, 0)),
            pl.BlockSpec((D, tf), lambda i, f: (0, f)),
            pl.BlockSpec((D, tf), lambda i, f: (0, f)),
            pl.BlockSpec((tf, D), lambda i, f: (f, 0)),
        ] + cast_specs,
        out_specs=[pl.BlockSpec((tm, D), lambda i, f: (i, 0))] + cast_specs,
        out_shape=[jax.ShapeDtypeStruct((M, D), F32)] + cast_shapes,
        scratch_shapes=[pltpu.VMEM((tm, D), BF16), pltpu.VMEM((tm, D), F32)],
        compiler_params=_cparams(("arbitrary", "arbitrary"), vmem=VMEM_LIMIT_BIG if cast else VMEM_LIMIT),
        name="ffn_swiglu",
    )(x, g.reshape(1, D), wg, wu, wd, *cast)
    return outs[0], tuple(outs[1:])


def _rwkv_prep_kernel(cur_ref, prev_ref, next_ref, n1_ref, win_ref, mu_ref, w0f_ref, wupf_ref, w0b_ref, wupb_ref,
                      a0f_ref, aupf_ref, a0b_ref, aupb_ref, gup_ref, kk_ref, hsum_ref,
                      r_ref, k_ref, v_ref, kkn_ref, g_ref, lwf_ref, lf_ref, asf_ref, lwb_ref, lb_ref, asb_ref):
    i = pl.program_id(1)
    nt = pl.num_programs(1)

    def project(xv):
        ms = jnp.mean(xv * xv, axis=-1, keepdims=True)
        xn = (xv * lax.rsqrt(ms + NORM_EPS) * n1_ref[...]).astype(BF16)
        return jnp.dot(xn, win_ref[...], preferred_element_type=F32)

    Tt = cur_ref.shape[1]
    pall = project(jnp.concatenate([cur_ref[0], prev_ref[0], next_ref[0]], axis=0))
    x = pall[:Tt]
    halo = pall[Tt:]
    row = _iota2(x.shape, 0)
    prev_row = jnp.where(i > 0, halo[7:8, :], 0.0)
    next_row = jnp.where(i < nt - 1, halo[8:9, :], 0.0)
    prev = jnp.where(row == 0, prev_row, pltpu.roll(x, 1, 0))
    nxt = jnp.where(row == Tt - 1, next_row, pltpu.roll(x, Tt - 1, 0))
    pm = x + (0.5 * (prev + nxt) - x) * mu_ref[...]
    o1 = 3 * RW_DIM
    o2 = o1 + DECAY_RANK
    o3 = o2 + ICLR_RANK
    r_ref[0] = pm[:, :RW_DIM]
    k = pm[:, RW_DIM:2 * RW_DIM]
    k_ref[0] = k
    v_ref[0] = pm[:, 2 * RW_DIM:o1]
    wd = jnp.tanh(pm[:, o1:o2])
    ad = pm[:, o2:o3]
    gin = jax.nn.sigmoid(pm[:, o3:])
    g_ref[0] = jnp.dot(gin.astype(BF16), gup_ref[...].astype(BF16), preferred_element_type=F32)
    kk = k * kk_ref[...]
    ss = _dot_sel(kk * kk, hsum_ref[...])
    kkn_ref[0] = kk / jnp.maximum(jnp.sqrt(ss), 1e-12)

    ci = _iota2((RW_CHUNK, RW_CHUNK), 0)
    cj = _iota2((RW_CHUNK, RW_CHUNK), 1)
    tril = (cj <= ci).astype(F32)
    triu = (cj >= ci).astype(F32)

    def direction(w0_ref, wup_ref, a0_ref, aup_ref, tri, lw_ref, l_ref, as_ref):
        z = w0_ref[...] + _dot3(wd, wup_ref[...])
        sp = jnp.maximum(-z, 0.0) + jnp.log(1.0 + jnp.exp(-jnp.abs(z)))
        lw = -jnp.exp(-sp - 0.5)
        lw_ref[0] = lw
        as_ref[0] = jax.nn.sigmoid(a0_ref[...] + _dot3(ad, aup_ref[...]))
        for c in range(Tt // RW_CHUNK):
            sl = slice(c * RW_CHUNK, (c + 1) * RW_CHUNK)
            l_ref[0, sl, :] = _sel_dot(tri, lw[sl, :])

    direction(w0f_ref, wupf_ref, a0f_ref, aupf_ref, tril, lwf_ref, lf_ref, asf_ref)
    direction(w0b_ref, wupb_ref, a0b_ref, aupb_ref, triu, lwb_ref, lb_ref, asb_ref)


def rwkv_prep(x, norm1, w_in, shift_mu, w0_f, w_up_f, w0_b, w_up_b, a0_f, a_up_f, a0_b, a_up_b, g_up, k_k, *, tt=256):
    B, S, D = x.shape
    nt = S // tt
    hsum = (jnp.arange(RW_DIM)[:, None] // HEAD_DIM == jnp.arange(RW_DIM)[None, :] // HEAD_DIM).astype(F32)
    row = lambda a: a.reshape(1, -1)
    full = lambda a: pl.BlockSpec(a.shape, lambda b, i: (0,) * a.ndim)
    params = [row(norm1), w_in, row(shift_mu), row(w0_f), w_up_f, row(w0_b), w_up_b, row(a0_f), a_up_f, row(a0_b), a_up_b,
              g_up, row(k_k), hsum]
    tb = tt // 8
    in_specs = [
        pl.BlockSpec((1, tt, D), lambda b, i: (b, i, 0)),
        pl.BlockSpec((1, 8, D), lambda b, i: (b, jnp.maximum(i * tb - 1, 0), 0)),
        pl.BlockSpec((1, 8, D), lambda b, i: (b, jnp.minimum((i + 1) * tb, S // 8 - 1), 0)),
    ] + [full(a) for a in params]
    out_spec = pl.BlockSpec((1, tt, RW_DIM), lambda b, i: (b, i, 0))
    out_sds = jax.ShapeDtypeStruct((B, S, RW_DIM), F32)
    return pl.pallas_call(
        _rwkv_prep_kernel,
        grid=(B, nt),
        in_specs=in_specs,
        out_specs=[out_spec] * 11,
        out_shape=[out_sds] * 11,
        compiler_params=_cparams(("parallel", "arbitrary")),
        name="rwkv_prep",
    )(x, x, x, *params)


def _bd(y, bd_mask):
    return jnp.where(bd_mask, jnp.concatenate([y, y], axis=0), jnp.zeros((), y.dtype))


def _chunk_terms(chains, out, bd_mask):
    C = chains[0]["r"].shape[0]
    n = len(chains)
    R = range(n)
    dot = lambda u, v: jnp.dot(u, v, preferred_element_type=F32)
    bd = lambda y: _bd(y, bd_mask)
    pre = []
    for ch in chains:
        r, k, v, kk, lw, L, asig, ka, rev = (ch[x] for x in ("r", "k", "v", "kk", "lw", "L", "asig", "ka", "rev"))
        a = -kk
        b = kk * asig
        kd = k * (1.0 + (asig - 1.0) * ka)
        Lp = L - lw
        Lr = Lp if rev else L
        Lend = L[0:1, :] if rev else L[C - 1:C, :]
        Lmid = L[C // 2:C // 2 + 1, :]
        einv = jnp.exp(Lmid - L)
        eend = jnp.exp(Lend - L)
        pre.append(dict(
            ar=jnp.concatenate([a * jnp.exp(Lp - Lmid), r * jnp.exp(Lr - Lmid)], axis=0).astype(BF16),
            bt=bd((b * einv).astype(BF16)), kt=bd((kd * einv).astype(BF16)),
            a0=bd((a * jnp.exp(Lp)).astype(BF16)), r0=r * jnp.exp(Lr),
            bh=(b * eend).astype(BF16), kh=(kd * eend).astype(BF16), vb=v.astype(BF16),
            dec=jnp.exp(Lend)))
    mk = [ch["masks"] for ch in chains]
    Ab = [_bdot_nt(pre[i]["ar"], pre[i]["bt"]) for i in R]
    Ak = [_bdot_nt(pre[i]["ar"], pre[i]["kt"]) for i in R]
    yield
    Aab = [jnp.where(mk[i][0], Ab[i][:C], 0.0) for i in R]
    Arb = [jnp.where(mk[i][1], Ab[i][C:], 0.0).astype(BF16) for i in R]
    AakArk = [jnp.concatenate([jnp.where(mk[i][0], Ak[i][:C], 0.0), jnp.where(mk[i][1], Ak[i][C:], 0.0)],
                              axis=0).astype(BF16) for i in R]
    AV = [dot(AakArk[i], bd(pre[i]["vb"])) for i in R]
    Xb = [jnp.where(mk[i][2], Aab[i], 0.0).astype(BF16) for i in R]
    X2b = [dot(Xb[i], bd(Xb[i])).astype(BF16) for i in R]
    yield
    T = [mk[i][6] + Xb[i].astype(F32) for i in R]
    T = [T[i] + dot(T[i].astype(BF16), bd(X2b[i])) for i in R]
    X4b = [dot(X2b[i], bd(X2b[i])).astype(BF16) for i in R]
    yield
    T = [T[i] + dot(T[i].astype(BF16), bd(X4b[i])) for i in R]
    yield
    for lvl in (3, 4, 5):
        Tb = [T[i].astype(BF16) for i in R]
        ET = [dot(jnp.where(mk[i][lvl], Aab[i], 0.0).astype(BF16), bd(Tb[i])).astype(BF16) for i in R]
        yield
        T = [T[i] + dot(Tb[i], bd(ET[i])) for i in R]
        yield
    Tb = [T[i].astype(BF16) for i in R]
    A0p = [dot(Tb[i], pre[i]["a0"]).astype(BF16) for i in R]
    Uv = [dot(Tb[i], bd(AV[i][:C].astype(BF16))).astype(BF16) for i in R]
    yield
    Rpp = [pre[i]["r0"] + dot(Arb[i], bd(A0p[i])) for i in R]
    Yv = [dot(Arb[i], bd(Uv[i])) + AV[i][C:] for i in R]
    eye2 = (_iota2(bd_mask.shape, 0) == _iota2(bd_mask.shape, 1)).astype(F32)
    P = [jnp.where(bd_mask, _bdot_tn(pre[i]["bh"], A0p[i]), 0.0) + eye2 * pre[i]["dec"] for i in R]
    Q = [jnp.where(bd_mask, _bdot_tn(pre[i]["bh"], Uv[i]) + _bdot_tn(pre[i]["kh"], pre[i]["vb"]), 0.0) for i in R]
    out.extend((Rpp[i], Yv[i], P[i], Q[i]) for i in R)


def _make_masks(C, N, rev):
    ii = _iota2((C, 2 * N), 0)
    jj = _iota2((C, 2 * N), 1) % N
    strict = (jj > ii) if rev else (jj < ii)
    rmask = strict if rev else (jj <= ii)
    blk = lambda s: (ii // s) == (jj // s)
    m8 = blk(8)
    e16 = blk(16) & jnp.logical_not(blk(8))
    e32 = blk(32) & jnp.logical_not(blk(16))
    e64 = jnp.logical_not(blk(32))
    eye = (ii == jj).astype(F32)
    return strict, rmask, m8, e16, e32, e64, eye


RW_UNROLL = 8


def _rwkv_scan_kernel(r_ref, k_ref, v_ref, kk_ref, g_ref, lwf_ref, lf_ref, asf_ref, lwb_ref, lb_ref, asb_ref,
                      ka_ref, rk_ref, gnw_ref, gnb_ref, havg_ref, o_ref,
                      yf_ref, yb_ref, h_ref, rpp_ref, yv_ref, p_ref, q_ref):
    S = r_ref.shape[1]
    C = RW_CHUNK
    N = HEAD_DIM
    W = 2 * N
    n = S // C
    U = RW_UNROLL
    nset = n // U
    masks_f = _make_masks(C, N, False)
    masks_b = _make_masks(C, N, True)
    bd_mask = (_iota2((W, W), 0) // N) == (_iota2((W, W), 1) // N)
    h_ref[...] = jnp.zeros_like(h_ref)
    dirs = ((False, lwf_ref, lf_ref, asf_ref, yf_ref, masks_f), (True, lwb_ref, lb_ref, asb_ref, yb_ref, masks_b))

    def chunk_rows(rev, st, u):
        c = st * U + u
        c = (n - 1 - c) if rev else c
        return pl.ds(pl.multiple_of(c * C, C), C)

    def state_step(st, u):
        slot = st % 2
        for di, (rev, lw_ref, l_ref, as_ref, y_ref, masks) in enumerate(dirs):
            Hb = h_ref[di].astype(BF16)
            y_ref[chunk_rows(rev, st, u), :] = (jnp.dot(rpp_ref[slot, di, u], Hb, preferred_element_type=F32)
                                                + yv_ref[slot, di, u])
            h_ref[di] = jnp.dot(p_ref[slot, di, u], Hb, preferred_element_type=F32) + q_ref[slot, di, u]

    def chunk_set(st, state_of):
        chains, slots = [], []
        for u in range(U):
            for di, (rev, lw_ref, l_ref, as_ref, y_ref, masks) in enumerate(dirs):
                rows = chunk_rows(rev, st, u)
                chains.append(dict(
                    r=r_ref[0, rows, :], k=k_ref[0, rows, :], v=v_ref[0, rows, :], kk=kk_ref[0, rows, :],
                    lw=lw_ref[0, rows, :], L=l_ref[0, rows, :], asig=as_ref[0, rows, :], ka=ka_ref[...],
                    rev=rev, masks=masks))
                slots.append((di, u))
        res = []
        pending = list(range(U)) if state_of is not None else []
        for stage, _ in enumerate(_chunk_terms(chains, res, bd_mask)):
            if pending and stage % 2 == 1:
                state_step(state_of, pending.pop(0))
        for u in pending:
            state_step(state_of, u)
        slot = st % 2
        for (di, u), (Rpp, Yv, P, Q) in zip(slots, res):
            rpp_ref[slot, di, u] = Rpp.astype(BF16)
            yv_ref[slot, di, u] = Yv
            p_ref[slot, di, u] = P.astype(BF16)
            q_ref[slot, di, u] = Q

    chunk_set(0, None)

    def body(st, carry):
        chunk_set(st, st - 1)
        return carry

    lax.fori_loop(1, nset, body, 0)
    for u in range(U):
        state_step(nset - 1, u)

    RT = 256

    def fin(i, carry):
        rows = pl.ds(pl.multiple_of(i * RT, RT), RT)
        y = yf_ref[rows, :] + yb_ref[rows, :]
        mean = _dot_sel(y, havg_ref[...])
        d = y - mean
        var = _dot_sel(d * d, havg_ref[...])
        yn = d * lax.rsqrt(var + RWKV_GN_EPS) * gnw_ref[...] + gnb_ref[...]
        r = r_ref[0, rows, :]
        kf = k_ref[0, rows, :] * (1.0 + (asf_ref[0, rows, :] - 1.0) * ka_ref[...])
        bonus = _dot_sel(r * kf * rk_ref[...], havg_ref[...]) * float(N) * v_ref[0, rows, :]
        o_ref[0, rows, :] = ((yn + bonus) * g_ref[0, rows, :]).astype(o_ref.dtype)
        return carry

    lax.fori_loop(0, S // RT, fin, 0)


def rwkv_scan(r, k, v, kk, g, lw_f, l_f, as_f, lw_b, l_b, as_b, k_a, r_k, gn_w, gn_b, out_dtype=F32):
    B, S, _ = r.shape
    W = 2 * HEAD_DIM
    N = HEAD_DIM
    havg = (jnp.arange(W)[:, None] // HEAD_DIM == jnp.arange(W)[None, :] // HEAD_DIM).astype(F32) / HEAD_DIM
    seq = pl.BlockSpec((1, S, W), lambda b, h: (b, 0, h))
    par = pl.BlockSpec((1, W), lambda b, h: (0, h))
    row = lambda a: a.reshape(1, -1)
    U = RW_UNROLL
    return pl.pallas_call(
        _rwkv_scan_kernel,
        grid=(B, RW_DIM // W),
        in_specs=[seq] * 11 + [par] * 4 + [pl.BlockSpec((W, W), lambda b, h: (0, 0))],
        out_specs=seq,
        out_shape=jax.ShapeDtypeStruct((B, S, RW_DIM), out_dtype),
        scratch_shapes=[pltpu.VMEM((S, W), F32), pltpu.VMEM((S, W), F32), pltpu.VMEM((2, W, W), F32),
                        pltpu.VMEM((2, 2, U, RW_CHUNK, W), BF16), pltpu.VMEM((2, 2, U, RW_CHUNK, W), F32),
                        pltpu.VMEM((2, 2, U, W, W), BF16), pltpu.VMEM((2, 2, U, W, W), F32)],
        compiler_params=_cparams(("parallel", "parallel")),
        name="rwkv_scan",
    )(r, k, v, kk, g, lw_f, l_f, as_f, lw_b, l_b, as_b, row(k_a), row(r_k), row(gn_w), row(gn_b), havg)


def rwkv7_mixer(x, norm1, w_in, shift_mu, w0_f, w_up_f, w0_b, w_up_b, a0_f, a_up_f, a0_b, a_up_b, g_up, k_k, k_a,
                r_k, gn_w, gn_b, out_dtype=F32, tt=512):
    outs = rwkv_prep(x, norm1, w_in, shift_mu, w0_f, w_up_f, w0_b, w_up_b, a0_f, a_up_f, a0_b, a_up_b, g_up, k_k, tt=tt)
    return rwkv_scan(*outs, k_a, r_k, gn_w, gn_b, out_dtype=out_dtype)


def _head_norm_rope(x, gain, cos, sin_signed, havg):
    ms = _dot_sel(x * x, havg)
    xn = x * lax.rsqrt(ms + NORM_EPS) * gain
    W = x.shape[1]
    even = (_iota2(x.shape, 1) % 2) == 0
    partner = jnp.where(even, pltpu.roll(xn, W - 1, 1), pltpu.roll(xn, 1, 1))
    return xn * cos + partner * sin_signed


def _attn_kernel(q_ref, k_ref, v_ref, cosq_ref, sinq_ref, cosk_ref, sink_ref, qg_ref, kg_ref, hq_ref, hk_ref,
                 o_ref, ks_ref, vt_ref):
    i = pl.program_id(1)
    G = ATT_HEADS // ATT_KV_HEADS
    D = HEAD_DIM

    @pl.when(i == 0)
    def _():
        kr = _head_norm_rope(k_ref[0], kg_ref[...], cosk_ref[...], sink_ref[...], hk_ref[...])
        vt = jnp.transpose(v_ref[0])
        ones = jnp.ones((ATT_VT_ROWS - D, vt.shape[1]), BF16)
        for kv in range(ATT_KV_HEADS):
            ks_ref[kv] = kr[:, kv * D:(kv + 1) * D].astype(BF16)
            vt_ref[kv] = jnp.concatenate([vt[kv * D:(kv + 1) * D, :].astype(BF16), ones], axis=0)

    q = _head_norm_rope(q_ref[0], qg_ref[...], cosq_ref[...], sinq_ref[...], hq_ref[...]) * (D ** -0.5 * LOG2_E)
    qb = q.astype(BF16)
    outs = []
    for kv in range(ATT_KV_HEADS):
        hs = range(kv * G, (kv + 1) * G)
        st = [lax.dot_general(ks_ref[kv], qb[:, h * D:(h + 1) * D], (((1,), (1,)), ((), ())),
                              preferred_element_type=F32) for h in hs]
        m = [jnp.max(x, axis=0, keepdims=True) for x in st]
        p = [jnp.exp2(x - mm).astype(BF16) for x, mm in zip(st, m)]
        ot = [jnp.dot(vt_ref[kv], x, preferred_element_type=F32) for x in p]
        outs.extend(o[:D] / o[D:D + 1] for o in ot)
    pairs = [jnp.transpose(jnp.concatenate(outs[2 * j:2 * j + 2], axis=0)) for j in range(ATT_HEADS // 2)]
    o_ref[0] = jnp.concatenate(pairs, axis=1).astype(o_ref.dtype)


def _rope_tables(S):
    rows = S // GRID_W
    row = jnp.repeat(jnp.arange(rows), GRID_W).astype(F32)
    col = jnp.tile(jnp.arange(GRID_W), rows).astype(F32)
    half = HEAD_DIM // 2
    freq = ROPE_THETA ** (-jnp.arange(0, half, 2, dtype=F32) / half)
    ang = jnp.concatenate([row[:, None] * freq, col[:, None] * freq], axis=-1)
    cos = jnp.repeat(jnp.cos(ang), 2, axis=-1)
    sin = jnp.repeat(jnp.sin(ang), 2, axis=-1)
    sign = jnp.where(jnp.arange(HEAD_DIM) % 2 == 0, -1.0, 1.0).astype(F32)
    return cos, sin * sign


def gqa_attention(p, q_norm, k_norm, *, tq=512, out_dtype=F32):
    B, S, _ = p.shape
    cos, sin = _rope_tables(S)
    tile = lambda t, n: jnp.tile(t, (1, n))
    hq = (jnp.arange(ATT_DIM)[:, None] // HEAD_DIM == jnp.arange(ATT_DIM)[None, :] // HEAD_DIM).astype(F32) / HEAD_DIM
    hk = hq[:ATT_KV_DIM, :ATT_KV_DIM]
    qg = jnp.tile(q_norm, ATT_HEADS).reshape(1, ATT_DIM)
    kg = jnp.tile(k_norm, ATT_KV_HEADS).reshape(1, ATT_KV_DIM)
    nq = ATT_DIM // ATT_KV_DIM
    const = lambda a: pl.BlockSpec(a.shape, lambda b, i: (0, 0))
    return pl.pallas_call(
        _attn_kernel,
        grid=(B, S // tq),
        in_specs=[
            pl.BlockSpec((1, tq, ATT_DIM), lambda b, i: (b, i, 0)),
            pl.BlockSpec((1, S, ATT_KV_DIM), lambda b, i: (b, 0, nq)),
            pl.BlockSpec((1, S, ATT_KV_DIM), lambda b, i: (b, 0, nq + 1)),
            pl.BlockSpec((tq, ATT_DIM), lambda b, i: (i, 0)),
            pl.BlockSpec((tq, ATT_DIM), lambda b, i: (i, 0)),
            pl.BlockSpec((S, ATT_KV_DIM), lambda b, i: (0, 0)),
            pl.BlockSpec((S, ATT_KV_DIM), lambda b, i: (0, 0)),
            const(qg), const(kg), const(hq), const(hk),
        ],
        out_specs=pl.BlockSpec((1, tq, ATT_DIM), lambda b, i: (b, i, 0)),
        out_shape=jax.ShapeDtypeStruct((B, S, ATT_DIM), out_dtype),
        scratch_shapes=[pltpu.VMEM((ATT_KV_HEADS, S, HEAD_DIM), BF16), pltpu.VMEM((ATT_KV_HEADS, ATT_VT_ROWS, S), BF16)],
        compiler_params=_cparams(("parallel", "arbitrary")),
        name="gqa_attention",
    )(p, p, p, tile(cos, ATT_HEADS), tile(sin, ATT_HEADS), tile(cos, ATT_KV_HEADS), tile(sin, ATT_KV_HEADS),
      qg, kg, hq, hk)


GLA_UNROLL = 8


def _gla_kernel(x_ref, n1_ref, wqk_ref, wv_ref, wog_ref, wgd_ref, upf_ref, bf_ref, upb_ref, bb_ref, on_ref, o_ref,
                xn_ref, gd_ref, q_ref, k_ref, v_ref, og_ref, yf_ref, yb_ref, st_ref, kv_ref, qe_ref, dec_ref):
    S = x_ref.shape[1]
    RT = 256

    @pl.when(pl.program_id(1) == 0)
    def _():
        def norm_tile(i, carry):
            rows = pl.ds(pl.multiple_of(i * RT, RT), RT)
            xv = x_ref[0, rows, :]
            ms = jnp.mean(xv * xv, axis=-1, keepdims=True)
            xn = xv * lax.rsqrt(ms + NORM_EPS) * n1_ref[...]
            gd_ref[rows, :] = _dot3(xn, wgd_ref[...])
            xn_ref[rows, :] = xn.astype(BF16)
            return carry

        lax.fori_loop(0, S // RT, norm_tile, 0)

    PT = 512

    def project_tile(i, carry):
        rows = pl.ds(pl.multiple_of(i * PT, PT), PT)
        xn = xn_ref[rows, :]
        qk = jnp.dot(xn, wqk_ref[...], preferred_element_type=F32)
        q_ref[rows, :] = qk[:, :GLA_DKH]
        k_ref[rows, :] = qk[:, GLA_DKH:]
        v_ref[rows, :] = jnp.dot(xn, wv_ref[...], preferred_element_type=F32)
        og_ref[rows, :] = jnp.dot(xn, wog_ref[...], preferred_element_type=F32)
        return carry

    lax.fori_loop(0, S // PT, project_tile, 0)

    C = GLA_CHUNK
    n = S // C
    U = GLA_UNROLL
    nset = n // U
    ii = _iota2((C, C), 0)
    jj = _iota2((C, C), 1)
    dirs = (
        (False, (jj <= ii).astype(F32), jj <= ii, upf_ref, bf_ref, yf_ref),
        (True, (jj >= ii).astype(F32), jj > ii, upb_ref, bb_ref, yb_ref),
    )
    scale = GLA_DKH ** -0.5
    st_ref[...] = jnp.zeros_like(st_ref)

    def chunk_rows(rev, st, u):
        c = st * U + u
        c = (n - 1 - c) if rev else c
        return pl.ds(pl.multiple_of(c * C, C), C)

    def state_step(st, u):
        slot = st % 2
        for di, (rev, tri, mask, up_ref, b_ref, y_ref) in enumerate(dirs):
            state = st_ref[di]
            rows = chunk_rows(rev, st, u)
            y_ref[rows, :] += lax.dot_general(qe_ref[slot, di, u], state.astype(BF16), (((1,), (1,)), ((), ())),
                                              preferred_element_type=F32)
            st_ref[di] = state * dec_ref[slot, di, u] + kv_ref[slot, di, u]

    def chunk_set(st, state_of):
        pending = list(range(U)) if state_of is not None else []
        items = []
        for u in range(U):
            for di, (rev, tri, mask, up_ref, b_ref, y_ref) in enumerate(dirs):
                rows = chunk_rows(rev, st, u)
                z = _dot3(gd_ref[rows, 0:GLA_GATE_RANK], up_ref[...]) + b_ref[...]
                g = (jnp.minimum(z, 0.0) - jnp.log(1.0 + jnp.exp(-jnp.abs(z)))) * (1.0 / GLA_GATE_NORM)
                items.append(dict(u=u, di=di, rows=rows, rev=rev, mask=mask, y_ref=y_ref, g=g, tri=tri))
        for x in items:
            x["b"] = _sel_dot(x["tri"], x["g"])
        if pending:
            state_step(state_of, pending.pop(0))
        for x in items:
            b = x["b"]
            q = q_ref[x["rows"], :] * scale
            k = k_ref[x["rows"], :]
            b_mid = b[C // 2:C // 2 + 1, :]
            b_last = b[0:1, :] if x["rev"] else b[C - 1:C, :]
            x["vb"] = v_ref[x["rows"], :].astype(BF16)
            x["qm"] = (q * jnp.exp(b - b_mid)).astype(BF16)
            x["km"] = (k * jnp.exp(b_mid - b)).astype(BF16)
            x["ke"] = (k * jnp.exp(b_last - b)).astype(BF16)
            x["qe"] = (q * jnp.exp(b)).astype(BF16)
            x["dec"] = jnp.exp(b_last)
        att = [lax.dot_general(x["qm"], x["km"], (((1,), (1,)), ((), ())), preferred_element_type=F32) for x in items]
        if pending:
            state_step(state_of, pending.pop(0))
        att = [jnp.where(x["mask"], a, 0.0).astype(BF16) for x, a in zip(items, att)]
        slot = st % 2
        for idx, (x, a) in enumerate(zip(items, att)):
            x["y_ref"][x["rows"], :] = jnp.dot(a, x["vb"], preferred_element_type=F32)
            kv_ref[slot, x["di"], x["u"]] = lax.dot_general(x["vb"], x["ke"], (((0,), (0,)), ((), ())),
                                                            preferred_element_type=F32)
            qe_ref[slot, x["di"], x["u"]] = x["qe"]
            dec_ref[slot, x["di"], x["u"]] = x["dec"]
            if pending and idx % 3 == 2:
                state_step(state_of, pending.pop(0))
        for u in pending:
            state_step(state_of, u)

    chunk_set(0, None)

    def body(st, carry):
        chunk_set(st, st - 1)
        return carry

    lax.fori_loop(1, nset, body, 0)
    for u in range(U):
        state_step(nset - 1, u)

    def fin(i, carry):
        rows = pl.ds(pl.multiple_of(i * RT, RT), RT)
        o = yf_ref[rows, :] + yb_ref[rows, :]
        ms = jnp.mean(o * o, axis=-1, keepdims=True)
        on = o * lax.rsqrt(ms + NORM_EPS) * on_ref[...]
        og = og_ref[rows, :]
        o_ref[0, rows, :] = (on * (og * jax.nn.sigmoid(og))).astype(o_ref.dtype)
        return carry

    lax.fori_loop(0, S // RT, fin, 0)


def gla_mixer(x, norm1, wqk, wv, wog, wgd, gate_up_f, gate_bias_f, gate_up_b, gate_bias_b, out_norm, out_dtype=F32):
    B, S, D = x.shape
    U = GLA_UNROLL
    H = GLA_HEADS
    return pl.pallas_call(
        _gla_kernel,
        grid=(B, H),
        in_specs=[
            pl.BlockSpec((1, S, D), lambda b, h: (b, 0, 0)),
            pl.BlockSpec((1, D), lambda b, h: (0, 0)),
            pl.BlockSpec((D, 2 * GLA_DKH), lambda b, h: (0, h)),
            pl.BlockSpec((D, GLA_DVH), lambda b, h: (0, h)),
            pl.BlockSpec((D, GLA_DVH), lambda b, h: (0, h)),
            pl.BlockSpec((D, LANES), lambda b, h: (0, 0)),
            pl.BlockSpec((GLA_GATE_RANK, GLA_DKH), lambda b, h: (0, h)),
            pl.BlockSpec((1, GLA_DKH), lambda b, h: (0, h)),
            pl.BlockSpec((GLA_GATE_RANK, GLA_DKH), lambda b, h: (0, h)),
            pl.BlockSpec((1, GLA_DKH), lambda b, h: (0, h)),
            pl.BlockSpec((1, GLA_DVH), lambda b, h: (0, 0)),
        ],
        out_specs=pl.BlockSpec((1, S, GLA_DVH), lambda b, h: (b, 0, h)),
        out_shape=jax.ShapeDtypeStruct((B, S, GLA_DV), out_dtype),
        scratch_shapes=[pltpu.VMEM((S, D), BF16), pltpu.VMEM((S, LANES), F32),
                        pltpu.VMEM((S, GLA_DKH), F32), pltpu.VMEM((S, GLA_DKH), F32),
                        pltpu.VMEM((S, GLA_DVH), F32), pltpu.VMEM((S, GLA_DVH), F32),
                        pltpu.VMEM((S, GLA_DVH), F32), pltpu.VMEM((S, GLA_DVH), F32),
                        pltpu.VMEM((2, GLA_DVH, GLA_DKH), F32),
                        pltpu.VMEM((2, 2, U, GLA_DVH, GLA_DKH), F32), pltpu.VMEM((2, 2, U, GLA_CHUNK, GLA_DKH), BF16),
                        pltpu.VMEM((2, 2, U, 1, GLA_DKH), F32)],
        compiler_params=_cparams(("parallel", "arbitrary"), vmem=VMEM_LIMIT_BIG),
        name="gla_mixer",
    )(x, norm1.reshape(1, D), wqk, wv, wog, wgd, gate_up_f, gate_bias_f.reshape(1, -1), gate_up_b,
      gate_bias_b.reshape(1, -1), out_norm.reshape(1, -1))


def gla_weights(w_in):
    D = w_in.shape[0]
    o1, o2 = GLA_DK, 2 * GLA_DK
    o3 = o2 + GLA_DV
    o4 = o3 + GLA_GATE_RANK
    wgd = jnp.zeros((D, LANES), F32).at[:, :GLA_GATE_RANK].set(w_in[:, o3:o4])
    wq = w_in[:, :o1].reshape(D, GLA_HEADS, GLA_DKH)
    wk = w_in[:, o1:o2].reshape(D, GLA_HEADS, GLA_DKH)
    wqk = jnp.concatenate([wq, wk], axis=2).reshape(D, 2 * GLA_DK)
    return wqk.astype(BF16), w_in[:, o2:o3].astype(BF16), w_in[:, o4:].astype(BF16), wgd


def _router_kernel(y_ref, wo_ref, r_ref, g_ref, wr_ref, x_ref, hn_ref, idx_ref, gate_ref):
    x = r_ref[...] + jnp.dot(y_ref[...], wo_ref[...], preferred_element_type=F32)
    x_ref[...] = x
    ms = jnp.mean(x * x, axis=-1, keepdims=True)
    hn = x * lax.rsqrt(ms + NORM_EPS) * g_ref[...]
    hn_ref[...] = hn
    logits = _dot3(hn, wr_ref[...])
    lane = _iota2(logits.shape, 1)
    neg = jnp.float32(-jnp.inf)
    logits = jnp.where(lane < N_EXPERTS, logits, neg)
    m1 = jnp.max(logits, axis=-1, keepdims=True)
    i1 = jnp.min(jnp.where(logits == m1, lane, LANES), axis=-1, keepdims=True)
    rest = jnp.where(lane == i1, neg, logits)
    m2 = jnp.max(rest, axis=-1, keepdims=True)
    i2 = jnp.min(jnp.where(rest == m2, lane, LANES), axis=-1, keepdims=True)
    e2 = jnp.exp(m2 - m1)
    g1 = 1.0 / (1.0 + e2)
    g2 = e2 / (1.0 + e2)
    idx_ref[...] = jnp.where(lane == 0, i1, jnp.where(lane == 1, i2, 0))
    gate_ref[...] = jnp.where(lane == 0, g1, jnp.where(lane == 1, g2, 0.0))


def outproj_router(y, wo, r, g, router, *, tm=512):
    T, D = r.shape
    K = y.shape[1]
    wr = jnp.zeros((D, LANES), F32).at[:, :N_EXPERTS].set(router)
    row_blk = lambda w: pl.BlockSpec((tm, w), lambda i: (i, 0))
    return pl.pallas_call(
        _router_kernel,
        grid=(T // tm,),
        in_specs=[
            row_blk(K),
            pl.BlockSpec((K, D), lambda i: (0, 0)),
            row_blk(D),
            pl.BlockSpec((1, D), lambda i: (0, 0)),
            pl.BlockSpec((D, LANES), lambda i: (0, 0)),
        ],
        out_specs=[row_blk(D), row_blk(D), row_blk(LANES), row_blk(LANES)],
        out_shape=[jax.ShapeDtypeStruct((T, D), F32), jax.ShapeDtypeStruct((T, D), F32),
                   jax.ShapeDtypeStruct((T, LANES), jnp.int32), jax.ShapeDtypeStruct((T, LANES), F32)],
        compiler_params=_cparams(("parallel",)),
        name="outproj_router",
    )(y, wo, r, g.reshape(1, D), wr)


def _expert_kernel(te_ref, tv_ref, tok_ref, hn_hbm, wg_ref, wu_ref, wd_ref, o_ref, xg_ref, xb_ref, acc_ref, sem, *, nf):
    i = pl.program_id(0)
    f = pl.program_id(1)
    tm = xb_ref.shape[0]
    valid = tv_ref[i] > 0
    prev_valid = tv_ref[jnp.maximum(i - 1, 0)] > 0
    slot = i % 2

    def issue_row(tile, dst_slot, r):
        tok = tok_ref[tile * tm + r]
        pltpu.make_async_copy(hn_hbm.at[pl.ds(tok, 1), :], xg_ref.at[dst_slot, pl.ds(r, 1), :], sem.at[dst_slot]).start()

    @pl.when(jnp.logical_and(valid, jnp.logical_and(i == 0, f == 0)))
    def _():
        def body(r, c):
            issue_row(0, 0, r)
            return c

        lax.fori_loop(0, tm, body, 0, unroll=8)

    @pl.when(jnp.logical_and(f == 0, jnp.logical_or(valid, jnp.logical_and(i > 0, prev_valid))))
    def _():
        pltpu.make_async_copy(hn_hbm.at[pl.ds(0, tm), :], xg_ref.at[slot], sem.at[slot]).wait()

    @pl.when(jnp.logical_and(valid, f == 0))
    def _():
        xb_ref[...] = xg_ref[slot].astype(BF16)

    @pl.when(valid)
    def _():
        per = tm // nf
        row0 = f * per
        xb = xb_ref[...]
        a = jnp.dot(xb, wg_ref[0], preferred_element_type=F32)
        for j in range(per // 2):
            issue_row(i + 1, 1 - slot, row0 + j)
        b = jnp.dot(xb, wu_ref[0], preferred_element_type=F32)
        for j in range(per // 2, per):
            issue_row(i + 1, 1 - slot, row0 + j)
        h = (a * jax.nn.sigmoid(a) * b).astype(BF16)
        part = jnp.dot(h, wd_ref[0], preferred_element_type=F32)

        @pl.when(f == 0)
        def _():
            acc_ref[...] = part

        @pl.when(f > 0)
        def _():
            acc_ref[...] += part

    @pl.when(f == nf - 1)
    def _():
        o_ref[...] = jnp.where(valid, acc_ref[...], 0.0)


def moe_experts(hn, tile_expert, tile_valid, row_tok, wg, wu, wd, *, tm, tf):
    T, D = hn.shape
    F = wg.shape[2]
    P = row_tok.shape[0] - tm
    nt = P // tm + 1
    nf = F // tf

    def w_in_map(i, f, te, tv, tok):
        return (te[i], 0, jnp.where(tv[i] > 0, f, nf - 1))

    def w_out_map(i, f, te, tv, tok):
        return (te[i], jnp.where(tv[i] > 0, f, nf - 1), 0)

    grid_spec = pltpu.PrefetchScalarGridSpec(
        num_scalar_prefetch=3,
        grid=(nt, nf),
        in_specs=[
            pl.BlockSpec(memory_space=pl.ANY),
            pl.BlockSpec((1, D, tf), w_in_map),
            pl.BlockSpec((1, D, tf), w_in_map),
            pl.BlockSpec((1, tf, D), w_out_map),
        ],
        out_specs=pl.BlockSpec((tm, D), lambda i, f, te, tv, tok: (i, 0)),
        scratch_shapes=[pltpu.VMEM((2, tm, D), F32), pltpu.VMEM((tm, D), BF16), pltpu.VMEM((tm, D), F32),
                        pltpu.SemaphoreType.DMA((2,))],
    )
    return pl.pallas_call(
        functools.partial(_expert_kernel, nf=nf),
        grid_spec=grid_spec,
        out_shape=jax.ShapeDtypeStruct((P + tm, D), F32),
        compiler_params=_cparams(("arbitrary", "arbitrary")),
        name="moe_experts",
    )(tile_expert, tile_valid, row_tok, hn, wg, wu, wd)


def _combine_kernel(dest_ref, x_ref, gate_ref, yb_hbm, o_ref, buf_ref, sem):
    i = pl.program_id(0)
    n = pl.num_programs(0)
    tc = x_ref.shape[0]
    slot = i % 2

    def issue(step, dst_slot):
        base = step * tc * TOP_K

        for r in range(tc):
            for k in range(TOP_K):
                pltpu.make_async_copy(yb_hbm.at[pl.ds(dest_ref[base + r * TOP_K + k], 1), :],
                                      buf_ref.at[dst_slot, k, pl.ds(r, 1), :], sem.at[dst_slot]).start()

    @pl.when(i == 0)
    def _():
        issue(0, 0)

    for k in range(TOP_K):
        pltpu.make_async_copy(yb_hbm.at[pl.ds(0, tc), :], buf_ref.at[slot, k], sem.at[slot]).wait()

    @pl.when(i + 1 < n)
    def _():
        issue(i + 1, 1 - slot)

    g = gate_ref[...]
    o_ref[...] = x_ref[...] + g[:, 0:1] * buf_ref[slot, 0] + g[:, 1:2] * buf_ref[slot, 1]


def moe_combine(x, gates, yb, dest, *, tc=256):
    T, D = x.shape
    grid_spec = pltpu.PrefetchScalarGridSpec(
        num_scalar_prefetch=1,
        grid=(T // tc,),
        in_specs=[
            pl.BlockSpec((tc, D), lambda i, d: (i, 0)),
            pl.BlockSpec((tc, LANES), lambda i, d: (i, 0)),
            pl.BlockSpec(memory_space=pl.ANY),
        ],
        out_specs=pl.BlockSpec((tc, D), lambda i, d: (i, 0)),
        scratch_shapes=[pltpu.VMEM((2, TOP_K, tc, D), F32), pltpu.SemaphoreType.DMA((2,))],
    )
    return pl.pallas_call(
        _combine_kernel,
        grid_spec=grid_spec,
        out_shape=jax.ShapeDtypeStruct((T, D), F32),
        compiler_params=_cparams(("arbitrary",)),
        name="moe_combine",
    )(dest, x, gates, yb)


def moe_dispatch_plan(idx, *, tm):
    T = idx.shape[0]
    A = T * TOP_K
    e_flat = idx[:, :TOP_K].reshape(A)
    onehot = (e_flat[:, None] == jnp.arange(N_EXPERTS, dtype=jnp.int32)[None, :]).astype(jnp.int32)
    rank = jnp.sum((jnp.cumsum(onehot, axis=0) - onehot) * onehot, axis=1)
    counts = jnp.sum(onehot, axis=0)
    padded = (counts + tm - 1) // tm * tm
    ends = jnp.cumsum(padded)
    pstart = ends - padded
    dest = pstart[e_flat] + rank
    P = (A // tm + N_EXPERTS + 1) * tm
    nt = P // tm
    tok_flat = jnp.arange(A, dtype=jnp.int32) // TOP_K
    row_tok = jnp.zeros((P,), jnp.int32).at[dest].set(tok_flat)
    tile_start = jnp.arange(nt, dtype=jnp.int32) * tm
    tile_expert = jnp.minimum(jnp.sum((tile_start[:, None] >= ends[None, :]).astype(jnp.int32), axis=1), N_EXPERTS - 1)
    tile_valid = (tile_start < ends[-1]).astype(jnp.int32)
    last_valid = jnp.maximum(jnp.sum(tile_valid) - 1, 0)
    tile_expert = jnp.where(tile_valid > 0, tile_expert, tile_expert[last_valid])
    return dest.astype(jnp.int32), row_tok, tile_expert, tile_valid


def moe_layer(y, wo, r, norm2, router, wg, wu, wd, *, tm=512, tf=1792, tr=512, tc=256):
    x, hn, idx, gates = outproj_router(y, wo, r, norm2, router, tm=tr)
    dest, row_tok, tile_expert, tile_valid = moe_dispatch_plan(idx, tm=tm)
    yb = moe_experts(hn, tile_expert, tile_valid, row_tok, wg, wu, wd, tm=tm, tf=tf)
    return moe_combine(x, gates, yb, dest, tc=tc)


def _even_layer(x, norm1, w_in, shift_mu, w0_f, w_up_f, w0_b, w_up_b, a0_f, a_up_f, a0_b, a_up_b, g_up,
                k_k, k_a, r_k, gn_w, gn_b, q_norm, k_norm, w_out, norm2, ffn_gate, ffn_up, ffn_down, cast):
    Bn, S, D = x.shape
    T = Bn * S
    xf = x.reshape(T, D)
    w_in = w_in.astype(BF16)
    p_att = norm_matmul(xf, norm1, w_in[:, RW_IN:], tm=1024, tn=ATT_IN).reshape(Bn, S, ATT_IN)
    y_a = rwkv7_mixer(x, norm1, w_in[:, :RW_IN], shift_mu, w0_f, w_up_f, w0_b, w_up_b, a0_f, a_up_f, a0_b, a_up_b,
                      g_up, k_k, k_a, r_k, gn_w, gn_b, out_dtype=BF16)
    y_b = gqa_attention(p_att, q_norm, k_norm, out_dtype=BF16)
    w_out = w_out.astype(BF16)
    xf = matmul_residual([y_a.reshape(T, RW_DIM), y_b.reshape(T, ATT_DIM)], [w_out[:RW_DIM], w_out[RW_DIM:]], xf,
                         tm=1024, tn=1024)
    return ffn_swiglu(xf, norm2, ffn_gate.astype(BF16), ffn_up.astype(BF16), ffn_down.astype(BF16), tm=512, tf=1408,
                      cast=cast)


def _odd_layer(xf, Bn, S, norm1, w_in, gate_up_f, gate_bias_f, gate_up_b, gate_bias_b, out_norm, w_out,
               norm2, router, exp_gate, exp_up, exp_down):
    T, D = xf.shape
    o = gla_mixer(xf.reshape(Bn, S, D), norm1, *gla_weights(w_in), gate_up_f, gate_bias_f, gate_up_b, gate_bias_b,
                  out_norm, out_dtype=BF16)
    return moe_layer(o.reshape(T, GLA_DV), w_out.astype(BF16), xf, norm2, router, exp_gate, exp_up, exp_down)


def kernel(x, e_norm1, e_w_in, e_shift_mu, e_w0_f, e_w_up_f, e_w0_b, e_w_up_b, e_a0_f, e_a_up_f, e_a0_b, e_a_up_b, e_g_up, e_k_k, e_k_a, e_r_k, e_gn_w, e_gn_b, e_q_norm, e_k_norm, e_w_out, e_norm2, e_ffn_gate, e_ffn_up, e_ffn_down, o_norm1, o_w_in, o_gate_up_f, o_gate_bias_f, o_gate_up_b, o_gate_bias_b, o_out_norm, o_w_out, o_norm2, o_router, o_exp_gate, o_exp_up, o_exp_down):
    Bn, S, D = x.shape
    xf, (exp_gate, exp_up, exp_down) = _even_layer(
        x, e_norm1[0], e_w_in[0], e_shift_mu[0], e_w0_f[0], e_w_up_f[0], e_w0_b[0],
        e_w_up_b[0], e_a0_f[0], e_a_up_f[0], e_a0_b[0], e_a_up_b[0], e_g_up[0], e_k_k[0],
        e_k_a[0], e_r_k[0], e_gn_w[0], e_gn_b[0], e_q_norm[0], e_k_norm[0], e_w_out[0],
        e_norm2[0], e_ffn_gate[0], e_ffn_up[0], e_ffn_down[0], (o_exp_gate[0], o_exp_up[0], o_exp_down[0]))
    xf = _odd_layer(xf, Bn, S, o_norm1[0], o_w_in[0], o_gate_up_f[0], o_gate_bias_f[0], o_gate_up_b[0],
                    o_gate_bias_b[0], o_out_norm[0], o_w_out[0], o_norm2[0], o_router[0],
                    exp_gate, exp_up, exp_down)
    return xf.reshape(Bn, S, D)
```

```python
import functools

import jax
import jax.numpy as jnp
from jax import lax
from jax.experimental import pallas as pl
from jax.experimental.pallas import tpu as pltpu

F32 = jnp.float32
BF16 = jnp.bfloat16
HI = lax.Precision.HIGHEST

D_MODEL = 1024
GRID_W = 64
HEAD_DIM = 64
NORM_EPS = 1e-6
RW_HEADS = 8
RW_DIM = 512
DECAY_RANK = 64
ICLR_RANK = 64
GATE_RANK = 128
RWKV_GN_EPS = 64e-5
RW_IN = 3 * RW_DIM + DECAY_RANK + ICLR_RANK + GATE_RANK
RW_CHUNK = 64
ATT_HEADS = 8
ATT_KV_HEADS = 2
ATT_DIM = 512
ATT_KV_DIM = 128
ATT_IN = ATT_DIM + 2 * ATT_KV_DIM
ROPE_THETA = 10000.0
ATT_VT_ROWS = HEAD_DIM + 16
LOG2_E = 1.4426950408889634
ATT_LOCKSTEP = 2
GLA_HEADS = 4
GLA_DK = 512
GLA_DV = 1024
GLA_DKH = 128
GLA_DVH = 256
GLA_GATE_RANK = 16
GLA_GATE_NORM = 16.0
GLA_CHUNK = 64
N_EXPERTS = 8
TOP_K = 2
LANES = 128

VMEM_LIMIT = 48 * 1024 * 1024
VMEM_LIMIT_BIG = 56 * 1024 * 1024


def _cparams(sem, vmem=VMEM_LIMIT):
    return pltpu.CompilerParams(dimension_semantics=sem, vmem_limit_bytes=vmem)


def _iota2(shape, dim):
    return lax.broadcasted_iota(jnp.int32, shape, dim)


def _split(x, parts):
    out = []
    for _ in range(parts):
        t = x.astype(BF16)
        out.append(t)
        x = x - t.astype(F32)
    return out


def _dot_sel(x, m, parts=2):
    mb = m.astype(BF16)
    acc = None
    for t in _split(x, parts):
        d = jnp.dot(t, mb, preferred_element_type=F32)
        acc = d if acc is None else acc + d
    return acc


def _sel_dot(m, x, parts=3):
    mb = m.astype(BF16)
    acc = None
    for t in _split(x, parts):
        d = jnp.dot(mb, t, preferred_element_type=F32)
        acc = d if acc is None else acc + d
    return acc


def _dot3(a, b):
    ah, al = _split(a, 2)
    bh, bl = _split(b, 2)
    d = lambda u, v: jnp.dot(u, v, preferred_element_type=F32)
    return d(ah, bh) + (d(ah, bl) + d(al, bh))


def _bdot(a, b):
    return jnp.dot(a.astype(BF16), b.astype(BF16), preferred_element_type=F32)


def _bdot_nt(a, b):
    return lax.dot_general(a.astype(BF16), b.astype(BF16), (((1,), (1,)), ((), ())), preferred_element_type=F32)


def _bdot_tn(a, b):
    return lax.dot_general(a.astype(BF16), b.astype(BF16), (((0,), (0,)), ((), ())), preferred_element_type=F32)


def _norm_mm_kernel(x_ref, g_ref, w_ref, o_ref, xn_ref):
    @pl.when(pl.program_id(1) == 0)
    def _():
        x = x_ref[...]
        ms = jnp.mean(x * x, axis=-1, keepdims=True)
        xn_ref[...] = (x * lax.rsqrt(ms + NORM_EPS) * g_ref[...]).astype(xn_ref.dtype)

    if xn_ref.dtype == F32:
        o_ref[...] = _dot3(xn_ref[...], w_ref[...]).astype(o_ref.dtype)
    else:
        o_ref[...] = jnp.dot(xn_ref[...], w_ref[...], preferred_element_type=F32).astype(o_ref.dtype)


def norm_matmul(x, g, w, *, tm, tn, out_dtype=F32):
    M, K = x.shape
    N = w.shape[1]
    return pl.pallas_call(
        _norm_mm_kernel,
        grid=(M // tm, N // tn),
        in_specs=[
            pl.BlockSpec((tm, K), lambda i, j: (i, 0)),
            pl.BlockSpec((1, K), lambda i, j: (0, 0)),
            pl.BlockSpec((K, tn), lambda i, j: (0, j)),
        ],
        out_specs=pl.BlockSpec((tm, tn), lambda i, j: (i, j)),
        out_shape=jax.ShapeDtypeStruct((M, N), out_dtype),
        scratch_shapes=[pltpu.VMEM((tm, K), w.dtype)],
        compiler_params=_cparams(("parallel", "arbitrary")),
        name="norm_matmul",
    )(x, g.reshape(1, K), w)


def _mm_res_kernel(*refs):
    n = (len(refs) - 2) // 2
    r_ref, o_ref = refs[2 * n], refs[2 * n + 1]
    acc = r_ref[...]
    for y_ref, w_ref in zip(refs[:n], refs[n:2 * n]):
        acc = acc + jnp.dot(y_ref[...].astype(BF16), w_ref[...], preferred_element_type=F32)
    o_ref[...] = acc


def matmul_residual(ys, ws, r, *, tm, tn):
    M, N = r.shape
    y_specs = [pl.BlockSpec((tm, y.shape[1]), lambda i, j: (i, 0)) for y in ys]
    w_specs = [pl.BlockSpec((w.shape[0], tn), lambda i, j: (0, j)) for w in ws]
    return pl.pallas_call(
        _mm_res_kernel,
        grid=(M // tm, N // tn),
        in_specs=y_specs + w_specs + [pl.BlockSpec((tm, tn), lambda i, j: (i, j))],
        out_specs=pl.BlockSpec((tm, tn), lambda i, j: (i, j)),
        out_shape=jax.ShapeDtypeStruct((M, N), F32),
        compiler_params=_cparams(("parallel", "arbitrary")),
        name="matmul_residual",
    )(*ys, *ws, r)


def _ffn_kernel(*refs, n_cast):
    x_ref, g_ref, wg_ref, wu_ref, wd_ref = refs[:5]
    cast_in = refs[5:5 + n_cast]
    o_ref = refs[5 + n_cast]
    cast_out = refs[6 + n_cast:6 + 2 * n_cast]
    xn_ref, acc_ref = refs[6 + 2 * n_cast:]
    f = pl.program_id(1)

    @pl.when(f == 0)
    def _():
        x = x_ref[...]
        ms = jnp.mean(x * x, axis=-1, keepdims=True)
        xn_ref[...] = (x * lax.rsqrt(ms + NORM_EPS) * g_ref[...]).astype(BF16)

    for src, dst in zip(cast_in, cast_out):
        dst[...] = src[...].astype(BF16)

    xn = xn_ref[...]
    a = jnp.dot(xn, wg_ref[...], preferred_element_type=F32)
    b = jnp.dot(xn, wu_ref[...], preferred_element_type=F32)
    h = (a * jax.nn.sigmoid(a) * b).astype(BF16)
    part = jnp.dot(h, wd_ref[...], preferred_element_type=F32)

    @pl.when(f == 0)
    def _():
        acc_ref[...] = part

    @pl.when(f > 0)
    def _():
        acc_ref[...] += part

    @pl.when(f == pl.num_programs(1) - 1)
    def _():
        o_ref[...] = x_ref[...] + acc_ref[...]


def ffn_swiglu(x, g, wg, wu, wd, *, tm, tf, cast=()):
    M, D = x.shape
    F = wg.shape[1]
    nf = F // tf
    steps = (M // tm) * nf
    cast_specs, cast_shapes = [], []
    for w in cast:
        E, R, C = w.shape
        per = steps // E
        cast_specs.append(pl.BlockSpec((1, R // per, C), lambda i, f, per=per: ((i * nf + f) // per, (i * nf + f) % per, 0)))
        cast_shapes.append(jax.ShapeDtypeStruct(w.shape, BF16))
    outs = pl.pallas_call(
        functools.partial(_ffn_kernel, n_cast=len(cast)),
        grid=(M // tm, nf),
        in_specs=[
            pl.BlockSpec((tm, D), lambda i, f: (i, 0)),
            pl.BlockSpec((1, D), lambda i, f: (0, 0)),
            pl.BlockSpec((D, tf), lambda i, f: (0, f)),
            pl.BlockSpec((D, tf), lambda i, f: (0, f)),
            pl.BlockSpec((tf, D), lambda i, f: (f, 0)),
        ] + cast_specs,
        out_specs=[pl.BlockSpec((tm, D), lambda i, f: (i, 0))] + cast_specs,
        out_shape=[jax.ShapeDtypeStruct((M, D), F32)] + cast_shapes,
        scratch_shapes=[pltpu.VMEM((tm, D), BF16), pltpu.VMEM((tm, D), F32)],
        compiler_params=_cparams(("arbitrary", "arbitrary"), vmem=VMEM_LIMIT_BIG if cast else VMEM_LIMIT),
        name="ffn_swiglu",
    )(x, g.reshape(1, D), wg, wu, wd, *cast)
    return outs[0], tuple(outs[1:])


def _rwkv_prep_kernel(cur_ref, prev_ref, next_ref, n1_ref, win_ref, mu_ref, w0f_ref, wupf_ref, w0b_ref, wupb_ref,
                      a0f_ref, aupf_ref, a0b_ref, aupb_ref, gup_ref, kk_ref, hsum_ref,
                      r_ref, k_ref, v_ref, kkn_ref, g_ref, lwf_ref, lf_ref, asf_ref, lwb_ref, lb_ref, asb_ref):
    i = pl.program_id(1)
    nt = pl.num_programs(1)

    def project(xv):
        ms = jnp.mean(xv * xv, axis=-1, keepdims=True)
        xn = (xv * lax.rsqrt(ms + NORM_EPS) * n1_ref[...]).astype(BF16)
        return jnp.dot(xn, win_ref[...], preferred_element_type=F32)

    Tt = cur_ref.shape[1]
    pall = project(jnp.concatenate([cur_ref[0], prev_ref[0], next_ref[0]], axis=0))
    x = pall[:Tt]
    halo = pall[Tt:]
    row = _iota2(x.shape, 0)
    prev_row = jnp.where(i > 0, halo[7:8, :], 0.0)
    next_row = jnp.where(i < nt - 1, halo[8:9, :], 0.0)
    prev = jnp.where(row == 0, prev_row, pltpu.roll(x, 1, 0))
    nxt = jnp.where(row == Tt - 1, next_row, pltpu.roll(x, Tt - 1, 0))
    pm = x + (0.5 * (prev + nxt) - x) * mu_ref[...]
    o1 = 3 * RW_DIM
    o2 = o1 + DECAY_RANK
    o3 = o2 + ICLR_RANK
    r_ref[0] = pm[:, :RW_DIM]
    k = pm[:, RW_DIM:2 * RW_DIM]
    k_ref[0] = k
    v_ref[0] = pm[:, 2 * RW_DIM:o1]
    wd = jnp.tanh(pm[:, o1:o2])
    ad = pm[:, o2:o3]
    gin = jax.nn.sigmoid(pm[:, o3:])
    g_ref[0] = jnp.dot(gin.astype(BF16), gup_ref[...].astype(BF16), preferred_element_type=F32)
    kk = k * kk_ref[...]
    ss = _dot_sel(kk * kk, hsum_ref[...])
    kkn_ref[0] = kk / jnp.maximum(jnp.sqrt(ss), 1e-12)

    ci = _iota2((RW_CHUNK, RW_CHUNK), 0)
    cj = _iota2((RW_CHUNK, RW_CHUNK), 1)
    tril = (cj <= ci).astype(F32)
    triu = (cj >= ci).astype(F32)

    def direction(w0_ref, wup_ref, a0_ref, aup_ref, tri, lw_ref, l_ref, as_ref):
        z = w0_ref[...] + _dot3(wd, wup_ref[...])
        sp = jnp.maximum(-z, 0.0) + jnp.log(1.0 + jnp.exp(-jnp.abs(z)))
        lw = -jnp.exp(-sp - 0.5)
        lw_ref[0] = lw
        as_ref[0] = jax.nn.sigmoid(a0_ref[...] + _dot3(ad, aup_ref[...]))
        for c in range(Tt // RW_CHUNK):
            sl = slice(c * RW_CHUNK, (c + 1) * RW_CHUNK)
            l_ref[0, sl, :] = _sel_dot(tri, lw[sl, :])

    direction(w0f_ref, wupf_ref, a0f_ref, aupf_ref, tril, lwf_ref, lf_ref, asf_ref)
    direction(w0b_ref, wupb_ref, a0b_ref, aupb_ref, triu, lwb_ref, lb_ref, asb_ref)


def rwkv_prep(x, norm1, w_in, shift_mu, w0_f, w_up_f, w0_b, w_up_b, a0_f, a_up_f, a0_b, a_up_b, g_up, k_k, *, tt=256):
    B, S, D = x.shape
    nt = S // tt
    hsum = (jnp.arange(RW_DIM)[:, None] // HEAD_DIM == jnp.arange(RW_DIM)[None, :] // HEAD_DIM).astype(F32)
    row = lambda a: a.reshape(1, -1)
    full = lambda a: pl.BlockSpec(a.shape, lambda b, i: (0,) * a.ndim)
    params = [row(norm1), w_in, row(shift_mu), row(w0_f), w_up_f, row(w0_b), w_up_b, row(a0_f), a_up_f, row(a0_b), a_up_b,
              g_up, row(k_k), hsum]
    tb = tt // 8
    in_specs = [
        pl.BlockSpec((1, tt, D), lambda b, i: (b, i, 0)),
        pl.BlockSpec((1, 8, D), lambda b, i: (b, jnp.maximum(i * tb - 1, 0), 0)),
        pl.BlockSpec((1, 8, D), lambda b, i: (b, jnp.minimum((i + 1) * tb, S // 8 - 1), 0)),
    ] + [full(a) for a in params]
    out_spec = pl.BlockSpec((1, tt, RW_DIM), lambda b, i: (b, i, 0))
    out_sds = jax.ShapeDtypeStruct((B, S, RW_DIM), F32)
    return pl.pallas_call(
        _rwkv_prep_kernel,
        grid=(B, nt),
        in_specs=in_specs,
        out_specs=[out_spec] * 11,
        out_shape=[out_sds] * 11,
        compiler_params=_cparams(("parallel", "arbitrary")),
        name="rwkv_prep",
    )(x, x, x, *params)


def _bd(y, bd_mask):
    return jnp.where(bd_mask, jnp.concatenate([y, y], axis=0), jnp.zeros((), y.dtype))


def _chunk_terms(chains, out, bd_mask):
    C = chains[0]["r"].shape[0]
    n = len(chains)
    R = range(n)
    dot = lambda u, v: jnp.dot(u, v, preferred_element_type=F32)
    bd = lambda y: _bd(y, bd_mask)
    pre = []
    for ch in chains:
        r, k, v, kk, lw, L, asig, ka, rev = (ch[x] for x in ("r", "k", "v", "kk", "lw", "L", "asig", "ka", "rev"))
        a = -kk
        b = kk * asig
        kd = k * (1.0 + (asig - 1.0) * ka)
        Lp = L - lw
        Lr = Lp if rev else L
        Lend = L[0:1, :] if rev else L[C - 1:C, :]
        Lmid = L[C // 2:C // 2 + 1, :]
        einv = jnp.exp(Lmid - L)
        eend = jnp.exp(Lend - L)
        pre.append(dict(
            ar=jnp.concatenate([a * jnp.exp(Lp - Lmid), r * jnp.exp(Lr - Lmid)], axis=0).astype(BF16),
            bt=bd((b * einv).astype(BF16)), kt=bd((kd * einv).astype(BF16)),
            a0=bd((a * jnp.exp(Lp)).astype(BF16)), r0=r * jnp.exp(Lr),
            bh=(b * eend).astype(BF16), kh=(kd * eend).astype(BF16), vb=v.astype(BF16),
            dec=jnp.exp(Lend)))
    mk = [ch["masks"] for ch in chains]
    Ab = [_bdot_nt(pre[i]["ar"], pre[i]["bt"]) for i in R]
    Ak = [_bdot_nt(pre[i]["ar"], pre[i]["kt"]) for i in R]
    yield
    Aab = [jnp.where(mk[i][0], Ab[i][:C], 0.0) for i in R]
    Arb = [jnp.where(mk[i][1], Ab[i][C:], 0.0).astype(BF16) for i in R]
    AakArk = [jnp.concatenate([jnp.where(mk[i][0], Ak[i][:C], 0.0), jnp.where(mk[i][1], Ak[i][C:], 0.0)],
                              axis=0).astype(BF16) for i in R]
    AV = [dot(AakArk[i], bd(pre[i]["vb"])) for i in R]
    Xb = [jnp.where(mk[i][2], Aab[i], 0.0).astype(BF16) for i in R]
    X2b = [dot(Xb[i], bd(Xb[i])).astype(BF16) for i in R]
    yield
    T = [mk[i][6] + Xb[i].astype(F32) for i in R]
    T = [T[i] + dot(T[i].astype(BF16), bd(X2b[i])) for i in R]
    X4b = [dot(X2b[i], bd(X2b[i])).astype(BF16) for i in R]
    yield
    T = [T[i] + dot(T[i].astype(BF16), bd(X4b[i])) for i in R]
    yield
    for lvl in (3, 4, 5):
        Tb = [T[i].astype(BF16) for i in R]
        ET = [dot(jnp.where(mk[i][lvl], Aab[i], 0.0).astype(BF16), bd(Tb[i])).astype(BF16) for i in R]
        yield
        T = [T[i] + dot(Tb[i], bd(ET[i])) for i in R]
        yield
    Tb = [T[i].astype(BF16) for i in R]
    A0p = [dot(Tb[i], pre[i]["a0"]).astype(BF16) for i in R]
    Uv = [dot(Tb[i], bd(AV[i][:C].astype(BF16))).astype(BF16) for i in R]
    yield
    Rpp = [pre[i]["r0"] + dot(Arb[i], bd(A0p[i])) for i in R]
    Yv = [dot(Arb[i], bd(Uv[i])) + AV[i][C:] for i in R]
    eye2 = (_iota2(bd_mask.shape, 0) == _iota2(bd_mask.shape, 1)).astype(F32)
    P = [jnp.where(bd_mask, _bdot_tn(pre[i]["bh"], A0p[i]), 0.0) + eye2 * pre[i]["dec"] for i in R]
    Q = [jnp.where(bd_mask, _bdot_tn(pre[i]["bh"], Uv[i]) + _bdot_tn(pre[i]["kh"], pre[i]["vb"]), 0.0) for i in R]
    out.extend((Rpp[i], Yv[i], P[i], Q[i]) for i in R)


def _make_masks(C, N, rev):
    ii = _iota2((C, 2 * N), 0)
    jj = _iota2((C, 2 * N), 1) % N
    strict = (jj > ii) if rev else (jj < ii)
    rmask = strict if rev else (jj <= ii)
    blk = lambda s: (ii // s) == (jj // s)
    m8 = blk(8)
    e16 = blk(16) & jnp.logical_not(blk(8))
    e32 = blk(32) & jnp.logical_not(blk(16))
    e64 = jnp.logical_not(blk(32))
    eye = (ii == jj).astype(F32)
    return strict, rmask, m8, e16, e32, e64, eye


RW_UNROLL = 4


def _rwkv_scan_kernel(r_ref, k_ref, v_ref, kk_ref, g_ref, lwf_ref, lf_ref, asf_ref, lwb_ref, lb_ref, asb_ref,
                      ka_ref, rk_ref, gnw_ref, gnb_ref, havg_ref, o_ref,
                      yf_ref, yb_ref, h_ref, rpp_ref, yv_ref, p_ref, q_ref):
    S = r_ref.shape[1]
    C = RW_CHUNK
    N = HEAD_DIM
    W = 2 * N
    n = S // C
    U = RW_UNROLL
    nset = n // U
    masks_f = _make_masks(C, N, False)
    masks_b = _make_masks(C, N, True)
    bd_mask = (_iota2((W, W), 0) // N) == (_iota2((W, W), 1) // N)
    h_ref[...] = jnp.zeros_like(h_ref)
    dirs = ((False, lwf_ref, lf_ref, asf_ref, yf_ref, masks_f), (True, lwb_ref, lb_ref, asb_ref, yb_ref, masks_b))

    def chunk_rows(rev, st, u):
        c = st * U + u
        c = (n - 1 - c) if rev else c
        return pl.ds(pl.multiple_of(c * C, C), C)

    def state_step(st, u):
        slot = st % 2
        for di, (rev, lw_ref, l_ref, as_ref, y_ref, masks) in enumerate(dirs):
            Hb = h_ref[di].astype(BF16)
            y_ref[chunk_rows(rev, st, u), :] = (jnp.dot(rpp_ref[slot, di, u], Hb, preferred_element_type=F32)
                                                + yv_ref[slot, di, u])
            h_ref[di] = jnp.dot(p_ref[slot, di, u], Hb, preferred_element_type=F32) + q_ref[slot, di, u]

    def chunk_set(st, state_of):
        chains, slots = [], []
        for u in range(U):
            for di, (rev, lw_ref, l_ref, as_ref, y_ref, masks) in enumerate(dirs):
                rows = chunk_rows(rev, st, u)
                chains.append(dict(
                    r=r_ref[0, rows, :], k=k_ref[0, rows, :], v=v_ref[0, rows, :], kk=kk_ref[0, rows, :],
                    lw=lw_ref[0, rows, :], L=l_ref[0, rows, :], asig=as_ref[0, rows, :], ka=ka_ref[...],
                    rev=rev, masks=masks))
                slots.append((di, u))
        res = []
        pending = list(range(U)) if state_of is not None else []
        for stage, _ in enumerate(_chunk_terms(chains, res, bd_mask)):
            if pending and stage % 2 == 1:
                state_step(state_of, pending.pop(0))
        for u in pending:
            state_step(state_of, u)
        slot = st % 2
        for (di, u), (Rpp, Yv, P, Q) in zip(slots, res):
            rpp_ref[slot, di, u] = Rpp.astype(BF16)
            yv_ref[slot, di, u] = Yv
            p_ref[slot, di, u] = P.astype(BF16)
            q_ref[slot, di, u] = Q

    chunk_set(0, None)

    def body(st, carry):
        chunk_set(st, st - 1)
        return carry

    lax.fori_loop(1, nset, body, 0)
    for u in range(U):
        state_step(nset - 1, u)

    RT = 256

    def fin(i, carry):
        rows = pl.ds(pl.multiple_of(i * RT, RT), RT)
        y = yf_ref[rows, :] + yb_ref[rows, :]
        mean = _dot_sel(y, havg_ref[...])
        d = y - mean
        var = _dot_sel(d * d, havg_ref[...])
        yn = d * lax.rsqrt(var + RWKV_GN_EPS) * gnw_ref[...] + gnb_ref[...]
        r = r_ref[0, rows, :]
        kf = k_ref[0, rows, :] * (1.0 + (asf_ref[0, rows, :] - 1.0) * ka_ref[...])
        bonus = _dot_sel(r * kf * rk_ref[...], havg_ref[...]) * float(N) * v_ref[0, rows, :]
        o_ref[0, rows, :] = ((yn + bonus) * g_ref[0, rows, :]).astype(o_ref.dtype)
        return carry

    lax.fori_loop(0, S // RT, fin, 0)


def rwkv_scan(r, k, v, kk, g, lw_f, l_f, as_f, lw_b, l_b, as_b, k_a, r_k, gn_w, gn_b, out_dtype=F32):
    B, S, _ = r.shape
    W = 2 * HEAD_DIM
    N = HEAD_DIM
    havg = (jnp.arange(W)[:, None] // HEAD_DIM == jnp.arange(W)[None, :] // HEAD_DIM).astype(F32) / HEAD_DIM
    seq = pl.BlockSpec((1, S, W), lambda b, h: (b, 0, h))
    par = pl.BlockSpec((1, W), lambda b, h: (0, h))
    row = lambda a: a.reshape(1, -1)
    U = RW_UNROLL
    return pl.pallas_call(
        _rwkv_scan_kernel,
        grid=(B, RW_DIM // W),
        in_specs=[seq] * 11 + [par] * 4 + [pl.BlockSpec((W, W), lambda b, h: (0, 0))],
        out_specs=seq,
        out_shape=jax.ShapeDtypeStruct((B, S, RW_DIM), out_dtype),
        scratch_shapes=[pltpu.VMEM((S, W), F32), pltpu.VMEM((S, W), F32), pltpu.VMEM((2, W, W), F32),
                        pltpu.VMEM((2, 2, U, RW_CHUNK, W), BF16), pltpu.VMEM((2, 2, U, RW_CHUNK, W), F32),
                        pltpu.VMEM((2, 2, U, W, W), BF16), pltpu.VMEM((2, 2, U, W, W), F32)],
        compiler_params=_cparams(("parallel", "parallel")),
        name="rwkv_scan",
    )(r, k, v, kk, g, lw_f, l_f, as_f, lw_b, l_b, as_b, row(k_a), row(r_k), row(gn_w), row(gn_b), havg)


def rwkv7_mixer(x, norm1, w_in, shift_mu, w0_f, w_up_f, w0_b, w_up_b, a0_f, a_up_f, a0_b, a_up_b, g_up, k_k, k_a,
                r_k, gn_w, gn_b, out_dtype=F32, tt=512):
    outs = rwkv_prep(x, norm1, w_in, shift_mu, w0_f, w_up_f, w0_b, w_up_b, a0_f, a_up_f, a0_b, a_up_b, g_up, k_k, tt=tt)
    return rwkv_scan(*outs, k_a, r_k, gn_w, gn_b, out_dtype=out_dtype)


def _head_norm_rope(x, gain, cos, sin_signed, havg):
    ms = _dot_sel(x * x, havg)
    xn = x * lax.rsqrt(ms + NORM_EPS) * gain
    W = x.shape[1]
    even = (_iota2(x.shape, 1) % 2) == 0
    partner = jnp.where(even, pltpu.roll(xn, W - 1, 1), pltpu.roll(xn, 1, 1))
    return xn * cos + partner * sin_signed


def _attn_kernel(q_ref, k_ref, v_ref, cosq_ref, sinq_ref, cosk_ref, sink_ref, qg_ref, kg_ref, hq_ref, hk_ref,
                 o_ref, ks_ref, vt_ref):
    i = pl.program_id(1)
    G = ATT_HEADS // ATT_KV_HEADS
    D = HEAD_DIM

    @pl.when(i == 0)
    def _():
        kr = _head_norm_rope(k_ref[0], kg_ref[...], cosk_ref[...], sink_ref[...], hk_ref[...])
        vt = jnp.transpose(v_ref[0])
        ones = jnp.ones((ATT_VT_ROWS - D, vt.shape[1]), BF16)
        for kv in range(ATT_KV_HEADS):
            ks_ref[kv] = kr[:, kv * D:(kv + 1) * D].astype(BF16)
            vt_ref[kv] = jnp.concatenate([vt[kv * D:(kv + 1) * D, :].astype(BF16), ones], axis=0)

    q = _head_norm_rope(q_ref[0], qg_ref[...], cosq_ref[...], sinq_ref[...], hq_ref[...]) * (D ** -0.5 * LOG2_E)
    qb = q.astype(BF16)
    outs = []
    for h0 in range(0, ATT_HEADS, ATT_LOCKSTEP):
        hs = range(h0, h0 + ATT_LOCKSTEP)
        st = [lax.dot_general(ks_ref[h // G], qb[:, h * D:(h + 1) * D], (((1,), (1,)), ((), ())),
                              preferred_element_type=F32) for h in hs]
        m = [jnp.max(x, axis=0, keepdims=True) for x in st]
        p = [jnp.exp2(x - mm).astype(BF16) for x, mm in zip(st, m)]
        ot = [jnp.dot(vt_ref[h // G], x, preferred_element_type=F32) for h, x in zip(hs, p)]
        outs.extend(o[:D] / o[D:D + 1] for o in ot)
    pairs = [jnp.transpose(jnp.concatenate(outs[2 * j:2 * j + 2], axis=0)) for j in range(ATT_HEADS // 2)]
    o_ref[0] = jnp.concatenate(pairs, axis=1).astype(o_ref.dtype)


def _rope_tables(S):
    rows = S // GRID_W
    row = jnp.repeat(jnp.arange(rows), GRID_W).astype(F32)
    col = jnp.tile(jnp.arange(GRID_W), rows).astype(F32)
    half = HEAD_DIM // 2
    freq = ROPE_THETA ** (-jnp.arange(0, half, 2, dtype=F32) / half)
    ang = jnp.concatenate([row[:, None] * freq, col[:, None] * freq], axis=-1)
    cos = jnp.repeat(jnp.cos(ang), 2, axis=-1)
    sin = jnp.repeat(jnp.sin(ang), 2, axis=-1)
    sign = jnp.where(jnp.arange(HEAD_DIM) % 2 == 0, -1.0, 1.0).astype(F32)
    return cos, sin * sign


def gqa_attention(p, q_norm, k_norm, *, tq=512, out_dtype=F32):
    B, S, _ = p.shape
    cos, sin = _rope_tables(S)
    tile = lambda t, n: jnp.tile(t, (1, n))
    hq = (jnp.arange(ATT_DIM)[:, None] // HEAD_DIM == jnp.arange(ATT_DIM)[None, :] // HEAD_DIM).astype(F32) / HEAD_DIM
    hk = hq[:ATT_KV_DIM, :ATT_KV_DIM]
    qg = jnp.tile(q_norm, ATT_HEADS).reshape(1, ATT_DIM)
    kg = jnp.tile(k_norm, ATT_KV_HEADS).reshape(1, ATT_KV_DIM)
    nq = ATT_DIM // ATT_KV_DIM
    const = lambda a: pl.BlockSpec(a.shape, lambda b, i: (0, 0))
    return pl.pallas_call(
        _attn_kernel,
        grid=(B, S // tq),
        in_specs=[
            pl.BlockSpec((1, tq, ATT_DIM), lambda b, i: (b, i, 0)),
            pl.BlockSpec((1, S, ATT_KV_DIM), lambda b, i: (b, 0, nq)),
            pl.BlockSpec((1, S, ATT_KV_DIM), lambda b, i: (b, 0, nq + 1)),
            pl.BlockSpec((tq, ATT_DIM), lambda b, i: (i, 0)),
            pl.BlockSpec((tq, ATT_DIM), lambda b, i: (i, 0)),
            pl.BlockSpec((S, ATT_KV_DIM), lambda b, i: (0, 0)),
            pl.BlockSpec((S, ATT_KV_DIM), lambda b, i: (0, 0)),
            const(qg), const(kg), const(hq), const(hk),
        ],
        out_specs=pl.BlockSpec((1, tq, ATT_DIM), lambda b, i: (b, i, 0)),
        out_shape=jax.ShapeDtypeStruct((B, S, ATT_DIM), out_dtype),
        scratch_shapes=[pltpu.VMEM((ATT_KV_HEADS, S, HEAD_DIM), BF16), pltpu.VMEM((ATT_KV_HEADS, ATT_VT_ROWS, S), BF16)],
        compiler_params=_cparams(("parallel", "arbitrary")),
        name="gqa_attention",
    )(p, p, p, tile(cos, ATT_HEADS), tile(sin, ATT_HEADS), tile(cos, ATT_KV_HEADS), tile(sin, ATT_KV_HEADS),
      qg, kg, hq, hk)


GLA_UNROLL = 8


def _gla_kernel(q_ref, k_ref, v_ref, gd_ref, og_ref, upf_ref, bf_ref, upb_ref, bb_ref, on_ref, o_ref,
                yf_ref, yb_ref, st_ref, kv_ref, qe_ref, dec_ref):
    S = q_ref.shape[1]
    C = GLA_CHUNK
    n = S // C
    U = GLA_UNROLL
    nset = n // U
    ii = _iota2((C, C), 0)
    jj = _iota2((C, C), 1)
    dirs = (
        (False, (jj <= ii).astype(F32), jj <= ii, upf_ref, bf_ref, yf_ref),
        (True, (jj >= ii).astype(F32), jj > ii, upb_ref, bb_ref, yb_ref),
    )
    scale = GLA_DKH ** -0.5
    st_ref[...] = jnp.zeros_like(st_ref)

    def chunk_rows(rev, st, u):
        c = st * U + u
        c = (n - 1 - c) if rev else c
        return pl.ds(pl.multiple_of(c * C, C), C)

    def state_step(st, u):
        slot = st % 2
        for di, (rev, tri, mask, up_ref, b_ref, y_ref) in enumerate(dirs):
            state = st_ref[di]
            rows = chunk_rows(rev, st, u)
            y_ref[rows, :] += lax.dot_general(qe_ref[slot, di, u], state.astype(BF16), (((1,), (1,)), ((), ())),
                                              preferred_element_type=F32)
            st_ref[di] = state * dec_ref[slot, di, u] + kv_ref[slot, di, u]

    def chunk_set(st, state_of):
        pending = list(range(U)) if state_of is not None else []
        items = []
        for u in range(U):
            for di, (rev, tri, mask, up_ref, b_ref, y_ref) in enumerate(dirs):
                rows = chunk_rows(rev, st, u)
                z = _dot3(gd_ref[0, rows, 0:GLA_GATE_RANK], up_ref[...]) + b_ref[...]
                g = (jnp.minimum(z, 0.0) - jnp.log(1.0 + jnp.exp(-jnp.abs(z)))) * (1.0 / GLA_GATE_NORM)
                items.append(dict(u=u, di=di, rows=rows, rev=rev, mask=mask, y_ref=y_ref, g=g, tri=tri))
        for x in items:
            x["b"] = _sel_dot(x["tri"], x["g"])
        if pending:
            state_step(state_of, pending.pop(0))
        for x in items:
            b = x["b"]
            q = q_ref[0, x["rows"], :] * scale
            k = k_ref[0, x["rows"], :]
            b_mid = b[C // 2:C // 2 + 1, :]
            b_last = b[0:1, :] if x["rev"] else b[C - 1:C, :]
            x["vb"] = v_ref[0, x["rows"], :].astype(BF16)
            x["qm"] = (q * jnp.exp(b - b_mid)).astype(BF16)
            x["km"] = (k * jnp.exp(b_mid - b)).astype(BF16)
            x["ke"] = (k * jnp.exp(b_last - b)).astype(BF16)
            x["qe"] = (q * jnp.exp(b)).astype(BF16)
            x["dec"] = jnp.exp(b_last)
        att = [lax.dot_general(x["qm"], x["km"], (((1,), (1,)), ((), ())), preferred_element_type=F32) for x in items]
        if pending:
            state_step(state_of, pending.pop(0))
        att = [jnp.where(x["mask"], a, 0.0).astype(BF16) for x, a in zip(items, att)]
        slot = st % 2
        for idx, (x, a) in enumerate(zip(items, att)):
            x["y_ref"][x["rows"], :] = jnp.dot(a, x["vb"], preferred_element_type=F32)
            kv_ref[slot, x["di"], x["u"]] = lax.dot_general(x["vb"], x["ke"], (((0,), (0,)), ((), ())),
                                                            preferred_element_type=F32)
            qe_ref[slot, x["di"], x["u"]] = x["qe"]
            dec_ref[slot, x["di"], x["u"]] = x["dec"]
            if pending and idx % 3 == 2:
                state_step(state_of, pending.pop(0))
        for u in pending:
            state_step(state_of, u)

    chunk_set(0, None)

    def body(st, carry):
        chunk_set(st, st - 1)
        return carry

    lax.fori_loop(1, nset, body, 0)
    for u in range(U):
        state_step(nset - 1, u)

    RT = 256

    def fin(i, carry):
        rows = pl.ds(pl.multiple_of(i * RT, RT), RT)
        o = yf_ref[rows, :] + yb_ref[rows, :]
        ms = jnp.mean(o * o, axis=-1, keepdims=True)
        on = o * lax.rsqrt(ms + NORM_EPS) * on_ref[...]
        og = og_ref[0, rows, :]
        o_ref[0, rows, :] = (on * (og * jax.nn.sigmoid(og))).astype(o_ref.dtype)
        return carry

    lax.fori_loop(0, S // RT, fin, 0)


def gla_mixer(pm, gd, gate_up_f, gate_bias_f, gate_up_b, gate_bias_b, out_norm, out_dtype=F32):
    B, S, _ = pm.shape
    U = GLA_UNROLL
    H = GLA_HEADS
    kb = GLA_DK // GLA_DKH
    vb0 = 2 * GLA_DK // GLA_DVH
    ob0 = vb0 + GLA_DV // GLA_DVH
    return pl.pallas_call(
        _gla_kernel,
        grid=(B, H),
        in_specs=[
            pl.BlockSpec((1, S, GLA_DKH), lambda b, h: (b, 0, h)),
            pl.BlockSpec((1, S, GLA_DKH), lambda b, h: (b, 0, kb + h)),
            pl.BlockSpec((1, S, GLA_DVH), lambda b, h: (b, 0, vb0 + h)),
            pl.BlockSpec((1, S, LANES), lambda b, h: (b, 0, 0)),
            pl.BlockSpec((1, S, GLA_DVH), lambda b, h: (b, 0, ob0 + h)),
            pl.BlockSpec((GLA_GATE_RANK, GLA_DKH), lambda b, h: (0, h)),
            pl.BlockSpec((1, GLA_DKH), lambda b, h: (0, h)),
            pl.BlockSpec((GLA_GATE_RANK, GLA_DKH), lambda b, h: (0, h)),
            pl.BlockSpec((1, GLA_DKH), lambda b, h: (0, h)),
            pl.BlockSpec((1, GLA_DVH), lambda b, h: (0, 0)),
        ],
        out_specs=pl.BlockSpec((1, S, GLA_DVH), lambda b, h: (b, 0, h)),
        out_shape=jax.ShapeDtypeStruct((B, S, GLA_DV), out_dtype),
        scratch_shapes=[pltpu.VMEM((S, GLA_DVH), F32), pltpu.VMEM((S, GLA_DVH), F32),
                        pltpu.VMEM((2, GLA_DVH, GLA_DKH), F32),
                        pltpu.VMEM((2, 2, U, GLA_DVH, GLA_DKH), F32), pltpu.VMEM((2, 2, U, GLA_CHUNK, GLA_DKH), BF16),
                        pltpu.VMEM((2, 2, U, 1, GLA_DKH), F32)],
        compiler_params=_cparams(("parallel", "parallel")),
        name="gla_mixer",
    )(pm, pm, pm, gd, pm, gate_up_f, gate_bias_f.reshape(1, -1), gate_up_b, gate_bias_b.reshape(1, -1),
      out_norm.reshape(1, -1))


def _router_kernel(x_ref, g_ref, wr_ref, hn_ref, idx_ref, gate_ref):
    x = x_ref[...]
    ms = jnp.mean(x * x, axis=-1, keepdims=True)
    hn = x * lax.rsqrt(ms + NORM_EPS) * g_ref[...]
    hn_ref[...] = hn
    logits = _dot3(hn, wr_ref[...])
    lane = _iota2(logits.shape, 1)
    neg = jnp.float32(-jnp.inf)
    logits = jnp.where(lane < N_EXPERTS, logits, neg)
    m1 = jnp.max(logits, axis=-1, keepdims=True)
    i1 = jnp.min(jnp.where(logits == m1, lane, LANES), axis=-1, keepdims=True)
    rest = jnp.where(lane == i1, neg, logits)
    m2 = jnp.max(rest, axis=-1, keepdims=True)
    i2 = jnp.min(jnp.where(rest == m2, lane, LANES), axis=-1, keepdims=True)
    e2 = jnp.exp(m2 - m1)
    g1 = 1.0 / (1.0 + e2)
    g2 = e2 / (1.0 + e2)
    idx_ref[...] = jnp.where(lane == 0, i1, jnp.where(lane == 1, i2, 0))
    gate_ref[...] = jnp.where(lane == 0, g1, jnp.where(lane == 1, g2, 0.0))


def moe_router(x, g, router, *, tm=512):
    T, D = x.shape
    wr = jnp.zeros((D, LANES), F32).at[:, :N_EXPERTS].set(router)
    return pl.pallas_call(
        _router_kernel,
        grid=(T // tm,),
        in_specs=[
            pl.BlockSpec((tm, D), lambda i: (i, 0)),
            pl.BlockSpec((1, D), lambda i: (0, 0)),
            pl.BlockSpec((D, LANES), lambda i: (0, 0)),
        ],
        out_specs=[
            pl.BlockSpec((tm, D), lambda i: (i, 0)),
            pl.BlockSpec((tm, LANES), lambda i: (i, 0)),
            pl.BlockSpec((tm, LANES), lambda i: (i, 0)),
        ],
        out_shape=[jax.ShapeDtypeStruct((T, D), F32), jax.ShapeDtypeStruct((T, LANES), jnp.int32),
                   jax.ShapeDtypeStruct((T, LANES), F32)],
        compiler_params=_cparams(("parallel",)),
        name="moe_router",
    )(x, g.reshape(1, D), wr)


def _expert_kernel(te_ref, tv_ref, tok_ref, hn_hbm, wg_ref, wu_ref, wd_ref, o_ref, xg_ref, xb_ref, acc_ref, sem, *, nf):
    i = pl.program_id(0)
    f = pl.program_id(1)
    tm = xb_ref.shape[0]
    valid = tv_ref[i] > 0
    prev_valid = tv_ref[jnp.maximum(i - 1, 0)] > 0
    slot = i % 2

    def issue_row(tile, dst_slot, r):
        tok = tok_ref[tile * tm + r]
        pltpu.make_async_copy(hn_hbm.at[pl.ds(tok, 1), :], xg_ref.at[dst_slot, pl.ds(r, 1), :], sem.at[dst_slot]).start()

    @pl.when(jnp.logical_and(valid, jnp.logical_and(i == 0, f == 0)))
    def _():
        def body(r, c):
            issue_row(0, 0, r)
            return c

        lax.fori_loop(0, tm, body, 0, unroll=8)

    @pl.when(jnp.logical_and(f == 0, jnp.logical_or(valid, jnp.logical_and(i > 0, prev_valid))))
    def _():
        pltpu.make_async_copy(hn_hbm.at[pl.ds(0, tm), :], xg_ref.at[slot], sem.at[slot]).wait()

    @pl.when(jnp.logical_and(valid, f == 0))
    def _():
        xb_ref[...] = xg_ref[slot].astype(BF16)

    @pl.when(valid)
    def _():
        per = tm // nf
        row0 = f * per
        xb = xb_ref[...]
        a = jnp.dot(xb, wg_ref[0], preferred_element_type=F32)
        for j in range(per // 2):
            issue_row(i + 1, 1 - slot, row0 + j)
        b = jnp.dot(xb, wu_ref[0], preferred_element_type=F32)
        for j in range(per // 2, per):
            issue_row(i + 1, 1 - slot, row0 + j)
        h = (a * jax.nn.sigmoid(a) * b).astype(BF16)
        part = jnp.dot(h, wd_ref[0], preferred_element_type=F32)

        @pl.when(f == 0)
        def _():
            acc_ref[...] = part

        @pl.when(f > 0)
        def _():
            acc_ref[...] += part

    @pl.when(f == nf - 1)
    def _():
        o_ref[...] = jnp.where(valid, acc_ref[...], 0.0)


def moe_experts(hn, tile_expert, tile_valid, row_tok, wg, wu, wd, *, tm, tf):
    T, D = hn.shape
    F = wg.shape[2]
    P = row_tok.shape[0] - tm
    nt = P // tm + 1
    nf = F // tf

    def w_in_map(i, f, te, tv, tok):
        return (te[i], 0, jnp.where(tv[i] > 0, f, nf - 1))

    def w_out_map(i, f, te, tv, tok):
        return (te[i], jnp.where(tv[i] > 0, f, nf - 1), 0)

    grid_spec = pltpu.PrefetchScalarGridSpec(
        num_scalar_prefetch=3,
        grid=(nt, nf),
        in_specs=[
            pl.BlockSpec(memory_space=pl.ANY),
            pl.BlockSpec((1, D, tf), w_in_map),
            pl.BlockSpec((1, D, tf), w_in_map),
            pl.BlockSpec((1, tf, D), w_out_map),
        ],
        out_specs=pl.BlockSpec((tm, D), lambda i, f, te, tv, tok: (i, 0)),
        scratch_shapes=[pltpu.VMEM((2, tm, D), F32), pltpu.VMEM((tm, D), BF16), pltpu.VMEM((tm, D), F32),
                        pltpu.SemaphoreType.DMA((2,))],
    )
    return pl.pallas_call(
        functools.partial(_expert_kernel, nf=nf),
        grid_spec=grid_spec,
        out_shape=jax.ShapeDtypeStruct((P + tm, D), F32),
        compiler_params=_cparams(("arbitrary", "arbitrary")),
        name="moe_experts",
    )(tile_expert, tile_valid, row_tok, hn, wg, wu, wd)


def _combine_kernel(dest_ref, x_ref, gate_ref, yb_hbm, o_ref, buf_ref, sem):
    i = pl.program_id(0)
    n = pl.num_programs(0)
    tc = x_ref.shape[0]
    slot = i % 2

    def issue(step, dst_slot):
        base = step * tc * TOP_K

        for r in range(tc):
            for k in range(TOP_K):
                pltpu.make_async_copy(yb_hbm.at[pl.ds(dest_ref[base + r * TOP_K + k], 1), :],
                                      buf_ref.at[dst_slot, k, pl.ds(r, 1), :], sem.at[dst_slot]).start()

    @pl.when(i == 0)
    def _():
        issue(0, 0)

    for k in range(TOP_K):
        pltpu.make_async_copy(yb_hbm.at[pl.ds(0, tc), :], buf_ref.at[slot, k], sem.at[slot]).wait()

    @pl.when(i + 1 < n)
    def _():
        issue(i + 1, 1 - slot)

    g = gate_ref[...]
    o_ref[...] = x_ref[...] + g[:, 0:1] * buf_ref[slot, 0] + g[:, 1:2] * buf_ref[slot, 1]


def moe_combine(x, gates, yb, dest, *, tc=256):
    T, D = x.shape
    grid_spec = pltpu.PrefetchScalarGridSpec(
        num_scalar_prefetch=1,
        grid=(T // tc,),
        in_specs=[
            pl.BlockSpec((tc, D), lambda i, d: (i, 0)),
            pl.BlockSpec((tc, LANES), lambda i, d: (i, 0)),
            pl.BlockSpec(memory_space=pl.ANY),
        ],
        out_specs=pl.BlockSpec((tc, D), lambda i, d: (i, 0)),
        scratch_shapes=[pltpu.VMEM((2, TOP_K, tc, D), F32), pltpu.SemaphoreType.DMA((2,))],
    )
    return pl.pallas_call(
        _combine_kernel,
        grid_spec=grid_spec,
        out_shape=jax.ShapeDtypeStruct((T, D), F32),
        compiler_params=_cparams(("arbitrary",)),
        name="moe_combine",
    )(dest, x, gates, yb)


def moe_dispatch_plan(idx, *, tm):
    T = idx.shape[0]
    A = T * TOP_K
    e_flat = idx[:, :TOP_K].reshape(A)
    onehot = (e_flat[:, None] == jnp.arange(N_EXPERTS, dtype=jnp.int32)[None, :]).astype(jnp.int32)
    rank = jnp.sum((jnp.cumsum(onehot, axis=0) - onehot) * onehot, axis=1)
    counts = jnp.sum(onehot, axis=0)
    padded = (counts + tm - 1) // tm * tm
    ends = jnp.cumsum(padded)
    pstart = ends - padded
    dest = pstart[e_flat] + rank
    P = (A // tm + N_EXPERTS + 1) * tm
    nt = P // tm
    tok_flat = jnp.arange(A, dtype=jnp.int32) // TOP_K
    row_tok = jnp.zeros((P,), jnp.int32).at[dest].set(tok_flat, unique_indices=True)
    tile_start = jnp.arange(nt, dtype=jnp.int32) * tm
    tile_expert = jnp.minimum(jnp.sum((tile_start[:, None] >= ends[None, :]).astype(jnp.int32), axis=1), N_EXPERTS - 1)
    tile_valid = (tile_start < ends[-1]).astype(jnp.int32)
    last_valid = jnp.maximum(jnp.sum(tile_valid) - 1, 0)
    tile_expert = jnp.where(tile_valid > 0, tile_expert, tile_expert[last_valid])
    return dest.astype(jnp.int32), row_tok, tile_expert, tile_valid


def moe_layer(x, norm2, router, wg, wu, wd, *, tm=512, tf=1792, tr=512, tc=256):
    hn, idx, gates = moe_router(x, norm2, router, tm=tr)
    dest, row_tok, tile_expert, tile_valid = moe_dispatch_plan(idx, tm=tm)
    yb = moe_experts(hn, tile_expert, tile_valid, row_tok, wg, wu, wd, tm=tm, tf=tf)
    return moe_combine(x, gates, yb, dest, tc=tc)


def _even_layer(x, norm1, w_in, shift_mu, w0_f, w_up_f, w0_b, w_up_b, a0_f, a_up_f, a0_b, a_up_b, g_up,
                k_k, k_a, r_k, gn_w, gn_b, q_norm, k_norm, w_out, norm2, ffn_gate, ffn_up, ffn_down, cast):
    Bn, S, D = x.shape
    T = Bn * S
    xf = x.reshape(T, D)
    w_in = w_in.astype(BF16)
    p_att = norm_matmul(xf, norm1, w_in[:, RW_IN:], tm=2048, tn=ATT_IN).reshape(Bn, S, ATT_IN)
    y_a = rwkv7_mixer(x, norm1, w_in[:, :RW_IN], shift_mu, w0_f, w_up_f, w0_b, w_up_b, a0_f, a_up_f, a0_b, a_up_b,
                      g_up, k_k, k_a, r_k, gn_w, gn_b, out_dtype=BF16)
    y_b = gqa_attention(p_att, q_norm, k_norm, out_dtype=BF16)
    w_out = w_out.astype(BF16)
    xf = matmul_residual([y_a.reshape(T, RW_DIM), y_b.reshape(T, ATT_DIM)], [w_out[:RW_DIM], w_out[RW_DIM:]], xf,
                         tm=1024, tn=1024)
    return ffn_swiglu(xf, norm2, ffn_gate.astype(BF16), ffn_up.astype(BF16), ffn_down.astype(BF16), tm=512, tf=1408,
                      cast=cast)


def _odd_layer(xf, Bn, S, norm1, w_in, gate_up_f, gate_bias_f, gate_up_b, gate_bias_b, out_norm, w_out,
               norm2, router, exp_gate, exp_up, exp_down):
    T, D = xf.shape
    o3 = 2 * GLA_DK + GLA_DV
    o4 = o3 + GLA_GATE_RANK
    w_main = jnp.concatenate([w_in[:, :o3], w_in[:, o4:]], axis=1).astype(BF16)
    w_gd = jnp.zeros((D, LANES), F32).at[:, :GLA_GATE_RANK].set(w_in[:, o3:o4])
    p_main = norm_matmul(xf, norm1, w_main, tm=2048, tn=1024).reshape(Bn, S, -1)
    gd = norm_matmul(xf, norm1, w_gd, tm=1024, tn=LANES).reshape(Bn, S, LANES)
    o = gla_mixer(p_main, gd, gate_up_f, gate_bias_f, gate_up_b, gate_bias_b, out_norm, out_dtype=BF16)
    xf = matmul_residual([o.reshape(T, GLA_DV)], [w_out.astype(BF16)], xf, tm=1024, tn=1024)
    return moe_layer(xf, norm2, router, exp_gate, exp_up, exp_down)


def kernel(x, e_norm1, e_w_in, e_shift_mu, e_w0_f, e_w_up_f, e_w0_b, e_w_up_b, e_a0_f, e_a_up_f, e_a0_b, e_a_up_b, e_g_up, e_k_k, e_k_a, e_r_k, e_gn_w, e_gn_b, e_q_norm, e_k_norm, e_w_out, e_norm2, e_ffn_gate, e_ffn_up, e_ffn_down, o_norm1, o_w_in, o_gate_up_f, o_gate_bias_f, o_gate_up_b, o_gate_bias_b, o_out_norm, o_w_out, o_norm2, o_router, o_exp_gate, o_exp_up, o_exp_down):
    Bn, S, D = x.shape
    xf, (exp_gate, exp_up, exp_down) = _even_layer(
        x, e_norm1[0], e_w_in[0], e_shift_mu[0], e_w0_f[0], e_w_up_f[0], e_w0_b[0],
        e_w_up_b[0], e_a0_f[0], e_a_up_f[0], e_a0_b[0], e_a_up_b[0], e_g_up[0], e_k_k[0],
        e_k_a[0], e_r_k[0], e_gn_w[0], e_gn_b[0], e_q_norm[0], e_k_norm[0], e_w_out[0],
        e_norm2[0], e_ffn_gate[0], e_ffn_up[0], e_ffn_down[0], (o_exp_gate[0], o_exp_up[0], o_exp_down[0]))
    xf = _odd_layer(xf, Bn, S, o_norm1[0], o_w_in[0], o_gate_up_f[0], o_gate_bias_f[0], o_gate_up_b[0],
                    o_gate_bias_b[0], o_out_norm[0], o_w_out[0], o_norm2[0], o_router[0],
                    exp_gate, exp_up, exp_down)
    return xf.reshape(Bn, S, D)
```

```python
import functools

import jax
import jax.numpy as jnp
from jax import lax
from jax.experimental import pallas as pl
from jax.experimental.pallas import tpu as pltpu

F32 = jnp.float32
BF16 = jnp.bfloat16
HI = lax.Precision.HIGHEST

D_MODEL = 1024
GRID_W = 64
HEAD_DIM = 64
NORM_EPS = 1e-6
RW_HEADS = 8
RW_DIM = 512
DECAY_RANK = 64
ICLR_RANK = 64
GATE_RANK = 128
RWKV_GN_EPS = 64e-5
RW_IN = 3 * RW_DIM + DECAY_RANK + ICLR_RANK + GATE_RANK
RW_CHUNK = 64
ATT_HEADS = 8
ATT_KV_HEADS = 2
ATT_DIM = 512
ATT_KV_DIM = 128
ATT_IN = ATT_DIM + 2 * ATT_KV_DIM
ROPE_THETA = 10000.0
ATT_VT_ROWS = HEAD_DIM + 16
LOG2_E = 1.4426950408889634
ATT_LOCKSTEP = 2
GLA_HEADS = 4
GLA_DK = 512
GLA_DV = 1024
GLA_DKH = 128
GLA_DVH = 256
GLA_GATE_RANK = 16
GLA_GATE_NORM = 16.0
GLA_CHUNK = 64
N_EXPERTS = 8
TOP_K = 2
LANES = 128

VMEM_LIMIT = 48 * 1024 * 1024
VMEM_LIMIT_BIG = 56 * 1024 * 1024


def _cparams(sem, vmem=VMEM_LIMIT):
    return pltpu.CompilerParams(dimension_semantics=sem, vmem_limit_bytes=vmem)


def _iota2(shape, dim):
    return lax.broadcasted_iota(jnp.int32, shape, dim)


def _split(x, parts):
    out = []
    for _ in range(parts):
        t = x.astype(BF16)
        out.append(t)
        x = x - t.astype(F32)
    return out


def _dot_sel(x, m, parts=2):
    mb = m.astype(BF16)
    acc = None
    for t in _split(x, parts):
        d = jnp.dot(t, mb, preferred_element_type=F32)
        acc = d if acc is None else acc + d
    return acc


def _sel_dot(m, x, parts=3):
    mb = m.astype(BF16)
    acc = None
    for t in _split(x, parts):
        d = jnp.dot(mb, t, preferred_element_type=F32)
        acc = d if acc is None else acc + d
    return acc


def _dot3(a, b):
    ah, al = _split(a, 2)
    bh, bl = _split(b, 2)
    d = lambda u, v: jnp.dot(u, v, preferred_element_type=F32)
    return d(ah, bh) + (d(ah, bl) + d(al, bh))


def _bdot(a, b):
    return jnp.dot(a.astype(BF16), b.astype(BF16), preferred_element_type=F32)


def _bdot_nt(a, b):
    return lax.dot_general(a.astype(BF16), b.astype(BF16), (((1,), (1,)), ((), ())), preferred_element_type=F32)


def _bdot_tn(a, b):
    return lax.dot_general(a.astype(BF16), b.astype(BF16), (((0,), (0,)), ((), ())), preferred_element_type=F32)


def _norm_mm_kernel(x_ref, g_ref, w_ref, o_ref, xn_ref):
    @pl.when(pl.program_id(1) == 0)
    def _():
        x = x_ref[...]
        ms = jnp.mean(x * x, axis=-1, keepdims=True)
        xn_ref[...] = (x * lax.rsqrt(ms + NORM_EPS) * g_ref[...]).astype(xn_ref.dtype)

    if xn_ref.dtype == F32:
        o_ref[...] = _dot3(xn_ref[...], w_ref[...]).astype(o_ref.dtype)
    else:
        o_ref[...] = jnp.dot(xn_ref[...], w_ref[...], preferred_element_type=F32).astype(o_ref.dtype)


def norm_matmul(x, g, w, *, tm, tn, out_dtype=F32):
    M, K = x.shape
    N = w.shape[1]
    return pl.pallas_call(
        _norm_mm_kernel,
        grid=(M // tm, N // tn),
        in_specs=[
            pl.BlockSpec((tm, K), lambda i, j: (i, 0)),
            pl.BlockSpec((1, K), lambda i, j: (0, 0)),
            pl.BlockSpec((K, tn), lambda i, j: (0, j)),
        ],
        out_specs=pl.BlockSpec((tm, tn), lambda i, j: (i, j)),
        out_shape=jax.ShapeDtypeStruct((M, N), out_dtype),
        scratch_shapes=[pltpu.VMEM((tm, K), w.dtype)],
        compiler_params=_cparams(("parallel", "arbitrary")),
        name="norm_matmul",
    )(x, g.reshape(1, K), w)


def _mm_res_kernel(*refs):
    n = (len(refs) - 2) // 2
    r_ref, o_ref = refs[2 * n], refs[2 * n + 1]
    acc = r_ref[...]
    for y_ref, w_ref in zip(refs[:n], refs[n:2 * n]):
        acc = acc + jnp.dot(y_ref[...].astype(BF16), w_ref[...], preferred_element_type=F32)
    o_ref[...] = acc


def matmul_residual(ys, ws, r, *, tm, tn):
    M, N = r.shape
    y_specs = [pl.BlockSpec((tm, y.shape[1]), lambda i, j: (i, 0)) for y in ys]
    w_specs = [pl.BlockSpec((w.shape[0], tn), lambda i, j: (0, j)) for w in ws]
    return pl.pallas_call(
        _mm_res_kernel,
        grid=(M // tm, N // tn),
        in_specs=y_specs + w_specs + [pl.BlockSpec((tm, tn), lambda i, j: (i, j))],
        out_specs=pl.BlockSpec((tm, tn), lambda i, j: (i, j)),
        out_shape=jax.ShapeDtypeStruct((M, N), F32),
        compiler_params=_cparams(("parallel", "arbitrary")),
        name="matmul_residual",
    )(*ys, *ws, r)


def _ffn_kernel(*refs, n_cast):
    x_ref, g_ref, wg_ref, wu_ref, wd_ref = refs[:5]
    cast_in = refs[5:5 + n_cast]
    o_ref = refs[5 + n_cast]
    cast_out = refs[6 + n_cast:6 + 2 * n_cast]
    xn_ref, acc_ref = refs[6 + 2 * n_cast:]
    f = pl.program_id(1)

    @pl.when(f == 0)
    def _():
        x = x_ref[...]
        ms = jnp.mean(x * x, axis=-1, keepdims=True)
        xn_ref[...] = (x * lax.rsqrt(ms + NORM_EPS) * g_ref[...]).astype(BF16)

    for src, dst in zip(cast_in, cast_out):
        dst[...] = src[...].astype(BF16)

    xn = xn_ref[...]
    a = jnp.dot(xn, wg_ref[...], preferred_element_type=F32)
    b = jnp.dot(xn, wu_ref[...], preferred_element_type=F32)
    h = (a * jax.nn.sigmoid(a) * b).astype(BF16)
    part = jnp.dot(h, wd_ref[...], preferred_element_type=F32)

    @pl.when(f == 0)
    def _():
        acc_ref[...] = part

    @pl.when(f > 0)
    def _():
        acc_ref[...] += part

    @pl.when(f == pl.num_programs(1) - 1)
    def _():
        o_ref[...] = x_ref[...] + acc_ref[...]


def ffn_swiglu(x, g, wg, wu, wd, *, tm, tf, cast=()):
    M, D = x.shape
    F = wg.shape[1]
    nf = F // tf
    steps = (M // tm) * nf
    cast_specs, cast_shapes = [], []
    for w in cast:
        E, R, C = w.shape
        per = steps // E
        cast_specs.append(pl.BlockSpec((1, R // per, C), lambda i, f, per=per: ((i * nf + f) // per, (i * nf + f) % per, 0)))
        cast_shapes.append(jax.ShapeDtypeStruct(w.shape, BF16))
    outs = pl.pallas_call(
        functools.partial(_ffn_kernel, n_cast=len(cast)),
        grid=(M // tm, nf),
        in_specs=[
            pl.BlockSpec((tm, D), lambda i, f: (i, 0)),
            pl.BlockSpec((1, D), lambda i, f: (0, 0)),
            pl.BlockSpec((D, tf), lambda i, f: (0, f)),
            pl.BlockSpec((D, tf), lambda i, f: (0, f)),
            pl.BlockSpec((tf, D), lambda i, f: (f, 0)),
        ] + cast_specs,
        out_specs=[pl.BlockSpec((tm, D), lambda i, f: (i, 0))] + cast_specs,
        out_shape=[jax.ShapeDtypeStruct((M, D), F32)] + cast_shapes,
        scratch_shapes=[pltpu.VMEM((tm, D), BF16), pltpu.VMEM((tm, D), F32)],
        compiler_params=_cparams(("arbitrary", "arbitrary"), vmem=VMEM_LIMIT_BIG if cast else VMEM_LIMIT),
        name="ffn_swiglu",
    )(x, g.reshape(1, D), wg, wu, wd, *cast)
    return outs[0], tuple(outs[1:])


def _rwkv_prep_kernel(cur_ref, prev_ref, next_ref, n1_ref, win_ref, mu_ref, w0f_ref, wupf_ref, w0b_ref, wupb_ref,
                      a0f_ref, aupf_ref, a0b_ref, aupb_ref, gup_ref, kk_ref, hsum_ref,
                      r_ref, k_ref, v_ref, kkn_ref, g_ref, lwf_ref, lf_ref, asf_ref, lwb_ref, lb_ref, asb_ref):
    i = pl.program_id(1)
    nt = pl.num_programs(1)

    def project(xv):
        ms = jnp.mean(xv * xv, axis=-1, keepdims=True)
        xn = (xv * lax.rsqrt(ms + NORM_EPS) * n1_ref[...]).astype(BF16)
        return jnp.dot(xn, win_ref[...], preferred_element_type=F32)

    Tt = cur_ref.shape[1]
    pall = project(jnp.concatenate([cur_ref[0], prev_ref[0], next_ref[0]], axis=0))
    x = pall[:Tt]
    halo = pall[Tt:]
    row = _iota2(x.shape, 0)
    prev_row = jnp.where(i > 0, halo[7:8, :], 0.0)
    next_row = jnp.where(i < nt - 1, halo[8:9, :], 0.0)
    prev = jnp.where(row == 0, prev_row, pltpu.roll(x, 1, 0))
    nxt = jnp.where(row == Tt - 1, next_row, pltpu.roll(x, Tt - 1, 0))
    pm = x + (0.5 * (prev + nxt) - x) * mu_ref[...]
    o1 = 3 * RW_DIM
    o2 = o1 + DECAY_RANK
    o3 = o2 + ICLR_RANK
    r_ref[0] = pm[:, :RW_DIM]
    k = pm[:, RW_DIM:2 * RW_DIM]
    k_ref[0] = k
    v_ref[0] = pm[:, 2 * RW_DIM:o1]
    wd = jnp.tanh(pm[:, o1:o2])
    ad = pm[:, o2:o3]
    gin = jax.nn.sigmoid(pm[:, o3:])
    g_ref[0] = jnp.dot(gin.astype(BF16), gup_ref[...].astype(BF16), preferred_element_type=F32)
    kk = k * kk_ref[...]
    ss = _dot_sel(kk * kk, hsum_ref[...])
    kkn_ref[0] = kk / jnp.maximum(jnp.sqrt(ss), 1e-12)

    ci = _iota2((RW_CHUNK, RW_CHUNK), 0)
    cj = _iota2((RW_CHUNK, RW_CHUNK), 1)
    tril = (cj <= ci).astype(F32)
    triu = (cj >= ci).astype(F32)

    def direction(w0_ref, wup_ref, a0_ref, aup_ref, tri, lw_ref, l_ref, as_ref):
        z = w0_ref[...] + _dot3(wd, wup_ref[...])
        sp = jnp.maximum(-z, 0.0) + jnp.log(1.0 + jnp.exp(-jnp.abs(z)))
        lw = -jnp.exp(-sp - 0.5)
        lw_ref[0] = lw
        as_ref[0] = jax.nn.sigmoid(a0_ref[...] + _dot3(ad, aup_ref[...]))
        for c in range(Tt // RW_CHUNK):
            sl = slice(c * RW_CHUNK, (c + 1) * RW_CHUNK)
            l_ref[0, sl, :] = _sel_dot(tri, lw[sl, :])

    direction(w0f_ref, wupf_ref, a0f_ref, aupf_ref, tril, lwf_ref, lf_ref, asf_ref)
    direction(w0b_ref, wupb_ref, a0b_ref, aupb_ref, triu, lwb_ref, lb_ref, asb_ref)


def rwkv_prep(x, norm1, w_in, shift_mu, w0_f, w_up_f, w0_b, w_up_b, a0_f, a_up_f, a0_b, a_up_b, g_up, k_k, *, tt=256):
    B, S, D = x.shape
    nt = S // tt
    hsum = (jnp.arange(RW_DIM)[:, None] // HEAD_DIM == jnp.arange(RW_DIM)[None, :] // HEAD_DIM).astype(F32)
    row = lambda a: a.reshape(1, -1)
    full = lambda a: pl.BlockSpec(a.shape, lambda b, i: (0,) * a.ndim)
    params = [row(norm1), w_in, row(shift_mu), row(w0_f), w_up_f, row(w0_b), w_up_b, row(a0_f), a_up_f, row(a0_b), a_up_b,
              g_up, row(k_k), hsum]
    tb = tt // 8
    in_specs = [
        pl.BlockSpec((1, tt, D), lambda b, i: (b, i, 0)),
        pl.BlockSpec((1, 8, D), lambda b, i: (b, jnp.maximum(i * tb - 1, 0), 0)),
        pl.BlockSpec((1, 8, D), lambda b, i: (b, jnp.minimum((i + 1) * tb, S // 8 - 1), 0)),
    ] + [full(a) for a in params]
    out_spec = pl.BlockSpec((1, tt, RW_DIM), lambda b, i: (b, i, 0))
    out_sds = jax.ShapeDtypeStruct((B, S, RW_DIM), F32)
    return pl.pallas_call(
        _rwkv_prep_kernel,
        grid=(B, nt),
        in_specs=in_specs,
        out_specs=[out_spec] * 11,
        out_shape=[out_sds] * 11,
        compiler_params=_cparams(("parallel", "arbitrary")),
        name="rwkv_prep",
    )(x, x, x, *params)


def _bd(y, bd_mask):
    return jnp.where(bd_mask, jnp.concatenate([y, y], axis=0), jnp.zeros((), y.dtype))


def _chunk_terms(chains, out, bd_mask):
    C = chains[0]["r"].shape[0]
    n = len(chains)
    R = range(n)
    dot = lambda u, v: jnp.dot(u, v, preferred_element_type=F32)
    bd = lambda y: _bd(y, bd_mask)
    pre = []
    for ch in chains:
        r, k, v, kk, lw, L, asig, ka, rev = (ch[x] for x in ("r", "k", "v", "kk", "lw", "L", "asig", "ka", "rev"))
        a = -kk
        b = kk * asig
        kd = k * (1.0 + (asig - 1.0) * ka)
        Lp = L - lw
        Lr = Lp if rev else L
        Lend = L[0:1, :] if rev else L[C - 1:C, :]
        Lmid = L[C // 2:C // 2 + 1, :]
        einv = jnp.exp(Lmid - L)
        eend = jnp.exp(Lend - L)
        pre.append(dict(
            ar=jnp.concatenate([a * jnp.exp(Lp - Lmid), r * jnp.exp(Lr - Lmid)], axis=0).astype(BF16),
            bt=bd((b * einv).astype(BF16)), kt=bd((kd * einv).astype(BF16)),
            a0=bd((a * jnp.exp(Lp)).astype(BF16)), r0=r * jnp.exp(Lr),
            bh=(b * eend).astype(BF16), kh=(kd * eend).astype(BF16), vb=v.astype(BF16),
            dec=jnp.exp(Lend)))
    mk = [ch["masks"] for ch in chains]
    Ab = [_bdot_nt(pre[i]["ar"], pre[i]["bt"]) for i in R]
    Ak = [_bdot_nt(pre[i]["ar"], pre[i]["kt"]) for i in R]
    yield
    Aab = [jnp.where(mk[i][0], Ab[i][:C], 0.0) for i in R]
    Arb = [jnp.where(mk[i][1], Ab[i][C:], 0.0).astype(BF16) for i in R]
    AakArk = [jnp.concatenate([jnp.where(mk[i][0], Ak[i][:C], 0.0), jnp.where(mk[i][1], Ak[i][C:], 0.0)],
                              axis=0).astype(BF16) for i in R]
    AV = [dot(AakArk[i], bd(pre[i]["vb"])) for i in R]
    Xb = [jnp.where(mk[i][2], Aab[i], 0.0).astype(BF16) for i in R]
    X2b = [dot(Xb[i], bd(Xb[i])).astype(BF16) for i in R]
    yield
    T = [mk[i][6] + Xb[i].astype(F32) for i in R]
    T = [T[i] + dot(T[i].astype(BF16), bd(X2b[i])) for i in R]
    X4b = [dot(X2b[i], bd(X2b[i])).astype(BF16) for i in R]
    yield
    T = [T[i] + dot(T[i].astype(BF16), bd(X4b[i])) for i in R]
    yield
    for lvl in (3, 4, 5):
        Tb = [T[i].astype(BF16) for i in R]
        ET = [dot(jnp.where(mk[i][lvl], Aab[i], 0.0).astype(BF16), bd(Tb[i])).astype(BF16) for i in R]
        yield
        T = [T[i] + dot(Tb[i], bd(ET[i])) for i in R]
        yield
    Tb = [T[i].astype(BF16) for i in R]
    A0p = [dot(Tb[i], pre[i]["a0"]).astype(BF16) for i in R]
    Uv = [dot(Tb[i], bd(AV[i][:C].astype(BF16))).astype(BF16) for i in R]
    yield
    Rpp = [pre[i]["r0"] + dot(Arb[i], bd(A0p[i])) for i in R]
    Yv = [dot(Arb[i], bd(Uv[i])) + AV[i][C:] for i in R]
    eye2 = (_iota2(bd_mask.shape, 0) == _iota2(bd_mask.shape, 1)).astype(F32)
    P = [jnp.where(bd_mask, _bdot_tn(pre[i]["bh"], A0p[i]), 0.0) + eye2 * pre[i]["dec"] for i in R]
    Q = [jnp.where(bd_mask, _bdot_tn(pre[i]["bh"], Uv[i]) + _bdot_tn(pre[i]["kh"], pre[i]["vb"]), 0.0) for i in R]
    out.extend((Rpp[i], Yv[i], P[i], Q[i]) for i in R)


def _make_masks(C, N, rev):
    ii = _iota2((C, 2 * N), 0)
    jj = _iota2((C, 2 * N), 1) % N
    strict = (jj > ii) if rev else (jj < ii)
    rmask = strict if rev else (jj <= ii)
    blk = lambda s: (ii // s) == (jj // s)
    m8 = blk(8)
    e16 = blk(16) & jnp.logical_not(blk(8))
    e32 = blk(32) & jnp.logical_not(blk(16))
    e64 = jnp.logical_not(blk(32))
    eye = (ii == jj).astype(F32)
    return strict, rmask, m8, e16, e32, e64, eye


RW_UNROLL = 16


def _rwkv_scan_kernel(r_ref, k_ref, v_ref, kk_ref, g_ref, lwf_ref, lf_ref, asf_ref, lwb_ref, lb_ref, asb_ref,
                      ka_ref, rk_ref, gnw_ref, gnb_ref, havg_ref, o_ref,
                      yf_ref, yb_ref, h_ref, rpp_ref, yv_ref, p_ref, q_ref):
    S = r_ref.shape[1]
    C = RW_CHUNK
    N = HEAD_DIM
    W = 2 * N
    n = S // C
    U = RW_UNROLL
    nset = n // U
    masks_f = _make_masks(C, N, False)
    masks_b = _make_masks(C, N, True)
    bd_mask = (_iota2((W, W), 0) // N) == (_iota2((W, W), 1) // N)
    h_ref[...] = jnp.zeros_like(h_ref)
    dirs = ((False, lwf_ref, lf_ref, asf_ref, yf_ref, masks_f), (True, lwb_ref, lb_ref, asb_ref, yb_ref, masks_b))

    def chunk_rows(rev, st, u):
        c = st * U + u
        c = (n - 1 - c) if rev else c
        return pl.ds(pl.multiple_of(c * C, C), C)

    def state_step(st, u):
        slot = st % 2
        for di, (rev, lw_ref, l_ref, as_ref, y_ref, masks) in enumerate(dirs):
            Hb = h_ref[di].astype(BF16)
            y_ref[chunk_rows(rev, st, u), :] = (jnp.dot(rpp_ref[slot, di, u], Hb, preferred_element_type=F32)
                                                + yv_ref[slot, di, u])
            h_ref[di] = jnp.dot(p_ref[slot, di, u], Hb, preferred_element_type=F32) + q_ref[slot, di, u]

    def chunk_set(st, state_of):
        chains, slots = [], []
        for u in range(U):
            for di, (rev, lw_ref, l_ref, as_ref, y_ref, masks) in enumerate(dirs):
                rows = chunk_rows(rev, st, u)
                chains.append(dict(
                    r=r_ref[0, rows, :], k=k_ref[0, rows, :], v=v_ref[0, rows, :], kk=kk_ref[0, rows, :],
                    lw=lw_ref[0, rows, :], L=l_ref[0, rows, :], asig=as_ref[0, rows, :], ka=ka_ref[...],
                    rev=rev, masks=masks))
                slots.append((di, u))
        res = []
        pending = list(range(U)) if state_of is not None else []
        for stage, _ in enumerate(_chunk_terms(chains, res, bd_mask)):
            if pending and stage % 2 == 1:
                state_step(state_of, pending.pop(0))
        for u in pending:
            state_step(state_of, u)
        slot = st % 2
        for (di, u), (Rpp, Yv, P, Q) in zip(slots, res):
            rpp_ref[slot, di, u] = Rpp.astype(BF16)
            yv_ref[slot, di, u] = Yv
            p_ref[slot, di, u] = P.astype(BF16)
            q_ref[slot, di, u] = Q

    chunk_set(0, None)

    def body(st, carry):
        chunk_set(st, st - 1)
        return carry

    lax.fori_loop(1, nset, body, 0)
    for u in range(U):
        state_step(nset - 1, u)

    RT = 256

    def fin(i, carry):
        rows = pl.ds(pl.multiple_of(i * RT, RT), RT)
        y = yf_ref[rows, :] + yb_ref[rows, :]
        mean = _dot_sel(y, havg_ref[...])
        d = y - mean
        var = _dot_sel(d * d, havg_ref[...])
        yn = d * lax.rsqrt(var + RWKV_GN_EPS) * gnw_ref[...] + gnb_ref[...]
        r = r_ref[0, rows, :]
        kf = k_ref[0, rows, :] * (1.0 + (asf_ref[0, rows, :] - 1.0) * ka_ref[...])
        bonus = _dot_sel(r * kf * rk_ref[...], havg_ref[...]) * float(N) * v_ref[0, rows, :]
        o_ref[0, rows, :] = ((yn + bonus) * g_ref[0, rows, :]).astype(o_ref.dtype)
        return carry

    lax.fori_loop(0, S // RT, fin, 0)


def rwkv_scan(r, k, v, kk, g, lw_f, l_f, as_f, lw_b, l_b, as_b, k_a, r_k, gn_w, gn_b, out_dtype=F32):
    B, S, _ = r.shape
    W = 2 * HEAD_DIM
    N = HEAD_DIM
    havg = (jnp.arange(W)[:, None] // HEAD_DIM == jnp.arange(W)[None, :] // HEAD_DIM).astype(F32) / HEAD_DIM
    seq = pl.BlockSpec((1, S, W), lambda b, h: (b, 0, h))
    par = pl.BlockSpec((1, W), lambda b, h: (0, h))
    row = lambda a: a.reshape(1, -1)
    U = RW_UNROLL
    return pl.pallas_call(
        _rwkv_scan_kernel,
        grid=(B, RW_DIM // W),
        in_specs=[seq] * 11 + [par] * 4 + [pl.BlockSpec((W, W), lambda b, h: (0, 0))],
        out_specs=seq,
        out_shape=jax.ShapeDtypeStruct((B, S, RW_DIM), out_dtype),
        scratch_shapes=[pltpu.VMEM((S, W), F32), pltpu.VMEM((S, W), F32), pltpu.VMEM((2, W, W), F32),
                        pltpu.VMEM((2, 2, U, RW_CHUNK, W), BF16), pltpu.VMEM((2, 2, U, RW_CHUNK, W), F32),
                        pltpu.VMEM((2, 2, U, W, W), BF16), pltpu.VMEM((2, 2, U, W, W), F32)],
        compiler_params=_cparams(("parallel", "parallel")),
        name="rwkv_scan",
    )(r, k, v, kk, g, lw_f, l_f, as_f, lw_b, l_b, as_b, row(k_a), row(r_k), row(gn_w), row(gn_b), havg)


def rwkv7_mixer(x, norm1, w_in, shift_mu, w0_f, w_up_f, w0_b, w_up_b, a0_f, a_up_f, a0_b, a_up_b, g_up, k_k, k_a,
                r_k, gn_w, gn_b, out_dtype=F32, tt=512):
    outs = rwkv_prep(x, norm1, w_in, shift_mu, w0_f, w_up_f, w0_b, w_up_b, a0_f, a_up_f, a0_b, a_up_b, g_up, k_k, tt=tt)
    return rwkv_scan(*outs, k_a, r_k, gn_w, gn_b, out_dtype=out_dtype)


def _head_norm_rope(x, gain, cos, sin_signed, havg):
    ms = _dot_sel(x * x, havg)
    xn = x * lax.rsqrt(ms + NORM_EPS) * gain
    W = x.shape[1]
    even = (_iota2(x.shape, 1) % 2) == 0
    partner = jnp.where(even, pltpu.roll(xn, W - 1, 1), pltpu.roll(xn, 1, 1))
    return xn * cos + partner * sin_signed


def _attn_kernel(q_ref, k_ref, v_ref, cosq_ref, sinq_ref, cosk_ref, sink_ref, qg_ref, kg_ref, hq_ref, hk_ref,
                 o_ref, ks_ref, vt_ref):
    i = pl.program_id(1)
    G = ATT_HEADS // ATT_KV_HEADS
    D = HEAD_DIM

    @pl.when(i == 0)
    def _():
        kr = _head_norm_rope(k_ref[0], kg_ref[...], cosk_ref[...], sink_ref[...], hk_ref[...])
        vt = jnp.transpose(v_ref[0])
        ones = jnp.ones((ATT_VT_ROWS - D, vt.shape[1]), BF16)
        for kv in range(ATT_KV_HEADS):
            ks_ref[kv] = kr[:, kv * D:(kv + 1) * D].astype(BF16)
            vt_ref[kv] = jnp.concatenate([vt[kv * D:(kv + 1) * D, :].astype(BF16), ones], axis=0)

    q = _head_norm_rope(q_ref[0], qg_ref[...], cosq_ref[...], sinq_ref[...], hq_ref[...]) * (D ** -0.5 * LOG2_E)
    qb = q.astype(BF16)
    outs = []
    for h0 in range(0, ATT_HEADS, ATT_LOCKSTEP):
        hs = range(h0, h0 + ATT_LOCKSTEP)
        st = [lax.dot_general(ks_ref[h // G], qb[:, h * D:(h + 1) * D], (((1,), (1,)), ((), ())),
                              preferred_element_type=F32) for h in hs]
        m = [jnp.max(x, axis=0, keepdims=True) for x in st]
        p = [jnp.exp2(x - mm).astype(BF16) for x, mm in zip(st, m)]
        ot = [jnp.dot(vt_ref[h // G], x, preferred_element_type=F32) for h, x in zip(hs, p)]
        outs.extend(o[:D] / o[D:D + 1] for o in ot)
    pairs = [jnp.transpose(jnp.concatenate(outs[2 * j:2 * j + 2], axis=0)) for j in range(ATT_HEADS // 2)]
    o_ref[0] = jnp.concatenate(pairs, axis=1).astype(o_ref.dtype)


def _rope_tables(S):
    rows = S // GRID_W
    row = jnp.repeat(jnp.arange(rows), GRID_W).astype(F32)
    col = jnp.tile(jnp.arange(GRID_W), rows).astype(F32)
    half = HEAD_DIM // 2
    freq = ROPE_THETA ** (-jnp.arange(0, half, 2, dtype=F32) / half)
    ang = jnp.concatenate([row[:, None] * freq, col[:, None] * freq], axis=-1)
    cos = jnp.repeat(jnp.cos(ang), 2, axis=-1)
    sin = jnp.repeat(jnp.sin(ang), 2, axis=-1)
    sign = jnp.where(jnp.arange(HEAD_DIM) % 2 == 0, -1.0, 1.0).astype(F32)
    return cos, sin * sign


def gqa_attention(p, q_norm, k_norm, *, tq=512, out_dtype=F32):
    B, S, _ = p.shape
    cos, sin = _rope_tables(S)
    tile = lambda t, n: jnp.tile(t, (1, n))
    hq = (jnp.arange(ATT_DIM)[:, None] // HEAD_DIM == jnp.arange(ATT_DIM)[None, :] // HEAD_DIM).astype(F32) / HEAD_DIM
    hk = hq[:ATT_KV_DIM, :ATT_KV_DIM]
    qg = jnp.tile(q_norm, ATT_HEADS).reshape(1, ATT_DIM)
    kg = jnp.tile(k_norm, ATT_KV_HEADS).reshape(1, ATT_KV_DIM)
    nq = ATT_DIM // ATT_KV_DIM
    const = lambda a: pl.BlockSpec(a.shape, lambda b, i: (0, 0))
    return pl.pallas_call(
        _attn_kernel,
        grid=(B, S // tq),
        in_specs=[
            pl.BlockSpec((1, tq, ATT_DIM), lambda b, i: (b, i, 0)),
            pl.BlockSpec((1, S, ATT_KV_DIM), lambda b, i: (b, 0, nq)),
            pl.BlockSpec((1, S, ATT_KV_DIM), lambda b, i: (b, 0, nq + 1)),
            pl.BlockSpec((tq, ATT_DIM), lambda b, i: (i, 0)),
            pl.BlockSpec((tq, ATT_DIM), lambda b, i: (i, 0)),
            pl.BlockSpec((S, ATT_KV_DIM), lambda b, i: (0, 0)),
            pl.BlockSpec((S, ATT_KV_DIM), lambda b, i: (0, 0)),
            const(qg), const(kg), const(hq), const(hk),
        ],
        out_specs=pl.BlockSpec((1, tq, ATT_DIM), lambda b, i: (b, i, 0)),
        out_shape=jax.ShapeDtypeStruct((B, S, ATT_DIM), out_dtype),
        scratch_shapes=[pltpu.VMEM((ATT_KV_HEADS, S, HEAD_DIM), BF16), pltpu.VMEM((ATT_KV_HEADS, ATT_VT_ROWS, S), BF16)],
        compiler_params=_cparams(("parallel", "arbitrary")),
        name="gqa_attention",
    )(p, p, p, tile(cos, ATT_HEADS), tile(sin, ATT_HEADS), tile(cos, ATT_KV_HEADS), tile(sin, ATT_KV_HEADS),
      qg, kg, hq, hk)


GLA_UNROLL = 16


def _gla_kernel(q_ref, k_ref, v_ref, gd_ref, og_ref, upf_ref, bf_ref, upb_ref, bb_ref, on_ref, o_ref,
                yf_ref, yb_ref, st_ref, kv_ref, qe_ref, dec_ref):
    S = q_ref.shape[1]
    C = GLA_CHUNK
    n = S // C
    U = GLA_UNROLL
    nset = n // U
    ii = _iota2((C, C), 0)
    jj = _iota2((C, C), 1)
    dirs = (
        (False, (jj <= ii).astype(F32), jj <= ii, upf_ref, bf_ref, yf_ref),
        (True, (jj >= ii).astype(F32), jj > ii, upb_ref, bb_ref, yb_ref),
    )
    scale = GLA_DKH ** -0.5
    st_ref[...] = jnp.zeros_like(st_ref)

    def chunk_rows(rev, st, u):
        c = st * U + u
        c = (n - 1 - c) if rev else c
        return pl.ds(pl.multiple_of(c * C, C), C)

    def state_step(st, u):
        slot = st % 2
        for di, (rev, tri, mask, up_ref, b_ref, y_ref) in enumerate(dirs):
            state = st_ref[di]
            rows = chunk_rows(rev, st, u)
            y_ref[rows, :] += lax.dot_general(qe_ref[slot, di, u], state.astype(BF16), (((1,), (1,)), ((), ())),
                                              preferred_element_type=F32)
            st_ref[di] = state * dec_ref[slot, di, u] + kv_ref[slot, di, u]

    def chunk_set(st, state_of):
        pending = list(range(U)) if state_of is not None else []
        items = []
        for u in range(U):
            for di, (rev, tri, mask, up_ref, b_ref, y_ref) in enumerate(dirs):
                rows = chunk_rows(rev, st, u)
                z = _dot3(gd_ref[0, rows, :], up_ref[...]) + b_ref[...]
                g = (jnp.minimum(z, 0.0) - jnp.log(1.0 + jnp.exp(-jnp.abs(z)))) * (1.0 / GLA_GATE_NORM)
                items.append(dict(u=u, di=di, rows=rows, rev=rev, mask=mask, y_ref=y_ref, g=g, tri=tri))
        for x in items:
            x["b"] = _sel_dot(x["tri"], x["g"])
        if pending:
            state_step(state_of, pending.pop(0))
        for x in items:
            b = x["b"]
            q = q_ref[0, x["rows"], :] * scale
            k = k_ref[0, x["rows"], :]
            b_mid = b[C // 2:C // 2 + 1, :]
            b_last = b[0:1, :] if x["rev"] else b[C - 1:C, :]
            x["vb"] = v_ref[0, x["rows"], :].astype(BF16)
            x["qm"] = (q * jnp.exp(b - b_mid)).astype(BF16)
            x["km"] = (k * jnp.exp(b_mid - b)).astype(BF16)
            x["ke"] = (k * jnp.exp(b_last - b)).astype(BF16)
            x["qe"] = (q * jnp.exp(b)).astype(BF16)
            x["dec"] = jnp.exp(b_last)
        att = [lax.dot_general(x["qm"], x["km"], (((1,), (1,)), ((), ())), preferred_element_type=F32) for x in items]
        if pending:
            state_step(state_of, pending.pop(0))
        att = [jnp.where(x["mask"], a, 0.0).astype(BF16) for x, a in zip(items, att)]
        slot = st % 2
        for idx, (x, a) in enumerate(zip(items, att)):
            x["y_ref"][x["rows"], :] = jnp.dot(a, x["vb"], preferred_element_type=F32)
            kv_ref[slot, x["di"], x["u"]] = lax.dot_general(x["vb"], x["ke"], (((0,), (0,)), ((), ())),
                                                            preferred_element_type=F32)
            qe_ref[slot, x["di"], x["u"]] = x["qe"]
            dec_ref[slot, x["di"], x["u"]] = x["dec"]
            if pending and idx % 3 == 2:
                state_step(state_of, pending.pop(0))
        for u in pending:
            state_step(state_of, u)

    chunk_set(0, None)

    def body(st, carry):
        chunk_set(st, st - 1)
        return carry

    lax.fori_loop(1, nset, body, 0)
    for u in range(U):
        state_step(nset - 1, u)

    RT = 256

    def fin(i, carry):
        rows = pl.ds(pl.multiple_of(i * RT, RT), RT)
        o = yf_ref[rows, :] + yb_ref[rows, :]
        ms = jnp.mean(o * o, axis=-1, keepdims=True)
        on = o * lax.rsqrt(ms + NORM_EPS) * on_ref[...]
        og = og_ref[0, rows, :]
        o_ref[0, rows, :] = (on * (og * jax.nn.sigmoid(og))).astype(o_ref.dtype)
        return carry

    lax.fori_loop(0, S // RT, fin, 0)


def gla_mixer(pm, gd, gate_up_f, gate_bias_f, gate_up_b, gate_bias_b, out_norm, out_dtype=F32):
    B, S, _ = pm.shape
    U = GLA_UNROLL
    H = GLA_HEADS
    kb = GLA_DK // GLA_DKH
    vb0 = 2 * GLA_DK // GLA_DVH
    ob0 = vb0 + GLA_DV // GLA_DVH
    return pl.pallas_call(
        _gla_kernel,
        grid=(B, H),
        in_specs=[
            pl.BlockSpec((1, S, GLA_DKH), lambda b, h: (b, 0, h)),
            pl.BlockSpec((1, S, GLA_DKH), lambda b, h: (b, 0, kb + h)),
            pl.BlockSpec((1, S, GLA_DVH), lambda b, h: (b, 0, vb0 + h)),
            pl.BlockSpec((1, S, GLA_GATE_RANK), lambda b, h: (b, 0, 0)),
            pl.BlockSpec((1, S, GLA_DVH), lambda b, h: (b, 0, ob0 + h)),
            pl.BlockSpec((GLA_GATE_RANK, GLA_DKH), lambda b, h: (0, h)),
            pl.BlockSpec((1, GLA_DKH), lambda b, h: (0, h)),
            pl.BlockSpec((GLA_GATE_RANK, GLA_DKH), lambda b, h: (0, h)),
            pl.BlockSpec((1, GLA_DKH), lambda b, h: (0, h)),
            pl.BlockSpec((1, GLA_DVH), lambda b, h: (0, 0)),
        ],
        out_specs=pl.BlockSpec((1, S, GLA_DVH), lambda b, h: (b, 0, h)),
        out_shape=jax.ShapeDtypeStruct((B, S, GLA_DV), out_dtype),
        scratch_shapes=[pltpu.VMEM((S, GLA_DVH), F32), pltpu.VMEM((S, GLA_DVH), F32),
                        pltpu.VMEM((2, GLA_DVH, GLA_DKH), F32),
                        pltpu.VMEM((2, 2, U, GLA_DVH, GLA_DKH), F32), pltpu.VMEM((2, 2, U, GLA_CHUNK, GLA_DKH), BF16),
                        pltpu.VMEM((2, 2, U, 1, GLA_DKH), F32)],
        compiler_params=_cparams(("parallel", "parallel")),
        name="gla_mixer",
    )(pm, pm, pm, gd, pm, gate_up_f, gate_bias_f.reshape(1, -1), gate_up_b, gate_bias_b.reshape(1, -1),
      out_norm.reshape(1, -1))


def _router_kernel(x_ref, g_ref, wr_ref, hn_ref, idx_ref, gate_ref):
    x = x_ref[...]
    ms = jnp.mean(x * x, axis=-1, keepdims=True)
    hn = x * lax.rsqrt(ms + NORM_EPS) * g_ref[...]
    hn_ref[...] = hn
    logits = _dot3(hn, wr_ref[...])
    lane = _iota2(logits.shape, 1)
    neg = jnp.float32(-jnp.inf)
    logits = jnp.where(lane < N_EXPERTS, logits, neg)
    m1 = jnp.max(logits, axis=-1, keepdims=True)
    i1 = jnp.min(jnp.where(logits == m1, lane, LANES), axis=-1, keepdims=True)
    rest = jnp.where(lane == i1, neg, logits)
    m2 = jnp.max(rest, axis=-1, keepdims=True)
    i2 = jnp.min(jnp.where(rest == m2, lane, LANES), axis=-1, keepdims=True)
    e2 = jnp.exp(m2 - m1)
    g1 = 1.0 / (1.0 + e2)
    g2 = e2 / (1.0 + e2)
    idx_ref[...] = jnp.where(lane == 0, i1, jnp.where(lane == 1, i2, 0))
    gate_ref[...] = jnp.where(lane == 0, g1, jnp.where(lane == 1, g2, 0.0))


def moe_router(x, g, router, *, tm=512):
    T, D = x.shape
    wr = jnp.zeros((D, LANES), F32).at[:, :N_EXPERTS].set(router)
    return pl.pallas_call(
        _router_kernel,
        grid=(T // tm,),
        in_specs=[
            pl.BlockSpec((tm, D), lambda i: (i, 0)),
            pl.BlockSpec((1, D), lambda i: (0, 0)),
            pl.BlockSpec((D, LANES), lambda i: (0, 0)),
        ],
        out_specs=[
            pl.BlockSpec((tm, D), lambda i: (i, 0)),
            pl.BlockSpec((tm, LANES), lambda i: (i, 0)),
            pl.BlockSpec((tm, LANES), lambda i: (i, 0)),
        ],
        out_shape=[jax.ShapeDtypeStruct((T, D), F32), jax.ShapeDtypeStruct((T, LANES), jnp.int32),
                   jax.ShapeDtypeStruct((T, LANES), F32)],
        compiler_params=_cparams(("parallel",)),
        name="moe_router",
    )(x, g.reshape(1, D), wr)


def _expert_kernel(te_ref, tv_ref, tok_ref, hn_hbm, wg_ref, wu_ref, wd_ref, o_ref, xg_ref, xb_ref, acc_ref, sem, *, nf):
    i = pl.program_id(0)
    f = pl.program_id(1)
    tm = xb_ref.shape[0]
    valid = tv_ref[i] > 0
    prev_valid = tv_ref[jnp.maximum(i - 1, 0)] > 0
    slot = i % 2

    def issue_row(tile, dst_slot, r):
        tok = tok_ref[tile * tm + r]
        pltpu.make_async_copy(hn_hbm.at[pl.ds(tok, 1), :], xg_ref.at[dst_slot, pl.ds(r, 1), :], sem.at[dst_slot]).start()

    @pl.when(jnp.logical_and(valid, jnp.logical_and(i == 0, f == 0)))
    def _():
        def body(r, c):
            issue_row(0, 0, r)
            return c

        lax.fori_loop(0, tm, body, 0, unroll=8)

    @pl.when(jnp.logical_and(f == 0, jnp.logical_or(valid, jnp.logical_and(i > 0, prev_valid))))
    def _():
        pltpu.make_async_copy(hn_hbm.at[pl.ds(0, tm), :], xg_ref.at[slot], sem.at[slot]).wait()

    @pl.when(jnp.logical_and(valid, f == 0))
    def _():
        xb_ref[...] = xg_ref[slot].astype(BF16)

    @pl.when(valid)
    def _():
        per = tm // nf
        row0 = f * per
        xb = xb_ref[...]
        a = jnp.dot(xb, wg_ref[0], preferred_element_type=F32)
        for j in range(per // 2):
            issue_row(i + 1, 1 - slot, row0 + j)
        b = jnp.dot(xb, wu_ref[0], preferred_element_type=F32)
        for j in range(per // 2, per):
            issue_row(i + 1, 1 - slot, row0 + j)
        h = (a * jax.nn.sigmoid(a) * b).astype(BF16)
        part = jnp.dot(h, wd_ref[0], preferred_element_type=F32)

        @pl.when(f == 0)
        def _():
            acc_ref[...] = part

        @pl.when(f > 0)
        def _():
            acc_ref[...] += part

    @pl.when(f == nf - 1)
    def _():
        o_ref[...] = jnp.where(valid, acc_ref[...], 0.0)


def moe_experts(hn, tile_expert, tile_valid, row_tok, wg, wu, wd, *, tm, tf):
    T, D = hn.shape
    F = wg.shape[2]
    P = row_tok.shape[0] - tm
    nt = P // tm + 1
    nf = F // tf

    def w_in_map(i, f, te, tv, tok):
        return (te[i], 0, jnp.where(tv[i] > 0, f, nf - 1))

    def w_out_map(i, f, te, tv, tok):
        return (te[i], jnp.where(tv[i] > 0, f, nf - 1), 0)

    grid_spec = pltpu.PrefetchScalarGridSpec(
        num_scalar_prefetch=3,
        grid=(nt, nf),
        in_specs=[
            pl.BlockSpec(memory_space=pl.ANY),
            pl.BlockSpec((1, D, tf), w_in_map),
            pl.BlockSpec((1, D, tf), w_in_map),
            pl.BlockSpec((1, tf, D), w_out_map),
        ],
        out_specs=pl.BlockSpec((tm, D), lambda i, f, te, tv, tok: (i, 0)),
        scratch_shapes=[pltpu.VMEM((2, tm, D), F32), pltpu.VMEM((tm, D), BF16), pltpu.VMEM((tm, D), F32),
                        pltpu.SemaphoreType.DMA((2,))],
    )
    return pl.pallas_call(
        functools.partial(_expert_kernel, nf=nf),
        grid_spec=grid_spec,
        out_shape=jax.ShapeDtypeStruct((P + tm, D), F32),
        compiler_params=_cparams(("arbitrary", "arbitrary")),
        name="moe_experts",
    )(tile_expert, tile_valid, row_tok, hn, wg, wu, wd)


def _combine_kernel(dest_ref, x_ref, gate_ref, yb_hbm, o_ref, buf_ref, sem):
    i = pl.program_id(0)
    n = pl.num_programs(0)
    tc = x_ref.shape[0]
    slot = i % 2

    def issue(step, dst_slot):
        base = step * tc * TOP_K

        for r in range(tc):
            for k in range(TOP_K):
                pltpu.make_async_copy(yb_hbm.at[pl.ds(dest_ref[base + r * TOP_K + k], 1), :],
                                      buf_ref.at[dst_slot, k, pl.ds(r, 1), :], sem.at[dst_slot]).start()

    @pl.when(i == 0)
    def _():
        issue(0, 0)

    for k in range(TOP_K):
        pltpu.make_async_copy(yb_hbm.at[pl.ds(0, tc), :], buf_ref.at[slot, k], sem.at[slot]).wait()

    @pl.when(i + 1 < n)
    def _():
        issue(i + 1, 1 - slot)

    g = gate_ref[...]
    o_ref[...] = x_ref[...] + g[:, 0:1] * buf_ref[slot, 0] + g[:, 1:2] * buf_ref[slot, 1]


def moe_combine(x, gates, yb, dest, *, tc=256):
    T, D = x.shape
    grid_spec = pltpu.PrefetchScalarGridSpec(
        num_scalar_prefetch=1,
        grid=(T // tc,),
        in_specs=[
            pl.BlockSpec((tc, D), lambda i, d: (i, 0)),
            pl.BlockSpec((tc, LANES), lambda i, d: (i, 0)),
            pl.BlockSpec(memory_space=pl.ANY),
        ],
        out_specs=pl.BlockSpec((tc, D), lambda i, d: (i, 0)),
        scratch_shapes=[pltpu.VMEM((2, TOP_K, tc, D), F32), pltpu.SemaphoreType.DMA((2,))],
    )
    return pl.pallas_call(
        _combine_kernel,
        grid_spec=grid_spec,
        out_shape=jax.ShapeDtypeStruct((T, D), F32),
        compiler_params=_cparams(("arbitrary",)),
        name="moe_combine",
    )(dest, x, gates, yb)


def moe_dispatch_plan(idx, *, tm):
    T = idx.shape[0]
    A = T * TOP_K
    e_flat = idx[:, :TOP_K].reshape(A)
    onehot = (e_flat[:, None] == jnp.arange(N_EXPERTS, dtype=jnp.int32)[None, :]).astype(jnp.int32)
    rank = jnp.sum((jnp.cumsum(onehot, axis=0) - onehot) * onehot, axis=1)
    counts = jnp.sum(onehot, axis=0)
    padded = (counts + tm - 1) // tm * tm
    ends = jnp.cumsum(padded)
    pstart = ends - padded
    dest = pstart[e_flat] + rank
    P = (A // tm + N_EXPERTS + 1) * tm
    nt = P // tm
    tok_flat = jnp.arange(A, dtype=jnp.int32) // TOP_K
    row_tok = jnp.zeros((P,), jnp.int32).at[dest].set(tok_flat)
    tile_start = jnp.arange(nt, dtype=jnp.int32) * tm
    tile_expert = jnp.minimum(jnp.sum((tile_start[:, None] >= ends[None, :]).astype(jnp.int32), axis=1), N_EXPERTS - 1)
    tile_valid = (tile_start < ends[-1]).astype(jnp.int32)
    last_valid = jnp.maximum(jnp.sum(tile_valid) - 1, 0)
    tile_expert = jnp.where(tile_valid > 0, tile_expert, tile_expert[last_valid])
    return dest.astype(jnp.int32), row_tok, tile_expert, tile_valid


def moe_layer(x, norm2, router, wg, wu, wd, *, tm=512, tf=1792, tr=512, tc=256):
    hn, idx, gates = moe_router(x, norm2, router, tm=tr)
    dest, row_tok, tile_expert, tile_valid = moe_dispatch_plan(idx, tm=tm)
    yb = moe_experts(hn, tile_expert, tile_valid, row_tok, wg, wu, wd, tm=tm, tf=tf)
    return moe_combine(x, gates, yb, dest, tc=tc)


def _even_layer(x, norm1, w_in, shift_mu, w0_f, w_up_f, w0_b, w_up_b, a0_f, a_up_f, a0_b, a_up_b, g_up,
                k_k, k_a, r_k, gn_w, gn_b, q_norm, k_norm, w_out, norm2, ffn_gate, ffn_up, ffn_down, cast):
    Bn, S, D = x.shape
    T = Bn * S
    xf = x.reshape(T, D)
    w_in = w_in.astype(BF16)
    p_att = norm_matmul(xf, norm1, w_in[:, RW_IN:], tm=2048, tn=ATT_IN).reshape(Bn, S, ATT_IN)
    y_a = rwkv7_mixer(x, norm1, w_in[:, :RW_IN], shift_mu, w0_f, w_up_f, w0_b, w_up_b, a0_f, a_up_f, a0_b, a_up_b,
                      g_up, k_k, k_a, r_k, gn_w, gn_b, out_dtype=BF16)
    y_b = gqa_attention(p_att, q_norm, k_norm, out_dtype=BF16)
    w_out = w_out.astype(BF16)
    xf = matmul_residual([y_a.reshape(T, RW_DIM), y_b.reshape(T, ATT_DIM)], [w_out[:RW_DIM], w_out[RW_DIM:]], xf,
                         tm=1024, tn=1024)
    return ffn_swiglu(xf, norm2, ffn_gate.astype(BF16), ffn_up.astype(BF16), ffn_down.astype(BF16), tm=512, tf=1408,
                      cast=cast)


def _odd_layer(xf, Bn, S, norm1, w_in, gate_up_f, gate_bias_f, gate_up_b, gate_bias_b, out_norm, w_out,
               norm2, router, exp_gate, exp_up, exp_down):
    T, D = xf.shape
    o3 = 2 * GLA_DK + GLA_DV
    o4 = o3 + GLA_GATE_RANK
    w_main = jnp.concatenate([w_in[:, :o3], w_in[:, o4:]], axis=1).astype(BF16)
    w_gd = jnp.zeros((D, LANES), F32).at[:, :GLA_GATE_RANK].set(w_in[:, o3:o4])
    p_main = norm_matmul(xf, norm1, w_main, tm=2048, tn=1024).reshape(Bn, S, -1)
    gd = norm_matmul(xf, norm1, w_gd, tm=1024, tn=LANES)[:, :GLA_GATE_RANK].reshape(Bn, S, GLA_GATE_RANK)
    o = gla_mixer(p_main, gd, gate_up_f, gate_bias_f, gate_up_b, gate_bias_b, out_norm, out_dtype=BF16)
    xf = matmul_residual([o.reshape(T, GLA_DV)], [w_out.astype(BF16)], xf, tm=1024, tn=1024)
    return moe_layer(xf, norm2, router, exp_gate, exp_up, exp_down)


def kernel(x, e_norm1, e_w_in, e_shift_mu, e_w0_f, e_w_up_f, e_w0_b, e_w_up_b, e_a0_f, e_a_up_f, e_a0_b, e_a_up_b, e_g_up, e_k_k, e_k_a, e_r_k, e_gn_w, e_gn_b, e_q_norm, e_k_norm, e_w_out, e_norm2, e_ffn_gate, e_ffn_up, e_ffn_down, o_norm1, o_w_in, o_gate_up_f, o_gate_bias_f, o_gate_up_b, o_gate_bias_b, o_out_norm, o_w_out, o_norm2, o_router, o_exp_gate, o_exp_up, o_exp_down):
    Bn, S, D = x.shape
    xf, (exp_gate, exp_up, exp_down) = _even_layer(
        x, e_norm1[0], e_w_in[0], e_shift_mu[0], e_w0_f[0], e_w_up_f[0], e_w0_b[0],
        e_w_up_b[0], e_a0_f[0], e_a_up_f[0], e_a0_b[0], e_a_up_b[0], e_g_up[0], e_k_k[0],
        e_k_a[0], e_r_k[0], e_gn_w[0], e_gn_b[0], e_q_norm[0], e_k_norm[0], e_w_out[0],
        e_norm2[0], e_ffn_gate[0], e_ffn_up[0], e_ffn_down[0], (o_exp_gate[0], o_exp_up[0], o_exp_down[0]))
    xf = _odd_layer(xf, Bn, S, o_norm1[0], o_w_in[0], o_gate_up_f[0], o_gate_bias_f[0], o_gate_up_b[0],
                    o_gate_bias_b[0], o_out_norm[0], o_w_out[0], o_norm2[0], o_router[0],
                    exp_gate, exp_up, exp_down)
    return xf.reshape(Bn, S, D)
```

```python
import functools

import jax
import jax.numpy as jnp
from jax import lax
from jax.experimental import pallas as pl
from jax.experimental.pallas import tpu as pltpu

F32 = jnp.float32
BF16 = jnp.bfloat16
HI = lax.Precision.HIGHEST

D_MODEL = 1024
GRID_W = 64
HEAD_DIM = 64
NORM_EPS = 1e-6
RW_HEADS = 8
RW_DIM = 512
DECAY_RANK = 64
ICLR_RANK = 64
GATE_RANK = 128
RWKV_GN_EPS = 64e-5
RW_IN = 3 * RW_DIM + DECAY_RANK + ICLR_RANK + GATE_RANK
RW_CHUNK = 64
ATT_HEADS = 8
ATT_KV_HEADS = 2
ATT_DIM = 512
ATT_KV_DIM = 128
ATT_IN = ATT_DIM + 2 * ATT_KV_DIM
ROPE_THETA = 10000.0
ATT_VT_ROWS = HEAD_DIM + 16
LOG2_E = 1.4426950408889634
ATT_LOCKSTEP = 2
GLA_HEADS = 4
GLA_DK = 512
GLA_DV = 1024
GLA_DKH = 128
GLA_DVH = 256
GLA_GATE_RANK = 16
GLA_GATE_NORM = 16.0
GLA_CHUNK = 64
N_EXPERTS = 8
TOP_K = 2
LANES = 128

VMEM_LIMIT = 48 * 1024 * 1024
VMEM_LIMIT_BIG = 56 * 1024 * 1024


def _cparams(sem, vmem=VMEM_LIMIT):
    return pltpu.CompilerParams(dimension_semantics=sem, vmem_limit_bytes=vmem)


def _iota2(shape, dim):
    return lax.broadcasted_iota(jnp.int32, shape, dim)


def _split(x, parts):
    out = []
    for _ in range(parts):
        t = x.astype(BF16)
        out.append(t)
        x = x - t.astype(F32)
    return out


def _dot_sel(x, m, parts=2):
    mb = m.astype(BF16)
    acc = None
    for t in _split(x, parts):
        d = jnp.dot(t, mb, preferred_element_type=F32)
        acc = d if acc is None else acc + d
    return acc


def _sel_dot(m, x, parts=3):
    mb = m.astype(BF16)
    acc = None
    for t in _split(x, parts):
        d = jnp.dot(mb, t, preferred_element_type=F32)
        acc = d if acc is None else acc + d
    return acc


def _dot3(a, b):
    ah, al = _split(a, 2)
    bh, bl = _split(b, 2)
    d = lambda u, v: jnp.dot(u, v, preferred_element_type=F32)
    return d(ah, bh) + (d(ah, bl) + d(al, bh))


def _bdot(a, b):
    return jnp.dot(a.astype(BF16), b.astype(BF16), preferred_element_type=F32)


def _bdot_nt(a, b):
    return lax.dot_general(a.astype(BF16), b.astype(BF16), (((1,), (1,)), ((), ())), preferred_element_type=F32)


def _bdot_tn(a, b):
    return lax.dot_general(a.astype(BF16), b.astype(BF16), (((0,), (0,)), ((), ())), preferred_element_type=F32)


def _norm_mm_kernel(x_ref, g_ref, w_ref, o_ref, xn_ref):
    @pl.when(pl.program_id(1) == 0)
    def _():
        x = x_ref[...]
        ms = jnp.mean(x * x, axis=-1, keepdims=True)
        xn_ref[...] = (x * lax.rsqrt(ms + NORM_EPS) * g_ref[...]).astype(xn_ref.dtype)

    if xn_ref.dtype == F32:
        o_ref[...] = _dot3(xn_ref[...], w_ref[...]).astype(o_ref.dtype)
    else:
        o_ref[...] = jnp.dot(xn_ref[...], w_ref[...], preferred_element_type=F32).astype(o_ref.dtype)


def norm_matmul(x, g, w, *, tm, tn, out_dtype=F32):
    M, K = x.shape
    N = w.shape[1]
    return pl.pallas_call(
        _norm_mm_kernel,
        grid=(M // tm, N // tn),
        in_specs=[
            pl.BlockSpec((tm, K), lambda i, j: (i, 0)),
            pl.BlockSpec((1, K), lambda i, j: (0, 0)),
            pl.BlockSpec((K, tn), lambda i, j: (0, j)),
        ],
        out_specs=pl.BlockSpec((tm, tn), lambda i, j: (i, j)),
        out_shape=jax.ShapeDtypeStruct((M, N), out_dtype),
        scratch_shapes=[pltpu.VMEM((tm, K), w.dtype)],
        compiler_params=_cparams(("parallel", "arbitrary")),
        name="norm_matmul",
    )(x, g.reshape(1, K), w)


def _mm_res_kernel(*refs):
    n = (len(refs) - 2) // 2
    r_ref, o_ref = refs[2 * n], refs[2 * n + 1]
    acc = r_ref[...]
    for y_ref, w_ref in zip(refs[:n], refs[n:2 * n]):
        acc = acc + jnp.dot(y_ref[...].astype(BF16), w_ref[...], preferred_element_type=F32)
    o_ref[...] = acc


def matmul_residual(ys, ws, r, *, tm, tn):
    M, N = r.shape
    y_specs = [pl.BlockSpec((tm, y.shape[1]), lambda i, j: (i, 0)) for y in ys]
    w_specs = [pl.BlockSpec((w.shape[0], tn), lambda i, j: (0, j)) for w in ws]
    return pl.pallas_call(
        _mm_res_kernel,
        grid=(M // tm, N // tn),
        in_specs=y_specs + w_specs + [pl.BlockSpec((tm, tn), lambda i, j: (i, j))],
        out_specs=pl.BlockSpec((tm, tn), lambda i, j: (i, j)),
        out_shape=jax.ShapeDtypeStruct((M, N), F32),
        compiler_params=_cparams(("parallel", "arbitrary")),
        name="matmul_residual",
    )(*ys, *ws, r)


def _ffn_kernel(*refs, n_cast):
    x_ref, g_ref, wg_ref, wu_ref, wd_ref = refs[:5]
    cast_in = refs[5:5 + n_cast]
    o_ref = refs[5 + n_cast]
    cast_out = refs[6 + n_cast:6 + 2 * n_cast]
    xn_ref, acc_ref = refs[6 + 2 * n_cast:]
    f = pl.program_id(1)

    @pl.when(f == 0)
    def _():
        x = x_ref[...]
        ms = jnp.mean(x * x, axis=-1, keepdims=True)
        xn_ref[...] = (x * lax.rsqrt(ms + NORM_EPS) * g_ref[...]).astype(BF16)

    for src, dst in zip(cast_in, cast_out):
        dst[...] = src[...].astype(BF16)

    xn = xn_ref[...]
    a = jnp.dot(xn, wg_ref[...], preferred_element_type=F32)
    b = jnp.dot(xn, wu_ref[...], preferred_element_type=F32)
    h = (a * jax.nn.sigmoid(a) * b).astype(BF16)
    part = jnp.dot(h, wd_ref[...], preferred_element_type=F32)

    @pl.when(f == 0)
    def _():
        acc_ref[...] = part

    @pl.when(f > 0)
    def _():
        acc_ref[...] += part

    @pl.when(f == pl.num_programs(1) - 1)
    def _():
        o_ref[...] = x_ref[...] + acc_ref[...]


def ffn_swiglu(x, g, wg, wu, wd, *, tm, tf, cast=()):
    M, D = x.shape
    F = wg.shape[1]
    nf = F // tf
    steps = (M // tm) * nf
    cast_specs, cast_shapes = [], []
    for w in cast:
        E, R, C = w.shape
        per = steps // E
        cast_specs.append(pl.BlockSpec((1, R // per, C), lambda i, f, per=per: ((i * nf + f) // per, (i * nf + f) % per, 0)))
        cast_shapes.append(jax.ShapeDtypeStruct(w.shape, BF16))
    outs = pl.pallas_call(
        functools.partial(_ffn_kernel, n_cast=len(cast)),
        grid=(M // tm, nf),
        in_specs=[
            pl.BlockSpec((tm, D), lambda i, f: (i, 0)),
            pl.BlockSpec((1, D), lambda i, f: (0, 0)),
            pl.BlockSpec((D, tf), lambda i, f: (0, f)),
            pl.BlockSpec((D, tf), lambda i, f: (0, f)),
            pl.BlockSpec((tf, D), lambda i, f: (f, 0)),
        ] + cast_specs,
        out_specs=[pl.BlockSpec((tm, D), lambda i, f: (i, 0))] + cast_specs,
        out_shape=[jax.ShapeDtypeStruct((M, D), F32)] + cast_shapes,
        scratch_shapes=[pltpu.VMEM((tm, D), BF16), pltpu.VMEM((tm, D), F32)],
        compiler_params=_cparams(("arbitrary", "arbitrary"), vmem=VMEM_LIMIT_BIG if cast else VMEM_LIMIT),
        name="ffn_swiglu",
    )(x, g.reshape(1, D), wg, wu, wd, *cast)
    return outs[0], tuple(outs[1:])


def _rwkv_prep_kernel(cur_ref, prev_ref, next_ref, n1_ref, win_ref, mu_ref, w0f_ref, wupf_ref, w0b_ref, wupb_ref,
                      a0f_ref, aupf_ref, a0b_ref, aupb_ref, gup_ref, kk_ref, hsum_ref,
                      r_ref, k_ref, v_ref, kkn_ref, g_ref, lwf_ref, lf_ref, asf_ref, lwb_ref, lb_ref, asb_ref):
    i = pl.program_id(1)
    nt = pl.num_programs(1)

    def project(xv):
        ms = jnp.mean(xv * xv, axis=-1, keepdims=True)
        xn = (xv * lax.rsqrt(ms + NORM_EPS) * n1_ref[...]).astype(BF16)
        return jnp.dot(xn, win_ref[...], preferred_element_type=F32)

    Tt = cur_ref.shape[1]
    pall = project(jnp.concatenate([cur_ref[0], prev_ref[0], next_ref[0]], axis=0))
    x = pall[:Tt]
    halo = pall[Tt:]
    row = _iota2(x.shape, 0)
    prev_row = jnp.where(i > 0, halo[7:8, :], 0.0)
    next_row = jnp.where(i < nt - 1, halo[8:9, :], 0.0)
    prev = jnp.where(row == 0, prev_row, pltpu.roll(x, 1, 0))
    nxt = jnp.where(row == Tt - 1, next_row, pltpu.roll(x, Tt - 1, 0))
    pm = x + (0.5 * (prev + nxt) - x) * mu_ref[...]
    o1 = 3 * RW_DIM
    o2 = o1 + DECAY_RANK
    o3 = o2 + ICLR_RANK
    r_ref[0] = pm[:, :RW_DIM]
    k = pm[:, RW_DIM:2 * RW_DIM]
    k_ref[0] = k
    v_ref[0] = pm[:, 2 * RW_DIM:o1]
    wd = jnp.tanh(pm[:, o1:o2])
    ad = pm[:, o2:o3]
    gin = jax.nn.sigmoid(pm[:, o3:])
    g_ref[0] = jnp.dot(gin.astype(BF16), gup_ref[...].astype(BF16), preferred_element_type=F32)
    kk = k * kk_ref[...]
    ss = _dot_sel(kk * kk, hsum_ref[...])
    kkn_ref[0] = kk / jnp.maximum(jnp.sqrt(ss), 1e-12)

    ci = _iota2((RW_CHUNK, RW_CHUNK), 0)
    cj = _iota2((RW_CHUNK, RW_CHUNK), 1)
    tril = (cj <= ci).astype(F32)
    triu = (cj >= ci).astype(F32)

    def direction(w0_ref, wup_ref, a0_ref, aup_ref, tri, lw_ref, l_ref, as_ref):
        z = w0_ref[...] + _dot3(wd, wup_ref[...])
        sp = jnp.maximum(-z, 0.0) + jnp.log(1.0 + jnp.exp(-jnp.abs(z)))
        lw = -jnp.exp(-sp - 0.5)
        lw_ref[0] = lw
        as_ref[0] = jax.nn.sigmoid(a0_ref[...] + _dot3(ad, aup_ref[...]))
        for c in range(Tt // RW_CHUNK):
            sl = slice(c * RW_CHUNK, (c + 1) * RW_CHUNK)
            l_ref[0, sl, :] = _sel_dot(tri, lw[sl, :])

    direction(w0f_ref, wupf_ref, a0f_ref, aupf_ref, tril, lwf_ref, lf_ref, asf_ref)
    direction(w0b_ref, wupb_ref, a0b_ref, aupb_ref, triu, lwb_ref, lb_ref, asb_ref)


def rwkv_prep(x, norm1, w_in, shift_mu, w0_f, w_up_f, w0_b, w_up_b, a0_f, a_up_f, a0_b, a_up_b, g_up, k_k, *, tt=256):
    B, S, D = x.shape
    nt = S // tt
    hsum = (jnp.arange(RW_DIM)[:, None] // HEAD_DIM == jnp.arange(RW_DIM)[None, :] // HEAD_DIM).astype(F32)
    row = lambda a: a.reshape(1, -1)
    full = lambda a: pl.BlockSpec(a.shape, lambda b, i: (0,) * a.ndim)
    params = [row(norm1), w_in, row(shift_mu), row(w0_f), w_up_f, row(w0_b), w_up_b, row(a0_f), a_up_f, row(a0_b), a_up_b,
              g_up, row(k_k), hsum]
    tb = tt // 8
    in_specs = [
        pl.BlockSpec((1, tt, D), lambda b, i: (b, i, 0)),
        pl.BlockSpec((1, 8, D), lambda b, i: (b, jnp.maximum(i * tb - 1, 0), 0)),
        pl.BlockSpec((1, 8, D), lambda b, i: (b, jnp.minimum((i + 1) * tb, S // 8 - 1), 0)),
    ] + [full(a) for a in params]
    out_spec = pl.BlockSpec((1, tt, RW_DIM), lambda b, i: (b, i, 0))
    out_sds = jax.ShapeDtypeStruct((B, S, RW_DIM), F32)
    return pl.pallas_call(
        _rwkv_prep_kernel,
        grid=(B, nt),
        in_specs=in_specs,
        out_specs=[out_spec] * 11,
        out_shape=[out_sds] * 11,
        compiler_params=_cparams(("parallel", "arbitrary")),
        name="rwkv_prep",
    )(x, x, x, *params)


def _bd(y, bd_mask):
    return jnp.where(bd_mask, jnp.concatenate([y, y], axis=0), jnp.zeros((), y.dtype))


def _chunk_terms(chains, out, bd_mask):
    C = chains[0]["r"].shape[0]
    n = len(chains)
    R = range(n)
    dot = lambda u, v: jnp.dot(u, v, preferred_element_type=F32)
    bd = lambda y: _bd(y, bd_mask)
    pre = []
    for ch in chains:
        r, k, v, kk, lw, L, asig, ka, rev = (ch[x] for x in ("r", "k", "v", "kk", "lw", "L", "asig", "ka", "rev"))
        a = -kk
        b = kk * asig
        kd = k * (1.0 + (asig - 1.0) * ka)
        Lp = L - lw
        Lr = Lp if rev else L
        Lend = L[0:1, :] if rev else L[C - 1:C, :]
        Lmid = L[C // 2:C // 2 + 1, :]
        einv = jnp.exp(Lmid - L)
        eend = jnp.exp(Lend - L)
        pre.append(dict(
            ar=jnp.concatenate([a * jnp.exp(Lp - Lmid), r * jnp.exp(Lr - Lmid)], axis=0).astype(BF16),
            bt=bd((b * einv).astype(BF16)), kt=bd((kd * einv).astype(BF16)),
            a0=bd((a * jnp.exp(Lp)).astype(BF16)), r0=r * jnp.exp(Lr),
            bh=(b * eend).astype(BF16), kh=(kd * eend).astype(BF16), vb=v.astype(BF16),
            dec=jnp.exp(Lend)))
    mk = [ch["masks"] for ch in chains]
    Ab = [_bdot_nt(pre[i]["ar"], pre[i]["bt"]) for i in R]
    Ak = [_bdot_nt(pre[i]["ar"], pre[i]["kt"]) for i in R]
    yield
    Aab = [jnp.where(mk[i][0], Ab[i][:C], 0.0) for i in R]
    Arb = [jnp.where(mk[i][1], Ab[i][C:], 0.0).astype(BF16) for i in R]
    AakArk = [jnp.concatenate([jnp.where(mk[i][0], Ak[i][:C], 0.0), jnp.where(mk[i][1], Ak[i][C:], 0.0)],
                              axis=0).astype(BF16) for i in R]
    AV = [dot(AakArk[i], bd(pre[i]["vb"])) for i in R]
    Xb = [jnp.where(mk[i][2], Aab[i], 0.0).astype(BF16) for i in R]
    X2b = [dot(Xb[i], bd(Xb[i])).astype(BF16) for i in R]
    yield
    T = [mk[i][6] + Xb[i].astype(F32) for i in R]
    T = [T[i] + dot(T[i].astype(BF16), bd(X2b[i])) for i in R]
    X4b = [dot(X2b[i], bd(X2b[i])).astype(BF16) for i in R]
    yield
    T = [T[i] + dot(T[i].astype(BF16), bd(X4b[i])) for i in R]
    yield
    for lvl in (3, 4, 5):
        Tb = [T[i].astype(BF16) for i in R]
        ET = [dot(jnp.where(mk[i][lvl], Aab[i], 0.0).astype(BF16), bd(Tb[i])).astype(BF16) for i in R]
        yield
        T = [T[i] + dot(Tb[i], bd(ET[i])) for i in R]
        yield
    Tb = [T[i].astype(BF16) for i in R]
    A0p = [dot(Tb[i], pre[i]["a0"]).astype(BF16) for i in R]
    Uv = [dot(Tb[i], bd(AV[i][:C].astype(BF16))).astype(BF16) for i in R]
    yield
    Rpp = [pre[i]["r0"] + dot(Arb[i], bd(A0p[i])) for i in R]
    Yv = [dot(Arb[i], bd(Uv[i])) + AV[i][C:] for i in R]
    eye2 = (_iota2(bd_mask.shape, 0) == _iota2(bd_mask.shape, 1)).astype(F32)
    P = [jnp.where(bd_mask, _bdot_tn(pre[i]["bh"], A0p[i]), 0.0) + eye2 * pre[i]["dec"] for i in R]
    Q = [jnp.where(bd_mask, _bdot_tn(pre[i]["bh"], Uv[i]) + _bdot_tn(pre[i]["kh"], pre[i]["vb"]), 0.0) for i in R]
    out.extend((Rpp[i], Yv[i], P[i], Q[i]) for i in R)


def _make_masks(C, N, rev):
    ii = _iota2((C, 2 * N), 0)
    jj = _iota2((C, 2 * N), 1) % N
    strict = (jj > ii) if rev else (jj < ii)
    rmask = strict if rev else (jj <= ii)
    blk = lambda s: (ii // s) == (jj // s)
    m8 = blk(8)
    e16 = blk(16) & jnp.logical_not(blk(8))
    e32 = blk(32) & jnp.logical_not(blk(16))
    e64 = jnp.logical_not(blk(32))
    eye = (ii == jj).astype(F32)
    return strict, rmask, m8, e16, e32, e64, eye


RW_UNROLL = 8


def _rwkv_scan_kernel(r_ref, k_ref, v_ref, kk_ref, g_ref, lwf_ref, lf_ref, asf_ref, lwb_ref, lb_ref, asb_ref,
                      ka_ref, rk_ref, gnw_ref, gnb_ref, havg_ref, o_ref,
                      yf_ref, yb_ref, h_ref, rpp_ref, yv_ref, p_ref, q_ref):
    S = r_ref.shape[1]
    C = RW_CHUNK
    N = HEAD_DIM
    W = 2 * N
    n = S // C
    U = RW_UNROLL
    nset = n // U
    masks_f = _make_masks(C, N, False)
    masks_b = _make_masks(C, N, True)
    bd_mask = (_iota2((W, W), 0) // N) == (_iota2((W, W), 1) // N)
    h_ref[...] = jnp.zeros_like(h_ref)
    dirs = ((False, lwf_ref, lf_ref, asf_ref, yf_ref, masks_f), (True, lwb_ref, lb_ref, asb_ref, yb_ref, masks_b))

    def chunk_rows(rev, st, u):
        c = st * U + u
        c = (n - 1 - c) if rev else c
        return pl.ds(pl.multiple_of(c * C, C), C)

    def state_step(st, u):
        slot = st % 2
        for di, (rev, lw_ref, l_ref, as_ref, y_ref, masks) in enumerate(dirs):
            Hb = h_ref[di].astype(BF16)
            y_ref[chunk_rows(rev, st, u), :] = (jnp.dot(rpp_ref[slot, di, u], Hb, preferred_element_type=F32)
                                                + yv_ref[slot, di, u])
            h_ref[di] = jnp.dot(p_ref[slot, di, u], Hb, preferred_element_type=F32) + q_ref[slot, di, u]

    def chunk_set(st, state_of):
        chains, slots = [], []
        for u in range(U):
            for di, (rev, lw_ref, l_ref, as_ref, y_ref, masks) in enumerate(dirs):
                rows = chunk_rows(rev, st, u)
                chains.append(dict(
                    r=r_ref[0, rows, :], k=k_ref[0, rows, :], v=v_ref[0, rows, :], kk=kk_ref[0, rows, :],
                    lw=lw_ref[0, rows, :], L=l_ref[0, rows, :], asig=as_ref[0, rows, :], ka=ka_ref[...],
                    rev=rev, masks=masks))
                slots.append((di, u))
        res = []
        pending = list(range(U)) if state_of is not None else []
        for stage, _ in enumerate(_chunk_terms(chains, res, bd_mask)):
            if pending and stage % 2 == 1:
                state_step(state_of, pending.pop(0))
        for u in pending:
            state_step(state_of, u)
        slot = st % 2
        for (di, u), (Rpp, Yv, P, Q) in zip(slots, res):
            rpp_ref[slot, di, u] = Rpp.astype(BF16)
            yv_ref[slot, di, u] = Yv
            p_ref[slot, di, u] = P.astype(BF16)
            q_ref[slot, di, u] = Q

    chunk_set(0, None)

    def body(st, carry):
        chunk_set(st, st - 1)
        return carry

    lax.fori_loop(1, nset, body, 0)
    for u in range(U):
        state_step(nset - 1, u)

    RT = 256

    def fin(i, carry):
        rows = pl.ds(pl.multiple_of(i * RT, RT), RT)
        y = yf_ref[rows, :] + yb_ref[rows, :]
        mean = _dot_sel(y, havg_ref[...])
        d = y - mean
        var = _dot_sel(d * d, havg_ref[...])
        yn = d * lax.rsqrt(var + RWKV_GN_EPS) * gnw_ref[...] + gnb_ref[...]
        r = r_ref[0, rows, :]
        kf = k_ref[0, rows, :] * (1.0 + (asf_ref[0, rows, :] - 1.0) * ka_ref[...])
        bonus = _dot_sel(r * kf * rk_ref[...], havg_ref[...]) * float(N) * v_ref[0, rows, :]
        o_ref[0, rows, :] = ((yn + bonus) * g_ref[0, rows, :]).astype(o_ref.dtype)
        return carry

    lax.fori_loop(0, S // RT, fin, 0)


def rwkv_scan(r, k, v, kk, g, lw_f, l_f, as_f, lw_b, l_b, as_b, k_a, r_k, gn_w, gn_b, out_dtype=F32):
    B, S, _ = r.shape
    W = 2 * HEAD_DIM
    N = HEAD_DIM
    havg = (jnp.arange(W)[:, None] // HEAD_DIM == jnp.arange(W)[None, :] // HEAD_DIM).astype(F32) / HEAD_DIM
    seq = pl.BlockSpec((1, S, W), lambda b, h: (b, 0, h))
    par = pl.BlockSpec((1, W), lambda b, h: (0, h))
    row = lambda a: a.reshape(1, -1)
    U = RW_UNROLL
    return pl.pallas_call(
        _rwkv_scan_kernel,
        grid=(B, RW_DIM // W),
        in_specs=[seq] * 11 + [par] * 4 + [pl.BlockSpec((W, W), lambda b, h: (0, 0))],
        out_specs=seq,
        out_shape=jax.ShapeDtypeStruct((B, S, RW_DIM), out_dtype),
        scratch_shapes=[pltpu.VMEM((S, W), F32), pltpu.VMEM((S, W), F32), pltpu.VMEM((2, W, W), F32),
                        pltpu.VMEM((2, 2, U, RW_CHUNK, W), BF16), pltpu.VMEM((2, 2, U, RW_CHUNK, W), F32),
                        pltpu.VMEM((2, 2, U, W, W), BF16), pltpu.VMEM((2, 2, U, W, W), F32)],
        compiler_params=_cparams(("parallel", "parallel")),
        name="rwkv_scan",
    )(r, k, v, kk, g, lw_f, l_f, as_f, lw_b, l_b, as_b, row(k_a), row(r_k), row(gn_w), row(gn_b), havg)


def rwkv7_mixer(x, norm1, w_in, shift_mu, w0_f, w_up_f, w0_b, w_up_b, a0_f, a_up_f, a0_b, a_up_b, g_up, k_k, k_a,
                r_k, gn_w, gn_b, out_dtype=F32, tt=512):
    outs = rwkv_prep(x, norm1, w_in, shift_mu, w0_f, w_up_f, w0_b, w_up_b, a0_f, a_up_f, a0_b, a_up_b, g_up, k_k, tt=tt)
    return rwkv_scan(*outs, k_a, r_k, gn_w, gn_b, out_dtype=out_dtype)


def _head_norm_rope(x, gain, cos, sin_signed, havg):
    ms = _dot_sel(x * x, havg)
    xn = x * lax.rsqrt(ms + NORM_EPS) * gain
    W = x.shape[1]
    even = (_iota2(x.shape, 1) % 2) == 0
    partner = jnp.where(even, pltpu.roll(xn, W - 1, 1), pltpu.roll(xn, 1, 1))
    return xn * cos + partner * sin_signed


def _attn_kernel(q_ref, k_ref, v_ref, cosq_ref, sinq_ref, cosk_ref, sink_ref, qg_ref, kg_ref, hq_ref, hk_ref,
                 o_ref, ks_ref, vt_ref):
    i = pl.program_id(1)
    G = ATT_HEADS // ATT_KV_HEADS
    D = HEAD_DIM

    @pl.when(i == 0)
    def _():
        kr = _head_norm_rope(k_ref[0], kg_ref[...], cosk_ref[...], sink_ref[...], hk_ref[...])
        vt = jnp.transpose(v_ref[0])
        ones = jnp.ones((ATT_VT_ROWS - D, vt.shape[1]), BF16)
        for kv in range(ATT_KV_HEADS):
            ks_ref[kv] = kr[:, kv * D:(kv + 1) * D].astype(BF16)
            vt_ref[kv] = jnp.concatenate([vt[kv * D:(kv + 1) * D, :].astype(BF16), ones], axis=0)

    q = _head_norm_rope(q_ref[0], qg_ref[...], cosq_ref[...], sinq_ref[...], hq_ref[...]) * (D ** -0.5 * LOG2_E)
    qb = q.astype(BF16)
    outs = []
    for h0 in range(0, ATT_HEADS, ATT_LOCKSTEP):
        hs = range(h0, h0 + ATT_LOCKSTEP)
        st = [lax.dot_general(ks_ref[h // G], qb[:, h * D:(h + 1) * D], (((1,), (1,)), ((), ())),
                              preferred_element_type=F32) for h in hs]
        m = [jnp.max(x, axis=0, keepdims=True) for x in st]
        p = [jnp.exp2(x - mm).astype(BF16) for x, mm in zip(st, m)]
        ot = [jnp.dot(vt_ref[h // G], x, preferred_element_type=F32) for h, x in zip(hs, p)]
        outs.extend(o[:D] / o[D:D + 1] for o in ot)
    pairs = [jnp.transpose(jnp.concatenate(outs[2 * j:2 * j + 2], axis=0)) for j in range(ATT_HEADS // 2)]
    o_ref[0] = jnp.concatenate(pairs, axis=1).astype(o_ref.dtype)


def _rope_tables(S):
    rows = S // GRID_W
    row = jnp.repeat(jnp.arange(rows), GRID_W).astype(F32)
    col = jnp.tile(jnp.arange(GRID_W), rows).astype(F32)
    half = HEAD_DIM // 2
    freq = ROPE_THETA ** (-jnp.arange(0, half, 2, dtype=F32) / half)
    ang = jnp.concatenate([row[:, None] * freq, col[:, None] * freq], axis=-1)
    cos = jnp.repeat(jnp.cos(ang), 2, axis=-1)
    sin = jnp.repeat(jnp.sin(ang), 2, axis=-1)
    sign = jnp.where(jnp.arange(HEAD_DIM) % 2 == 0, -1.0, 1.0).astype(F32)
    return cos, sin * sign


def gqa_attention(p, q_norm, k_norm, *, tq=512, out_dtype=F32):
    B, S, _ = p.shape
    cos, sin = _rope_tables(S)
    tile = lambda t, n: jnp.tile(t, (1, n))
    hq = (jnp.arange(ATT_DIM)[:, None] // HEAD_DIM == jnp.arange(ATT_DIM)[None, :] // HEAD_DIM).astype(F32) / HEAD_DIM
    hk = hq[:ATT_KV_DIM, :ATT_KV_DIM]
    qg = jnp.tile(q_norm, ATT_HEADS).reshape(1, ATT_DIM)
    kg = jnp.tile(k_norm, ATT_KV_HEADS).reshape(1, ATT_KV_DIM)
    nq = ATT_DIM // ATT_KV_DIM
    const = lambda a: pl.BlockSpec(a.shape, lambda b, i: (0, 0))
    return pl.pallas_call(
        _attn_kernel,
        grid=(B, S // tq),
        in_specs=[
            pl.BlockSpec((1, tq, ATT_DIM), lambda b, i: (b, i, 0)),
            pl.BlockSpec((1, S, ATT_KV_DIM), lambda b, i: (b, 0, nq)),
            pl.BlockSpec((1, S, ATT_KV_DIM), lambda b, i: (b, 0, nq + 1)),
            pl.BlockSpec((tq, ATT_DIM), lambda b, i: (i, 0)),
            pl.BlockSpec((tq, ATT_DIM), lambda b, i: (i, 0)),
            pl.BlockSpec((S, ATT_KV_DIM), lambda b, i: (0, 0)),
            pl.BlockSpec((S, ATT_KV_DIM), lambda b, i: (0, 0)),
            const(qg), const(kg), const(hq), const(hk),
        ],
        out_specs=pl.BlockSpec((1, tq, ATT_DIM), lambda b, i: (b, i, 0)),
        out_shape=jax.ShapeDtypeStruct((B, S, ATT_DIM), out_dtype),
        scratch_shapes=[pltpu.VMEM((ATT_KV_HEADS, S, HEAD_DIM), BF16), pltpu.VMEM((ATT_KV_HEADS, ATT_VT_ROWS, S), BF16)],
        compiler_params=_cparams(("parallel", "arbitrary")),
        name="gqa_attention",
    )(p, p, p, tile(cos, ATT_HEADS), tile(sin, ATT_HEADS), tile(cos, ATT_KV_HEADS), tile(sin, ATT_KV_HEADS),
      qg, kg, hq, hk)


GLA_UNROLL = 16


def _gla_kernel(q_ref, k_ref, v_ref, gd_ref, og_ref, upf_ref, bf_ref, upb_ref, bb_ref, on_ref, o_ref,
                yf_ref, yb_ref, st_ref, kv_ref, qe_ref, dec_ref):
    S = q_ref.shape[1]
    C = GLA_CHUNK
    n = S // C
    U = GLA_UNROLL
    nset = n // U
    ii = _iota2((C, C), 0)
    jj = _iota2((C, C), 1)
    dirs = (
        (False, (jj <= ii).astype(F32), jj <= ii, upf_ref, bf_ref, yf_ref),
        (True, (jj >= ii).astype(F32), jj > ii, upb_ref, bb_ref, yb_ref),
    )
    scale = GLA_DKH ** -0.5
    st_ref[...] = jnp.zeros_like(st_ref)

    def chunk_rows(rev, st, u):
        c = st * U + u
        c = (n - 1 - c) if rev else c
        return pl.ds(pl.multiple_of(c * C, C), C)

    def state_step(st, u):
        slot = st % 2
        for di, (rev, tri, mask, up_ref, b_ref, y_ref) in enumerate(dirs):
            state = st_ref[di]
            rows = chunk_rows(rev, st, u)
            y_ref[rows, :] += lax.dot_general(qe_ref[slot, di, u], state.astype(BF16), (((1,), (1,)), ((), ())),
                                              preferred_element_type=F32)
            st_ref[di] = state * dec_ref[slot, di, u] + kv_ref[slot, di, u]

    def chunk_set(st, state_of):
        pending = list(range(U)) if state_of is not None else []
        items = []
        for u in range(U):
            for di, (rev, tri, mask, up_ref, b_ref, y_ref) in enumerate(dirs):
                rows = chunk_rows(rev, st, u)
                z = _dot3(gd_ref[0, rows, :], up_ref[...]) + b_ref[...]
                g = (jnp.minimum(z, 0.0) - jnp.log(1.0 + jnp.exp(-jnp.abs(z)))) * (1.0 / GLA_GATE_NORM)
                items.append(dict(u=u, di=di, rows=rows, rev=rev, mask=mask, y_ref=y_ref, g=g, tri=tri))
        for x in items:
            x["b"] = _sel_dot(x["tri"], x["g"])
        if pending:
            state_step(state_of, pending.pop(0))
        for x in items:
            b = x["b"]
            q = q_ref[0, x["rows"], :] * scale
            k = k_ref[0, x["rows"], :]
            b_mid = b[C // 2:C // 2 + 1, :]
            b_last = b[0:1, :] if x["rev"] else b[C - 1:C, :]
            x["vb"] = v_ref[0, x["rows"], :].astype(BF16)
            x["qm"] = (q * jnp.exp(b - b_mid)).astype(BF16)
            x["km"] = (k * jnp.exp(b_mid - b)).astype(BF16)
            x["ke"] = (k * jnp.exp(b_last - b)).astype(BF16)
            x["qe"] = (q * jnp.exp(b)).astype(BF16)
            x["dec"] = jnp.exp(b_last)
        att = [lax.dot_general(x["qm"], x["km"], (((1,), (1,)), ((), ())), preferred_element_type=F32) for x in items]
        if pending:
            state_step(state_of, pending.pop(0))
        att = [jnp.where(x["mask"], a, 0.0).astype(BF16) for x, a in zip(items, att)]
        slot = st % 2
        for idx, (x, a) in enumerate(zip(items, att)):
            x["y_ref"][x["rows"], :] = jnp.dot(a, x["vb"], preferred_element_type=F32)
            kv_ref[slot, x["di"], x["u"]] = lax.dot_general(x["vb"], x["ke"], (((0,), (0,)), ((), ())),
                                                            preferred_element_type=F32)
            qe_ref[slot, x["di"], x["u"]] = x["qe"]
            dec_ref[slot, x["di"], x["u"]] = x["dec"]
            if pending and idx % 3 == 2:
                state_step(state_of, pending.pop(0))
        for u in pending:
            state_step(state_of, u)

    chunk_set(0, None)

    def body(st, carry):
        chunk_set(st, st - 1)
        return carry

    lax.fori_loop(1, nset, body, 0)
    for u in range(U):
        state_step(nset - 1, u)

    RT = 256

    def fin(i, carry):
        rows = pl.ds(pl.multiple_of(i * RT, RT), RT)
        o = yf_ref[rows, :] + yb_ref[rows, :]
        ms = jnp.mean(o * o, axis=-1, keepdims=True)
        on = o * lax.rsqrt(ms + NORM_EPS) * on_ref[...]
        og = og_ref[0, rows, :]
        o_ref[0, rows, :] = (on * (og * jax.nn.sigmoid(og))).astype(o_ref.dtype)
        return carry

    lax.fori_loop(0, S // RT, fin, 0)


def gla_mixer(pm, gd, gate_up_f, gate_bias_f, gate_up_b, gate_bias_b, out_norm, out_dtype=F32):
    B, S, _ = pm.shape
    U = GLA_UNROLL
    H = GLA_HEADS
    kb = GLA_DK // GLA_DKH
    vb0 = 2 * GLA_DK // GLA_DVH
    ob0 = vb0 + GLA_DV // GLA_DVH
    return pl.pallas_call(
        _gla_kernel,
        grid=(B, H),
        in_specs=[
            pl.BlockSpec((1, S, GLA_DKH), lambda b, h: (b, 0, h)),
            pl.BlockSpec((1, S, GLA_DKH), lambda b, h: (b, 0, kb + h)),
            pl.BlockSpec((1, S, GLA_DVH), lambda b, h: (b, 0, vb0 + h)),
            pl.BlockSpec((1, S, GLA_GATE_RANK), lambda b, h: (b, 0, 0)),
            pl.BlockSpec((1, S, GLA_DVH), lambda b, h: (b, 0, ob0 + h)),
            pl.BlockSpec((GLA_GATE_RANK, GLA_DKH), lambda b, h: (0, h)),
            pl.BlockSpec((1, GLA_DKH), lambda b, h: (0, h)),
            pl.BlockSpec((GLA_GATE_RANK, GLA_DKH), lambda b, h: (0, h)),
            pl.BlockSpec((1, GLA_DKH), lambda b, h: (0, h)),
            pl.BlockSpec((1, GLA_DVH), lambda b, h: (0, 0)),
        ],
        out_specs=pl.BlockSpec((1, S, GLA_DVH), lambda b, h: (b, 0, h)),
        out_shape=jax.ShapeDtypeStruct((B, S, GLA_DV), out_dtype),
        scratch_shapes=[pltpu.VMEM((S, GLA_DVH), F32), pltpu.VMEM((S, GLA_DVH), F32),
                        pltpu.VMEM((2, GLA_DVH, GLA_DKH), F32),
                        pltpu.VMEM((2, 2, U, GLA_DVH, GLA_DKH), F32), pltpu.VMEM((2, 2, U, GLA_CHUNK, GLA_DKH), BF16),
                        pltpu.VMEM((2, 2, U, 1, GLA_DKH), F32)],
        compiler_params=_cparams(("parallel", "parallel")),
        name="gla_mixer",
    )(pm, pm, pm, gd, pm, gate_up_f, gate_bias_f.reshape(1, -1), gate_up_b, gate_bias_b.reshape(1, -1),
      out_norm.reshape(1, -1))


def _router_kernel(x_ref, g_ref, wr_ref, hn_ref, idx_ref, gate_ref):
    x = x_ref[...]
    ms = jnp.mean(x * x, axis=-1, keepdims=True)
    hn = x * lax.rsqrt(ms + NORM_EPS) * g_ref[...]
    hn_ref[...] = hn
    logits = _dot3(hn, wr_ref[...])
    lane = _iota2(logits.shape, 1)
    neg = jnp.float32(-jnp.inf)
    logits = jnp.where(lane < N_EXPERTS, logits, neg)
    m1 = jnp.max(logits, axis=-1, keepdims=True)
    i1 = jnp.min(jnp.where(logits == m1, lane, LANES), axis=-1, keepdims=True)
    rest = jnp.where(lane == i1, neg, logits)
    m2 = jnp.max(rest, axis=-1, keepdims=True)
    i2 = jnp.min(jnp.where(rest == m2, lane, LANES), axis=-1, keepdims=True)
    e2 = jnp.exp(m2 - m1)
    g1 = 1.0 / (1.0 + e2)
    g2 = e2 / (1.0 + e2)
    idx_ref[...] = jnp.where(lane == 0, i1, jnp.where(lane == 1, i2, 0))
    gate_ref[...] = jnp.where(lane == 0, g1, jnp.where(lane == 1, g2, 0.0))


def moe_router(x, g, router, *, tm=512):
    T, D = x.shape
    wr = jnp.zeros((D, LANES), F32).at[:, :N_EXPERTS].set(router)
    return pl.pallas_call(
        _router_kernel,
        grid=(T // tm,),
        in_specs=[
            pl.BlockSpec((tm, D), lambda i: (i, 0)),
            pl.BlockSpec((1, D), lambda i: (0, 0)),
            pl.BlockSpec((D, LANES), lambda i: (0, 0)),
        ],
        out_specs=[
            pl.BlockSpec((tm, D), lambda i: (i, 0)),
            pl.BlockSpec((tm, LANES), lambda i: (i, 0)),
            pl.BlockSpec((tm, LANES), lambda i: (i, 0)),
        ],
        out_shape=[jax.ShapeDtypeStruct((T, D), F32), jax.ShapeDtypeStruct((T, LANES), jnp.int32),
                   jax.ShapeDtypeStruct((T, LANES), F32)],
        compiler_params=_cparams(("parallel",)),
        name="moe_router",
    )(x, g.reshape(1, D), wr)


def _expert_kernel(te_ref, tv_ref, tok_ref, hn_hbm, wg_ref, wu_ref, wd_ref, o_ref, xg_ref, xb_ref, acc_ref, sem, *, nf):
    i = pl.program_id(0)
    f = pl.program_id(1)
    tm = xb_ref.shape[0]
    valid = tv_ref[i] > 0
    prev_valid = tv_ref[jnp.maximum(i - 1, 0)] > 0
    slot = i % 2

    def issue_row(tile, dst_slot, r):
        tok = tok_ref[tile * tm + r]
        pltpu.make_async_copy(hn_hbm.at[pl.ds(tok, 1), :], xg_ref.at[dst_slot, pl.ds(r, 1), :], sem.at[dst_slot]).start()

    @pl.when(jnp.logical_and(valid, jnp.logical_and(i == 0, f == 0)))
    def _():
        def body(r, c):
            issue_row(0, 0, r)
            return c

        lax.fori_loop(0, tm, body, 0, unroll=8)

    @pl.when(jnp.logical_and(f == 0, jnp.logical_or(valid, jnp.logical_and(i > 0, prev_valid))))
    def _():
        pltpu.make_async_copy(hn_hbm.at[pl.ds(0, tm), :], xg_ref.at[slot], sem.at[slot]).wait()

    @pl.when(jnp.logical_and(valid, f == 0))
    def _():
        xb_ref[...] = xg_ref[slot].astype(BF16)

    @pl.when(valid)
    def _():
        per = tm // nf
        row0 = f * per
        xb = xb_ref[...]
        a = jnp.dot(xb, wg_ref[0], preferred_element_type=F32)
        for j in range(per // 2):
            issue_row(i + 1, 1 - slot, row0 + j)
        b = jnp.dot(xb, wu_ref[0], preferred_element_type=F32)
        for j in range(per // 2, per):
            issue_row(i + 1, 1 - slot, row0 + j)
        h = (a * jax.nn.sigmoid(a) * b).astype(BF16)
        part = jnp.dot(h, wd_ref[0], preferred_element_type=F32)

        @pl.when(f == 0)
        def _():
            acc_ref[...] = part

        @pl.when(f > 0)
        def _():
            acc_ref[...] += part

    @pl.when(f == nf - 1)
    def _():
        o_ref[...] = jnp.where(valid, acc_ref[...], 0.0)


def moe_experts(hn, tile_expert, tile_valid, row_tok, wg, wu, wd, *, tm, tf):
    T, D = hn.shape
    F = wg.shape[2]
    P = row_tok.shape[0] - tm
    nt = P // tm + 1
    nf = F // tf

    def w_in_map(i, f, te, tv, tok):
        return (te[i], 0, jnp.where(tv[i] > 0, f, nf - 1))

    def w_out_map(i, f, te, tv, tok):
        return (te[i], jnp.where(tv[i] > 0, f, nf - 1), 0)

    grid_spec = pltpu.PrefetchScalarGridSpec(
        num_scalar_prefetch=3,
        grid=(nt, nf),
        in_specs=[
            pl.BlockSpec(memory_space=pl.ANY),
            pl.BlockSpec((1, D, tf), w_in_map),
            pl.BlockSpec((1, D, tf), w_in_map),
            pl.BlockSpec((1, tf, D), w_out_map),
        ],
        out_specs=pl.BlockSpec((tm, D), lambda i, f, te, tv, tok: (i, 0)),
        scratch_shapes=[pltpu.VMEM((2, tm, D), F32), pltpu.VMEM((tm, D), BF16), pltpu.VMEM((tm, D), F32),
                        pltpu.SemaphoreType.DMA((2,))],
    )
    return pl.pallas_call(
        functools.partial(_expert_kernel, nf=nf),
        grid_spec=grid_spec,
        out_shape=jax.ShapeDtypeStruct((P + tm, D), F32),
        compiler_params=_cparams(("arbitrary", "arbitrary")),
        name="moe_experts",
    )(tile_expert, tile_valid, row_tok, hn, wg, wu, wd)


def _combine_kernel(dest_ref, x_ref, gate_ref, yb_hbm, o_ref, buf_ref, sem):
    i = pl.program_id(0)
    n = pl.num_programs(0)
    tc = x_ref.shape[0]
    slot = i % 2

    def issue(step, dst_slot):
        base = step * tc * TOP_K

        for r in range(tc):
            for k in range(TOP_K):
                pltpu.make_async_copy(yb_hbm.at[pl.ds(dest_ref[base + r * TOP_K + k], 1), :],
                                      buf_ref.at[dst_slot, k, pl.ds(r, 1), :], sem.at[dst_slot]).start()

    @pl.when(i == 0)
    def _():
        issue(0, 0)

    for k in range(TOP_K):
        pltpu.make_async_copy(yb_hbm.at[pl.ds(0, tc), :], buf_ref.at[slot, k], sem.at[slot]).wait()

    @pl.when(i + 1 < n)
    def _():
        issue(i + 1, 1 - slot)

    g = gate_ref[...]
    o_ref[...] = x_ref[...] + g[:, 0:1] * buf_ref[slot, 0] + g[:, 1:2] * buf_ref[slot, 1]


def moe_combine(x, gates, yb, dest, *, tc=256):
    T, D = x.shape
    grid_spec = pltpu.PrefetchScalarGridSpec(
        num_scalar_prefetch=1,
        grid=(T // tc,),
        in_specs=[
            pl.BlockSpec((tc, D), lambda i, d: (i, 0)),
            pl.BlockSpec((tc, LANES), lambda i, d: (i, 0)),
            pl.BlockSpec(memory_space=pl.ANY),
        ],
        out_specs=pl.BlockSpec((tc, D), lambda i, d: (i, 0)),
        scratch_shapes=[pltpu.VMEM((2, TOP_K, tc, D), F32), pltpu.SemaphoreType.DMA((2,))],
    )
    return pl.pallas_call(
        _combine_kernel,
        grid_spec=grid_spec,
        out_shape=jax.ShapeDtypeStruct((T, D), F32),
        compiler_params=_cparams(("arbitrary",)),
        name="moe_combine",
    )(dest, x, gates, yb)


def moe_dispatch_plan(idx, *, tm):
    T = idx.shape[0]
    A = T * TOP_K
    e_flat = idx[:, :TOP_K].reshape(A)
    onehot = (e_flat[:, None] == jnp.arange(N_EXPERTS, dtype=jnp.int32)[None, :]).astype(jnp.int32)
    rank = jnp.sum((jnp.cumsum(onehot, axis=0) - onehot) * onehot, axis=1)
    counts = jnp.sum(onehot, axis=0)
    padded = (counts + tm - 1) // tm * tm
    ends = jnp.cumsum(padded)
    pstart = ends - padded
    dest = pstart[e_flat] + rank
    P = (A // tm + N_EXPERTS + 1) * tm
    nt = P // tm
    tok_flat = jnp.arange(A, dtype=jnp.int32) // TOP_K
    row_tok = jnp.zeros((P,), jnp.int32).at[dest].set(tok_flat)
    tile_start = jnp.arange(nt, dtype=jnp.int32) * tm
    tile_expert = jnp.minimum(jnp.sum((tile_start[:, None] >= ends[None, :]).astype(jnp.int32), axis=1), N_EXPERTS - 1)
    tile_valid = (tile_start < ends[-1]).astype(jnp.int32)
    last_valid = jnp.maximum(jnp.sum(tile_valid) - 1, 0)
    tile_expert = jnp.where(tile_valid > 0, tile_expert, tile_expert[last_valid])
    return dest.astype(jnp.int32), row_tok, tile_expert, tile_valid


def moe_layer(x, norm2, router, wg, wu, wd, *, tm=512, tf=1792, tr=512, tc=512):
    hn, idx, gates = moe_router(x, norm2, router, tm=tr)
    dest, row_tok, tile_expert, tile_valid = moe_dispatch_plan(idx, tm=tm)
    yb = moe_experts(hn, tile_expert, tile_valid, row_tok, wg, wu, wd, tm=tm, tf=tf)
    return moe_combine(x, gates, yb, dest, tc=tc)


def _even_layer(x, norm1, w_in, shift_mu, w0_f, w_up_f, w0_b, w_up_b, a0_f, a_up_f, a0_b, a_up_b, g_up,
                k_k, k_a, r_k, gn_w, gn_b, q_norm, k_norm, w_out, norm2, ffn_gate, ffn_up, ffn_down, cast):
    Bn, S, D = x.shape
    T = Bn * S
    xf = x.reshape(T, D)
    w_in = w_in.astype(BF16)
    p_att = norm_matmul(xf, norm1, w_in[:, RW_IN:], tm=2048, tn=ATT_IN).reshape(Bn, S, ATT_IN)
    y_a = rwkv7_mixer(x, norm1, w_in[:, :RW_IN], shift_mu, w0_f, w_up_f, w0_b, w_up_b, a0_f, a_up_f, a0_b, a_up_b,
                      g_up, k_k, k_a, r_k, gn_w, gn_b, out_dtype=BF16)
    y_b = gqa_attention(p_att, q_norm, k_norm, out_dtype=BF16)
    w_out = w_out.astype(BF16)
    xf = matmul_residual([y_a.reshape(T, RW_DIM), y_b.reshape(T, ATT_DIM)], [w_out[:RW_DIM], w_out[RW_DIM:]], xf,
                         tm=1024, tn=1024)
    return ffn_swiglu(xf, norm2, ffn_gate.astype(BF16), ffn_up.astype(BF16), ffn_down.astype(BF16), tm=512, tf=1408,
                      cast=cast)


def _odd_layer(xf, Bn, S, norm1, w_in, gate_up_f, gate_bias_f, gate_up_b, gate_bias_b, out_norm, w_out,
               norm2, router, exp_gate, exp_up, exp_down):
    T, D = xf.shape
    o3 = 2 * GLA_DK + GLA_DV
    o4 = o3 + GLA_GATE_RANK
    w_main = jnp.concatenate([w_in[:, :o3], w_in[:, o4:]], axis=1).astype(BF16)
    w_gd = jnp.zeros((D, LANES), F32).at[:, :GLA_GATE_RANK].set(w_in[:, o3:o4])
    p_main = norm_matmul(xf, norm1, w_main, tm=2048, tn=1024).reshape(Bn, S, -1)
    gd = norm_matmul(xf, norm1, w_gd, tm=1024, tn=LANES)[:, :GLA_GATE_RANK].reshape(Bn, S, GLA_GATE_RANK)
    o = gla_mixer(p_main, gd, gate_up_f, gate_bias_f, gate_up_b, gate_bias_b, out_norm, out_dtype=BF16)
    xf = matmul_residual([o.reshape(T, GLA_DV)], [w_out.astype(BF16)], xf, tm=1024, tn=1024)
    return moe_layer(xf, norm2, router, exp_gate, exp_up, exp_down)


def kernel(x, e_norm1, e_w_in, e_shift_mu, e_w0_f, e_w_up_f, e_w0_b, e_w_up_b, e_a0_f, e_a_up_f, e_a0_b, e_a_up_b, e_g_up, e_k_k, e_k_a, e_r_k, e_gn_w, e_gn_b, e_q_norm, e_k_norm, e_w_out, e_norm2, e_ffn_gate, e_ffn_up, e_ffn_down, o_norm1, o_w_in, o_gate_up_f, o_gate_bias_f, o_gate_up_b, o_gate_bias_b, o_out_norm, o_w_out, o_norm2, o_router, o_exp_gate, o_exp_up, o_exp_down):
    Bn, S, D = x.shape
    xf, (exp_gate, exp_up, exp_down) = _even_layer(
        x, e_norm1[0], e_w_in[0], e_shift_mu[0], e_w0_f[0], e_w_up_f[0], e_w0_b[0],
        e_w_up_b[0], e_a0_f[0], e_a_up_f[0], e_a0_b[0], e_a_up_b[0], e_g_up[0], e_k_k[0],
        e_k_a[0], e_r_k[0], e_gn_w[0], e_gn_b[0], e_q_norm[0], e_k_norm[0], e_w_out[0],
        e_norm2[0], e_ffn_gate[0], e_ffn_up[0], e_ffn_down[0], (o_exp_gate[0], o_exp_up[0], o_exp_down[0]))
    xf = _odd_layer(xf, Bn, S, o_norm1[0], o_w_in[0], o_gate_up_f[0], o_gate_bias_f[0], o_gate_up_b[0],
                    o_gate_bias_b[0], o_out_norm[0], o_w_out[0], o_norm2[0], o_router[0],
                    exp_gate, exp_up, exp_down)
    return xf.reshape(Bn, S, D)
```

```python
import functools

import jax
import jax.numpy as jnp
from jax import lax
from jax.experimental import pallas as pl
from jax.experimental.pallas import tpu as pltpu

F32 = jnp.float32
BF16 = jnp.bfloat16
HI = lax.Precision.HIGHEST

D_MODEL = 1024
GRID_W = 64
HEAD_DIM = 64
NORM_EPS = 1e-6
RW_HEADS = 8
RW_DIM = 512
DECAY_RANK = 64
ICLR_RANK = 64
GATE_RANK = 128
RWKV_GN_EPS = 64e-5
RW_IN = 3 * RW_DIM + DECAY_RANK + ICLR_RANK + GATE_RANK
RW_CHUNK = 64
ATT_HEADS = 8
ATT_KV_HEADS = 2
ATT_DIM = 512
ATT_KV_DIM = 128
ATT_IN = ATT_DIM + 2 * ATT_KV_DIM
ROPE_THETA = 10000.0
ATT_VT_ROWS = HEAD_DIM + 16
LOG2_E = 1.4426950408889634
ATT_LOCKSTEP = 2
GLA_HEADS = 4
GLA_DK = 512
GLA_DV = 1024
GLA_DKH = 128
GLA_DVH = 256
GLA_GATE_RANK = 16
GLA_GATE_NORM = 16.0
GLA_CHUNK = 64
N_EXPERTS = 8
TOP_K = 2
LANES = 128

VMEM_LIMIT = 48 * 1024 * 1024
VMEM_LIMIT_BIG = 56 * 1024 * 1024


def _cparams(sem, vmem=VMEM_LIMIT):
    return pltpu.CompilerParams(dimension_semantics=sem, vmem_limit_bytes=vmem)


def _iota2(shape, dim):
    return lax.broadcasted_iota(jnp.int32, shape, dim)


def _split(x, parts):
    out = []
    for _ in range(parts):
        t = x.astype(BF16)
        out.append(t)
        x = x - t.astype(F32)
    return out


def _dot_sel(x, m, parts=2):
    mb = m.astype(BF16)
    acc = None
    for t in _split(x, parts):
        d = jnp.dot(t, mb, preferred_element_type=F32)
        acc = d if acc is None else acc + d
    return acc


def _sel_dot(m, x, parts=3):
    mb = m.astype(BF16)
    acc = None
    for t in _split(x, parts):
        d = jnp.dot(mb, t, preferred_element_type=F32)
        acc = d if acc is None else acc + d
    return acc


def _dot3(a, b):
    ah, al = _split(a, 2)
    bh, bl = _split(b, 2)
    d = lambda u, v: jnp.dot(u, v, preferred_element_type=F32)
    return d(ah, bh) + (d(ah, bl) + d(al, bh))


def _bdot(a, b):
    return jnp.dot(a.astype(BF16), b.astype(BF16), preferred_element_type=F32)


def _bdot_nt(a, b):
    return lax.dot_general(a.astype(BF16), b.astype(BF16), (((1,), (1,)), ((), ())), preferred_element_type=F32)


def _bdot_tn(a, b):
    return lax.dot_general(a.astype(BF16), b.astype(BF16), (((0,), (0,)), ((), ())), preferred_element_type=F32)


def _norm_mm_kernel(x_ref, g_ref, w_ref, o_ref, xn_ref):
    @pl.when(pl.program_id(1) == 0)
    def _():
        x = x_ref[...]
        ms = jnp.mean(x * x, axis=-1, keepdims=True)
        xn_ref[...] = (x * lax.rsqrt(ms + NORM_EPS) * g_ref[...]).astype(xn_ref.dtype)

    if xn_ref.dtype == F32:
        o_ref[...] = _dot3(xn_ref[...], w_ref[...]).astype(o_ref.dtype)
    else:
        o_ref[...] = jnp.dot(xn_ref[...], w_ref[...], preferred_element_type=F32).astype(o_ref.dtype)


def norm_matmul(x, g, w, *, tm, tn, out_dtype=F32):
    M, K = x.shape
    N = w.shape[1]
    return pl.pallas_call(
        _norm_mm_kernel,
        grid=(M // tm, N // tn),
        in_specs=[
            pl.BlockSpec((tm, K), lambda i, j: (i, 0)),
            pl.BlockSpec((1, K), lambda i, j: (0, 0)),
            pl.BlockSpec((K, tn), lambda i, j: (0, j)),
        ],
        out_specs=pl.BlockSpec((tm, tn), lambda i, j: (i, j)),
        out_shape=jax.ShapeDtypeStruct((M, N), out_dtype),
        scratch_shapes=[pltpu.VMEM((tm, K), w.dtype)],
        compiler_params=_cparams(("parallel", "arbitrary")),
        name="norm_matmul",
    )(x, g.reshape(1, K), w)


def _mm_res_kernel(*refs):
    n = (len(refs) - 2) // 2
    r_ref, o_ref = refs[2 * n], refs[2 * n + 1]
    acc = r_ref[...]
    for y_ref, w_ref in zip(refs[:n], refs[n:2 * n]):
        acc = acc + jnp.dot(y_ref[...].astype(BF16), w_ref[...], preferred_element_type=F32)
    o_ref[...] = acc


def matmul_residual(ys, ws, r, *, tm, tn):
    M, N = r.shape
    y_specs = [pl.BlockSpec((tm, y.shape[1]), lambda i, j: (i, 0)) for y in ys]
    w_specs = [pl.BlockSpec((w.shape[0], tn), lambda i, j: (0, j)) for w in ws]
    return pl.pallas_call(
        _mm_res_kernel,
        grid=(M // tm, N // tn),
        in_specs=y_specs + w_specs + [pl.BlockSpec((tm, tn), lambda i, j: (i, j))],
        out_specs=pl.BlockSpec((tm, tn), lambda i, j: (i, j)),
        out_shape=jax.ShapeDtypeStruct((M, N), F32),
        compiler_params=_cparams(("parallel", "arbitrary")),
        name="matmul_residual",
    )(*ys, *ws, r)


def _ffn_kernel(*refs, n_cast):
    x_ref, g_ref, wg_ref, wu_ref, wd_ref = refs[:5]
    cast_in = refs[5:5 + n_cast]
    o_ref = refs[5 + n_cast]
    cast_out = refs[6 + n_cast:6 + 2 * n_cast]
    xn_ref, acc_ref = refs[6 + 2 * n_cast:]
    f = pl.program_id(1)

    @pl.when(f == 0)
    def _():
        x = x_ref[...]
        ms = jnp.mean(x * x, axis=-1, keepdims=True)
        xn_ref[...] = (x * lax.rsqrt(ms + NORM_EPS) * g_ref[...]).astype(BF16)

    for src, dst in zip(cast_in, cast_out):
        dst[...] = src[...].astype(BF16)

    xn = xn_ref[...]
    a = jnp.dot(xn, wg_ref[...], preferred_element_type=F32)
    b = jnp.dot(xn, wu_ref[...], preferred_element_type=F32)
    h = (a * jax.nn.sigmoid(a) * b).astype(BF16)
    part = jnp.dot(h, wd_ref[...], preferred_element_type=F32)

    @pl.when(f == 0)
    def _():
        acc_ref[...] = part

    @pl.when(f > 0)
    def _():
        acc_ref[...] += part

    @pl.when(f == pl.num_programs(1) - 1)
    def _():
        o_ref[...] = x_ref[...] + acc_ref[...]


def ffn_swiglu(x, g, wg, wu, wd, *, tm, tf, cast=()):
    M, D = x.shape
    F = wg.shape[1]
    nf = F // tf
    steps = (M // tm) * nf
    cast_specs, cast_shapes = [], []
    for w in cast:
        E, R, C = w.shape
        per = steps // E
        cast_specs.append(pl.BlockSpec((1, R // per, C), lambda i, f, per=per: ((i * nf + f) // per, (i * nf + f) % per, 0)))
        cast_shapes.append(jax.ShapeDtypeStruct(w.shape, BF16))
    outs = pl.pallas_call(
        functools.partial(_ffn_kernel, n_cast=len(cast)),
        grid=(M // tm, nf),
        in_specs=[
            pl.BlockSpec((tm, D), lambda i, f: (i, 0)),
            pl.BlockSpec((1, D), lambda i, f: (0, 0)),
            pl.BlockSpec((D, tf), lambda i, f: (0, f)),
            pl.BlockSpec((D, tf), lambda i, f: (0, f)),
            pl.BlockSpec((tf, D), lambda i, f: (f, 0)),
        ] + cast_specs,
        out_specs=[pl.BlockSpec((tm, D), lambda i, f: (i, 0))] + cast_specs,
        out_shape=[jax.ShapeDtypeStruct((M, D), F32)] + cast_shapes,
        scratch_shapes=[pltpu.VMEM((tm, D), BF16), pltpu.VMEM((tm, D), F32)],
        compiler_params=_cparams(("arbitrary", "arbitrary"), vmem=VMEM_LIMIT_BIG if cast else VMEM_LIMIT),
        name="ffn_swiglu",
    )(x, g.reshape(1, D), wg, wu, wd, *cast)
    return outs[0], tuple(outs[1:])


def _rwkv_prep_kernel(cur_ref, prev_ref, next_ref, n1_ref, win_ref, mu_ref, w0f_ref, wupf_ref, w0b_ref, wupb_ref,
                      a0f_ref, aupf_ref, a0b_ref, aupb_ref, gup_ref, kk_ref, hsum_ref,
                      r_ref, k_ref, v_ref, kkn_ref, g_ref, lwf_ref, lf_ref, asf_ref, lwb_ref, lb_ref, asb_ref):
    i = pl.program_id(1)
    nt = pl.num_programs(1)

    def project(xv):
        ms = jnp.mean(xv * xv, axis=-1, keepdims=True)
        xn = (xv * lax.rsqrt(ms + NORM_EPS) * n1_ref[...]).astype(BF16)
        return jnp.dot(xn, win_ref[...], preferred_element_type=F32)

    Tt = cur_ref.shape[1]
    pall = project(jnp.concatenate([cur_ref[0], prev_ref[0], next_ref[0]], axis=0))
    x = pall[:Tt]
    halo = pall[Tt:]
    row = _iota2(x.shape, 0)
    prev_row = jnp.where(i > 0, halo[7:8, :], 0.0)
    next_row = jnp.where(i < nt - 1, halo[8:9, :], 0.0)
    prev = jnp.where(row == 0, prev_row, pltpu.roll(x, 1, 0))
    nxt = jnp.where(row == Tt - 1, next_row, pltpu.roll(x, Tt - 1, 0))
    pm = x + (0.5 * (prev + nxt) - x) * mu_ref[...]
    o1 = 3 * RW_DIM
    o2 = o1 + DECAY_RANK
    o3 = o2 + ICLR_RANK
    r_ref[0] = pm[:, :RW_DIM]
    k = pm[:, RW_DIM:2 * RW_DIM]
    k_ref[0] = k
    v_ref[0] = pm[:, 2 * RW_DIM:o1]
    wd = jnp.tanh(pm[:, o1:o2])
    ad = pm[:, o2:o3]
    gin = jax.nn.sigmoid(pm[:, o3:])
    g_ref[0] = jnp.dot(gin.astype(BF16), gup_ref[...].astype(BF16), preferred_element_type=F32)
    kk = k * kk_ref[...]
    ss = _dot_sel(kk * kk, hsum_ref[...])
    kkn_ref[0] = kk / jnp.maximum(jnp.sqrt(ss), 1e-12)

    ci = _iota2((RW_CHUNK, RW_CHUNK), 0)
    cj = _iota2((RW_CHUNK, RW_CHUNK), 1)
    tril = (cj <= ci).astype(F32)
    triu = (cj >= ci).astype(F32)

    def direction(w0_ref, wup_ref, a0_ref, aup_ref, tri, lw_ref, l_ref, as_ref):
        z = w0_ref[...] + _dot3(wd, wup_ref[...])
        sp = jnp.maximum(-z, 0.0) + jnp.log(1.0 + jnp.exp(-jnp.abs(z)))
        lw = -jnp.exp(-sp - 0.5)
        lw_ref[0] = lw
        as_ref[0] = jax.nn.sigmoid(a0_ref[...] + _dot3(ad, aup_ref[...]))
        for c in range(Tt // RW_CHUNK):
            sl = slice(c * RW_CHUNK, (c + 1) * RW_CHUNK)
            l_ref[0, sl, :] = _sel_dot(tri, lw[sl, :])

    direction(w0f_ref, wupf_ref, a0f_ref, aupf_ref, tril, lwf_ref, lf_ref, asf_ref)
    direction(w0b_ref, wupb_ref, a0b_ref, aupb_ref, triu, lwb_ref, lb_ref, asb_ref)


def rwkv_prep(x, norm1, w_in, shift_mu, w0_f, w_up_f, w0_b, w_up_b, a0_f, a_up_f, a0_b, a_up_b, g_up, k_k, *, tt=256):
    B, S, D = x.shape
    nt = S // tt
    hsum = (jnp.arange(RW_DIM)[:, None] // HEAD_DIM == jnp.arange(RW_DIM)[None, :] // HEAD_DIM).astype(F32)
    row = lambda a: a.reshape(1, -1)
    full = lambda a: pl.BlockSpec(a.shape, lambda b, i: (0,) * a.ndim)
    params = [row(norm1), w_in, row(shift_mu), row(w0_f), w_up_f, row(w0_b), w_up_b, row(a0_f), a_up_f, row(a0_b), a_up_b,
              g_up, row(k_k), hsum]
    tb = tt // 8
    in_specs = [
        pl.BlockSpec((1, tt, D), lambda b, i: (b, i, 0)),
        pl.BlockSpec((1, 8, D), lambda b, i: (b, jnp.maximum(i * tb - 1, 0), 0)),
        pl.BlockSpec((1, 8, D), lambda b, i: (b, jnp.minimum((i + 1) * tb, S // 8 - 1), 0)),
    ] + [full(a) for a in params]
    out_spec = pl.BlockSpec((1, tt, RW_DIM), lambda b, i: (b, i, 0))
    out_sds = jax.ShapeDtypeStruct((B, S, RW_DIM), F32)
    return pl.pallas_call(
        _rwkv_prep_kernel,
        grid=(B, nt),
        in_specs=in_specs,
        out_specs=[out_spec] * 11,
        out_shape=[out_sds] * 11,
        compiler_params=_cparams(("parallel", "arbitrary")),
        name="rwkv_prep",
    )(x, x, x, *params)


def _bd(y, bd_mask):
    return jnp.where(bd_mask, jnp.concatenate([y, y], axis=0), jnp.zeros((), y.dtype))


def _chunk_terms(chains, out, bd_mask):
    C = chains[0]["r"].shape[0]
    n = len(chains)
    R = range(n)
    dot = lambda u, v: jnp.dot(u, v, preferred_element_type=F32)
    bd = lambda y: _bd(y, bd_mask)
    pre = []
    for ch in chains:
        r, k, v, kk, lw, L, asig, ka, rev = (ch[x] for x in ("r", "k", "v", "kk", "lw", "L", "asig", "ka", "rev"))
        a = -kk
        b = kk * asig
        kd = k * (1.0 + (asig - 1.0) * ka)
        Lp = L - lw
        Lr = Lp if rev else L
        Lend = L[0:1, :] if rev else L[C - 1:C, :]
        Lmid = L[C // 2:C // 2 + 1, :]
        einv = jnp.exp(Lmid - L)
        eend = jnp.exp(Lend - L)
        pre.append(dict(
            ar=jnp.concatenate([a * jnp.exp(Lp - Lmid), r * jnp.exp(Lr - Lmid)], axis=0).astype(BF16),
            bt=bd((b * einv).astype(BF16)), kt=bd((kd * einv).astype(BF16)),
            a0=bd((a * jnp.exp(Lp)).astype(BF16)), r0=r * jnp.exp(Lr),
            bh=(b * eend).astype(BF16), kh=(kd * eend).astype(BF16), vb=v.astype(BF16),
            dec=jnp.exp(Lend)))
    mk = [ch["masks"] for ch in chains]
    Ab = [_bdot_nt(pre[i]["ar"], pre[i]["bt"]) for i in R]
    Ak = [_bdot_nt(pre[i]["ar"], pre[i]["kt"]) for i in R]
    yield
    Aab = [jnp.where(mk[i][0], Ab[i][:C], 0.0) for i in R]
    Arb = [jnp.where(mk[i][1], Ab[i][C:], 0.0).astype(BF16) for i in R]
    AakArk = [jnp.concatenate([jnp.where(mk[i][0], Ak[i][:C], 0.0), jnp.where(mk[i][1], Ak[i][C:], 0.0)],
                              axis=0).astype(BF16) for i in R]
    AV = [dot(AakArk[i], bd(pre[i]["vb"])) for i in R]
    Xb = [jnp.where(mk[i][2], Aab[i], 0.0).astype(BF16) for i in R]
    X2b = [dot(Xb[i], bd(Xb[i])).astype(BF16) for i in R]
    yield
    T = [mk[i][6] + Xb[i].astype(F32) for i in R]
    T = [T[i] + dot(T[i].astype(BF16), bd(X2b[i])) for i in R]
    X4b = [dot(X2b[i], bd(X2b[i])).astype(BF16) for i in R]
    yield
    T = [T[i] + dot(T[i].astype(BF16), bd(X4b[i])) for i in R]
    yield
    for lvl in (3, 4, 5):
        Tb = [T[i].astype(BF16) for i in R]
        ET = [dot(jnp.where(mk[i][lvl], Aab[i], 0.0).astype(BF16), bd(Tb[i])).astype(BF16) for i in R]
        yield
        T = [T[i] + dot(Tb[i], bd(ET[i])) for i in R]
        yield
    Tb = [T[i].astype(BF16) for i in R]
    A0p = [dot(Tb[i], pre[i]["a0"]).astype(BF16) for i in R]
    Uv = [dot(Tb[i], bd(AV[i][:C].astype(BF16))).astype(BF16) for i in R]
    yield
    Rpp = [pre[i]["r0"] + dot(Arb[i], bd(A0p[i])) for i in R]
    Yv = [dot(Arb[i], bd(Uv[i])) + AV[i][C:] for i in R]
    eye2 = (_iota2(bd_mask.shape, 0) == _iota2(bd_mask.shape, 1)).astype(F32)
    P = [jnp.where(bd_mask, _bdot_tn(pre[i]["bh"], A0p[i]), 0.0) + eye2 * pre[i]["dec"] for i in R]
    Q = [jnp.where(bd_mask, _bdot_tn(pre[i]["bh"], Uv[i]) + _bdot_tn(pre[i]["kh"], pre[i]["vb"]), 0.0) for i in R]
    out.extend((Rpp[i], Yv[i], P[i], Q[i]) for i in R)


def _make_masks(C, N, rev):
    ii = _iota2((C, 2 * N), 0)
    jj = _iota2((C, 2 * N), 1) % N
    strict = (jj > ii) if rev else (jj < ii)
    rmask = strict if rev else (jj <= ii)
    blk = lambda s: (ii // s) == (jj // s)
    m8 = blk(8)
    e16 = blk(16) & jnp.logical_not(blk(8))
    e32 = blk(32) & jnp.logical_not(blk(16))
    e64 = jnp.logical_not(blk(32))
    eye = (ii == jj).astype(F32)
    return strict, rmask, m8, e16, e32, e64, eye


RW_UNROLL = 8


def _rwkv_scan_kernel(r_ref, k_ref, v_ref, kk_ref, g_ref, lwf_ref, lf_ref, asf_ref, lwb_ref, lb_ref, asb_ref,
                      ka_ref, rk_ref, gnw_ref, gnb_ref, havg_ref, o_ref,
                      yf_ref, yb_ref, h_ref, rpp_ref, yv_ref, p_ref, q_ref):
    S = r_ref.shape[1]
    C = RW_CHUNK
    N = HEAD_DIM
    W = 2 * N
    n = S // C
    U = RW_UNROLL
    nset = n // U
    masks_f = _make_masks(C, N, False)
    masks_b = _make_masks(C, N, True)
    bd_mask = (_iota2((W, W), 0) // N) == (_iota2((W, W), 1) // N)
    h_ref[...] = jnp.zeros_like(h_ref)
    dirs = ((False, lwf_ref, lf_ref, asf_ref, yf_ref, masks_f), (True, lwb_ref, lb_ref, asb_ref, yb_ref, masks_b))

    def chunk_rows(rev, st, u):
        c = st * U + u
        c = (n - 1 - c) if rev else c
        return pl.ds(pl.multiple_of(c * C, C), C)

    def state_step(st, u):
        slot = st % 2
        for di, (rev, lw_ref, l_ref, as_ref, y_ref, masks) in enumerate(dirs):
            Hb = h_ref[di].astype(BF16)
            y_ref[chunk_rows(rev, st, u), :] = (jnp.dot(rpp_ref[slot, di, u], Hb, preferred_element_type=F32)
                                                + yv_ref[slot, di, u])
            h_ref[di] = jnp.dot(p_ref[slot, di, u], Hb, preferred_element_type=F32) + q_ref[slot, di, u]

    def chunk_set(st, state_of):
        chains, slots = [], []
        for u in range(U):
            for di, (rev, lw_ref, l_ref, as_ref, y_ref, masks) in enumerate(dirs):
                rows = chunk_rows(rev, st, u)
                chains.append(dict(
                    r=r_ref[0, rows, :], k=k_ref[0, rows, :], v=v_ref[0, rows, :], kk=kk_ref[0, rows, :],
                    lw=lw_ref[0, rows, :], L=l_ref[0, rows, :], asig=as_ref[0, rows, :], ka=ka_ref[...],
                    rev=rev, masks=masks))
                slots.append((di, u))
        res = []
        pending = list(range(U)) if state_of is not None else []
        for stage, _ in enumerate(_chunk_terms(chains, res, bd_mask)):
            if pending and stage % 2 == 1:
                state_step(state_of, pending.pop(0))
        for u in pending:
            state_step(state_of, u)
        slot = st % 2
        for (di, u), (Rpp, Yv, P, Q) in zip(slots, res):
            rpp_ref[slot, di, u] = Rpp.astype(BF16)
            yv_ref[slot, di, u] = Yv
            p_ref[slot, di, u] = P.astype(BF16)
            q_ref[slot, di, u] = Q

    chunk_set(0, None)

    def body(st, carry):
        chunk_set(st, st - 1)
        return carry

    lax.fori_loop(1, nset, body, 0)
    for u in range(U):
        state_step(nset - 1, u)

    RT = 256

    def fin(i, carry):
        rows = pl.ds(pl.multiple_of(i * RT, RT), RT)
        y = yf_ref[rows, :] + yb_ref[rows, :]
        mean = _dot_sel(y, havg_ref[...])
        d = y - mean
        var = _dot_sel(d * d, havg_ref[...])
        yn = d * lax.rsqrt(var + RWKV_GN_EPS) * gnw_ref[...] + gnb_ref[...]
        r = r_ref[0, rows, :]
        kf = k_ref[0, rows, :] * (1.0 + (asf_ref[0, rows, :] - 1.0) * ka_ref[...])
        bonus = _dot_sel(r * kf * rk_ref[...], havg_ref[...]) * float(N) * v_ref[0, rows, :]
        o_ref[0, rows, :] = ((yn + bonus) * g_ref[0, rows, :]).astype(o_ref.dtype)
        return carry

    lax.fori_loop(0, S // RT, fin, 0)


def rwkv_scan(r, k, v, kk, g, lw_f, l_f, as_f, lw_b, l_b, as_b, k_a, r_k, gn_w, gn_b, out_dtype=F32):
    B, S, _ = r.shape
    W = 2 * HEAD_DIM
    N = HEAD_DIM
    havg = (jnp.arange(W)[:, None] // HEAD_DIM == jnp.arange(W)[None, :] // HEAD_DIM).astype(F32) / HEAD_DIM
    seq = pl.BlockSpec((1, S, W), lambda b, h: (b, 0, h))
    par = pl.BlockSpec((1, W), lambda b, h: (0, h))
    row = lambda a: a.reshape(1, -1)
    U = RW_UNROLL
    return pl.pallas_call(
        _rwkv_scan_kernel,
        grid=(B, RW_DIM // W),
        in_specs=[seq] * 11 + [par] * 4 + [pl.BlockSpec((W, W), lambda b, h: (0, 0))],
        out_specs=seq,
        out_shape=jax.ShapeDtypeStruct((B, S, RW_DIM), out_dtype),
        scratch_shapes=[pltpu.VMEM((S, W), F32), pltpu.VMEM((S, W), F32), pltpu.VMEM((2, W, W), F32),
                        pltpu.VMEM((2, 2, U, RW_CHUNK, W), BF16), pltpu.VMEM((2, 2, U, RW_CHUNK, W), F32),
                        pltpu.VMEM((2, 2, U, W, W), BF16), pltpu.VMEM((2, 2, U, W, W), F32)],
        compiler_params=_cparams(("parallel", "parallel")),
        name="rwkv_scan",
    )(r, k, v, kk, g, lw_f, l_f, as_f, lw_b, l_b, as_b, row(k_a), row(r_k), row(gn_w), row(gn_b), havg)


def rwkv7_mixer(x, norm1, w_in, shift_mu, w0_f, w_up_f, w0_b, w_up_b, a0_f, a_up_f, a0_b, a_up_b, g_up, k_k, k_a,
                r_k, gn_w, gn_b, out_dtype=F32, tt=512):
    outs = rwkv_prep(x, norm1, w_in, shift_mu, w0_f, w_up_f, w0_b, w_up_b, a0_f, a_up_f, a0_b, a_up_b, g_up, k_k, tt=tt)
    return rwkv_scan(*outs, k_a, r_k, gn_w, gn_b, out_dtype=out_dtype)


def _head_norm_rope(x, gain, cos, sin_signed, havg):
    ms = _dot_sel(x * x, havg)
    xn = x * lax.rsqrt(ms + NORM_EPS) * gain
    W = x.shape[1]
    even = (_iota2(x.shape, 1) % 2) == 0
    partner = jnp.where(even, pltpu.roll(xn, W - 1, 1), pltpu.roll(xn, 1, 1))
    return xn * cos + partner * sin_signed


def _attn_kernel(q_ref, k_ref, v_ref, cosq_ref, sinq_ref, cosk_ref, sink_ref, qg_ref, kg_ref, hq_ref, hk_ref,
                 o_ref, ks_ref, vt_ref):
    i = pl.program_id(1)
    G = ATT_HEADS // ATT_KV_HEADS
    D = HEAD_DIM

    @pl.when(i == 0)
    def _():
        kr = _head_norm_rope(k_ref[0], kg_ref[...], cosk_ref[...], sink_ref[...], hk_ref[...])
        vt = jnp.transpose(v_ref[0])
        ones = jnp.ones((ATT_VT_ROWS - D, vt.shape[1]), BF16)
        for kv in range(ATT_KV_HEADS):
            ks_ref[kv] = kr[:, kv * D:(kv + 1) * D].astype(BF16)
            vt_ref[kv] = jnp.concatenate([vt[kv * D:(kv + 1) * D, :].astype(BF16), ones], axis=0)

    q = _head_norm_rope(q_ref[0], qg_ref[...], cosq_ref[...], sinq_ref[...], hq_ref[...]) * (D ** -0.5 * LOG2_E)
    qb = q.astype(BF16)
    outs = []
    for h0 in range(0, ATT_HEADS, ATT_LOCKSTEP):
        hs = range(h0, h0 + ATT_LOCKSTEP)
        st = [lax.dot_general(ks_ref[h // G], qb[:, h * D:(h + 1) * D], (((1,), (1,)), ((), ())),
                              preferred_element_type=F32) for h in hs]
        m = [jnp.max(x, axis=0, keepdims=True) for x in st]
        p = [jnp.exp2(x - mm).astype(BF16) for x, mm in zip(st, m)]
        ot = [jnp.dot(vt_ref[h // G], x, preferred_element_type=F32) for h, x in zip(hs, p)]
        outs.extend(o[:D] / o[D:D + 1] for o in ot)
    pairs = [jnp.transpose(jnp.concatenate(outs[2 * j:2 * j + 2], axis=0)) for j in range(ATT_HEADS // 2)]
    o_ref[0] = jnp.concatenate(pairs, axis=1).astype(o_ref.dtype)


def _rope_tables(S):
    rows = S // GRID_W
    row = jnp.repeat(jnp.arange(rows), GRID_W).astype(F32)
    col = jnp.tile(jnp.arange(GRID_W), rows).astype(F32)
    half = HEAD_DIM // 2
    freq = ROPE_THETA ** (-jnp.arange(0, half, 2, dtype=F32) / half)
    ang = jnp.concatenate([row[:, None] * freq, col[:, None] * freq], axis=-1)
    cos = jnp.repeat(jnp.cos(ang), 2, axis=-1)
    sin = jnp.repeat(jnp.sin(ang), 2, axis=-1)
    sign = jnp.where(jnp.arange(HEAD_DIM) % 2 == 0, -1.0, 1.0).astype(F32)
    return cos, sin * sign


def gqa_attention(p, q_norm, k_norm, *, tq=1024, out_dtype=F32):
    B, S, _ = p.shape
    cos, sin = _rope_tables(S)
    tile = lambda t, n: jnp.tile(t, (1, n))
    hq = (jnp.arange(ATT_DIM)[:, None] // HEAD_DIM == jnp.arange(ATT_DIM)[None, :] // HEAD_DIM).astype(F32) / HEAD_DIM
    hk = hq[:ATT_KV_DIM, :ATT_KV_DIM]
    qg = jnp.tile(q_norm, ATT_HEADS).reshape(1, ATT_DIM)
    kg = jnp.tile(k_norm, ATT_KV_HEADS).reshape(1, ATT_KV_DIM)
    nq = ATT_DIM // ATT_KV_DIM
    const = lambda a: pl.BlockSpec(a.shape, lambda b, i: (0, 0))
    return pl.pallas_call(
        _attn_kernel,
        grid=(B, S // tq),
        in_specs=[
            pl.BlockSpec((1, tq, ATT_DIM), lambda b, i: (b, i, 0)),
            pl.BlockSpec((1, S, ATT_KV_DIM), lambda b, i: (b, 0, nq)),
            pl.BlockSpec((1, S, ATT_KV_DIM), lambda b, i: (b, 0, nq + 1)),
            pl.BlockSpec((tq, ATT_DIM), lambda b, i: (i, 0)),
            pl.BlockSpec((tq, ATT_DIM), lambda b, i: (i, 0)),
            pl.BlockSpec((S, ATT_KV_DIM), lambda b, i: (0, 0)),
            pl.BlockSpec((S, ATT_KV_DIM), lambda b, i: (0, 0)),
            const(qg), const(kg), const(hq), const(hk),
        ],
        out_specs=pl.BlockSpec((1, tq, ATT_DIM), lambda b, i: (b, i, 0)),
        out_shape=jax.ShapeDtypeStruct((B, S, ATT_DIM), out_dtype),
        scratch_shapes=[pltpu.VMEM((ATT_KV_HEADS, S, HEAD_DIM), BF16), pltpu.VMEM((ATT_KV_HEADS, ATT_VT_ROWS, S), BF16)],
        compiler_params=_cparams(("parallel", "arbitrary")),
        name="gqa_attention",
    )(p, p, p, tile(cos, ATT_HEADS), tile(sin, ATT_HEADS), tile(cos, ATT_KV_HEADS), tile(sin, ATT_KV_HEADS),
      qg, kg, hq, hk)


GLA_UNROLL = 16


def _gla_kernel(q_ref, k_ref, v_ref, gd_ref, og_ref, upf_ref, bf_ref, upb_ref, bb_ref, on_ref, o_ref,
                yf_ref, yb_ref, st_ref, kv_ref, qe_ref, dec_ref):
    S = q_ref.shape[1]
    C = GLA_CHUNK
    n = S // C
    U = GLA_UNROLL
    nset = n // U
    ii = _iota2((C, C), 0)
    jj = _iota2((C, C), 1)
    dirs = (
        (False, (jj <= ii).astype(F32), jj <= ii, upf_ref, bf_ref, yf_ref),
        (True, (jj >= ii).astype(F32), jj > ii, upb_ref, bb_ref, yb_ref),
    )
    scale = GLA_DKH ** -0.5
    st_ref[...] = jnp.zeros_like(st_ref)

    def chunk_rows(rev, st, u):
        c = st * U + u
        c = (n - 1 - c) if rev else c
        return pl.ds(pl.multiple_of(c * C, C), C)

    def state_step(st, u):
        slot = st % 2
        for di, (rev, tri, mask, up_ref, b_ref, y_ref) in enumerate(dirs):
            state = st_ref[di]
            rows = chunk_rows(rev, st, u)
            y_ref[rows, :] += lax.dot_general(qe_ref[slot, di, u], state.astype(BF16), (((1,), (1,)), ((), ())),
                                              preferred_element_type=F32)
            st_ref[di] = state * dec_ref[slot, di, u] + kv_ref[slot, di, u]

    def chunk_set(st, state_of):
        pending = list(range(U)) if state_of is not None else []
        items = []
        for u in range(U):
            for di, (rev, tri, mask, up_ref, b_ref, y_ref) in enumerate(dirs):
                rows = chunk_rows(rev, st, u)
                z = _dot3(gd_ref[0, rows, :], up_ref[...]) + b_ref[...]
                g = (jnp.minimum(z, 0.0) - jnp.log(1.0 + jnp.exp(-jnp.abs(z)))) * (1.0 / GLA_GATE_NORM)
                items.append(dict(u=u, di=di, rows=rows, rev=rev, mask=mask, y_ref=y_ref, g=g, tri=tri))
        for x in items:
            x["b"] = _sel_dot(x["tri"], x["g"])
        if pending:
            state_step(state_of, pending.pop(0))
        for x in items:
            b = x["b"]
            q = q_ref[0, x["rows"], :] * scale
            k = k_ref[0, x["rows"], :]
            b_mid = b[C // 2:C // 2 + 1, :]
            b_last = b[0:1, :] if x["rev"] else b[C - 1:C, :]
            x["vb"] = v_ref[0, x["rows"], :].astype(BF16)
            x["qm"] = (q * jnp.exp(b - b_mid)).astype(BF16)
            x["km"] = (k * jnp.exp(b_mid - b)).astype(BF16)
            x["ke"] = (k * jnp.exp(b_last - b)).astype(BF16)
            x["qe"] = (q * jnp.exp(b)).astype(BF16)
            x["dec"] = jnp.exp(b_last)
        att = [lax.dot_general(x["qm"], x["km"], (((1,), (1,)), ((), ())), preferred_element_type=F32) for x in items]
        if pending:
            state_step(state_of, pending.pop(0))
        att = [jnp.where(x["mask"], a, 0.0).astype(BF16) for x, a in zip(items, att)]
        slot = st % 2
        for idx, (x, a) in enumerate(zip(items, att)):
            x["y_ref"][x["rows"], :] = jnp.dot(a, x["vb"], preferred_element_type=F32)
            kv_ref[slot, x["di"], x["u"]] = lax.dot_general(x["vb"], x["ke"], (((0,), (0,)), ((), ())),
                                                            preferred_element_type=F32)
            qe_ref[slot, x["di"], x["u"]] = x["qe"]
            dec_ref[slot, x["di"], x["u"]] = x["dec"]
            if pending and idx % 3 == 2:
                state_step(state_of, pending.pop(0))
        for u in pending:
            state_step(state_of, u)

    chunk_set(0, None)

    def body(st, carry):
        chunk_set(st, st - 1)
        return carry

    lax.fori_loop(1, nset, body, 0)
    for u in range(U):
        state_step(nset - 1, u)

    RT = 256

    def fin(i, carry):
        rows = pl.ds(pl.multiple_of(i * RT, RT), RT)
        o = yf_ref[rows, :] + yb_ref[rows, :]
        ms = jnp.mean(o * o, axis=-1, keepdims=True)
        on = o * lax.rsqrt(ms + NORM_EPS) * on_ref[...]
        og = og_ref[0, rows, :]
        o_ref[0, rows, :] = (on * (og * jax.nn.sigmoid(og))).astype(o_ref.dtype)
        return carry

    lax.fori_loop(0, S // RT, fin, 0)


def gla_mixer(pm, gd, gate_up_f, gate_bias_f, gate_up_b, gate_bias_b, out_norm, out_dtype=F32):
    B, S, _ = pm.shape
    U = GLA_UNROLL
    H = GLA_HEADS
    kb = GLA_DK // GLA_DKH
    vb0 = 2 * GLA_DK // GLA_DVH
    ob0 = vb0 + GLA_DV // GLA_DVH
    return pl.pallas_call(
        _gla_kernel,
        grid=(B, H),
        in_specs=[
            pl.BlockSpec((1, S, GLA_DKH), lambda b, h: (b, 0, h)),
            pl.BlockSpec((1, S, GLA_DKH), lambda b, h: (b, 0, kb + h)),
            pl.BlockSpec((1, S, GLA_DVH), lambda b, h: (b, 0, vb0 + h)),
            pl.BlockSpec((1, S, GLA_GATE_RANK), lambda b, h: (b, 0, 0)),
            pl.BlockSpec((1, S, GLA_DVH), lambda b, h: (b, 0, ob0 + h)),
            pl.BlockSpec((GLA_GATE_RANK, GLA_DKH), lambda b, h: (0, h)),
            pl.BlockSpec((1, GLA_DKH), lambda b, h: (0, h)),
            pl.BlockSpec((GLA_GATE_RANK, GLA_DKH), lambda b, h: (0, h)),
            pl.BlockSpec((1, GLA_DKH), lambda b, h: (0, h)),
            pl.BlockSpec((1, GLA_DVH), lambda b, h: (0, 0)),
        ],
        out_specs=pl.BlockSpec((1, S, GLA_DVH), lambda b, h: (b, 0, h)),
        out_shape=jax.ShapeDtypeStruct((B, S, GLA_DV), out_dtype),
        scratch_shapes=[pltpu.VMEM((S, GLA_DVH), F32), pltpu.VMEM((S, GLA_DVH), F32),
                        pltpu.VMEM((2, GLA_DVH, GLA_DKH), F32),
                        pltpu.VMEM((2, 2, U, GLA_DVH, GLA_DKH), F32), pltpu.VMEM((2, 2, U, GLA_CHUNK, GLA_DKH), BF16),
                        pltpu.VMEM((2, 2, U, 1, GLA_DKH), F32)],
        compiler_params=_cparams(("parallel", "parallel")),
        name="gla_mixer",
    )(pm, pm, pm, gd, pm, gate_up_f, gate_bias_f.reshape(1, -1), gate_up_b, gate_bias_b.reshape(1, -1),
      out_norm.reshape(1, -1))


def _router_kernel(x_ref, g_ref, wr_ref, hn_ref, idx_ref, gate_ref):
    x = x_ref[...]
    ms = jnp.mean(x * x, axis=-1, keepdims=True)
    hn = x * lax.rsqrt(ms + NORM_EPS) * g_ref[...]
    hn_ref[...] = hn
    logits = _dot3(hn, wr_ref[...])
    lane = _iota2(logits.shape, 1)
    neg = jnp.float32(-jnp.inf)
    logits = jnp.where(lane < N_EXPERTS, logits, neg)
    m1 = jnp.max(logits, axis=-1, keepdims=True)
    i1 = jnp.min(jnp.where(logits == m1, lane, LANES), axis=-1, keepdims=True)
    rest = jnp.where(lane == i1, neg, logits)
    m2 = jnp.max(rest, axis=-1, keepdims=True)
    i2 = jnp.min(jnp.where(rest == m2, lane, LANES), axis=-1, keepdims=True)
    e2 = jnp.exp(m2 - m1)
    g1 = 1.0 / (1.0 + e2)
    g2 = e2 / (1.0 + e2)
    idx_ref[...] = jnp.where(lane == 0, i1, jnp.where(lane == 1, i2, 0))
    gate_ref[...] = jnp.where(lane == 0, g1, jnp.where(lane == 1, g2, 0.0))


def moe_router(x, g, router, *, tm=512):
    T, D = x.shape
    wr = jnp.zeros((D, LANES), F32).at[:, :N_EXPERTS].set(router)
    return pl.pallas_call(
        _router_kernel,
        grid=(T // tm,),
        in_specs=[
            pl.BlockSpec((tm, D), lambda i: (i, 0)),
            pl.BlockSpec((1, D), lambda i: (0, 0)),
            pl.BlockSpec((D, LANES), lambda i: (0, 0)),
        ],
        out_specs=[
            pl.BlockSpec((tm, D), lambda i: (i, 0)),
            pl.BlockSpec((tm, LANES), lambda i: (i, 0)),
            pl.BlockSpec((tm, LANES), lambda i: (i, 0)),
        ],
        out_shape=[jax.ShapeDtypeStruct((T, D), F32), jax.ShapeDtypeStruct((T, LANES), jnp.int32),
                   jax.ShapeDtypeStruct((T, LANES), F32)],
        compiler_params=_cparams(("parallel",)),
        name="moe_router",
    )(x, g.reshape(1, D), wr)


def _expert_kernel(te_ref, tv_ref, tok_ref, hn_hbm, wg_ref, wu_ref, wd_ref, o_ref, xg_ref, xb_ref, acc_ref, sem, *, nf):
    i = pl.program_id(0)
    f = pl.program_id(1)
    tm = xb_ref.shape[0]
    valid = tv_ref[i] > 0
    prev_valid = tv_ref[jnp.maximum(i - 1, 0)] > 0
    slot = i % 2

    def issue_row(tile, dst_slot, r):
        tok = tok_ref[tile * tm + r]
        pltpu.make_async_copy(hn_hbm.at[pl.ds(tok, 1), :], xg_ref.at[dst_slot, pl.ds(r, 1), :], sem.at[dst_slot]).start()

    @pl.when(jnp.logical_and(valid, jnp.logical_and(i == 0, f == 0)))
    def _():
        def body(r, c):
            issue_row(0, 0, r)
            return c

        lax.fori_loop(0, tm, body, 0, unroll=8)

    @pl.when(jnp.logical_and(f == 0, jnp.logical_or(valid, jnp.logical_and(i > 0, prev_valid))))
    def _():
        pltpu.make_async_copy(hn_hbm.at[pl.ds(0, tm), :], xg_ref.at[slot], sem.at[slot]).wait()

    @pl.when(jnp.logical_and(valid, f == 0))
    def _():
        xb_ref[...] = xg_ref[slot].astype(BF16)

    @pl.when(valid)
    def _():
        per = tm // nf
        row0 = f * per
        xb = xb_ref[...]
        a = jnp.dot(xb, wg_ref[0], preferred_element_type=F32)
        for j in range(per // 2):
            issue_row(i + 1, 1 - slot, row0 + j)
        b = jnp.dot(xb, wu_ref[0], preferred_element_type=F32)
        for j in range(per // 2, per):
            issue_row(i + 1, 1 - slot, row0 + j)
        h = (a * jax.nn.sigmoid(a) * b).astype(BF16)
        part = jnp.dot(h, wd_ref[0], preferred_element_type=F32)

        @pl.when(f == 0)
        def _():
            acc_ref[...] = part

        @pl.when(f > 0)
        def _():
            acc_ref[...] += part

    @pl.when(f == nf - 1)
    def _():
        o_ref[...] = jnp.where(valid, acc_ref[...], 0.0)


def moe_experts(hn, tile_expert, tile_valid, row_tok, wg, wu, wd, *, tm, tf):
    T, D = hn.shape
    F = wg.shape[2]
    P = row_tok.shape[0] - tm
    nt = P // tm + 1
    nf = F // tf

    def w_in_map(i, f, te, tv, tok):
        return (te[i], 0, jnp.where(tv[i] > 0, f, nf - 1))

    def w_out_map(i, f, te, tv, tok):
        return (te[i], jnp.where(tv[i] > 0, f, nf - 1), 0)

    grid_spec = pltpu.PrefetchScalarGridSpec(
        num_scalar_prefetch=3,
        grid=(nt, nf),
        in_specs=[
            pl.BlockSpec(memory_space=pl.ANY),
            pl.BlockSpec((1, D, tf), w_in_map),
            pl.BlockSpec((1, D, tf), w_in_map),
            pl.BlockSpec((1, tf, D), w_out_map),
        ],
        out_specs=pl.BlockSpec((tm, D), lambda i, f, te, tv, tok: (i, 0)),
        scratch_shapes=[pltpu.VMEM((2, tm, D), F32), pltpu.VMEM((tm, D), BF16), pltpu.VMEM((tm, D), F32),
                        pltpu.SemaphoreType.DMA((2,))],
    )
    return pl.pallas_call(
        functools.partial(_expert_kernel, nf=nf),
        grid_spec=grid_spec,
        out_shape=jax.ShapeDtypeStruct((P + tm, D), F32),
        compiler_params=_cparams(("arbitrary", "arbitrary")),
        name="moe_experts",
    )(tile_expert, tile_valid, row_tok, hn, wg, wu, wd)


def _combine_kernel(dest_ref, x_ref, gate_ref, yb_hbm, o_ref, buf_ref, sem):
    i = pl.program_id(0)
    n = pl.num_programs(0)
    tc = x_ref.shape[0]
    slot = i % 2

    def issue(step, dst_slot):
        base = step * tc * TOP_K

        for r in range(tc):
            for k in range(TOP_K):
                pltpu.make_async_copy(yb_hbm.at[pl.ds(dest_ref[base + r * TOP_K + k], 1), :],
                                      buf_ref.at[dst_slot, k, pl.ds(r, 1), :], sem.at[dst_slot]).start()

    @pl.when(i == 0)
    def _():
        issue(0, 0)

    for k in range(TOP_K):
        pltpu.make_async_copy(yb_hbm.at[pl.ds(0, tc), :], buf_ref.at[slot, k], sem.at[slot]).wait()

    @pl.when(i + 1 < n)
    def _():
        issue(i + 1, 1 - slot)

    g = gate_ref[...]
    o_ref[...] = x_ref[...] + g[:, 0:1] * buf_ref[slot, 0] + g[:, 1:2] * buf_ref[slot, 1]


def moe_combine(x, gates, yb, dest, *, tc=256):
    T, D = x.shape
    grid_spec = pltpu.PrefetchScalarGridSpec(
        num_scalar_prefetch=1,
        grid=(T // tc,),
        in_specs=[
            pl.BlockSpec((tc, D), lambda i, d: (i, 0)),
            pl.BlockSpec((tc, LANES), lambda i, d: (i, 0)),
            pl.BlockSpec(memory_space=pl.ANY),
        ],
        out_specs=pl.BlockSpec((tc, D), lambda i, d: (i, 0)),
        scratch_shapes=[pltpu.VMEM((2, TOP_K, tc, D), F32), pltpu.SemaphoreType.DMA((2,))],
    )
    return pl.pallas_call(
        _combine_kernel,
        grid_spec=grid_spec,
        out_shape=jax.ShapeDtypeStruct((T, D), F32),
        compiler_params=_cparams(("arbitrary",)),
        name="moe_combine",
    )(dest, x, gates, yb)


def moe_dispatch_plan(idx, *, tm):
    T = idx.shape[0]
    A = T * TOP_K
    e_flat = idx[:, :TOP_K].reshape(A)
    onehot = (e_flat[:, None] == jnp.arange(N_EXPERTS, dtype=jnp.int32)[None, :]).astype(jnp.int32)
    rank = jnp.sum((jnp.cumsum(onehot, axis=0) - onehot) * onehot, axis=1)
    counts = jnp.sum(onehot, axis=0)
    padded = (counts + tm - 1) // tm * tm
    ends = jnp.cumsum(padded)
    pstart = ends - padded
    dest = pstart[e_flat] + rank
    P = (A // tm + N_EXPERTS + 1) * tm
    nt = P // tm
    tok_flat = jnp.arange(A, dtype=jnp.int32) // TOP_K
    row_tok = jnp.zeros((P,), jnp.int32).at[dest].set(tok_flat)
    tile_start = jnp.arange(nt, dtype=jnp.int32) * tm
    tile_expert = jnp.minimum(jnp.sum((tile_start[:, None] >= ends[None, :]).astype(jnp.int32), axis=1), N_EXPERTS - 1)
    tile_valid = (tile_start < ends[-1]).astype(jnp.int32)
    last_valid = jnp.maximum(jnp.sum(tile_valid) - 1, 0)
    tile_expert = jnp.where(tile_valid > 0, tile_expert, tile_expert[last_valid])
    return dest.astype(jnp.int32), row_tok, tile_expert, tile_valid


def moe_layer(x, norm2, router, wg, wu, wd, *, tm=512, tf=1792, tr=1024, tc=1024):
    hn, idx, gates = moe_router(x, norm2, router, tm=tr)
    dest, row_tok, tile_expert, tile_valid = moe_dispatch_plan(idx, tm=tm)
    yb = moe_experts(hn, tile_expert, tile_valid, row_tok, wg, wu, wd, tm=tm, tf=tf)
    return moe_combine(x, gates, yb, dest, tc=tc)


def _even_layer(x, norm1, w_in, shift_mu, w0_f, w_up_f, w0_b, w_up_b, a0_f, a_up_f, a0_b, a_up_b, g_up,
                k_k, k_a, r_k, gn_w, gn_b, q_norm, k_norm, w_out, norm2, ffn_gate, ffn_up, ffn_down, cast):
    Bn, S, D = x.shape
    T = Bn * S
    xf = x.reshape(T, D)
    w_in = w_in.astype(BF16)
    p_att = norm_matmul(xf, norm1, w_in[:, RW_IN:], tm=2048, tn=ATT_IN).reshape(Bn, S, ATT_IN)
    y_a = rwkv7_mixer(x, norm1, w_in[:, :RW_IN], shift_mu, w0_f, w_up_f, w0_b, w_up_b, a0_f, a_up_f, a0_b, a_up_b,
                      g_up, k_k, k_a, r_k, gn_w, gn_b, out_dtype=BF16)
    y_b = gqa_attention(p_att, q_norm, k_norm, out_dtype=BF16)
    w_out = w_out.astype(BF16)
    xf = matmul_residual([y_a.reshape(T, RW_DIM), y_b.reshape(T, ATT_DIM)], [w_out[:RW_DIM], w_out[RW_DIM:]], xf,
                         tm=1024, tn=1024)
    return ffn_swiglu(xf, norm2, ffn_gate.astype(BF16), ffn_up.astype(BF16), ffn_down.astype(BF16), tm=512, tf=1408,
                      cast=cast)


def _odd_layer(xf, Bn, S, norm1, w_in, gate_up_f, gate_bias_f, gate_up_b, gate_bias_b, out_norm, w_out,
               norm2, router, exp_gate, exp_up, exp_down):
    T, D = xf.shape
    o3 = 2 * GLA_DK + GLA_DV
    o4 = o3 + GLA_GATE_RANK
    w_main = jnp.concatenate([w_in[:, :o3], w_in[:, o4:]], axis=1).astype(BF16)
    w_gd = jnp.zeros((D, LANES), F32).at[:, :GLA_GATE_RANK].set(w_in[:, o3:o4])
    p_main = norm_matmul(xf, norm1, w_main, tm=2048, tn=1024).reshape(Bn, S, -1)
    gd = norm_matmul(xf, norm1, w_gd, tm=1024, tn=LANES)[:, :GLA_GATE_RANK].reshape(Bn, S, GLA_GATE_RANK)
    o = gla_mixer(p_main, gd, gate_up_f, gate_bias_f, gate_up_b, gate_bias_b, out_norm, out_dtype=BF16)
    xf = matmul_residual([o.reshape(T, GLA_DV)], [w_out.astype(BF16)], xf, tm=1024, tn=1024)
    return moe_layer(xf, norm2, router, exp_gate, exp_up, exp_down)


def kernel(x, e_norm1, e_w_in, e_shift_mu, e_w0_f, e_w_up_f, e_w0_b, e_w_up_b, e_a0_f, e_a_up_f, e_a0_b, e_a_up_b, e_g_up, e_k_k, e_k_a, e_r_k, e_gn_w, e_gn_b, e_q_norm, e_k_norm, e_w_out, e_norm2, e_ffn_gate, e_ffn_up, e_ffn_down, o_norm1, o_w_in, o_gate_up_f, o_gate_bias_f, o_gate_up_b, o_gate_bias_b, o_out_norm, o_w_out, o_norm2, o_router, o_exp_gate, o_exp_up, o_exp_down):
    Bn, S, D = x.shape
    xf, (exp_gate, exp_up, exp_down) = _even_layer(
        x, e_norm1[0], e_w_in[0], e_shift_mu[0], e_w0_f[0], e_w_up_f[0], e_w0_b[0],
        e_w_up_b[0], e_a0_f[0], e_a_up_f[0], e_a0_b[0], e_a_up_b[0], e_g_up[0], e_k_k[0],
        e_k_a[0], e_r_k[0], e_gn_w[0], e_gn_b[0], e_q_norm[0], e_k_norm[0], e_w_out[0],
        e_norm2[0], e_ffn_gate[0], e_ffn_up[0], e_ffn_down[0], (o_exp_gate[0], o_exp_up[0], o_exp_down[0]))
    xf = _odd_layer(xf, Bn, S, o_norm1[0], o_w_in[0], o_gate_up_f[0], o_gate_bias_f[0], o_gate_up_b[0],
                    o_gate_bias_b[0], o_out_norm[0], o_w_out[0], o_norm2[0], o_router[0],
                    exp_gate, exp_up, exp_down)
    return xf.reshape(Bn, S, D)
```

```python
import functools

import jax
import jax.numpy as jnp
from jax import lax
from jax.experimental import pallas as pl
from jax.experimental.pallas import tpu as pltpu

F32 = jnp.float32
BF16 = jnp.bfloat16
HI = lax.Precision.HIGHEST

D_MODEL = 1024
GRID_W = 64
HEAD_DIM = 64
NORM_EPS = 1e-6
RW_HEADS = 8
RW_DIM = 512
DECAY_RANK = 64
ICLR_RANK = 64
GATE_RANK = 128
RWKV_GN_EPS = 64e-5
RW_IN = 3 * RW_DIM + DECAY_RANK + ICLR_RANK + GATE_RANK
RW_CHUNK = 64
ATT_HEADS = 8
ATT_KV_HEADS = 2
ATT_DIM = 512
ATT_KV_DIM = 128
ATT_IN = ATT_DIM + 2 * ATT_KV_DIM
ROPE_THETA = 10000.0
ATT_VT_ROWS = HEAD_DIM + 16
LOG2_E = 1.4426950408889634
ATT_LOCKSTEP = 2
GLA_HEADS = 4
GLA_DK = 512
GLA_DV = 1024
GLA_DKH = 128
GLA_DVH = 256
GLA_GATE_RANK = 16
GLA_GATE_NORM = 16.0
GLA_CHUNK = 64
N_EXPERTS = 8
TOP_K = 2
LANES = 128

IN_PROJ_ROWS = 2048
OUT_PROJ_ROWS = 1024
FFN_ROWS, FFN_HIDDEN = 512, 1408
PREP_ROWS = 512
ATT_Q_ROWS = 1024
MOE_ROWS, MOE_HIDDEN = 512, 1792
ROUTER_ROWS = 1024
COMBINE_ROWS = 1024

VMEM_LIMIT = 48 * 1024 * 1024
VMEM_LIMIT_BIG = 56 * 1024 * 1024


def _cparams(sem, vmem=VMEM_LIMIT):
    return pltpu.CompilerParams(dimension_semantics=sem, vmem_limit_bytes=vmem)


def _iota2(shape, dim):
    return lax.broadcasted_iota(jnp.int32, shape, dim)


def _split(x, parts):
    out = []
    for _ in range(parts):
        t = x.astype(BF16)
        out.append(t)
        x = x - t.astype(F32)
    return out


def _dot_sel(x, m, parts=2):
    mb = m.astype(BF16)
    acc = None
    for t in _split(x, parts):
        d = jnp.dot(t, mb, preferred_element_type=F32)
        acc = d if acc is None else acc + d
    return acc


def _sel_dot(m, x, parts=3):
    mb = m.astype(BF16)
    acc = None
    for t in _split(x, parts):
        d = jnp.dot(mb, t, preferred_element_type=F32)
        acc = d if acc is None else acc + d
    return acc


def _dot3(a, b):
    ah, al = _split(a, 2)
    bh, bl = _split(b, 2)
    d = lambda u, v: jnp.dot(u, v, preferred_element_type=F32)
    return d(ah, bh) + (d(ah, bl) + d(al, bh))


def _bdot(a, b):
    return jnp.dot(a.astype(BF16), b.astype(BF16), preferred_element_type=F32)


def _bdot_nt(a, b):
    return lax.dot_general(a.astype(BF16), b.astype(BF16), (((1,), (1,)), ((), ())), preferred_element_type=F32)


def _bdot_tn(a, b):
    return lax.dot_general(a.astype(BF16), b.astype(BF16), (((0,), (0,)), ((), ())), preferred_element_type=F32)


def _norm_mm_kernel(x_ref, g_ref, w_ref, o_ref, xn_ref):
    @pl.when(pl.program_id(1) == 0)
    def _():
        x = x_ref[...]
        ms = jnp.mean(x * x, axis=-1, keepdims=True)
        xn_ref[...] = (x * lax.rsqrt(ms + NORM_EPS) * g_ref[...]).astype(xn_ref.dtype)

    if xn_ref.dtype == F32:
        o_ref[...] = _dot3(xn_ref[...], w_ref[...]).astype(o_ref.dtype)
    else:
        o_ref[...] = jnp.dot(xn_ref[...], w_ref[...], preferred_element_type=F32).astype(o_ref.dtype)


def norm_matmul(x, g, w, *, tm, tn, out_dtype=F32):
    M, K = x.shape
    N = w.shape[1]
    return pl.pallas_call(
        _norm_mm_kernel,
        grid=(M // tm, N // tn),
        in_specs=[
            pl.BlockSpec((tm, K), lambda i, j: (i, 0)),
            pl.BlockSpec((1, K), lambda i, j: (0, 0)),
            pl.BlockSpec((K, tn), lambda i, j: (0, j)),
        ],
        out_specs=pl.BlockSpec((tm, tn), lambda i, j: (i, j)),
        out_shape=jax.ShapeDtypeStruct((M, N), out_dtype),
        scratch_shapes=[pltpu.VMEM((tm, K), w.dtype)],
        compiler_params=_cparams(("parallel", "arbitrary")),
        name="norm_matmul",
    )(x, g.reshape(1, K), w)


def _mm_res_kernel(*refs):
    n = (len(refs) - 2) // 2
    r_ref, o_ref = refs[2 * n], refs[2 * n + 1]
    acc = r_ref[...]
    for y_ref, w_ref in zip(refs[:n], refs[n:2 * n]):
        acc = acc + jnp.dot(y_ref[...].astype(BF16), w_ref[...], preferred_element_type=F32)
    o_ref[...] = acc


def matmul_residual(ys, ws, r, *, tm, tn):
    M, N = r.shape
    y_specs = [pl.BlockSpec((tm, y.shape[1]), lambda i, j: (i, 0)) for y in ys]
    w_specs = [pl.BlockSpec((w.shape[0], tn), lambda i, j: (0, j)) for w in ws]
    return pl.pallas_call(
        _mm_res_kernel,
        grid=(M // tm, N // tn),
        in_specs=y_specs + w_specs + [pl.BlockSpec((tm, tn), lambda i, j: (i, j))],
        out_specs=pl.BlockSpec((tm, tn), lambda i, j: (i, j)),
        out_shape=jax.ShapeDtypeStruct((M, N), F32),
        compiler_params=_cparams(("parallel", "arbitrary")),
        name="matmul_residual",
    )(*ys, *ws, r)


def _ffn_kernel(*refs, n_cast):
    x_ref, g_ref, wg_ref, wu_ref, wd_ref = refs[:5]
    cast_in = refs[5:5 + n_cast]
    o_ref = refs[5 + n_cast]
    cast_out = refs[6 + n_cast:6 + 2 * n_cast]
    (xn_ref,) = refs[6 + 2 * n_cast:]
    f = pl.program_id(1)

    @pl.when(f == 0)
    def _():
        x = x_ref[...]
        ms = jnp.mean(x * x, axis=-1, keepdims=True)
        xn_ref[...] = (x * lax.rsqrt(ms + NORM_EPS) * g_ref[...]).astype(BF16)

    for src, dst in zip(cast_in, cast_out):
        dst[...] = src[...].astype(BF16)

    xn = xn_ref[...]
    a = jnp.dot(xn, wg_ref[...], preferred_element_type=F32)
    b = jnp.dot(xn, wu_ref[...], preferred_element_type=F32)
    h = (a * jax.nn.sigmoid(a) * b).astype(BF16)
    part = jnp.dot(h, wd_ref[...], preferred_element_type=F32)

    @pl.when(f == 0)
    def _():
        o_ref[...] = x_ref[...] + part

    @pl.when(f > 0)
    def _():
        o_ref[...] += part


def ffn_swiglu(x, g, wg, wu, wd, *, tm, tf, cast=()):
    M, D = x.shape
    F = wg.shape[1]
    nf = F // tf
    steps = (M // tm) * nf
    cast_specs, cast_shapes = [], []
    for w in cast:
        E, R, C = w.shape
        per = steps // E
        cast_specs.append(pl.BlockSpec((1, R // per, C), lambda i, f, per=per: ((i * nf + f) // per, (i * nf + f) % per, 0)))
        cast_shapes.append(jax.ShapeDtypeStruct(w.shape, BF16))
    outs = pl.pallas_call(
        functools.partial(_ffn_kernel, n_cast=len(cast)),
        grid=(M // tm, nf),
        in_specs=[
            pl.BlockSpec((tm, D), lambda i, f: (i, 0)),
            pl.BlockSpec((1, D), lambda i, f: (0, 0)),
            pl.BlockSpec((D, tf), lambda i, f: (0, f)),
            pl.BlockSpec((D, tf), lambda i, f: (0, f)),
            pl.BlockSpec((tf, D), lambda i, f: (f, 0)),
        ] + cast_specs,
        out_specs=[pl.BlockSpec((tm, D), lambda i, f: (i, 0))] + cast_specs,
        out_shape=[jax.ShapeDtypeStruct((M, D), F32)] + cast_shapes,
        scratch_shapes=[pltpu.VMEM((tm, D), BF16)],
        compiler_params=_cparams(("arbitrary", "arbitrary"), vmem=VMEM_LIMIT_BIG if cast else VMEM_LIMIT),
        name="ffn_swiglu",
    )(x, g.reshape(1, D), wg, wu, wd, *cast)
    return outs[0], tuple(outs[1:])


def _rwkv_prep_kernel(cur_ref, prev_ref, next_ref, n1_ref, win_ref, mu_ref, w0f_ref, wupf_ref, w0b_ref, wupb_ref,
                      a0f_ref, aupf_ref, a0b_ref, aupb_ref, gup_ref, kk_ref, hsum_ref,
                      r_ref, k_ref, v_ref, kkn_ref, g_ref, lwf_ref, lf_ref, asf_ref, lwb_ref, lb_ref, asb_ref):
    i = pl.program_id(1)
    nt = pl.num_programs(1)

    def project(xv):
        ms = jnp.mean(xv * xv, axis=-1, keepdims=True)
        xn = (xv * lax.rsqrt(ms + NORM_EPS) * n1_ref[...]).astype(BF16)
        return jnp.dot(xn, win_ref[...], preferred_element_type=F32)

    Tt = cur_ref.shape[1]
    pall = project(jnp.concatenate([cur_ref[0], prev_ref[0], next_ref[0]], axis=0))
    x = pall[:Tt]
    halo = pall[Tt:]
    row = _iota2(x.shape, 0)
    prev_row = jnp.where(i > 0, halo[7:8, :], 0.0)
    next_row = jnp.where(i < nt - 1, halo[8:9, :], 0.0)
    prev = jnp.where(row == 0, prev_row, pltpu.roll(x, 1, 0))
    nxt = jnp.where(row == Tt - 1, next_row, pltpu.roll(x, Tt - 1, 0))
    pm = x + (0.5 * (prev + nxt) - x) * mu_ref[...]
    o1 = 3 * RW_DIM
    o2 = o1 + DECAY_RANK
    o3 = o2 + ICLR_RANK
    r_ref[0] = pm[:, :RW_DIM]
    k = pm[:, RW_DIM:2 * RW_DIM]
    k_ref[0] = k
    v_ref[0] = pm[:, 2 * RW_DIM:o1]
    wd = jnp.tanh(pm[:, o1:o2])
    ad = pm[:, o2:o3]
    gin = jax.nn.sigmoid(pm[:, o3:])
    g_ref[0] = jnp.dot(gin.astype(BF16), gup_ref[...].astype(BF16), preferred_element_type=F32)
    kk = k * kk_ref[...]
    ss = _dot_sel(kk * kk, hsum_ref[...])
    kkn_ref[0] = kk / jnp.maximum(jnp.sqrt(ss), 1e-12)

    ci = _iota2((RW_CHUNK, RW_CHUNK), 0)
    cj = _iota2((RW_CHUNK, RW_CHUNK), 1)
    tril = (cj <= ci).astype(F32)
    triu = (cj >= ci).astype(F32)

    def direction(w0_ref, wup_ref, a0_ref, aup_ref, tri, lw_ref, l_ref, as_ref):
        z = w0_ref[...] + _dot3(wd, wup_ref[...])
        sp = jnp.maximum(-z, 0.0) + jnp.log(1.0 + jnp.exp(-jnp.abs(z)))
        lw = -jnp.exp(-sp - 0.5)
        lw_ref[0] = lw
        as_ref[0] = jax.nn.sigmoid(a0_ref[...] + _dot3(ad, aup_ref[...]))
        for c in range(Tt // RW_CHUNK):
            sl = slice(c * RW_CHUNK, (c + 1) * RW_CHUNK)
            l_ref[0, sl, :] = _sel_dot(tri, lw[sl, :])

    direction(w0f_ref, wupf_ref, a0f_ref, aupf_ref, tril, lwf_ref, lf_ref, asf_ref)
    direction(w0b_ref, wupb_ref, a0b_ref, aupb_ref, triu, lwb_ref, lb_ref, asb_ref)


def rwkv_prep(x, norm1, w_in, shift_mu, w0_f, w_up_f, w0_b, w_up_b, a0_f, a_up_f, a0_b, a_up_b, g_up, k_k, *, tt):
    B, S, D = x.shape
    nt = S // tt
    hsum = (jnp.arange(RW_DIM)[:, None] // HEAD_DIM == jnp.arange(RW_DIM)[None, :] // HEAD_DIM).astype(F32)
    row = lambda a: a.reshape(1, -1)
    full = lambda a: pl.BlockSpec(a.shape, lambda b, i: (0,) * a.ndim)
    params = [row(norm1), w_in, row(shift_mu), row(w0_f), w_up_f, row(w0_b), w_up_b, row(a0_f), a_up_f, row(a0_b), a_up_b,
              g_up, row(k_k), hsum]
    tb = tt // 8
    in_specs = [
        pl.BlockSpec((1, tt, D), lambda b, i: (b, i, 0)),
        pl.BlockSpec((1, 8, D), lambda b, i: (b, jnp.maximum(i * tb - 1, 0), 0)),
        pl.BlockSpec((1, 8, D), lambda b, i: (b, jnp.minimum((i + 1) * tb, S // 8 - 1), 0)),
    ] + [full(a) for a in params]
    out_spec = pl.BlockSpec((1, tt, RW_DIM), lambda b, i: (b, i, 0))
    out_sds = jax.ShapeDtypeStruct((B, S, RW_DIM), F32)
    return pl.pallas_call(
        _rwkv_prep_kernel,
        grid=(B, nt),
        in_specs=in_specs,
        out_specs=[out_spec] * 11,
        out_shape=[out_sds] * 11,
        compiler_params=_cparams(("parallel", "arbitrary")),
        name="rwkv_prep",
    )(x, x, x, *params)


def _bd(y, bd_mask):
    return jnp.where(bd_mask, jnp.concatenate([y, y], axis=0), jnp.zeros((), y.dtype))


def _chunk_terms(chains, out, bd_mask):
    C = chains[0]["r"].shape[0]
    n = len(chains)
    R = range(n)
    dot = lambda u, v: jnp.dot(u, v, preferred_element_type=F32)
    bd = lambda y: _bd(y, bd_mask)
    pre = []
    for ch in chains:
        r, k, v, kk, lw, L, asig, ka, rev = (ch[x] for x in ("r", "k", "v", "kk", "lw", "L", "asig", "ka", "rev"))
        a = -kk
        b = kk * asig
        kd = k * (1.0 + (asig - 1.0) * ka)
        Lp = L - lw
        Lr = Lp if rev else L
        Lend = L[0:1, :] if rev else L[C - 1:C, :]
        Lmid = L[C // 2:C // 2 + 1, :]
        einv = jnp.exp(Lmid - L)
        eend = jnp.exp(Lend - L)
        pre.append(dict(
            ar=jnp.concatenate([a * jnp.exp(Lp - Lmid), r * jnp.exp(Lr - Lmid)], axis=0).astype(BF16),
            bt=bd((b * einv).astype(BF16)), kt=bd((kd * einv).astype(BF16)),
            a0=bd((a * jnp.exp(Lp)).astype(BF16)), r0=r * jnp.exp(Lr),
            bh=(b * eend).astype(BF16), kh=(kd * eend).astype(BF16), vb=v.astype(BF16),
            dec=jnp.exp(Lend)))
    mk = [ch["masks"] for ch in chains]
    Ab = [_bdot_nt(pre[i]["ar"], pre[i]["bt"]) for i in R]
    Ak = [_bdot_nt(pre[i]["ar"], pre[i]["kt"]) for i in R]
    yield
    Aab = [jnp.where(mk[i][0], Ab[i][:C], 0.0) for i in R]
    Arb = [jnp.where(mk[i][1], Ab[i][C:], 0.0).astype(BF16) for i in R]
    AakArk = [jnp.concatenate([jnp.where(mk[i][0], Ak[i][:C], 0.0), jnp.where(mk[i][1], Ak[i][C:], 0.0)],
                              axis=0).astype(BF16) for i in R]
    AV = [dot(AakArk[i], bd(pre[i]["vb"])) for i in R]
    Xb = [jnp.where(mk[i][2], Aab[i], 0.0).astype(BF16) for i in R]
    X2b = [dot(Xb[i], bd(Xb[i])).astype(BF16) for i in R]
    yield
    T = [mk[i][6] + Xb[i].astype(F32) for i in R]
    T = [T[i] + dot(T[i].astype(BF16), bd(X2b[i])) for i in R]
    X4b = [dot(X2b[i], bd(X2b[i])).astype(BF16) for i in R]
    yield
    T = [T[i] + dot(T[i].astype(BF16), bd(X4b[i])) for i in R]
    yield
    for lvl in (3, 4, 5):
        Tb = [T[i].astype(BF16) for i in R]
        ET = [dot(jnp.where(mk[i][lvl], Aab[i], 0.0).astype(BF16), bd(Tb[i])).astype(BF16) for i in R]
        yield
        T = [T[i] + dot(Tb[i], bd(ET[i])) for i in R]
        yield
    Tb = [T[i].astype(BF16) for i in R]
    A0p = [dot(Tb[i], pre[i]["a0"]).astype(BF16) for i in R]
    Uv = [dot(Tb[i], bd(AV[i][:C].astype(BF16))).astype(BF16) for i in R]
    yield
    Rpp = [pre[i]["r0"] + dot(Arb[i], bd(A0p[i])) for i in R]
    Yv = [dot(Arb[i], bd(Uv[i])) + AV[i][C:] for i in R]
    eye2 = (_iota2(bd_mask.shape, 0) == _iota2(bd_mask.shape, 1)).astype(F32)
    P = [jnp.where(bd_mask, _bdot_tn(pre[i]["bh"], A0p[i]), 0.0) + eye2 * pre[i]["dec"] for i in R]
    Q = [jnp.where(bd_mask, _bdot_tn(pre[i]["bh"], Uv[i]) + _bdot_tn(pre[i]["kh"], pre[i]["vb"]), 0.0) for i in R]
    out.extend((Rpp[i], Yv[i], P[i], Q[i]) for i in R)


def _make_masks(C, N, rev):
    ii = _iota2((C, 2 * N), 0)
    jj = _iota2((C, 2 * N), 1) % N
    strict = (jj > ii) if rev else (jj < ii)
    rmask = strict if rev else (jj <= ii)
    blk = lambda s: (ii // s) == (jj // s)
    m8 = blk(8)
    e16 = blk(16) & jnp.logical_not(blk(8))
    e32 = blk(32) & jnp.logical_not(blk(16))
    e64 = jnp.logical_not(blk(32))
    eye = (ii == jj).astype(F32)
    return strict, rmask, m8, e16, e32, e64, eye


RW_UNROLL = 8


def _rwkv_scan_kernel(r_ref, k_ref, v_ref, kk_ref, g_ref, lwf_ref, lf_ref, asf_ref, lwb_ref, lb_ref, asb_ref,
                      ka_ref, rk_ref, gnw_ref, gnb_ref, havg_ref, o_ref,
                      yf_ref, yb_ref, h_ref, rpp_ref, yv_ref, p_ref, q_ref):
    S = r_ref.shape[1]
    C = RW_CHUNK
    N = HEAD_DIM
    W = 2 * N
    n = S // C
    U = RW_UNROLL
    nset = n // U
    masks_f = _make_masks(C, N, False)
    masks_b = _make_masks(C, N, True)
    bd_mask = (_iota2((W, W), 0) // N) == (_iota2((W, W), 1) // N)
    h_ref[...] = jnp.zeros_like(h_ref)
    dirs = ((False, lwf_ref, lf_ref, asf_ref, yf_ref, masks_f), (True, lwb_ref, lb_ref, asb_ref, yb_ref, masks_b))

    def chunk_rows(rev, st, u):
        c = st * U + u
        c = (n - 1 - c) if rev else c
        return pl.ds(pl.multiple_of(c * C, C), C)

    def state_step(st, u):
        slot = st % 2
        for di, (rev, lw_ref, l_ref, as_ref, y_ref, masks) in enumerate(dirs):
            Hb = h_ref[di].astype(BF16)
            y_ref[chunk_rows(rev, st, u), :] = (jnp.dot(rpp_ref[slot, di, u], Hb, preferred_element_type=F32)
                                                + yv_ref[slot, di, u])
            h_ref[di] = jnp.dot(p_ref[slot, di, u], Hb, preferred_element_type=F32) + q_ref[slot, di, u]

    def chunk_set(st, state_of):
        chains, slots = [], []
        for u in range(U):
            for di, (rev, lw_ref, l_ref, as_ref, y_ref, masks) in enumerate(dirs):
                rows = chunk_rows(rev, st, u)
                chains.append(dict(
                    r=r_ref[0, rows, :], k=k_ref[0, rows, :], v=v_ref[0, rows, :], kk=kk_ref[0, rows, :],
                    lw=lw_ref[0, rows, :], L=l_ref[0, rows, :], asig=as_ref[0, rows, :], ka=ka_ref[...],
                    rev=rev, masks=masks))
                slots.append((di, u))
        res = []
        pending = list(range(U)) if state_of is not None else []
        for stage, _ in enumerate(_chunk_terms(chains, res, bd_mask)):
            if pending and stage % 2 == 1:
                state_step(state_of, pending.pop(0))
        for u in pending:
            state_step(state_of, u)
        slot = st % 2
        for (di, u), (Rpp, Yv, P, Q) in zip(slots, res):
            rpp_ref[slot, di, u] = Rpp.astype(BF16)
            yv_ref[slot, di, u] = Yv
            p_ref[slot, di, u] = P.astype(BF16)
            q_ref[slot, di, u] = Q

    chunk_set(0, None)

    def body(st, carry):
        chunk_set(st, st - 1)
        return carry

    lax.fori_loop(1, nset, body, 0)
    for u in range(U):
        state_step(nset - 1, u)

    RT = 256

    def fin(i, carry):
        rows = pl.ds(pl.multiple_of(i * RT, RT), RT)
        y = yf_ref[rows, :] + yb_ref[rows, :]
        mean = _dot_sel(y, havg_ref[...])
        d = y - mean
        var = _dot_sel(d * d, havg_ref[...])
        yn = d * lax.rsqrt(var + RWKV_GN_EPS) * gnw_ref[...] + gnb_ref[...]
        r = r_ref[0, rows, :]
        kf = k_ref[0, rows, :] * (1.0 + (asf_ref[0, rows, :] - 1.0) * ka_ref[...])
        bonus = _dot_sel(r * kf * rk_ref[...], havg_ref[...]) * float(N) * v_ref[0, rows, :]
        o_ref[0, rows, :] = ((yn + bonus) * g_ref[0, rows, :]).astype(o_ref.dtype)
        return carry

    lax.fori_loop(0, S // RT, fin, 0)


def rwkv_scan(r, k, v, kk, g, lw_f, l_f, as_f, lw_b, l_b, as_b, k_a, r_k, gn_w, gn_b, out_dtype=F32):
    B, S, _ = r.shape
    W = 2 * HEAD_DIM
    N = HEAD_DIM
    havg = (jnp.arange(W)[:, None] // HEAD_DIM == jnp.arange(W)[None, :] // HEAD_DIM).astype(F32) / HEAD_DIM
    seq = pl.BlockSpec((1, S, W), lambda b, h: (b, 0, h))
    par = pl.BlockSpec((1, W), lambda b, h: (0, h))
    row = lambda a: a.reshape(1, -1)
    U = RW_UNROLL
    return pl.pallas_call(
        _rwkv_scan_kernel,
        grid=(B, RW_DIM // W),
        in_specs=[seq] * 11 + [par] * 4 + [pl.BlockSpec((W, W), lambda b, h: (0, 0))],
        out_specs=seq,
        out_shape=jax.ShapeDtypeStruct((B, S, RW_DIM), out_dtype),
        scratch_shapes=[pltpu.VMEM((S, W), F32), pltpu.VMEM((S, W), F32), pltpu.VMEM((2, W, W), F32),
                        pltpu.VMEM((2, 2, U, RW_CHUNK, W), BF16), pltpu.VMEM((2, 2, U, RW_CHUNK, W), F32),
                        pltpu.VMEM((2, 2, U, W, W), BF16), pltpu.VMEM((2, 2, U, W, W), F32)],
        compiler_params=_cparams(("parallel", "parallel")),
        name="rwkv_scan",
    )(r, k, v, kk, g, lw_f, l_f, as_f, lw_b, l_b, as_b, row(k_a), row(r_k), row(gn_w), row(gn_b), havg)


def rwkv7_mixer(x, norm1, w_in, shift_mu, w0_f, w_up_f, w0_b, w_up_b, a0_f, a_up_f, a0_b, a_up_b, g_up, k_k, k_a,
                r_k, gn_w, gn_b, out_dtype=F32, tt=PREP_ROWS):
    outs = rwkv_prep(x, norm1, w_in, shift_mu, w0_f, w_up_f, w0_b, w_up_b, a0_f, a_up_f, a0_b, a_up_b, g_up, k_k, tt=tt)
    return rwkv_scan(*outs, k_a, r_k, gn_w, gn_b, out_dtype=out_dtype)


def _head_norm_rope(x, gain, cos, sin_signed, havg):
    ms = _dot_sel(x * x, havg)
    xn = x * lax.rsqrt(ms + NORM_EPS) * gain
    W = x.shape[1]
    even = (_iota2(x.shape, 1) % 2) == 0
    partner = jnp.where(even, pltpu.roll(xn, W - 1, 1), pltpu.roll(xn, 1, 1))
    return xn * cos + partner * sin_signed


def _attn_kernel(q_ref, k_ref, v_ref, cosq_ref, sinq_ref, cosk_ref, sink_ref, qg_ref, kg_ref, hq_ref, hk_ref,
                 o_ref, ks_ref, vt_ref):
    i = pl.program_id(1)
    G = ATT_HEADS // ATT_KV_HEADS
    D = HEAD_DIM

    @pl.when(i == 0)
    def _():
        kr = _head_norm_rope(k_ref[0], kg_ref[...], cosk_ref[...], sink_ref[...], hk_ref[...])
        vt = jnp.transpose(v_ref[0])
        ones = jnp.ones((ATT_VT_ROWS - D, vt.shape[1]), BF16)
        for kv in range(ATT_KV_HEADS):
            ks_ref[kv] = kr[:, kv * D:(kv + 1) * D].astype(BF16)
            vt_ref[kv] = jnp.concatenate([vt[kv * D:(kv + 1) * D, :].astype(BF16), ones], axis=0)

    q = _head_norm_rope(q_ref[0], qg_ref[...], cosq_ref[...], sinq_ref[...], hq_ref[...]) * (D ** -0.5 * LOG2_E)
    qb = q.astype(BF16)
    outs = []
    for h0 in range(0, ATT_HEADS, ATT_LOCKSTEP):
        hs = range(h0, h0 + ATT_LOCKSTEP)
        st = [lax.dot_general(ks_ref[h // G], qb[:, h * D:(h + 1) * D], (((1,), (1,)), ((), ())),
                              preferred_element_type=F32) for h in hs]
        m = [jnp.max(x, axis=0, keepdims=True) for x in st]
        p = [jnp.exp2(x - mm).astype(BF16) for x, mm in zip(st, m)]
        ot = [jnp.dot(vt_ref[h // G], x, preferred_element_type=F32) for h, x in zip(hs, p)]
        outs.extend(o[:D] / o[D:D + 1] for o in ot)
    pairs = [jnp.transpose(jnp.concatenate(outs[2 * j:2 * j + 2], axis=0)) for j in range(ATT_HEADS // 2)]
    o_ref[0] = jnp.concatenate(pairs, axis=1).astype(o_ref.dtype)


def _rope_tables(S):
    rows = S // GRID_W
    row = jnp.repeat(jnp.arange(rows), GRID_W).astype(F32)
    col = jnp.tile(jnp.arange(GRID_W), rows).astype(F32)
    half = HEAD_DIM // 2
    freq = ROPE_THETA ** (-jnp.arange(0, half, 2, dtype=F32) / half)
    ang = jnp.concatenate([row[:, None] * freq, col[:, None] * freq], axis=-1)
    cos = jnp.repeat(jnp.cos(ang), 2, axis=-1)
    sin = jnp.repeat(jnp.sin(ang), 2, axis=-1)
    sign = jnp.where(jnp.arange(HEAD_DIM) % 2 == 0, -1.0, 1.0).astype(F32)
    return cos, sin * sign


def gqa_attention(p, q_norm, k_norm, *, tq=ATT_Q_ROWS, out_dtype=F32):
    B, S, _ = p.shape
    cos, sin = _rope_tables(S)
    tile = lambda t, n: jnp.tile(t, (1, n))
    hq = (jnp.arange(ATT_DIM)[:, None] // HEAD_DIM == jnp.arange(ATT_DIM)[None, :] // HEAD_DIM).astype(F32) / HEAD_DIM
    hk = hq[:ATT_KV_DIM, :ATT_KV_DIM]
    qg = jnp.tile(q_norm, ATT_HEADS).reshape(1, ATT_DIM)
    kg = jnp.tile(k_norm, ATT_KV_HEADS).reshape(1, ATT_KV_DIM)
    nq = ATT_DIM // ATT_KV_DIM
    const = lambda a: pl.BlockSpec(a.shape, lambda b, i: (0, 0))
    return pl.pallas_call(
        _attn_kernel,
        grid=(B, S // tq),
        in_specs=[
            pl.BlockSpec((1, tq, ATT_DIM), lambda b, i: (b, i, 0)),
            pl.BlockSpec((1, S, ATT_KV_DIM), lambda b, i: (b, 0, nq)),
            pl.BlockSpec((1, S, ATT_KV_DIM), lambda b, i: (b, 0, nq + 1)),
            pl.BlockSpec((tq, ATT_DIM), lambda b, i: (i, 0)),
            pl.BlockSpec((tq, ATT_DIM), lambda b, i: (i, 0)),
            pl.BlockSpec((S, ATT_KV_DIM), lambda b, i: (0, 0)),
            pl.BlockSpec((S, ATT_KV_DIM), lambda b, i: (0, 0)),
            const(qg), const(kg), const(hq), const(hk),
        ],
        out_specs=pl.BlockSpec((1, tq, ATT_DIM), lambda b, i: (b, i, 0)),
        out_shape=jax.ShapeDtypeStruct((B, S, ATT_DIM), out_dtype),
        scratch_shapes=[pltpu.VMEM((ATT_KV_HEADS, S, HEAD_DIM), BF16), pltpu.VMEM((ATT_KV_HEADS, ATT_VT_ROWS, S), BF16)],
        compiler_params=_cparams(("parallel", "arbitrary")),
        name="gqa_attention",
    )(p, p, p, tile(cos, ATT_HEADS), tile(sin, ATT_HEADS), tile(cos, ATT_KV_HEADS), tile(sin, ATT_KV_HEADS),
      qg, kg, hq, hk)


GLA_UNROLL = 16


def _gla_kernel(q_ref, k_ref, v_ref, gd_ref, og_ref, upf_ref, bf_ref, upb_ref, bb_ref, on_ref, o_ref,
                yf_ref, yb_ref, st_ref, kv_ref, qe_ref, dec_ref):
    S = q_ref.shape[1]
    C = GLA_CHUNK
    n = S // C
    U = GLA_UNROLL
    nset = n // U
    ii = _iota2((C, C), 0)
    jj = _iota2((C, C), 1)
    dirs = (
        (False, (jj <= ii).astype(F32), jj <= ii, upf_ref, bf_ref, yf_ref),
        (True, (jj >= ii).astype(F32), jj > ii, upb_ref, bb_ref, yb_ref),
    )
    scale = GLA_DKH ** -0.5
    st_ref[...] = jnp.zeros_like(st_ref)

    def chunk_rows(rev, st, u):
        c = st * U + u
        c = (n - 1 - c) if rev else c
        return pl.ds(pl.multiple_of(c * C, C), C)

    def state_step(st, u):
        slot = st % 2
        for di, (rev, tri, mask, up_ref, b_ref, y_ref) in enumerate(dirs):
            state = st_ref[di]
            rows = chunk_rows(rev, st, u)
            y_ref[rows, :] += lax.dot_general(qe_ref[slot, di, u], state.astype(BF16), (((1,), (1,)), ((), ())),
                                              preferred_element_type=F32)
            st_ref[di] = state * dec_ref[slot, di, u] + kv_ref[slot, di, u]

    def chunk_set(st, state_of):
        pending = list(range(U)) if state_of is not None else []
        items = []
        for u in range(U):
            for di, (rev, tri, mask, up_ref, b_ref, y_ref) in enumerate(dirs):
                rows = chunk_rows(rev, st, u)
                z = _dot3(gd_ref[0, rows, :], up_ref[...]) + b_ref[...]
                g = (jnp.minimum(z, 0.0) - jnp.log(1.0 + jnp.exp(-jnp.abs(z)))) * (1.0 / GLA_GATE_NORM)
                items.append(dict(u=u, di=di, rows=rows, rev=rev, mask=mask, y_ref=y_ref, g=g, tri=tri))
        for x in items:
            x["b"] = _sel_dot(x["tri"], x["g"])
        if pending:
            state_step(state_of, pending.pop(0))
        for x in items:
            b = x["b"]
            q = q_ref[0, x["rows"], :] * scale
            k = k_ref[0, x["rows"], :]
            b_mid = b[C // 2:C // 2 + 1, :]
            b_last = b[0:1, :] if x["rev"] else b[C - 1:C, :]
            x["vb"] = v_ref[0, x["rows"], :].astype(BF16)
            x["qm"] = (q * jnp.exp(b - b_mid)).astype(BF16)
            x["km"] = (k * jnp.exp(b_mid - b)).astype(BF16)
            x["ke"] = (k * jnp.exp(b_last - b)).astype(BF16)
            x["qe"] = (q * jnp.exp(b)).astype(BF16)
            x["dec"] = jnp.exp(b_last)
        att = [lax.dot_general(x["qm"], x["km"], (((1,), (1,)), ((), ())), preferred_element_type=F32) for x in items]
        if pending:
            state_step(state_of, pending.pop(0))
        att = [jnp.where(x["mask"], a, 0.0).astype(BF16) for x, a in zip(items, att)]
        slot = st % 2
        for idx, (x, a) in enumerate(zip(items, att)):
            x["y_ref"][x["rows"], :] = jnp.dot(a, x["vb"], preferred_element_type=F32)
            kv_ref[slot, x["di"], x["u"]] = lax.dot_general(x["vb"], x["ke"], (((0,), (0,)), ((), ())),
                                                            preferred_element_type=F32)
            qe_ref[slot, x["di"], x["u"]] = x["qe"]
            dec_ref[slot, x["di"], x["u"]] = x["dec"]
            if pending and idx % 3 == 2:
                state_step(state_of, pending.pop(0))
        for u in pending:
            state_step(state_of, u)

    chunk_set(0, None)

    def body(st, carry):
        chunk_set(st, st - 1)
        return carry

    lax.fori_loop(1, nset, body, 0)
    for u in range(U):
        state_step(nset - 1, u)

    RT = 256

    def fin(i, carry):
        rows = pl.ds(pl.multiple_of(i * RT, RT), RT)
        o = yf_ref[rows, :] + yb_ref[rows, :]
        ms = jnp.mean(o * o, axis=-1, keepdims=True)
        on = o * lax.rsqrt(ms + NORM_EPS) * on_ref[...]
        og = og_ref[0, rows, :]
        o_ref[0, rows, :] = (on * (og * jax.nn.sigmoid(og))).astype(o_ref.dtype)
        return carry

    lax.fori_loop(0, S // RT, fin, 0)


def gla_mixer(pm, gd, gate_up_f, gate_bias_f, gate_up_b, gate_bias_b, out_norm, out_dtype=F32):
    B, S, _ = pm.shape
    U = GLA_UNROLL
    H = GLA_HEADS
    kb = GLA_DK // GLA_DKH
    vb0 = 2 * GLA_DK // GLA_DVH
    ob0 = vb0 + GLA_DV // GLA_DVH
    return pl.pallas_call(
        _gla_kernel,
        grid=(B, H),
        in_specs=[
            pl.BlockSpec((1, S, GLA_DKH), lambda b, h: (b, 0, h)),
            pl.BlockSpec((1, S, GLA_DKH), lambda b, h: (b, 0, kb + h)),
            pl.BlockSpec((1, S, GLA_DVH), lambda b, h: (b, 0, vb0 + h)),
            pl.BlockSpec((1, S, GLA_GATE_RANK), lambda b, h: (b, 0, 0)),
            pl.BlockSpec((1, S, GLA_DVH), lambda b, h: (b, 0, ob0 + h)),
            pl.BlockSpec((GLA_GATE_RANK, GLA_DKH), lambda b, h: (0, h)),
            pl.BlockSpec((1, GLA_DKH), lambda b, h: (0, h)),
            pl.BlockSpec((GLA_GATE_RANK, GLA_DKH), lambda b, h: (0, h)),
            pl.BlockSpec((1, GLA_DKH), lambda b, h: (0, h)),
            pl.BlockSpec((1, GLA_DVH), lambda b, h: (0, 0)),
        ],
        out_specs=pl.BlockSpec((1, S, GLA_DVH), lambda b, h: (b, 0, h)),
        out_shape=jax.ShapeDtypeStruct((B, S, GLA_DV), out_dtype),
        scratch_shapes=[pltpu.VMEM((S, GLA_DVH), F32), pltpu.VMEM((S, GLA_DVH), F32),
                        pltpu.VMEM((2, GLA_DVH, GLA_DKH), F32),
                        pltpu.VMEM((2, 2, U, GLA_DVH, GLA_DKH), F32), pltpu.VMEM((2, 2, U, GLA_CHUNK, GLA_DKH), BF16),
                        pltpu.VMEM((2, 2, U, 1, GLA_DKH), F32)],
        compiler_params=_cparams(("parallel", "parallel")),
        name="gla_mixer",
    )(pm, pm, pm, gd, pm, gate_up_f, gate_bias_f.reshape(1, -1), gate_up_b, gate_bias_b.reshape(1, -1),
      out_norm.reshape(1, -1))


def _router_kernel(x_ref, g_ref, wr_ref, hn_ref, idx_ref, gate_ref):
    x = x_ref[...]
    ms = jnp.mean(x * x, axis=-1, keepdims=True)
    hn = x * lax.rsqrt(ms + NORM_EPS) * g_ref[...]
    hn_ref[...] = hn
    logits = _dot3(hn, wr_ref[...])
    lane = _iota2(logits.shape, 1)
    neg = jnp.float32(-jnp.inf)
    logits = jnp.where(lane < N_EXPERTS, logits, neg)
    m1 = jnp.max(logits, axis=-1, keepdims=True)
    i1 = jnp.min(jnp.where(logits == m1, lane, LANES), axis=-1, keepdims=True)
    rest = jnp.where(lane == i1, neg, logits)
    m2 = jnp.max(rest, axis=-1, keepdims=True)
    i2 = jnp.min(jnp.where(rest == m2, lane, LANES), axis=-1, keepdims=True)
    e2 = jnp.exp(m2 - m1)
    g1 = 1.0 / (1.0 + e2)
    g2 = e2 / (1.0 + e2)
    idx_ref[...] = jnp.where(lane == 0, i1, jnp.where(lane == 1, i2, 0))
    gate_ref[...] = jnp.where(lane == 0, g1, jnp.where(lane == 1, g2, 0.0))


def moe_router(x, g, router, *, tm):
    T, D = x.shape
    wr = jnp.zeros((D, LANES), F32).at[:, :N_EXPERTS].set(router)
    return pl.pallas_call(
        _router_kernel,
        grid=(T // tm,),
        in_specs=[
            pl.BlockSpec((tm, D), lambda i: (i, 0)),
            pl.BlockSpec((1, D), lambda i: (0, 0)),
            pl.BlockSpec((D, LANES), lambda i: (0, 0)),
        ],
        out_specs=[
            pl.BlockSpec((tm, D), lambda i: (i, 0)),
            pl.BlockSpec((tm, LANES), lambda i: (i, 0)),
            pl.BlockSpec((tm, LANES), lambda i: (i, 0)),
        ],
        out_shape=[jax.ShapeDtypeStruct((T, D), F32), jax.ShapeDtypeStruct((T, LANES), jnp.int32),
                   jax.ShapeDtypeStruct((T, LANES), F32)],
        compiler_params=_cparams(("parallel",)),
        name="moe_router",
    )(x, g.reshape(1, D), wr)


def _expert_kernel(te_ref, tv_ref, tok_ref, hn_hbm, wg_ref, wu_ref, wd_ref, o_ref, xg_ref, xb_ref, sem, *, nf):
    i = pl.program_id(0)
    f = pl.program_id(1)
    tm = xb_ref.shape[0]
    valid = tv_ref[i] > 0
    prev_valid = tv_ref[jnp.maximum(i - 1, 0)] > 0
    slot = i % 2

    def issue_row(tile, dst_slot, r):
        tok = tok_ref[tile * tm + r]
        pltpu.make_async_copy(hn_hbm.at[pl.ds(tok, 1), :], xg_ref.at[dst_slot, pl.ds(r, 1), :], sem.at[dst_slot]).start()

    @pl.when(jnp.logical_and(valid, jnp.logical_and(i == 0, f == 0)))
    def _():
        def body(r, c):
            issue_row(0, 0, r)
            return c

        lax.fori_loop(0, tm, body, 0, unroll=8)

    @pl.when(jnp.logical_and(f == 0, jnp.logical_or(valid, jnp.logical_and(i > 0, prev_valid))))
    def _():
        pltpu.make_async_copy(hn_hbm.at[pl.ds(0, tm), :], xg_ref.at[slot], sem.at[slot]).wait()

    @pl.when(jnp.logical_and(valid, f == 0))
    def _():
        xb_ref[...] = xg_ref[slot].astype(BF16)

    @pl.when(valid)
    def _():
        per = tm // nf
        row0 = f * per
        xb = xb_ref[...]
        a = jnp.dot(xb, wg_ref[0], preferred_element_type=F32)
        for j in range(per // 2):
            issue_row(i + 1, 1 - slot, row0 + j)
        b = jnp.dot(xb, wu_ref[0], preferred_element_type=F32)
        for j in range(per // 2, per):
            issue_row(i + 1, 1 - slot, row0 + j)
        h = (a * jax.nn.sigmoid(a) * b).astype(BF16)
        part = jnp.dot(h, wd_ref[0], preferred_element_type=F32)

        @pl.when(f == 0)
        def _():
            o_ref[...] = part

        @pl.when(f > 0)
        def _():
            o_ref[...] += part

    @pl.when(jnp.logical_and(f == nf - 1, jnp.logical_not(valid)))
    def _():
        o_ref[...] = jnp.zeros_like(o_ref)


def moe_experts(hn, tile_expert, tile_valid, row_tok, wg, wu, wd, *, tm, tf):
    T, D = hn.shape
    F = wg.shape[2]
    P = row_tok.shape[0] - tm
    nt = P // tm + 1
    nf = F // tf

    def w_in_map(i, f, te, tv, tok):
        return (te[i], 0, jnp.where(tv[i] > 0, f, nf - 1))

    def w_out_map(i, f, te, tv, tok):
        return (te[i], jnp.where(tv[i] > 0, f, nf - 1), 0)

    grid_spec = pltpu.PrefetchScalarGridSpec(
        num_scalar_prefetch=3,
        grid=(nt, nf),
        in_specs=[
            pl.BlockSpec(memory_space=pl.ANY),
            pl.BlockSpec((1, D, tf), w_in_map),
            pl.BlockSpec((1, D, tf), w_in_map),
            pl.BlockSpec((1, tf, D), w_out_map),
        ],
        out_specs=pl.BlockSpec((tm, D), lambda i, f, te, tv, tok: (i, 0)),
        scratch_shapes=[pltpu.VMEM((2, tm, D), F32), pltpu.VMEM((tm, D), BF16), pltpu.SemaphoreType.DMA((2,))],
    )
    return pl.pallas_call(
        functools.partial(_expert_kernel, nf=nf),
        grid_spec=grid_spec,
        out_shape=jax.ShapeDtypeStruct((P + tm, D), F32),
        compiler_params=_cparams(("arbitrary", "arbitrary")),
        name="moe_experts",
    )(tile_expert, tile_valid, row_tok, hn, wg, wu, wd)


def _combine_kernel(dest_ref, x_ref, gate_ref, yb_hbm, o_ref, buf_ref, sem):
    i = pl.program_id(0)
    n = pl.num_programs(0)
    tc = x_ref.shape[0]
    slot = i % 2

    def issue(step, dst_slot):
        base = step * tc * TOP_K

        for r in range(tc):
            for k in range(TOP_K):
                pltpu.make_async_copy(yb_hbm.at[pl.ds(dest_ref[base + r * TOP_K + k], 1), :],
                                      buf_ref.at[dst_slot, k, pl.ds(r, 1), :], sem.at[dst_slot]).start()

    @pl.when(i == 0)
    def _():
        issue(0, 0)

    for k in range(TOP_K):
        pltpu.make_async_copy(yb_hbm.at[pl.ds(0, tc), :], buf_ref.at[slot, k], sem.at[slot]).wait()

    @pl.when(i + 1 < n)
    def _():
        issue(i + 1, 1 - slot)

    g = gate_ref[...]
    o_ref[...] = x_ref[...] + g[:, 0:1] * buf_ref[slot, 0] + g[:, 1:2] * buf_ref[slot, 1]


def moe_combine(x, gates, yb, dest, *, tc):
    T, D = x.shape
    grid_spec = pltpu.PrefetchScalarGridSpec(
        num_scalar_prefetch=1,
        grid=(T // tc,),
        in_specs=[
            pl.BlockSpec((tc, D), lambda i, d: (i, 0)),
            pl.BlockSpec((tc, LANES), lambda i, d: (i, 0)),
            pl.BlockSpec(memory_space=pl.ANY),
        ],
        out_specs=pl.BlockSpec((tc, D), lambda i, d: (i, 0)),
        scratch_shapes=[pltpu.VMEM((2, TOP_K, tc, D), F32), pltpu.SemaphoreType.DMA((2,))],
    )
    return pl.pallas_call(
        _combine_kernel,
        grid_spec=grid_spec,
        out_shape=jax.ShapeDtypeStruct((T, D), F32),
        compiler_params=_cparams(("arbitrary",)),
        name="moe_combine",
    )(dest, x, gates, yb)


def moe_dispatch_plan(idx, *, tm):
    T = idx.shape[0]
    A = T * TOP_K
    e_flat = idx[:, :TOP_K].reshape(A)
    onehot = (e_flat[:, None] == jnp.arange(N_EXPERTS, dtype=jnp.int32)[None, :]).astype(jnp.int32)
    rank = jnp.sum((jnp.cumsum(onehot, axis=0) - onehot) * onehot, axis=1)
    counts = jnp.sum(onehot, axis=0)
    padded = (counts + tm - 1) // tm * tm
    ends = jnp.cumsum(padded)
    pstart = ends - padded
    dest = pstart[e_flat] + rank
    P = (A // tm + N_EXPERTS + 1) * tm
    nt = P // tm
    tok_flat = jnp.arange(A, dtype=jnp.int32) // TOP_K
    row_tok = jnp.zeros((P,), jnp.int32).at[dest].set(tok_flat)
    tile_start = jnp.arange(nt, dtype=jnp.int32) * tm
    tile_expert = jnp.minimum(jnp.sum((tile_start[:, None] >= ends[None, :]).astype(jnp.int32), axis=1), N_EXPERTS - 1)
    tile_valid = (tile_start < ends[-1]).astype(jnp.int32)
    last_valid = jnp.maximum(jnp.sum(tile_valid) - 1, 0)
    tile_expert = jnp.where(tile_valid > 0, tile_expert, tile_expert[last_valid])
    return dest.astype(jnp.int32), row_tok, tile_expert, tile_valid


def moe_layer(x, norm2, router, wg, wu, wd, *, tm=MOE_ROWS, tf=MOE_HIDDEN, tr=ROUTER_ROWS, tc=COMBINE_ROWS):
    hn, idx, gates = moe_router(x, norm2, router, tm=tr)
    dest, row_tok, tile_expert, tile_valid = moe_dispatch_plan(idx, tm=tm)
    yb = moe_experts(hn, tile_expert, tile_valid, row_tok, wg, wu, wd, tm=tm, tf=tf)
    return moe_combine(x, gates, yb, dest, tc=tc)


def _even_layer(x, norm1, w_in, shift_mu, w0_f, w_up_f, w0_b, w_up_b, a0_f, a_up_f, a0_b, a_up_b, g_up,
                k_k, k_a, r_k, gn_w, gn_b, q_norm, k_norm, w_out, norm2, ffn_gate, ffn_up, ffn_down, cast):
    Bn, S, D = x.shape
    T = Bn * S
    xf = x.reshape(T, D)
    w_in = w_in.astype(BF16)
    p_att = norm_matmul(xf, norm1, w_in[:, RW_IN:], tm=IN_PROJ_ROWS, tn=ATT_IN).reshape(Bn, S, ATT_IN)
    y_a = rwkv7_mixer(x, norm1, w_in[:, :RW_IN], shift_mu, w0_f, w_up_f, w0_b, w_up_b, a0_f, a_up_f, a0_b, a_up_b,
                      g_up, k_k, k_a, r_k, gn_w, gn_b, out_dtype=BF16)
    y_b = gqa_attention(p_att, q_norm, k_norm, out_dtype=BF16)
    w_out = w_out.astype(BF16)
    xf = matmul_residual([y_a.reshape(T, RW_DIM), y_b.reshape(T, ATT_DIM)], [w_out[:RW_DIM], w_out[RW_DIM:]], xf,
                         tm=OUT_PROJ_ROWS, tn=D)
    return ffn_swiglu(xf, norm2, ffn_gate.astype(BF16), ffn_up.astype(BF16), ffn_down.astype(BF16), tm=FFN_ROWS,
                      tf=FFN_HIDDEN, cast=cast)


def _odd_layer(xf, Bn, S, norm1, w_in, gate_up_f, gate_bias_f, gate_up_b, gate_bias_b, out_norm, w_out,
               norm2, router, exp_gate, exp_up, exp_down):
    T, D = xf.shape
    o3 = 2 * GLA_DK + GLA_DV
    o4 = o3 + GLA_GATE_RANK
    w_main = jnp.concatenate([w_in[:, :o3], w_in[:, o4:]], axis=1).astype(BF16)
    w_gd = jnp.zeros((D, LANES), F32).at[:, :GLA_GATE_RANK].set(w_in[:, o3:o4])
    p_main = norm_matmul(xf, norm1, w_main, tm=IN_PROJ_ROWS, tn=D).reshape(Bn, S, -1)
    gd = norm_matmul(xf, norm1, w_gd, tm=OUT_PROJ_ROWS, tn=LANES)[:, :GLA_GATE_RANK].reshape(Bn, S, GLA_GATE_RANK)
    o = gla_mixer(p_main, gd, gate_up_f, gate_bias_f, gate_up_b, gate_bias_b, out_norm, out_dtype=BF16)
    xf = matmul_residual([o.reshape(T, GLA_DV)], [w_out.astype(BF16)], xf, tm=OUT_PROJ_ROWS, tn=D)
    return moe_layer(xf, norm2, router, exp_gate, exp_up, exp_down)


def kernel(x, e_norm1, e_w_in, e_shift_mu, e_w0_f, e_w_up_f, e_w0_b, e_w_up_b, e_a0_f, e_a_up_f, e_a0_b, e_a_up_b, e_g_up, e_k_k, e_k_a, e_r_k, e_gn_w, e_gn_b, e_q_norm, e_k_norm, e_w_out, e_norm2, e_ffn_gate, e_ffn_up, e_ffn_down, o_norm1, o_w_in, o_gate_up_f, o_gate_bias_f, o_gate_up_b, o_gate_bias_b, o_out_norm, o_w_out, o_norm2, o_router, o_exp_gate, o_exp_up, o_exp_down):
    Bn, S, D = x.shape
    xf, (exp_gate, exp_up, exp_down) = _even_layer(
        x, e_norm1[0], e_w_in[0], e_shift_mu[0], e_w0_f[0], e_w_up_f[0], e_w0_b[0],
        e_w_up_b[0], e_a0_f[0], e_a_up_f[0], e_a0_b[0], e_a_up_b[0], e_g_up[0], e_k_k[0],
        e_k_a[0], e_r_k[0], e_gn_w[0], e_gn_b[0], e_q_norm[0], e_k_norm[0], e_w_out[0],
        e_norm2[0], e_ffn_gate[0], e_ffn_up[0], e_ffn_down[0], (o_exp_gate[0], o_exp_up[0], o_exp_down[0]))
    xf = _odd_layer(xf, Bn, S, o_norm1[0], o_w_in[0], o_gate_up_f[0], o_gate_bias_f[0], o_gate_up_b[0],
                    o_gate_bias_b[0], o_out_norm[0], o_w_out[0], o_norm2[0], o_router[0],
                    exp_gate, exp_up, exp_down)
    return xf.reshape(Bn, S, D)
```

```python
import functools

import jax
import jax.numpy as jnp
from jax import lax
from jax.experimental import pallas as pl
from jax.experimental.pallas import tpu as pltpu

F32 = jnp.float32
BF16 = jnp.bfloat16
HI = lax.Precision.HIGHEST

D_MODEL = 1024
GRID_W = 64
HEAD_DIM = 64
NORM_EPS = 1e-6
RW_HEADS = 8
RW_DIM = 512
DECAY_RANK = 64
ICLR_RANK = 64
GATE_RANK = 128
RWKV_GN_EPS = 64e-5
RW_IN = 3 * RW_DIM + DECAY_RANK + ICLR_RANK + GATE_RANK
RW_CHUNK = 64
ATT_HEADS = 8
ATT_KV_HEADS = 2
ATT_DIM = 512
ATT_KV_DIM = 128
ATT_IN = ATT_DIM + 2 * ATT_KV_DIM
ROPE_THETA = 10000.0
ATT_VT_ROWS = HEAD_DIM + 16
LOG2_E = 1.4426950408889634
ATT_LOCKSTEP = 2
GLA_HEADS = 4
GLA_DK = 512
GLA_DV = 1024
GLA_DKH = 128
GLA_DVH = 256
GLA_GATE_RANK = 16
GLA_GATE_NORM = 16.0
GLA_CHUNK = 64
N_EXPERTS = 8
TOP_K = 2
LANES = 128

IN_PROJ_ROWS = 2048
OUT_PROJ_ROWS = 1024
FFN_ROWS, FFN_HIDDEN = 512, 1408
PREP_ROWS = 512
ATT_Q_ROWS = 1024
MOE_ROWS, MOE_HIDDEN = 512, 1792
ROUTER_ROWS = 1024
COMBINE_ROWS = 1024

VMEM_LIMIT = 48 * 1024 * 1024
VMEM_LIMIT_BIG = 56 * 1024 * 1024


def _cparams(sem, vmem=VMEM_LIMIT):
    return pltpu.CompilerParams(dimension_semantics=sem, vmem_limit_bytes=vmem)


def _iota2(shape, dim):
    return lax.broadcasted_iota(jnp.int32, shape, dim)


def _split(x, parts):
    out = []
    for _ in range(parts):
        t = x.astype(BF16)
        out.append(t)
        x = x - t.astype(F32)
    return out


def _dot_sel(x, m, parts=2):
    mb = m.astype(BF16)
    acc = None
    for t in _split(x, parts):
        d = jnp.dot(t, mb, preferred_element_type=F32)
        acc = d if acc is None else acc + d
    return acc


def _sel_dot(m, x, parts=3):
    mb = m.astype(BF16)
    acc = None
    for t in _split(x, parts):
        d = jnp.dot(mb, t, preferred_element_type=F32)
        acc = d if acc is None else acc + d
    return acc


def _dot3(a, b):
    ah, al = _split(a, 2)
    bh, bl = _split(b, 2)
    d = lambda u, v: jnp.dot(u, v, preferred_element_type=F32)
    return d(ah, bh) + (d(ah, bl) + d(al, bh))


def _bdot(a, b):
    return jnp.dot(a.astype(BF16), b.astype(BF16), preferred_element_type=F32)


def _bdot_nt(a, b):
    return lax.dot_general(a.astype(BF16), b.astype(BF16), (((1,), (1,)), ((), ())), preferred_element_type=F32)


def _bdot_tn(a, b):
    return lax.dot_general(a.astype(BF16), b.astype(BF16), (((0,), (0,)), ((), ())), preferred_element_type=F32)


def _norm_mm_kernel(x_ref, g_ref, w_ref, o_ref, xn_ref):
    @pl.when(pl.program_id(1) == 0)
    def _():
        x = x_ref[...]
        ms = jnp.mean(x * x, axis=-1, keepdims=True)
        xn_ref[...] = (x * lax.rsqrt(ms + NORM_EPS) * g_ref[...]).astype(xn_ref.dtype)

    if xn_ref.dtype == F32:
        o_ref[...] = _dot3(xn_ref[...], w_ref[...]).astype(o_ref.dtype)
    else:
        o_ref[...] = jnp.dot(xn_ref[...], w_ref[...], preferred_element_type=F32).astype(o_ref.dtype)


def norm_matmul(x, g, w, *, tm, tn, out_dtype=F32):
    M, K = x.shape
    N = w.shape[1]
    return pl.pallas_call(
        _norm_mm_kernel,
        grid=(M // tm, N // tn),
        in_specs=[
            pl.BlockSpec((tm, K), lambda i, j: (i, 0)),
            pl.BlockSpec((1, K), lambda i, j: (0, 0)),
            pl.BlockSpec((K, tn), lambda i, j: (0, j)),
        ],
        out_specs=pl.BlockSpec((tm, tn), lambda i, j: (i, j)),
        out_shape=jax.ShapeDtypeStruct((M, N), out_dtype),
        scratch_shapes=[pltpu.VMEM((tm, K), w.dtype)],
        compiler_params=_cparams(("parallel", "arbitrary")),
        name="norm_matmul",
    )(x, g.reshape(1, K), w)


def _mm_res_kernel(*refs):
    n = (len(refs) - 2) // 2
    r_ref, o_ref = refs[2 * n], refs[2 * n + 1]
    acc = r_ref[...]
    for y_ref, w_ref in zip(refs[:n], refs[n:2 * n]):
        acc = acc + jnp.dot(y_ref[...].astype(BF16), w_ref[...], preferred_element_type=F32)
    o_ref[...] = acc


def matmul_residual(ys, ws, r, *, tm, tn):
    M, N = r.shape
    y_specs = [pl.BlockSpec((tm, y.shape[1]), lambda i, j: (i, 0)) for y in ys]
    w_specs = [pl.BlockSpec((w.shape[0], tn), lambda i, j: (0, j)) for w in ws]
    return pl.pallas_call(
        _mm_res_kernel,
        grid=(M // tm, N // tn),
        in_specs=y_specs + w_specs + [pl.BlockSpec((tm, tn), lambda i, j: (i, j))],
        out_specs=pl.BlockSpec((tm, tn), lambda i, j: (i, j)),
        out_shape=jax.ShapeDtypeStruct((M, N), F32),
        compiler_params=_cparams(("parallel", "arbitrary")),
        name="matmul_residual",
    )(*ys, *ws, r)


def _ffn_kernel(*refs, n_cast):
    x_ref, g_ref, wg_ref, wu_ref, wd_ref = refs[:5]
    cast_in = refs[5:5 + n_cast]
    o_ref = refs[5 + n_cast]
    cast_out = refs[6 + n_cast:6 + 2 * n_cast]
    (xn_ref,) = refs[6 + 2 * n_cast:]
    f = pl.program_id(1)

    @pl.when(f == 0)
    def _():
        x = x_ref[...]
        ms = jnp.mean(x * x, axis=-1, keepdims=True)
        xn_ref[...] = (x * lax.rsqrt(ms + NORM_EPS) * g_ref[...]).astype(BF16)

    for src, dst in zip(cast_in, cast_out):
        dst[...] = src[...].astype(BF16)

    xn = xn_ref[...]
    a = jnp.dot(xn, wg_ref[...], preferred_element_type=F32)
    b = jnp.dot(xn, wu_ref[...], preferred_element_type=F32)
    h = (a * jax.nn.sigmoid(a) * b).astype(BF16)
    part = jnp.dot(h, wd_ref[...], preferred_element_type=F32)

    @pl.when(f == 0)
    def _():
        o_ref[...] = x_ref[...] + part

    @pl.when(f > 0)
    def _():
        o_ref[...] += part


def ffn_swiglu(x, g, wg, wu, wd, *, tm, tf, cast=()):
    M, D = x.shape
    F = wg.shape[1]
    nf = F // tf
    steps = (M // tm) * nf
    cast_specs, cast_shapes = [], []
    for w in cast:
        E, R, C = w.shape
        per = steps // E
        cast_specs.append(pl.BlockSpec((1, R // per, C), lambda i, f, per=per: ((i * nf + f) // per, (i * nf + f) % per, 0)))
        cast_shapes.append(jax.ShapeDtypeStruct(w.shape, BF16))
    outs = pl.pallas_call(
        functools.partial(_ffn_kernel, n_cast=len(cast)),
        grid=(M // tm, nf),
        in_specs=[
            pl.BlockSpec((tm, D), lambda i, f: (i, 0)),
            pl.BlockSpec((1, D), lambda i, f: (0, 0)),
            pl.BlockSpec((D, tf), lambda i, f: (0, f)),
            pl.BlockSpec((D, tf), lambda i, f: (0, f)),
            pl.BlockSpec((tf, D), lambda i, f: (f, 0)),
        ] + cast_specs,
        out_specs=[pl.BlockSpec((tm, D), lambda i, f: (i, 0))] + cast_specs,
        out_shape=[jax.ShapeDtypeStruct((M, D), F32)] + cast_shapes,
        scratch_shapes=[pltpu.VMEM((tm, D), BF16)],
        compiler_params=_cparams(("arbitrary", "arbitrary"), vmem=VMEM_LIMIT_BIG if cast else VMEM_LIMIT),
        name="ffn_swiglu",
    )(x, g.reshape(1, D), wg, wu, wd, *cast)
    return outs[0], tuple(outs[1:])


def _rwkv_prep_kernel(cur_ref, prev_ref, next_ref, n1_ref, win_ref, mu_ref, w0f_ref, wupf_ref, w0b_ref, wupb_ref,
                      a0f_ref, aupf_ref, a0b_ref, aupb_ref, gup_ref, kk_ref, hsum_ref,
                      r_ref, k_ref, v_ref, kkn_ref, g_ref, lwf_ref, lf_ref, asf_ref, lwb_ref, lb_ref, asb_ref):
    i = pl.program_id(1)
    nt = pl.num_programs(1)

    def project(xv):
        ms = jnp.mean(xv * xv, axis=-1, keepdims=True)
        xn = (xv * lax.rsqrt(ms + NORM_EPS) * n1_ref[...]).astype(BF16)
        return jnp.dot(xn, win_ref[...], preferred_element_type=F32)

    Tt = cur_ref.shape[1]
    pall = project(jnp.concatenate([cur_ref[0], prev_ref[0], next_ref[0]], axis=0))
    x = pall[:Tt]
    halo = pall[Tt:]
    row = _iota2(x.shape, 0)
    prev_row = jnp.where(i > 0, halo[7:8, :], 0.0)
    next_row = jnp.where(i < nt - 1, halo[8:9, :], 0.0)
    prev = jnp.where(row == 0, prev_row, pltpu.roll(x, 1, 0))
    nxt = jnp.where(row == Tt - 1, next_row, pltpu.roll(x, Tt - 1, 0))
    pm = x + (0.5 * (prev + nxt) - x) * mu_ref[...]
    o1 = 3 * RW_DIM
    o2 = o1 + DECAY_RANK
    o3 = o2 + ICLR_RANK
    r_ref[0] = pm[:, :RW_DIM]
    k = pm[:, RW_DIM:2 * RW_DIM]
    k_ref[0] = k
    v_ref[0] = pm[:, 2 * RW_DIM:o1]
    wd = jnp.tanh(pm[:, o1:o2])
    ad = pm[:, o2:o3]
    gin = jax.nn.sigmoid(pm[:, o3:])
    g_ref[0] = jnp.dot(gin.astype(BF16), gup_ref[...].astype(BF16), preferred_element_type=F32)
    kk = k * kk_ref[...]
    ss = _dot_sel(kk * kk, hsum_ref[...])
    kkn_ref[0] = kk / jnp.maximum(jnp.sqrt(ss), 1e-12)

    ci = _iota2((RW_CHUNK, RW_CHUNK), 0)
    cj = _iota2((RW_CHUNK, RW_CHUNK), 1)
    tril = (cj <= ci).astype(F32)
    triu = (cj >= ci).astype(F32)

    def direction(w0_ref, wup_ref, a0_ref, aup_ref, tri, lw_ref, l_ref, as_ref):
        z = w0_ref[...] + _dot3(wd, wup_ref[...])
        sp = jnp.maximum(-z, 0.0) + jnp.log(1.0 + jnp.exp(-jnp.abs(z)))
        lw = -jnp.exp(-sp - 0.5)
        lw_ref[0] = lw
        as_ref[0] = jax.nn.sigmoid(a0_ref[...] + _dot3(ad, aup_ref[...]))
        for c in range(Tt // RW_CHUNK):
            sl = slice(c * RW_CHUNK, (c + 1) * RW_CHUNK)
            l_ref[0, sl, :] = _sel_dot(tri, lw[sl, :])

    direction(w0f_ref, wupf_ref, a0f_ref, aupf_ref, tril, lwf_ref, lf_ref, asf_ref)
    direction(w0b_ref, wupb_ref, a0b_ref, aupb_ref, triu, lwb_ref, lb_ref, asb_ref)


def rwkv_prep(x, norm1, w_in, shift_mu, w0_f, w_up_f, w0_b, w_up_b, a0_f, a_up_f, a0_b, a_up_b, g_up, k_k, *, tt):
    B, S, D = x.shape
    nt = S // tt
    hsum = (jnp.arange(RW_DIM)[:, None] // HEAD_DIM == jnp.arange(RW_DIM)[None, :] // HEAD_DIM).astype(F32)
    row = lambda a: a.reshape(1, -1)
    full = lambda a: pl.BlockSpec(a.shape, lambda b, i: (0,) * a.ndim)
    params = [row(norm1), w_in, row(shift_mu), row(w0_f), w_up_f, row(w0_b), w_up_b, row(a0_f), a_up_f, row(a0_b), a_up_b,
              g_up, row(k_k), hsum]
    tb = tt // 8
    in_specs = [
        pl.BlockSpec((1, tt, D), lambda b, i: (b, i, 0)),
        pl.BlockSpec((1, 8, D), lambda b, i: (b, jnp.maximum(i * tb - 1, 0), 0)),
        pl.BlockSpec((1, 8, D), lambda b, i: (b, jnp.minimum((i + 1) * tb, S // 8 - 1), 0)),
    ] + [full(a) for a in params]
    out_spec = pl.BlockSpec((1, tt, RW_DIM), lambda b, i: (b, i, 0))
    out_sds = jax.ShapeDtypeStruct((B, S, RW_DIM), F32)
    return pl.pallas_call(
        _rwkv_prep_kernel,
        grid=(B, nt),
        in_specs=in_specs,
        out_specs=[out_spec] * 11,
        out_shape=[out_sds] * 11,
        compiler_params=_cparams(("parallel", "arbitrary")),
        name="rwkv_prep",
    )(x, x, x, *params)


def _bd(y, bd_mask):
    return jnp.where(bd_mask, jnp.concatenate([y, y], axis=0), jnp.zeros((), y.dtype))


def _chunk_terms(chains, out, bd_mask):
    C = chains[0]["r"].shape[0]
    n = len(chains)
    R = range(n)
    dot = lambda u, v: jnp.dot(u, v, preferred_element_type=F32)
    bd = lambda y: _bd(y, bd_mask)
    pre = []
    for ch in chains:
        r, k, v, kk, lw, L, asig, ka, rev = (ch[x] for x in ("r", "k", "v", "kk", "lw", "L", "asig", "ka", "rev"))
        a = -kk
        b = kk * asig
        kd = k * (1.0 + (asig - 1.0) * ka)
        Lp = L - lw
        Lr = Lp if rev else L
        Lend = L[0:1, :] if rev else L[C - 1:C, :]
        Lmid = L[C // 2:C // 2 + 1, :]
        einv = jnp.exp(Lmid - L)
        eend = jnp.exp(Lend - L)
        pre.append(dict(
            ar=jnp.concatenate([a * jnp.exp(Lp - Lmid), r * jnp.exp(Lr - Lmid)], axis=0).astype(BF16),
            bt=bd((b * einv).astype(BF16)), kt=bd((kd * einv).astype(BF16)),
            a0=bd((a * jnp.exp(Lp)).astype(BF16)), r0=r * jnp.exp(Lr),
            bh=(b * eend).astype(BF16), kh=(kd * eend).astype(BF16), vb=v.astype(BF16),
            dec=jnp.exp(Lend)))
    mk = [ch["masks"] for ch in chains]
    Ab = [_bdot_nt(pre[i]["ar"], pre[i]["bt"]) for i in R]
    Ak = [_bdot_nt(pre[i]["ar"], pre[i]["kt"]) for i in R]
    yield
    Aab = [jnp.where(mk[i][0], Ab[i][:C], 0.0) for i in R]
    Arb = [jnp.where(mk[i][1], Ab[i][C:], 0.0).astype(BF16) for i in R]
    AakArk = [jnp.concatenate([jnp.where(mk[i][0], Ak[i][:C], 0.0), jnp.where(mk[i][1], Ak[i][C:], 0.0)],
                              axis=0).astype(BF16) for i in R]
    AV = [dot(AakArk[i], bd(pre[i]["vb"])) for i in R]
    Xb = [jnp.where(mk[i][2], Aab[i], 0.0).astype(BF16) for i in R]
    X2b = [dot(Xb[i], bd(Xb[i])).astype(BF16) for i in R]
    yield
    T = [mk[i][6] + Xb[i].astype(F32) for i in R]
    T = [T[i] + dot(T[i].astype(BF16), bd(X2b[i])) for i in R]
    X4b = [dot(X2b[i], bd(X2b[i])).astype(BF16) for i in R]
    yield
    T = [T[i] + dot(T[i].astype(BF16), bd(X4b[i])) for i in R]
    yield
    for lvl in (3, 4, 5):
        Tb = [T[i].astype(BF16) for i in R]
        ET = [dot(jnp.where(mk[i][lvl], Aab[i], 0.0).astype(BF16), bd(Tb[i])).astype(BF16) for i in R]
        yield
        T = [T[i] + dot(Tb[i], bd(ET[i])) for i in R]
        yield
    Tb = [T[i].astype(BF16) for i in R]
    A0p = [dot(Tb[i], pre[i]["a0"]).astype(BF16) for i in R]
    Uv = [dot(Tb[i], bd(AV[i][:C].astype(BF16))).astype(BF16) for i in R]
    yield
    Rpp = [pre[i]["r0"] + dot(Arb[i], bd(A0p[i])) for i in R]
    Yv = [dot(Arb[i], bd(Uv[i])) + AV[i][C:] for i in R]
    eye2 = (_iota2(bd_mask.shape, 0) == _iota2(bd_mask.shape, 1)).astype(F32)
    P = [jnp.where(bd_mask, _bdot_tn(pre[i]["bh"], A0p[i]), 0.0) + eye2 * pre[i]["dec"] for i in R]
    Q = [jnp.where(bd_mask, _bdot_tn(pre[i]["bh"], Uv[i]) + _bdot_tn(pre[i]["kh"], pre[i]["vb"]), 0.0) for i in R]
    out.extend((Rpp[i], Yv[i], P[i], Q[i]) for i in R)


def _make_masks(C, N, rev):
    ii = _iota2((C, 2 * N), 0)
    jj = _iota2((C, 2 * N), 1) % N
    strict = (jj > ii) if rev else (jj < ii)
    rmask = strict if rev else (jj <= ii)
    blk = lambda s: (ii // s) == (jj // s)
    m8 = blk(8)
    e16 = blk(16) & jnp.logical_not(blk(8))
    e32 = blk(32) & jnp.logical_not(blk(16))
    e64 = jnp.logical_not(blk(32))
    eye = (ii == jj).astype(F32)
    return strict, rmask, m8, e16, e32, e64, eye


RW_UNROLL = 8


def _rwkv_scan_kernel(r_ref, k_ref, v_ref, kk_ref, g_ref, lwf_ref, lf_ref, asf_ref, lwb_ref, lb_ref, asb_ref,
                      ka_ref, rk_ref, gnw_ref, gnb_ref, havg_ref, o_ref,
                      yf_ref, yb_ref, h_ref, rpp_ref, yv_ref, p_ref, q_ref):
    S = r_ref.shape[1]
    C = RW_CHUNK
    N = HEAD_DIM
    W = 2 * N
    n = S // C
    U = RW_UNROLL
    nset = n // U
    masks_f = _make_masks(C, N, False)
    masks_b = _make_masks(C, N, True)
    bd_mask = (_iota2((W, W), 0) // N) == (_iota2((W, W), 1) // N)
    h_ref[...] = jnp.zeros_like(h_ref)
    dirs = ((False, lwf_ref, lf_ref, asf_ref, yf_ref, masks_f), (True, lwb_ref, lb_ref, asb_ref, yb_ref, masks_b))

    def chunk_rows(rev, st, u):
        c = st * U + u
        c = (n - 1 - c) if rev else c
        return pl.ds(pl.multiple_of(c * C, C), C)

    def state_step(st, u):
        slot = st % 2
        for di, (rev, lw_ref, l_ref, as_ref, y_ref, masks) in enumerate(dirs):
            Hb = h_ref[di].astype(BF16)
            y_ref[chunk_rows(rev, st, u), :] = (jnp.dot(rpp_ref[slot, di, u], Hb, preferred_element_type=F32)
                                                + yv_ref[slot, di, u])
            h_ref[di] = jnp.dot(p_ref[slot, di, u], Hb, preferred_element_type=F32) + q_ref[slot, di, u]

    def chunk_set(st, state_of):
        chains, slots = [], []
        for u in range(U):
            for di, (rev, lw_ref, l_ref, as_ref, y_ref, masks) in enumerate(dirs):
                rows = chunk_rows(rev, st, u)
                chains.append(dict(
                    r=r_ref[0, rows, :], k=k_ref[0, rows, :], v=v_ref[0, rows, :], kk=kk_ref[0, rows, :],
                    lw=lw_ref[0, rows, :], L=l_ref[0, rows, :], asig=as_ref[0, rows, :], ka=ka_ref[...],
                    rev=rev, masks=masks))
                slots.append((di, u))
        res = []
        pending = list(range(U)) if state_of is not None else []
        for stage, _ in enumerate(_chunk_terms(chains, res, bd_mask)):
            if pending and stage % 2 == 1:
                state_step(state_of, pending.pop(0))
        for u in pending:
            state_step(state_of, u)
        slot = st % 2
        for (di, u), (Rpp, Yv, P, Q) in zip(slots, res):
            rpp_ref[slot, di, u] = Rpp.astype(BF16)
            yv_ref[slot, di, u] = Yv
            p_ref[slot, di, u] = P.astype(BF16)
            q_ref[slot, di, u] = Q

    chunk_set(0, None)

    def body(st, carry):
        chunk_set(st, st - 1)
        return carry

    lax.fori_loop(1, nset, body, 0)
    for u in range(U):
        state_step(nset - 1, u)

    RT = min(S, 1024)

    def fin(i, carry):
        rows = pl.ds(pl.multiple_of(i * RT, RT), RT)
        y = yf_ref[rows, :] + yb_ref[rows, :]
        mean = _dot_sel(y, havg_ref[...])
        d = y - mean
        var = _dot_sel(d * d, havg_ref[...])
        yn = d * lax.rsqrt(var + RWKV_GN_EPS) * gnw_ref[...] + gnb_ref[...]
        r = r_ref[0, rows, :]
        kf = k_ref[0, rows, :] * (1.0 + (asf_ref[0, rows, :] - 1.0) * ka_ref[...])
        bonus = _dot_sel(r * kf * rk_ref[...], havg_ref[...]) * float(N) * v_ref[0, rows, :]
        o_ref[0, rows, :] = ((yn + bonus) * g_ref[0, rows, :]).astype(o_ref.dtype)
        return carry

    lax.fori_loop(0, S // RT, fin, 0)


def rwkv_scan(r, k, v, kk, g, lw_f, l_f, as_f, lw_b, l_b, as_b, k_a, r_k, gn_w, gn_b, out_dtype=F32):
    B, S, _ = r.shape
    W = 2 * HEAD_DIM
    N = HEAD_DIM
    havg = (jnp.arange(W)[:, None] // HEAD_DIM == jnp.arange(W)[None, :] // HEAD_DIM).astype(F32) / HEAD_DIM
    seq = pl.BlockSpec((1, S, W), lambda b, h: (b, 0, h))
    par = pl.BlockSpec((1, W), lambda b, h: (0, h))
    row = lambda a: a.reshape(1, -1)
    U = RW_UNROLL
    return pl.pallas_call(
        _rwkv_scan_kernel,
        grid=(B, RW_DIM // W),
        in_specs=[seq] * 11 + [par] * 4 + [pl.BlockSpec((W, W), lambda b, h: (0, 0))],
        out_specs=seq,
        out_shape=jax.ShapeDtypeStruct((B, S, RW_DIM), out_dtype),
        scratch_shapes=[pltpu.VMEM((S, W), F32), pltpu.VMEM((S, W), F32), pltpu.VMEM((2, W, W), F32),
                        pltpu.VMEM((2, 2, U, RW_CHUNK, W), BF16), pltpu.VMEM((2, 2, U, RW_CHUNK, W), F32),
                        pltpu.VMEM((2, 2, U, W, W), BF16), pltpu.VMEM((2, 2, U, W, W), F32)],
        compiler_params=_cparams(("parallel", "parallel")),
        name="rwkv_scan",
    )(r, k, v, kk, g, lw_f, l_f, as_f, lw_b, l_b, as_b, row(k_a), row(r_k), row(gn_w), row(gn_b), havg)


def rwkv7_mixer(x, norm1, w_in, shift_mu, w0_f, w_up_f, w0_b, w_up_b, a0_f, a_up_f, a0_b, a_up_b, g_up, k_k, k_a,
                r_k, gn_w, gn_b, out_dtype=F32, tt=PREP_ROWS):
    outs = rwkv_prep(x, norm1, w_in, shift_mu, w0_f, w_up_f, w0_b, w_up_b, a0_f, a_up_f, a0_b, a_up_b, g_up, k_k, tt=tt)
    return rwkv_scan(*outs, k_a, r_k, gn_w, gn_b, out_dtype=out_dtype)


def _head_norm_rope(x, gain, cos, sin_signed, havg):
    ms = _dot_sel(x * x, havg)
    xn = x * lax.rsqrt(ms + NORM_EPS) * gain
    W = x.shape[1]
    even = (_iota2(x.shape, 1) % 2) == 0
    partner = jnp.where(even, pltpu.roll(xn, W - 1, 1), pltpu.roll(xn, 1, 1))
    return xn * cos + partner * sin_signed


def _attn_kernel(q_ref, k_ref, v_ref, cosq_ref, sinq_ref, cosk_ref, sink_ref, qg_ref, kg_ref, hq_ref, hk_ref,
                 o_ref, ks_ref, vt_ref):
    i = pl.program_id(1)
    G = ATT_HEADS // ATT_KV_HEADS
    D = HEAD_DIM

    @pl.when(i == 0)
    def _():
        kr = _head_norm_rope(k_ref[0], kg_ref[...], cosk_ref[...], sink_ref[...], hk_ref[...])
        vt = jnp.transpose(v_ref[0])
        ones = jnp.ones((ATT_VT_ROWS - D, vt.shape[1]), BF16)
        for kv in range(ATT_KV_HEADS):
            ks_ref[kv] = kr[:, kv * D:(kv + 1) * D].astype(BF16)
            vt_ref[kv] = jnp.concatenate([vt[kv * D:(kv + 1) * D, :].astype(BF16), ones], axis=0)

    q = _head_norm_rope(q_ref[0], qg_ref[...], cosq_ref[...], sinq_ref[...], hq_ref[...]) * (D ** -0.5 * LOG2_E)
    qb = q.astype(BF16)
    outs = []
    for h0 in range(0, ATT_HEADS, ATT_LOCKSTEP):
        hs = range(h0, h0 + ATT_LOCKSTEP)
        st = [lax.dot_general(ks_ref[h // G], qb[:, h * D:(h + 1) * D], (((1,), (1,)), ((), ())),
                              preferred_element_type=F32) for h in hs]
        m = [jnp.max(x, axis=0, keepdims=True) for x in st]
        p = [jnp.exp2(x - mm).astype(BF16) for x, mm in zip(st, m)]
        ot = [jnp.dot(vt_ref[h // G], x, preferred_element_type=F32) for h, x in zip(hs, p)]
        outs.extend(o[:D] / o[D:D + 1] for o in ot)
    pairs = [jnp.transpose(jnp.concatenate(outs[2 * j:2 * j + 2], axis=0)) for j in range(ATT_HEADS // 2)]
    o_ref[0] = jnp.concatenate(pairs, axis=1).astype(o_ref.dtype)


def _rope_tables(S):
    rows = S // GRID_W
    row = jnp.repeat(jnp.arange(rows), GRID_W).astype(F32)
    col = jnp.tile(jnp.arange(GRID_W), rows).astype(F32)
    half = HEAD_DIM // 2
    freq = ROPE_THETA ** (-jnp.arange(0, half, 2, dtype=F32) / half)
    ang = jnp.concatenate([row[:, None] * freq, col[:, None] * freq], axis=-1)
    cos = jnp.repeat(jnp.cos(ang), 2, axis=-1)
    sin = jnp.repeat(jnp.sin(ang), 2, axis=-1)
    sign = jnp.where(jnp.arange(HEAD_DIM) % 2 == 0, -1.0, 1.0).astype(F32)
    return cos, sin * sign


def gqa_attention(p, q_norm, k_norm, *, tq=ATT_Q_ROWS, out_dtype=F32):
    B, S, _ = p.shape
    cos, sin = _rope_tables(S)
    tile = lambda t, n: jnp.tile(t, (1, n))
    hq = (jnp.arange(ATT_DIM)[:, None] // HEAD_DIM == jnp.arange(ATT_DIM)[None, :] // HEAD_DIM).astype(F32) / HEAD_DIM
    hk = hq[:ATT_KV_DIM, :ATT_KV_DIM]
    qg = jnp.tile(q_norm, ATT_HEADS).reshape(1, ATT_DIM)
    kg = jnp.tile(k_norm, ATT_KV_HEADS).reshape(1, ATT_KV_DIM)
    nq = ATT_DIM // ATT_KV_DIM
    const = lambda a: pl.BlockSpec(a.shape, lambda b, i: (0, 0))
    return pl.pallas_call(
        _attn_kernel,
        grid=(B, S // tq),
        in_specs=[
            pl.BlockSpec((1, tq, ATT_DIM), lambda b, i: (b, i, 0)),
            pl.BlockSpec((1, S, ATT_KV_DIM), lambda b, i: (b, 0, nq)),
            pl.BlockSpec((1, S, ATT_KV_DIM), lambda b, i: (b, 0, nq + 1)),
            pl.BlockSpec((tq, ATT_DIM), lambda b, i: (i, 0)),
            pl.BlockSpec((tq, ATT_DIM), lambda b, i: (i, 0)),
            pl.BlockSpec((S, ATT_KV_DIM), lambda b, i: (0, 0)),
            pl.BlockSpec((S, ATT_KV_DIM), lambda b, i: (0, 0)),
            const(qg), const(kg), const(hq), const(hk),
        ],
        out_specs=pl.BlockSpec((1, tq, ATT_DIM), lambda b, i: (b, i, 0)),
        out_shape=jax.ShapeDtypeStruct((B, S, ATT_DIM), out_dtype),
        scratch_shapes=[pltpu.VMEM((ATT_KV_HEADS, S, HEAD_DIM), BF16), pltpu.VMEM((ATT_KV_HEADS, ATT_VT_ROWS, S), BF16)],
        compiler_params=_cparams(("parallel", "arbitrary")),
        name="gqa_attention",
    )(p, p, p, tile(cos, ATT_HEADS), tile(sin, ATT_HEADS), tile(cos, ATT_KV_HEADS), tile(sin, ATT_KV_HEADS),
      qg, kg, hq, hk)


GLA_UNROLL = 16


def _gla_kernel(q_ref, k_ref, v_ref, gd_ref, og_ref, upf_ref, bf_ref, upb_ref, bb_ref, on_ref, o_ref,
                yf_ref, yb_ref, st_ref, kv_ref, qe_ref, dec_ref):
    S = q_ref.shape[1]
    C = GLA_CHUNK
    n = S // C
    U = GLA_UNROLL
    nset = n // U
    ii = _iota2((C, C), 0)
    jj = _iota2((C, C), 1)
    dirs = (
        (False, (jj <= ii).astype(F32), jj <= ii, upf_ref, bf_ref, yf_ref),
        (True, (jj >= ii).astype(F32), jj > ii, upb_ref, bb_ref, yb_ref),
    )
    scale = GLA_DKH ** -0.5
    st_ref[...] = jnp.zeros_like(st_ref)

    def chunk_rows(rev, st, u):
        c = st * U + u
        c = (n - 1 - c) if rev else c
        return pl.ds(pl.multiple_of(c * C, C), C)

    def state_step(st, u):
        slot = st % 2
        for di, (rev, tri, mask, up_ref, b_ref, y_ref) in enumerate(dirs):
            state = st_ref[di]
            rows = chunk_rows(rev, st, u)
            y_ref[rows, :] += lax.dot_general(qe_ref[slot, di, u], state.astype(BF16), (((1,), (1,)), ((), ())),
                                              preferred_element_type=F32)
            st_ref[di] = state * dec_ref[slot, di, u] + kv_ref[slot, di, u]

    def chunk_set(st, state_of):
        pending = list(range(U)) if state_of is not None else []
        items = []
        for u in range(U):
            for di, (rev, tri, mask, up_ref, b_ref, y_ref) in enumerate(dirs):
                rows = chunk_rows(rev, st, u)
                z = _dot3(gd_ref[0, rows, :], up_ref[...]) + b_ref[...]
                g = (jnp.minimum(z, 0.0) - jnp.log(1.0 + jnp.exp(-jnp.abs(z)))) * (1.0 / GLA_GATE_NORM)
                items.append(dict(u=u, di=di, rows=rows, rev=rev, mask=mask, y_ref=y_ref, g=g, tri=tri))
        for x in items:
            x["b"] = _sel_dot(x["tri"], x["g"])
        if pending:
            state_step(state_of, pending.pop(0))
        for x in items:
            b = x["b"]
            q = q_ref[0, x["rows"], :] * scale
            k = k_ref[0, x["rows"], :]
            b_mid = b[C // 2:C // 2 + 1, :]
            b_last = b[0:1, :] if x["rev"] else b[C - 1:C, :]
            x["vb"] = v_ref[0, x["rows"], :].astype(BF16)
            x["qm"] = (q * jnp.exp(b - b_mid)).astype(BF16)
            x["km"] = (k * jnp.exp(b_mid - b)).astype(BF16)
            x["ke"] = (k * jnp.exp(b_last - b)).astype(BF16)
            x["qe"] = (q * jnp.exp(b)).astype(BF16)
            x["dec"] = jnp.exp(b_last)
        att = [lax.dot_general(x["qm"], x["km"], (((1,), (1,)), ((), ())), preferred_element_type=F32) for x in items]
        if pending:
            state_step(state_of, pending.pop(0))
        att = [jnp.where(x["mask"], a, 0.0).astype(BF16) for x, a in zip(items, att)]
        slot = st % 2
        for idx, (x, a) in enumerate(zip(items, att)):
            x["y_ref"][x["rows"], :] = jnp.dot(a, x["vb"], preferred_element_type=F32)
            kv_ref[slot, x["di"], x["u"]] = lax.dot_general(x["vb"], x["ke"], (((0,), (0,)), ((), ())),
                                                            preferred_element_type=F32)
            qe_ref[slot, x["di"], x["u"]] = x["qe"]
            dec_ref[slot, x["di"], x["u"]] = x["dec"]
            if pending and idx % 3 == 2:
                state_step(state_of, pending.pop(0))
        for u in pending:
            state_step(state_of, u)

    chunk_set(0, None)

    def body(st, carry):
        chunk_set(st, st - 1)
        return carry

    lax.fori_loop(1, nset, body, 0)
    for u in range(U):
        state_step(nset - 1, u)

    RT = min(S, 1024)

    def fin(i, carry):
        rows = pl.ds(pl.multiple_of(i * RT, RT), RT)
        o = yf_ref[rows, :] + yb_ref[rows, :]
        ms = jnp.mean(o * o, axis=-1, keepdims=True)
        on = o * lax.rsqrt(ms + NORM_EPS) * on_ref[...]
        og = og_ref[0, rows, :]
        o_ref[0, rows, :] = (on * (og * jax.nn.sigmoid(og))).astype(o_ref.dtype)
        return carry

    lax.fori_loop(0, S // RT, fin, 0)


def gla_mixer(pm, gd, gate_up_f, gate_bias_f, gate_up_b, gate_bias_b, out_norm, out_dtype=F32):
    B, S, _ = pm.shape
    U = GLA_UNROLL
    H = GLA_HEADS
    kb = GLA_DK // GLA_DKH
    vb0 = 2 * GLA_DK // GLA_DVH
    ob0 = vb0 + GLA_DV // GLA_DVH
    return pl.pallas_call(
        _gla_kernel,
        grid=(B, H),
        in_specs=[
            pl.BlockSpec((1, S, GLA_DKH), lambda b, h: (b, 0, h)),
            pl.BlockSpec((1, S, GLA_DKH), lambda b, h: (b, 0, kb + h)),
            pl.BlockSpec((1, S, GLA_DVH), lambda b, h: (b, 0, vb0 + h)),
            pl.BlockSpec((1, S, GLA_GATE_RANK), lambda b, h: (b, 0, 0)),
            pl.BlockSpec((1, S, GLA_DVH), lambda b, h: (b, 0, ob0 + h)),
            pl.BlockSpec((GLA_GATE_RANK, GLA_DKH), lambda b, h: (0, h)),
            pl.BlockSpec((1, GLA_DKH), lambda b, h: (0, h)),
            pl.BlockSpec((GLA_GATE_RANK, GLA_DKH), lambda b, h: (0, h)),
            pl.BlockSpec((1, GLA_DKH), lambda b, h: (0, h)),
            pl.BlockSpec((1, GLA_DVH), lambda b, h: (0, 0)),
        ],
        out_specs=pl.BlockSpec((1, S, GLA_DVH), lambda b, h: (b, 0, h)),
        out_shape=jax.ShapeDtypeStruct((B, S, GLA_DV), out_dtype),
        scratch_shapes=[pltpu.VMEM((S, GLA_DVH), F32), pltpu.VMEM((S, GLA_DVH), F32),
                        pltpu.VMEM((2, GLA_DVH, GLA_DKH), F32),
                        pltpu.VMEM((2, 2, U, GLA_DVH, GLA_DKH), F32), pltpu.VMEM((2, 2, U, GLA_CHUNK, GLA_DKH), BF16),
                        pltpu.VMEM((2, 2, U, 1, GLA_DKH), F32)],
        compiler_params=_cparams(("parallel", "parallel")),
        name="gla_mixer",
    )(pm, pm, pm, gd, pm, gate_up_f, gate_bias_f.reshape(1, -1), gate_up_b, gate_bias_b.reshape(1, -1),
      out_norm.reshape(1, -1))


def _router_kernel(x_ref, g_ref, wr_ref, hn_ref, idx_ref, gate_ref):
    x = x_ref[...]
    ms = jnp.mean(x * x, axis=-1, keepdims=True)
    hn = x * lax.rsqrt(ms + NORM_EPS) * g_ref[...]
    hn_ref[...] = hn
    logits = _dot3(hn, wr_ref[...])
    lane = _iota2(logits.shape, 1)
    neg = jnp.float32(-jnp.inf)
    logits = jnp.where(lane < N_EXPERTS, logits, neg)
    m1 = jnp.max(logits, axis=-1, keepdims=True)
    i1 = jnp.min(jnp.where(logits == m1, lane, LANES), axis=-1, keepdims=True)
    rest = jnp.where(lane == i1, neg, logits)
    m2 = jnp.max(rest, axis=-1, keepdims=True)
    i2 = jnp.min(jnp.where(rest == m2, lane, LANES), axis=-1, keepdims=True)
    e2 = jnp.exp(m2 - m1)
    g1 = 1.0 / (1.0 + e2)
    g2 = e2 / (1.0 + e2)
    idx_ref[...] = jnp.where(lane == 0, i1, jnp.where(lane == 1, i2, 0))
    gate_ref[...] = jnp.where(lane == 0, g1, jnp.where(lane == 1, g2, 0.0))


def moe_router(x, g, router, *, tm):
    T, D = x.shape
    wr = jnp.zeros((D, LANES), F32).at[:, :N_EXPERTS].set(router)
    return pl.pallas_call(
        _router_kernel,
        grid=(T // tm,),
        in_specs=[
            pl.BlockSpec((tm, D), lambda i: (i, 0)),
            pl.BlockSpec((1, D), lambda i: (0, 0)),
            pl.BlockSpec((D, LANES), lambda i: (0, 0)),
        ],
        out_specs=[
            pl.BlockSpec((tm, D), lambda i: (i, 0)),
            pl.BlockSpec((tm, LANES), lambda i: (i, 0)),
            pl.BlockSpec((tm, LANES), lambda i: (i, 0)),
        ],
        out_shape=[jax.ShapeDtypeStruct((T, D), F32), jax.ShapeDtypeStruct((T, LANES), jnp.int32),
                   jax.ShapeDtypeStruct((T, LANES), F32)],
        compiler_params=_cparams(("parallel",)),
        name="moe_router",
    )(x, g.reshape(1, D), wr)


def _expert_kernel(te_ref, tv_ref, tok_ref, hn_hbm, wg_ref, wu_ref, wd_ref, o_ref, xg_ref, xb_ref, acc_ref, sem, *, nf):
    i = pl.program_id(0)
    f = pl.program_id(1)
    tm = xb_ref.shape[0]
    valid = tv_ref[i] > 0
    prev_valid = tv_ref[jnp.maximum(i - 1, 0)] > 0
    slot = i % 2

    def issue_row(tile, dst_slot, r):
        tok = tok_ref[tile * tm + r]
        pltpu.make_async_copy(hn_hbm.at[pl.ds(tok, 1), :], xg_ref.at[dst_slot, pl.ds(r, 1), :], sem.at[dst_slot]).start()

    @pl.when(jnp.logical_and(valid, jnp.logical_and(i == 0, f == 0)))
    def _():
        def body(r, c):
            issue_row(0, 0, r)
            return c

        lax.fori_loop(0, tm, body, 0, unroll=8)

    @pl.when(jnp.logical_and(f == 0, jnp.logical_or(valid, jnp.logical_and(i > 0, prev_valid))))
    def _():
        pltpu.make_async_copy(hn_hbm.at[pl.ds(0, tm), :], xg_ref.at[slot], sem.at[slot]).wait()

    @pl.when(jnp.logical_and(valid, f == 0))
    def _():
        xb_ref[...] = xg_ref[slot].astype(BF16)

    @pl.when(valid)
    def _():
        per = tm // nf
        row0 = f * per
        xb = xb_ref[...]
        a = jnp.dot(xb, wg_ref[0], preferred_element_type=F32)
        for j in range(per // 2):
            issue_row(i + 1, 1 - slot, row0 + j)
        b = jnp.dot(xb, wu_ref[0], preferred_element_type=F32)
        for j in range(per // 2, per):
            issue_row(i + 1, 1 - slot, row0 + j)
        h = (a * jax.nn.sigmoid(a) * b).astype(BF16)
        part = jnp.dot(h, wd_ref[0], preferred_element_type=F32)

        @pl.when(f == 0)
        def _():
            acc_ref[...] = part

        @pl.when(f > 0)
        def _():
            acc_ref[...] += part

    @pl.when(f == nf - 1)
    def _():
        o_ref[...] = jnp.where(valid, acc_ref[...], 0.0)


def moe_experts(hn, tile_expert, tile_valid, row_tok, wg, wu, wd, *, tm, tf):
    T, D = hn.shape
    F = wg.shape[2]
    P = row_tok.shape[0] - tm
    nt = P // tm + 1
    nf = F // tf

    def w_in_map(i, f, te, tv, tok):
        return (te[i], 0, jnp.where(tv[i] > 0, f, nf - 1))

    def w_out_map(i, f, te, tv, tok):
        return (te[i], jnp.where(tv[i] > 0, f, nf - 1), 0)

    grid_spec = pltpu.PrefetchScalarGridSpec(
        num_scalar_prefetch=3,
        grid=(nt, nf),
        in_specs=[
            pl.BlockSpec(memory_space=pl.ANY),
            pl.BlockSpec((1, D, tf), w_in_map),
            pl.BlockSpec((1, D, tf), w_in_map),
            pl.BlockSpec((1, tf, D), w_out_map),
        ],
        out_specs=pl.BlockSpec((tm, D), lambda i, f, te, tv, tok: (i, 0)),
        scratch_shapes=[pltpu.VMEM((2, tm, D), F32), pltpu.VMEM((tm, D), BF16), pltpu.VMEM((tm, D), F32),
                        pltpu.SemaphoreType.DMA((2,))],
    )
    return pl.pallas_call(
        functools.partial(_expert_kernel, nf=nf),
        grid_spec=grid_spec,
        out_shape=jax.ShapeDtypeStruct((P + tm, D), F32),
        compiler_params=_cparams(("arbitrary", "arbitrary")),
        name="moe_experts",
    )(tile_expert, tile_valid, row_tok, hn, wg, wu, wd)


def _combine_kernel(dest_ref, x_ref, gate_ref, yb_hbm, o_ref, buf_ref, sem):
    i = pl.program_id(0)
    n = pl.num_programs(0)
    tc = x_ref.shape[0]
    slot = i % 2

    def issue(step, dst_slot):
        base = step * tc * TOP_K

        for r in range(tc):
            for k in range(TOP_K):
                pltpu.make_async_copy(yb_hbm.at[pl.ds(dest_ref[base + r * TOP_K + k], 1), :],
                                      buf_ref.at[dst_slot, k, pl.ds(r, 1), :], sem.at[dst_slot]).start()

    @pl.when(i == 0)
    def _():
        issue(0, 0)

    for k in range(TOP_K):
        pltpu.make_async_copy(yb_hbm.at[pl.ds(0, tc), :], buf_ref.at[slot, k], sem.at[slot]).wait()

    @pl.when(i + 1 < n)
    def _():
        issue(i + 1, 1 - slot)

    g = gate_ref[...]
    o_ref[...] = x_ref[...] + g[:, 0:1] * buf_ref[slot, 0] + g[:, 1:2] * buf_ref[slot, 1]


def moe_combine(x, gates, yb, dest, *, tc):
    T, D = x.shape
    grid_spec = pltpu.PrefetchScalarGridSpec(
        num_scalar_prefetch=1,
        grid=(T // tc,),
        in_specs=[
            pl.BlockSpec((tc, D), lambda i, d: (i, 0)),
            pl.BlockSpec((tc, LANES), lambda i, d: (i, 0)),
            pl.BlockSpec(memory_space=pl.ANY),
        ],
        out_specs=pl.BlockSpec((tc, D), lambda i, d: (i, 0)),
        scratch_shapes=[pltpu.VMEM((2, TOP_K, tc, D), F32), pltpu.SemaphoreType.DMA((2,))],
    )
    return pl.pallas_call(
        _combine_kernel,
        grid_spec=grid_spec,
        out_shape=jax.ShapeDtypeStruct((T, D), F32),
        compiler_params=_cparams(("arbitrary",)),
        name="moe_combine",
    )(dest, x, gates, yb)


def moe_dispatch_plan(idx, *, tm):
    T = idx.shape[0]
    A = T * TOP_K
    e_flat = idx[:, :TOP_K].reshape(A)
    onehot = (e_flat[:, None] == jnp.arange(N_EXPERTS, dtype=jnp.int32)[None, :]).astype(jnp.int32)
    rank = jnp.sum((jnp.cumsum(onehot, axis=0) - onehot) * onehot, axis=1)
    counts = jnp.sum(onehot, axis=0)
    padded = (counts + tm - 1) // tm * tm
    ends = jnp.cumsum(padded)
    pstart = ends - padded
    dest = pstart[e_flat] + rank
    P = (A // tm + N_EXPERTS + 1) * tm
    nt = P // tm
    tok_flat = jnp.arange(A, dtype=jnp.int32) // TOP_K
    row_tok = jnp.zeros((P,), jnp.int32).at[dest].set(tok_flat)
    tile_start = jnp.arange(nt, dtype=jnp.int32) * tm
    tile_expert = jnp.minimum(jnp.sum((tile_start[:, None] >= ends[None, :]).astype(jnp.int32), axis=1), N_EXPERTS - 1)
    tile_valid = (tile_start < ends[-1]).astype(jnp.int32)
    last_valid = jnp.maximum(jnp.sum(tile_valid) - 1, 0)
    tile_expert = jnp.where(tile_valid > 0, tile_expert, tile_expert[last_valid])
    return dest.astype(jnp.int32), row_tok, tile_expert, tile_valid


def moe_layer(x, norm2, router, wg, wu, wd, *, tm=MOE_ROWS, tf=MOE_HIDDEN, tr=ROUTER_ROWS, tc=COMBINE_ROWS):
    hn, idx, gates = moe_router(x, norm2, router, tm=tr)
    dest, row_tok, tile_expert, tile_valid = moe_dispatch_plan(idx, tm=tm)
    yb = moe_experts(hn, tile_expert, tile_valid, row_tok, wg, wu, wd, tm=tm, tf=tf)
    return moe_combine(x, gates, yb, dest, tc=tc)


def _even_layer(x, norm1, w_in, shift_mu, w0_f, w_up_f, w0_b, w_up_b, a0_f, a_up_f, a0_b, a_up_b, g_up,
                k_k, k_a, r_k, gn_w, gn_b, q_norm, k_norm, w_out, norm2, ffn_gate, ffn_up, ffn_down, cast):
    Bn, S, D = x.shape
    T = Bn * S
    xf = x.reshape(T, D)
    w_in = w_in.astype(BF16)
    p_att = norm_matmul(xf, norm1, w_in[:, RW_IN:], tm=IN_PROJ_ROWS, tn=ATT_IN).reshape(Bn, S, ATT_IN)
    y_a = rwkv7_mixer(x, norm1, w_in[:, :RW_IN], shift_mu, w0_f, w_up_f, w0_b, w_up_b, a0_f, a_up_f, a0_b, a_up_b,
                      g_up, k_k, k_a, r_k, gn_w, gn_b, out_dtype=BF16)
    y_b = gqa_attention(p_att, q_norm, k_norm, out_dtype=BF16)
    w_out = w_out.astype(BF16)
    xf = matmul_residual([y_a.reshape(T, RW_DIM), y_b.reshape(T, ATT_DIM)], [w_out[:RW_DIM], w_out[RW_DIM:]], xf,
                         tm=OUT_PROJ_ROWS, tn=D)
    return ffn_swiglu(xf, norm2, ffn_gate.astype(BF16), ffn_up.astype(BF16), ffn_down.astype(BF16), tm=FFN_ROWS,
                      tf=FFN_HIDDEN, cast=cast)


def _odd_layer(xf, Bn, S, norm1, w_in, gate_up_f, gate_bias_f, gate_up_b, gate_bias_b, out_norm, w_out,
               norm2, router, exp_gate, exp_up, exp_down):
    T, D = xf.shape
    o3 = 2 * GLA_DK + GLA_DV
    o4 = o3 + GLA_GATE_RANK
    w_main = jnp.concatenate([w_in[:, :o3], w_in[:, o4:]], axis=1).astype(BF16)
    w_gd = jnp.zeros((D, LANES), F32).at[:, :GLA_GATE_RANK].set(w_in[:, o3:o4])
    p_main = norm_matmul(xf, norm1, w_main, tm=IN_PROJ_ROWS, tn=D).reshape(Bn, S, -1)
    gd = norm_matmul(xf, norm1, w_gd, tm=OUT_PROJ_ROWS, tn=LANES)[:, :GLA_GATE_RANK].reshape(Bn, S, GLA_GATE_RANK)
    o = gla_mixer(p_main, gd, gate_up_f, gate_bias_f, gate_up_b, gate_bias_b, out_norm, out_dtype=BF16)
    xf = matmul_residual([o.reshape(T, GLA_DV)], [w_out.astype(BF16)], xf, tm=OUT_PROJ_ROWS, tn=D)
    return moe_layer(xf, norm2, router, exp_gate, exp_up, exp_down)


def kernel(x, e_norm1, e_w_in, e_shift_mu, e_w0_f, e_w_up_f, e_w0_b, e_w_up_b, e_a0_f, e_a_up_f, e_a0_b, e_a_up_b, e_g_up, e_k_k, e_k_a, e_r_k, e_gn_w, e_gn_b, e_q_norm, e_k_norm, e_w_out, e_norm2, e_ffn_gate, e_ffn_up, e_ffn_down, o_norm1, o_w_in, o_gate_up_f, o_gate_bias_f, o_gate_up_b, o_gate_bias_b, o_out_norm, o_w_out, o_norm2, o_router, o_exp_gate, o_exp_up, o_exp_down):
    Bn, S, D = x.shape
    xf, (exp_gate, exp_up, exp_down) = _even_layer(
        x, e_norm1[0], e_w_in[0], e_shift_mu[0], e_w0_f[0], e_w_up_f[0], e_w0_b[0],
        e_w_up_b[0], e_a0_f[0], e_a_up_f[0], e_a0_b[0], e_a_up_b[0], e_g_up[0], e_k_k[0],
        e_k_a[0], e_r_k[0], e_gn_w[0], e_gn_b[0], e_q_norm[0], e_k_norm[0], e_w_out[0],
        e_norm2[0], e_ffn_gate[0], e_ffn_up[0], e_ffn_down[0], (o_exp_gate[0], o_exp_up[0], o_exp_down[0]))
    xf = _odd_layer(xf, Bn, S, o_norm1[0], o_w_in[0], o_gate_up_f[0], o_gate_bias_f[0], o_gate_up_b[0],
                    o_gate_bias_b[0], o_out_norm[0], o_w_out[0], o_norm2[0], o_router[0],
                    exp_gate, exp_up, exp_down)
    return xf.reshape(Bn, S, D)
```

```python
import functools

import jax
import jax.numpy as jnp
from jax import lax
from jax.experimental import pallas as pl
from jax.experimental.pallas import tpu as pltpu

F32 = jnp.float32
BF16 = jnp.bfloat16
HI = lax.Precision.HIGHEST

D_MODEL = 1024
GRID_W = 64
HEAD_DIM = 64
NORM_EPS = 1e-6
RW_HEADS = 8
RW_DIM = 512
DECAY_RANK = 64
ICLR_RANK = 64
GATE_RANK = 128
RWKV_GN_EPS = 64e-5
RW_IN = 3 * RW_DIM + DECAY_RANK + ICLR_RANK + GATE_RANK
RW_CHUNK = 64
ATT_HEADS = 8
ATT_KV_HEADS = 2
ATT_DIM = 512
ATT_KV_DIM = 128
ATT_IN = ATT_DIM + 2 * ATT_KV_DIM
ROPE_THETA = 10000.0
ATT_VT_ROWS = HEAD_DIM + 16
LOG2_E = 1.4426950408889634
ATT_LOCKSTEP = 2
GLA_HEADS = 4
GLA_DK = 512
GLA_DV = 1024
GLA_DKH = 128
GLA_DVH = 256
GLA_GATE_RANK = 16
GLA_GATE_NORM = 16.0
GLA_CHUNK = 64
N_EXPERTS = 8
TOP_K = 2
LANES = 128

IN_PROJ_ROWS = 2048
OUT_PROJ_ROWS = 1024
FFN_ROWS, FFN_HIDDEN = 512, 1408
PREP_ROWS = 512
ATT_Q_ROWS = 1024
MOE_ROWS, MOE_HIDDEN = 512, 1792
ROUTER_ROWS = 1024
COMBINE_ROWS = 1024

VMEM_LIMIT = 48 * 1024 * 1024
VMEM_LIMIT_BIG = 56 * 1024 * 1024


def _cparams(sem, vmem=VMEM_LIMIT):
    return pltpu.CompilerParams(dimension_semantics=sem, vmem_limit_bytes=vmem)


def _iota2(shape, dim):
    return lax.broadcasted_iota(jnp.int32, shape, dim)


def _split(x, parts):
    out = []
    for _ in range(parts):
        t = x.astype(BF16)
        out.append(t)
        x = x - t.astype(F32)
    return out


def _dot_sel(x, m, parts=2):
    mb = m.astype(BF16)
    acc = None
    for t in _split(x, parts):
        d = jnp.dot(t, mb, preferred_element_type=F32)
        acc = d if acc is None else acc + d
    return acc


def _sel_dot(m, x, parts=3):
    mb = m.astype(BF16)
    acc = None
    for t in _split(x, parts):
        d = jnp.dot(mb, t, preferred_element_type=F32)
        acc = d if acc is None else acc + d
    return acc


def _dot3(a, b):
    ah, al = _split(a, 2)
    bh, bl = _split(b, 2)
    d = lambda u, v: jnp.dot(u, v, preferred_element_type=F32)
    return d(ah, bh) + (d(ah, bl) + d(al, bh))


def _bdot(a, b):
    return jnp.dot(a.astype(BF16), b.astype(BF16), preferred_element_type=F32)


def _bdot_nt(a, b):
    return lax.dot_general(a.astype(BF16), b.astype(BF16), (((1,), (1,)), ((), ())), preferred_element_type=F32)


def _bdot_tn(a, b):
    return lax.dot_general(a.astype(BF16), b.astype(BF16), (((0,), (0,)), ((), ())), preferred_element_type=F32)


def _norm_mm_kernel(x_ref, g_ref, w_ref, o_ref, xn_ref):
    @pl.when(pl.program_id(1) == 0)
    def _():
        x = x_ref[...]
        ms = jnp.mean(x * x, axis=-1, keepdims=True)
        xn_ref[...] = (x * lax.rsqrt(ms + NORM_EPS) * g_ref[...]).astype(xn_ref.dtype)

    if xn_ref.dtype == F32:
        o_ref[...] = _dot3(xn_ref[...], w_ref[...]).astype(o_ref.dtype)
    else:
        o_ref[...] = jnp.dot(xn_ref[...], w_ref[...], preferred_element_type=F32).astype(o_ref.dtype)


def norm_matmul(x, g, w, *, tm, tn, out_dtype=F32):
    M, K = x.shape
    N = w.shape[1]
    return pl.pallas_call(
        _norm_mm_kernel,
        grid=(M // tm, N // tn),
        in_specs=[
            pl.BlockSpec((tm, K), lambda i, j: (i, 0)),
            pl.BlockSpec((1, K), lambda i, j: (0, 0)),
            pl.BlockSpec((K, tn), lambda i, j: (0, j)),
        ],
        out_specs=pl.BlockSpec((tm, tn), lambda i, j: (i, j)),
        out_shape=jax.ShapeDtypeStruct((M, N), out_dtype),
        scratch_shapes=[pltpu.VMEM((tm, K), w.dtype)],
        compiler_params=_cparams(("parallel", "arbitrary")),
        name="norm_matmul",
    )(x, g.reshape(1, K), w)


def _mm_res_kernel(*refs):
    n = (len(refs) - 2) // 2
    r_ref, o_ref = refs[2 * n], refs[2 * n + 1]
    acc = r_ref[...]
    for y_ref, w_ref in zip(refs[:n], refs[n:2 * n]):
        acc = acc + jnp.dot(y_ref[...].astype(BF16), w_ref[...], preferred_element_type=F32)
    o_ref[...] = acc


def matmul_residual(ys, ws, r, *, tm, tn):
    M, N = r.shape
    y_specs = [pl.BlockSpec((tm, y.shape[1]), lambda i, j: (i, 0)) for y in ys]
    w_specs = [pl.BlockSpec((w.shape[0], tn), lambda i, j: (0, j)) for w in ws]
    return pl.pallas_call(
        _mm_res_kernel,
        grid=(M // tm, N // tn),
        in_specs=y_specs + w_specs + [pl.BlockSpec((tm, tn), lambda i, j: (i, j))],
        out_specs=pl.BlockSpec((tm, tn), lambda i, j: (i, j)),
        out_shape=jax.ShapeDtypeStruct((M, N), F32),
        compiler_params=_cparams(("parallel", "arbitrary")),
        name="matmul_residual",
    )(*ys, *ws, r)


def _ffn_kernel(*refs, n_cast):
    x_ref, g_ref, wg_ref, wu_ref, wd_ref = refs[:5]
    cast_in = refs[5:5 + n_cast]
    o_ref = refs[5 + n_cast]
    cast_out = refs[6 + n_cast:6 + 2 * n_cast]
    (xn_ref,) = refs[6 + 2 * n_cast:]
    f = pl.program_id(1)

    @pl.when(f == 0)
    def _():
        x = x_ref[...]
        ms = jnp.mean(x * x, axis=-1, keepdims=True)
        xn_ref[...] = (x * lax.rsqrt(ms + NORM_EPS) * g_ref[...]).astype(BF16)

    for src, dst in zip(cast_in, cast_out):
        dst[...] = src[...].astype(BF16)

    xn = xn_ref[...]
    a = jnp.dot(xn, wg_ref[...], preferred_element_type=F32)
    b = jnp.dot(xn, wu_ref[...], preferred_element_type=F32)
    h = (a * jax.nn.sigmoid(a) * b).astype(BF16)
    part = jnp.dot(h, wd_ref[...], preferred_element_type=F32)

    @pl.when(f == 0)
    def _():
        o_ref[...] = x_ref[...] + part

    @pl.when(f > 0)
    def _():
        o_ref[...] += part


def ffn_swiglu(x, g, wg, wu, wd, *, tm, tf, cast=()):
    M, D = x.shape
    F = wg.shape[1]
    nf = F // tf
    steps = (M // tm) * nf
    cast_specs, cast_shapes = [], []
    for w in cast:
        E, R, C = w.shape
        per = steps // E
        cast_specs.append(pl.BlockSpec((1, R // per, C), lambda i, f, per=per: ((i * nf + f) // per, (i * nf + f) % per, 0)))
        cast_shapes.append(jax.ShapeDtypeStruct(w.shape, BF16))
    outs = pl.pallas_call(
        functools.partial(_ffn_kernel, n_cast=len(cast)),
        grid=(M // tm, nf),
        in_specs=[
            pl.BlockSpec((tm, D), lambda i, f: (i, 0)),
            pl.BlockSpec((1, D), lambda i, f: (0, 0)),
            pl.BlockSpec((D, tf), lambda i, f: (0, f)),
            pl.BlockSpec((D, tf), lambda i, f: (0, f)),
            pl.BlockSpec((tf, D), lambda i, f: (f, 0)),
        ] + cast_specs,
        out_specs=[pl.BlockSpec((tm, D), lambda i, f: (i, 0))] + cast_specs,
        out_shape=[jax.ShapeDtypeStruct((M, D), F32)] + cast_shapes,
        scratch_shapes=[pltpu.VMEM((tm, D), BF16)],
        compiler_params=_cparams(("arbitrary", "arbitrary"), vmem=VMEM_LIMIT_BIG if cast else VMEM_LIMIT),
        name="ffn_swiglu",
    )(x, g.reshape(1, D), wg, wu, wd, *cast)
    return outs[0], tuple(outs[1:])


def _rwkv_prep_kernel(cur_ref, prev_ref, next_ref, n1_ref, win_ref, mu_ref, w0f_ref, wupf_ref, w0b_ref, wupb_ref,
                      a0f_ref, aupf_ref, a0b_ref, aupb_ref, gup_ref, kk_ref, hsum_ref,
                      r_ref, k_ref, v_ref, kkn_ref, g_ref, lwf_ref, lf_ref, asf_ref, lwb_ref, lb_ref, asb_ref):
    i = pl.program_id(1)
    nt = pl.num_programs(1)

    def project(xv):
        ms = jnp.mean(xv * xv, axis=-1, keepdims=True)
        xn = (xv * lax.rsqrt(ms + NORM_EPS) * n1_ref[...]).astype(BF16)
        return jnp.dot(xn, win_ref[...], preferred_element_type=F32)

    Tt = cur_ref.shape[1]
    pall = project(jnp.concatenate([cur_ref[0], prev_ref[0], next_ref[0]], axis=0))
    x = pall[:Tt]
    halo = pall[Tt:]
    row = _iota2(x.shape, 0)
    prev_row = jnp.where(i > 0, halo[7:8, :], 0.0)
    next_row = jnp.where(i < nt - 1, halo[8:9, :], 0.0)
    prev = jnp.where(row == 0, prev_row, pltpu.roll(x, 1, 0))
    nxt = jnp.where(row == Tt - 1, next_row, pltpu.roll(x, Tt - 1, 0))
    pm = x + (0.5 * (prev + nxt) - x) * mu_ref[...]
    o1 = 3 * RW_DIM
    o2 = o1 + DECAY_RANK
    o3 = o2 + ICLR_RANK
    r_ref[0] = pm[:, :RW_DIM]
    k = pm[:, RW_DIM:2 * RW_DIM]
    k_ref[0] = k
    v_ref[0] = pm[:, 2 * RW_DIM:o1]
    wd = jnp.tanh(pm[:, o1:o2])
    ad = pm[:, o2:o3]
    gin = jax.nn.sigmoid(pm[:, o3:])
    g_ref[0] = jnp.dot(gin.astype(BF16), gup_ref[...].astype(BF16), preferred_element_type=F32)
    kk = k * kk_ref[...]
    ss = _dot_sel(kk * kk, hsum_ref[...])
    kkn_ref[0] = kk / jnp.maximum(jnp.sqrt(ss), 1e-12)

    ci = _iota2((RW_CHUNK, RW_CHUNK), 0)
    cj = _iota2((RW_CHUNK, RW_CHUNK), 1)
    tril = (cj <= ci).astype(F32)
    triu = (cj >= ci).astype(F32)

    def direction(w0_ref, wup_ref, a0_ref, aup_ref, tri, lw_ref, l_ref, as_ref):
        z = w0_ref[...] + _dot3(wd, wup_ref[...])
        sp = jnp.maximum(-z, 0.0) + jnp.log(1.0 + jnp.exp(-jnp.abs(z)))
        lw = -jnp.exp(-sp - 0.5)
        lw_ref[0] = lw
        as_ref[0] = jax.nn.sigmoid(a0_ref[...] + _dot3(ad, aup_ref[...]))
        for c in range(Tt // RW_CHUNK):
            sl = slice(c * RW_CHUNK, (c + 1) * RW_CHUNK)
            l_ref[0, sl, :] = _sel_dot(tri, lw[sl, :])

    direction(w0f_ref, wupf_ref, a0f_ref, aupf_ref, tril, lwf_ref, lf_ref, asf_ref)
    direction(w0b_ref, wupb_ref, a0b_ref, aupb_ref, triu, lwb_ref, lb_ref, asb_ref)


def rwkv_prep(x, norm1, w_in, shift_mu, w0_f, w_up_f, w0_b, w_up_b, a0_f, a_up_f, a0_b, a_up_b, g_up, k_k, *, tt):
    B, S, D = x.shape
    nt = S // tt
    hsum = (jnp.arange(RW_DIM)[:, None] // HEAD_DIM == jnp.arange(RW_DIM)[None, :] // HEAD_DIM).astype(F32)
    row = lambda a: a.reshape(1, -1)
    full = lambda a: pl.BlockSpec(a.shape, lambda b, i: (0,) * a.ndim)
    params = [row(norm1), w_in, row(shift_mu), row(w0_f), w_up_f, row(w0_b), w_up_b, row(a0_f), a_up_f, row(a0_b), a_up_b,
              g_up, row(k_k), hsum]
    tb = tt // 8
    in_specs = [
        pl.BlockSpec((1, tt, D), lambda b, i: (b, i, 0)),
        pl.BlockSpec((1, 8, D), lambda b, i: (b, jnp.maximum(i * tb - 1, 0), 0)),
        pl.BlockSpec((1, 8, D), lambda b, i: (b, jnp.minimum((i + 1) * tb, S // 8 - 1), 0)),
    ] + [full(a) for a in params]
    out_spec = pl.BlockSpec((1, tt, RW_DIM), lambda b, i: (b, i, 0))
    out_sds = jax.ShapeDtypeStruct((B, S, RW_DIM), F32)
    return pl.pallas_call(
        _rwkv_prep_kernel,
        grid=(B, nt),
        in_specs=in_specs,
        out_specs=[out_spec] * 11,
        out_shape=[out_sds] * 11,
        compiler_params=_cparams(("parallel", "arbitrary")),
        name="rwkv_prep",
    )(x, x, x, *params)


def _bd(y, bd_mask):
    return jnp.where(bd_mask, jnp.concatenate([y, y], axis=0), jnp.zeros((), y.dtype))


def _chunk_terms(chains, out, bd_mask):
    C = chains[0]["r"].shape[0]
    n = len(chains)
    R = range(n)
    dot = lambda u, v: jnp.dot(u, v, preferred_element_type=F32)
    bd = lambda y: _bd(y, bd_mask)
    pre = []
    for ch in chains:
        r, k, v, kk, lw, L, asig, ka, rev = (ch[x] for x in ("r", "k", "v", "kk", "lw", "L", "asig", "ka", "rev"))
        a = -kk
        b = kk * asig
        kd = k * (1.0 + (asig - 1.0) * ka)
        Lp = L - lw
        Lr = Lp if rev else L
        Lend = L[0:1, :] if rev else L[C - 1:C, :]
        Lmid = L[C // 2:C // 2 + 1, :]
        einv = jnp.exp(Lmid - L)
        eend = jnp.exp(Lend - L)
        pre.append(dict(
            ar=jnp.concatenate([a * jnp.exp(Lp - Lmid), r * jnp.exp(Lr - Lmid)], axis=0).astype(BF16),
            bt=bd((b * einv).astype(BF16)), kt=bd((kd * einv).astype(BF16)),
            a0=bd((a * jnp.exp(Lp)).astype(BF16)), r0=r * jnp.exp(Lr),
            bh=(b * eend).astype(BF16), kh=(kd * eend).astype(BF16), vb=v.astype(BF16),
            dec=jnp.exp(Lend)))
    mk = [ch["masks"] for ch in chains]
    Ab = [_bdot_nt(pre[i]["ar"], pre[i]["bt"]) for i in R]
    Ak = [_bdot_nt(pre[i]["ar"], pre[i]["kt"]) for i in R]
    yield
    Aab = [jnp.where(mk[i][0], Ab[i][:C], 0.0) for i in R]
    Arb = [jnp.where(mk[i][1], Ab[i][C:], 0.0).astype(BF16) for i in R]
    AakArk = [jnp.concatenate([jnp.where(mk[i][0], Ak[i][:C], 0.0), jnp.where(mk[i][1], Ak[i][C:], 0.0)],
                              axis=0).astype(BF16) for i in R]
    AV = [dot(AakArk[i], bd(pre[i]["vb"])) for i in R]
    Xb = [jnp.where(mk[i][2], Aab[i], 0.0).astype(BF16) for i in R]
    X2b = [dot(Xb[i], bd(Xb[i])).astype(BF16) for i in R]
    yield
    T = [mk[i][6] + Xb[i].astype(F32) for i in R]
    T = [T[i] + dot(T[i].astype(BF16), bd(X2b[i])) for i in R]
    X4b = [dot(X2b[i], bd(X2b[i])).astype(BF16) for i in R]
    yield
    T = [T[i] + dot(T[i].astype(BF16), bd(X4b[i])) for i in R]
    yield
    for lvl in (3, 4, 5):
        Tb = [T[i].astype(BF16) for i in R]
        ET = [dot(jnp.where(mk[i][lvl], Aab[i], 0.0).astype(BF16), bd(Tb[i])).astype(BF16) for i in R]
        yield
        T = [T[i] + dot(Tb[i], bd(ET[i])) for i in R]
        yield
    Tb = [T[i].astype(BF16) for i in R]
    A0p = [dot(Tb[i], pre[i]["a0"]).astype(BF16) for i in R]
    Uv = [dot(Tb[i], bd(AV[i][:C].astype(BF16))).astype(BF16) for i in R]
    yield
    Rpp = [pre[i]["r0"] + dot(Arb[i], bd(A0p[i])) for i in R]
    Yv = [dot(Arb[i], bd(Uv[i])) + AV[i][C:] for i in R]
    eye2 = (_iota2(bd_mask.shape, 0) == _iota2(bd_mask.shape, 1)).astype(F32)
    P = [jnp.where(bd_mask, _bdot_tn(pre[i]["bh"], A0p[i]), 0.0) + eye2 * pre[i]["dec"] for i in R]
    Q = [jnp.where(bd_mask, _bdot_tn(pre[i]["bh"], Uv[i]) + _bdot_tn(pre[i]["kh"], pre[i]["vb"]), 0.0) for i in R]
    out.extend((Rpp[i], Yv[i], P[i], Q[i]) for i in R)


def _make_masks(C, N, rev):
    ii = _iota2((C, 2 * N), 0)
    jj = _iota2((C, 2 * N), 1) % N
    strict = (jj > ii) if rev else (jj < ii)
    rmask = strict if rev else (jj <= ii)
    blk = lambda s: (ii // s) == (jj // s)
    m8 = blk(8)
    e16 = blk(16) & jnp.logical_not(blk(8))
    e32 = blk(32) & jnp.logical_not(blk(16))
    e64 = jnp.logical_not(blk(32))
    eye = (ii == jj).astype(F32)
    return strict, rmask, m8, e16, e32, e64, eye


RW_UNROLL = 8


def _rwkv_scan_kernel(r_ref, k_ref, v_ref, kk_ref, g_ref, lwf_ref, lf_ref, asf_ref, lwb_ref, lb_ref, asb_ref,
                      ka_ref, rk_ref, gnw_ref, gnb_ref, havg_ref, o_ref,
                      yf_ref, yb_ref, h_ref, rpp_ref, yv_ref, p_ref, q_ref):
    S = r_ref.shape[1]
    C = RW_CHUNK
    N = HEAD_DIM
    W = 2 * N
    n = S // C
    U = RW_UNROLL
    nset = n // U
    masks_f = _make_masks(C, N, False)
    masks_b = _make_masks(C, N, True)
    bd_mask = (_iota2((W, W), 0) // N) == (_iota2((W, W), 1) // N)
    h_ref[...] = jnp.zeros_like(h_ref)
    dirs = ((False, lwf_ref, lf_ref, asf_ref, yf_ref, masks_f), (True, lwb_ref, lb_ref, asb_ref, yb_ref, masks_b))

    def chunk_rows(rev, st, u):
        c = st * U + u
        c = (n - 1 - c) if rev else c
        return pl.ds(pl.multiple_of(c * C, C), C)

    def state_step(st, u):
        slot = st % 2
        for di, (rev, lw_ref, l_ref, as_ref, y_ref, masks) in enumerate(dirs):
            Hb = h_ref[di].astype(BF16)
            y_ref[chunk_rows(rev, st, u), :] = (jnp.dot(rpp_ref[slot, di, u], Hb, preferred_element_type=F32)
                                                + yv_ref[slot, di, u])
            h_ref[di] = jnp.dot(p_ref[slot, di, u], Hb, preferred_element_type=F32) + q_ref[slot, di, u]

    def chunk_set(st, state_of):
        chains, slots = [], []
        for u in range(U):
            for di, (rev, lw_ref, l_ref, as_ref, y_ref, masks) in enumerate(dirs):
                rows = chunk_rows(rev, st, u)
                chains.append(dict(
                    r=r_ref[0, rows, :], k=k_ref[0, rows, :], v=v_ref[0, rows, :], kk=kk_ref[0, rows, :],
                    lw=lw_ref[0, rows, :], L=l_ref[0, rows, :], asig=as_ref[0, rows, :], ka=ka_ref[...],
                    rev=rev, masks=masks))
                slots.append((di, u))
        res = []
        pending = list(range(U)) if state_of is not None else []
        for stage, _ in enumerate(_chunk_terms(chains, res, bd_mask)):
            if pending and stage >= 1:
                state_step(state_of, pending.pop(0))
        for u in pending:
            state_step(state_of, u)
        slot = st % 2
        for (di, u), (Rpp, Yv, P, Q) in zip(slots, res):
            rpp_ref[slot, di, u] = Rpp.astype(BF16)
            yv_ref[slot, di, u] = Yv
            p_ref[slot, di, u] = P.astype(BF16)
            q_ref[slot, di, u] = Q

    chunk_set(0, None)

    def body(st, carry):
        chunk_set(st, st - 1)
        return carry

    lax.fori_loop(1, nset, body, 0)
    for u in range(U):
        state_step(nset - 1, u)

    RT = min(S, 1024)

    def fin(i, carry):
        rows = pl.ds(pl.multiple_of(i * RT, RT), RT)
        y = yf_ref[rows, :] + yb_ref[rows, :]
        mean = _dot_sel(y, havg_ref[...])
        d = y - mean
        var = _dot_sel(d * d, havg_ref[...])
        yn = d * lax.rsqrt(var + RWKV_GN_EPS) * gnw_ref[...] + gnb_ref[...]
        r = r_ref[0, rows, :]
        kf = k_ref[0, rows, :] * (1.0 + (asf_ref[0, rows, :] - 1.0) * ka_ref[...])
        bonus = _dot_sel(r * kf * rk_ref[...], havg_ref[...]) * float(N) * v_ref[0, rows, :]
        o_ref[0, rows, :] = ((yn + bonus) * g_ref[0, rows, :]).astype(o_ref.dtype)
        return carry

    lax.fori_loop(0, S // RT, fin, 0)


def rwkv_scan(r, k, v, kk, g, lw_f, l_f, as_f, lw_b, l_b, as_b, k_a, r_k, gn_w, gn_b, out_dtype=F32):
    B, S, _ = r.shape
    W = 2 * HEAD_DIM
    N = HEAD_DIM
    havg = (jnp.arange(W)[:, None] // HEAD_DIM == jnp.arange(W)[None, :] // HEAD_DIM).astype(F32) / HEAD_DIM
    seq = pl.BlockSpec((1, S, W), lambda b, h: (b, 0, h))
    par = pl.BlockSpec((1, W), lambda b, h: (0, h))
    row = lambda a: a.reshape(1, -1)
    U = RW_UNROLL
    return pl.pallas_call(
        _rwkv_scan_kernel,
        grid=(B, RW_DIM // W),
        in_specs=[seq] * 11 + [par] * 4 + [pl.BlockSpec((W, W), lambda b, h: (0, 0))],
        out_specs=seq,
        out_shape=jax.ShapeDtypeStruct((B, S, RW_DIM), out_dtype),
        scratch_shapes=[pltpu.VMEM((S, W), F32), pltpu.VMEM((S, W), F32), pltpu.VMEM((2, W, W), F32),
                        pltpu.VMEM((2, 2, U, RW_CHUNK, W), BF16), pltpu.VMEM((2, 2, U, RW_CHUNK, W), F32),
                        pltpu.VMEM((2, 2, U, W, W), BF16), pltpu.VMEM((2, 2, U, W, W), F32)],
        compiler_params=_cparams(("parallel", "parallel")),
        name="rwkv_scan",
    )(r, k, v, kk, g, lw_f, l_f, as_f, lw_b, l_b, as_b, row(k_a), row(r_k), row(gn_w), row(gn_b), havg)


def rwkv7_mixer(x, norm1, w_in, shift_mu, w0_f, w_up_f, w0_b, w_up_b, a0_f, a_up_f, a0_b, a_up_b, g_up, k_k, k_a,
                r_k, gn_w, gn_b, out_dtype=F32, tt=PREP_ROWS):
    outs = rwkv_prep(x, norm1, w_in, shift_mu, w0_f, w_up_f, w0_b, w_up_b, a0_f, a_up_f, a0_b, a_up_b, g_up, k_k, tt=tt)
    return rwkv_scan(*outs, k_a, r_k, gn_w, gn_b, out_dtype=out_dtype)


def _head_norm_rope(x, gain, cos, sin_signed, havg):
    ms = _dot_sel(x * x, havg)
    xn = x * lax.rsqrt(ms + NORM_EPS) * gain
    W = x.shape[1]
    even = (_iota2(x.shape, 1) % 2) == 0
    partner = jnp.where(even, pltpu.roll(xn, W - 1, 1), pltpu.roll(xn, 1, 1))
    return xn * cos + partner * sin_signed


def _attn_kernel(q_ref, k_ref, v_ref, cosq_ref, sinq_ref, cosk_ref, sink_ref, qg_ref, kg_ref, hq_ref, hk_ref,
                 o_ref, ks_ref, vt_ref):
    i = pl.program_id(1)
    G = ATT_HEADS // ATT_KV_HEADS
    D = HEAD_DIM

    @pl.when(i == 0)
    def _():
        kr = _head_norm_rope(k_ref[0], kg_ref[...], cosk_ref[...], sink_ref[...], hk_ref[...])
        vt = jnp.transpose(v_ref[0])
        ones = jnp.ones((ATT_VT_ROWS - D, vt.shape[1]), BF16)
        for kv in range(ATT_KV_HEADS):
            ks_ref[kv] = kr[:, kv * D:(kv + 1) * D].astype(BF16)
            vt_ref[kv] = jnp.concatenate([vt[kv * D:(kv + 1) * D, :].astype(BF16), ones], axis=0)

    q = _head_norm_rope(q_ref[0], qg_ref[...], cosq_ref[...], sinq_ref[...], hq_ref[...]) * (D ** -0.5 * LOG2_E)
    qb = q.astype(BF16)
    outs = []
    for h0 in range(0, ATT_HEADS, ATT_LOCKSTEP):
        hs = range(h0, h0 + ATT_LOCKSTEP)
        st = [lax.dot_general(ks_ref[h // G], qb[:, h * D:(h + 1) * D], (((1,), (1,)), ((), ())),
                              preferred_element_type=F32) for h in hs]
        m = [jnp.max(x, axis=0, keepdims=True) for x in st]
        p = [jnp.exp2(x - mm).astype(BF16) for x, mm in zip(st, m)]
        ot = [jnp.dot(vt_ref[h // G], x, preferred_element_type=F32) for h, x in zip(hs, p)]
        outs.extend(o[:D] / o[D:D + 1] for o in ot)
    pairs = [jnp.transpose(jnp.concatenate(outs[2 * j:2 * j + 2], axis=0)) for j in range(ATT_HEADS // 2)]
    o_ref[0] = jnp.concatenate(pairs, axis=1).astype(o_ref.dtype)


def _rope_tables(S):
    rows = S // GRID_W
    row = jnp.repeat(jnp.arange(rows), GRID_W).astype(F32)
    col = jnp.tile(jnp.arange(GRID_W), rows).astype(F32)
    half = HEAD_DIM // 2
    freq = ROPE_THETA ** (-jnp.arange(0, half, 2, dtype=F32) / half)
    ang = jnp.concatenate([row[:, None] * freq, col[:, None] * freq], axis=-1)
    cos = jnp.repeat(jnp.cos(ang), 2, axis=-1)
    sin = jnp.repeat(jnp.sin(ang), 2, axis=-1)
    sign = jnp.where(jnp.arange(HEAD_DIM) % 2 == 0, -1.0, 1.0).astype(F32)
    return cos, sin * sign


def gqa_attention(p, q_norm, k_norm, *, tq=ATT_Q_ROWS, out_dtype=F32):
    B, S, _ = p.shape
    cos, sin = _rope_tables(S)
    tile = lambda t, n: jnp.tile(t, (1, n))
    hq = (jnp.arange(ATT_DIM)[:, None] // HEAD_DIM == jnp.arange(ATT_DIM)[None, :] // HEAD_DIM).astype(F32) / HEAD_DIM
    hk = hq[:ATT_KV_DIM, :ATT_KV_DIM]
    qg = jnp.tile(q_norm, ATT_HEADS).reshape(1, ATT_DIM)
    kg = jnp.tile(k_norm, ATT_KV_HEADS).reshape(1, ATT_KV_DIM)
    nq = ATT_DIM // ATT_KV_DIM
    const = lambda a: pl.BlockSpec(a.shape, lambda b, i: (0, 0))
    return pl.pallas_call(
        _attn_kernel,
        grid=(B, S // tq),
        in_specs=[
            pl.BlockSpec((1, tq, ATT_DIM), lambda b, i: (b, i, 0)),
            pl.BlockSpec((1, S, ATT_KV_DIM), lambda b, i: (b, 0, nq)),
            pl.BlockSpec((1, S, ATT_KV_DIM), lambda b, i: (b, 0, nq + 1)),
            pl.BlockSpec((tq, ATT_DIM), lambda b, i: (i, 0)),
            pl.BlockSpec((tq, ATT_DIM), lambda b, i: (i, 0)),
            pl.BlockSpec((S, ATT_KV_DIM), lambda b, i: (0, 0)),
            pl.BlockSpec((S, ATT_KV_DIM), lambda b, i: (0, 0)),
            const(qg), const(kg), const(hq), const(hk),
        ],
        out_specs=pl.BlockSpec((1, tq, ATT_DIM), lambda b, i: (b, i, 0)),
        out_shape=jax.ShapeDtypeStruct((B, S, ATT_DIM), out_dtype),
        scratch_shapes=[pltpu.VMEM((ATT_KV_HEADS, S, HEAD_DIM), BF16), pltpu.VMEM((ATT_KV_HEADS, ATT_VT_ROWS, S), BF16)],
        compiler_params=_cparams(("parallel", "arbitrary")),
        name="gqa_attention",
    )(p, p, p, tile(cos, ATT_HEADS), tile(sin, ATT_HEADS), tile(cos, ATT_KV_HEADS), tile(sin, ATT_KV_HEADS),
      qg, kg, hq, hk)


GLA_UNROLL = 16


def _gla_kernel(q_ref, k_ref, v_ref, gd_ref, og_ref, upf_ref, bf_ref, upb_ref, bb_ref, on_ref, o_ref,
                yf_ref, yb_ref, st_ref, kv_ref, qe_ref, dec_ref):
    S = q_ref.shape[1]
    C = GLA_CHUNK
    n = S // C
    U = GLA_UNROLL
    nset = n // U
    ii = _iota2((C, C), 0)
    jj = _iota2((C, C), 1)
    dirs = (
        (False, (jj <= ii).astype(F32), jj <= ii, upf_ref, bf_ref, yf_ref),
        (True, (jj >= ii).astype(F32), jj > ii, upb_ref, bb_ref, yb_ref),
    )
    scale = GLA_DKH ** -0.5
    st_ref[...] = jnp.zeros_like(st_ref)

    def chunk_rows(rev, st, u):
        c = st * U + u
        c = (n - 1 - c) if rev else c
        return pl.ds(pl.multiple_of(c * C, C), C)

    def state_step(st, u):
        slot = st % 2
        for di, (rev, tri, mask, up_ref, b_ref, y_ref) in enumerate(dirs):
            state = st_ref[di]
            rows = chunk_rows(rev, st, u)
            y_ref[rows, :] += lax.dot_general(qe_ref[slot, di, u], state.astype(BF16), (((1,), (1,)), ((), ())),
                                              preferred_element_type=F32)
            st_ref[di] = state * dec_ref[slot, di, u] + kv_ref[slot, di, u]

    def chunk_set(st, state_of):
        pending = list(range(U)) if state_of is not None else []
        items = []
        for u in range(U):
            for di, (rev, tri, mask, up_ref, b_ref, y_ref) in enumerate(dirs):
                rows = chunk_rows(rev, st, u)
                z = _dot3(gd_ref[0, rows, :], up_ref[...]) + b_ref[...]
                g = (jnp.minimum(z, 0.0) - jnp.log(1.0 + jnp.exp(-jnp.abs(z)))) * (1.0 / GLA_GATE_NORM)
                items.append(dict(u=u, di=di, rows=rows, rev=rev, mask=mask, y_ref=y_ref, g=g, tri=tri))
        for x in items:
            x["b"] = _sel_dot(x["tri"], x["g"])
        if pending:
            state_step(state_of, pending.pop(0))
        for x in items:
            b = x["b"]
            q = q_ref[0, x["rows"], :] * scale
            k = k_ref[0, x["rows"], :]
            b_mid = b[C // 2:C // 2 + 1, :]
            b_last = b[0:1, :] if x["rev"] else b[C - 1:C, :]
            x["vb"] = v_ref[0, x["rows"], :].astype(BF16)
            x["qm"] = (q * jnp.exp(b - b_mid)).astype(BF16)
            x["km"] = (k * jnp.exp(b_mid - b)).astype(BF16)
            x["ke"] = (k * jnp.exp(b_last - b)).astype(BF16)
            x["qe"] = (q * jnp.exp(b)).astype(BF16)
            x["dec"] = jnp.exp(b_last)
        att = [lax.dot_general(x["qm"], x["km"], (((1,), (1,)), ((), ())), preferred_element_type=F32) for x in items]
        if pending:
            state_step(state_of, pending.pop(0))
        att = [jnp.where(x["mask"], a, 0.0).astype(BF16) for x, a in zip(items, att)]
        slot = st % 2
        for idx, (x, a) in enumerate(zip(items, att)):
            x["y_ref"][x["rows"], :] = jnp.dot(a, x["vb"], preferred_element_type=F32)
            kv_ref[slot, x["di"], x["u"]] = lax.dot_general(x["vb"], x["ke"], (((0,), (0,)), ((), ())),
                                                            preferred_element_type=F32)
            qe_ref[slot, x["di"], x["u"]] = x["qe"]
            dec_ref[slot, x["di"], x["u"]] = x["dec"]
            if pending and idx % 2 == 1:
                state_step(state_of, pending.pop(0))
        for u in pending:
            state_step(state_of, u)

    chunk_set(0, None)

    def body(st, carry):
        chunk_set(st, st - 1)
        return carry

    lax.fori_loop(1, nset, body, 0)
    for u in range(U):
        state_step(nset - 1, u)

    RT = min(S, 1024)

    def fin(i, carry):
        rows = pl.ds(pl.multiple_of(i * RT, RT), RT)
        o = yf_ref[rows, :] + yb_ref[rows, :]
        ms = jnp.mean(o * o, axis=-1, keepdims=True)
        on = o * lax.rsqrt(ms + NORM_EPS) * on_ref[...]
        og = og_ref[0, rows, :]
        o_ref[0, rows, :] = (on * (og * jax.nn.sigmoid(og))).astype(o_ref.dtype)
        return carry

    lax.fori_loop(0, S // RT, fin, 0)


def gla_mixer(pm, gd, gate_up_f, gate_bias_f, gate_up_b, gate_bias_b, out_norm, out_dtype=F32):
    B, S, _ = pm.shape
    U = GLA_UNROLL
    H = GLA_HEADS
    kb = GLA_DK // GLA_DKH
    vb0 = 2 * GLA_DK // GLA_DVH
    ob0 = vb0 + GLA_DV // GLA_DVH
    return pl.pallas_call(
        _gla_kernel,
        grid=(B, H),
        in_specs=[
            pl.BlockSpec((1, S, GLA_DKH), lambda b, h: (b, 0, h)),
            pl.BlockSpec((1, S, GLA_DKH), lambda b, h: (b, 0, kb + h)),
            pl.BlockSpec((1, S, GLA_DVH), lambda b, h: (b, 0, vb0 + h)),
            pl.BlockSpec((1, S, GLA_GATE_RANK), lambda b, h: (b, 0, 0)),
            pl.BlockSpec((1, S, GLA_DVH), lambda b, h: (b, 0, ob0 + h)),
            pl.BlockSpec((GLA_GATE_RANK, GLA_DKH), lambda b, h: (0, h)),
            pl.BlockSpec((1, GLA_DKH), lambda b, h: (0, h)),
            pl.BlockSpec((GLA_GATE_RANK, GLA_DKH), lambda b, h: (0, h)),
            pl.BlockSpec((1, GLA_DKH), lambda b, h: (0, h)),
            pl.BlockSpec((1, GLA_DVH), lambda b, h: (0, 0)),
        ],
        out_specs=pl.BlockSpec((1, S, GLA_DVH), lambda b, h: (b, 0, h)),
        out_shape=jax.ShapeDtypeStruct((B, S, GLA_DV), out_dtype),
        scratch_shapes=[pltpu.VMEM((S, GLA_DVH), F32), pltpu.VMEM((S, GLA_DVH), F32),
                        pltpu.VMEM((2, GLA_DVH, GLA_DKH), F32),
                        pltpu.VMEM((2, 2, U, GLA_DVH, GLA_DKH), F32), pltpu.VMEM((2, 2, U, GLA_CHUNK, GLA_DKH), BF16),
                        pltpu.VMEM((2, 2, U, 1, GLA_DKH), F32)],
        compiler_params=_cparams(("parallel", "parallel")),
        name="gla_mixer",
    )(pm, pm, pm, gd, pm, gate_up_f, gate_bias_f.reshape(1, -1), gate_up_b, gate_bias_b.reshape(1, -1),
      out_norm.reshape(1, -1))


def _router_kernel(x_ref, g_ref, wr_ref, hn_ref, idx_ref, gate_ref):
    x = x_ref[...]
    ms = jnp.mean(x * x, axis=-1, keepdims=True)
    hn = x * lax.rsqrt(ms + NORM_EPS) * g_ref[...]
    hn_ref[...] = hn
    logits = _dot3(hn, wr_ref[...])
    lane = _iota2(logits.shape, 1)
    neg = jnp.float32(-jnp.inf)
    logits = jnp.where(lane < N_EXPERTS, logits, neg)
    m1 = jnp.max(logits, axis=-1, keepdims=True)
    i1 = jnp.min(jnp.where(logits == m1, lane, LANES), axis=-1, keepdims=True)
    rest = jnp.where(lane == i1, neg, logits)
    m2 = jnp.max(rest, axis=-1, keepdims=True)
    i2 = jnp.min(jnp.where(rest == m2, lane, LANES), axis=-1, keepdims=True)
    e2 = jnp.exp(m2 - m1)
    g1 = 1.0 / (1.0 + e2)
    g2 = e2 / (1.0 + e2)
    idx_ref[...] = jnp.where(lane == 0, i1, jnp.where(lane == 1, i2, 0))
    gate_ref[...] = jnp.where(lane == 0, g1, jnp.where(lane == 1, g2, 0.0))


def moe_router(x, g, router, *, tm):
    T, D = x.shape
    wr = jnp.zeros((D, LANES), F32).at[:, :N_EXPERTS].set(router)
    return pl.pallas_call(
        _router_kernel,
        grid=(T // tm,),
        in_specs=[
            pl.BlockSpec((tm, D), lambda i: (i, 0)),
            pl.BlockSpec((1, D), lambda i: (0, 0)),
            pl.BlockSpec((D, LANES), lambda i: (0, 0)),
        ],
        out_specs=[
            pl.BlockSpec((tm, D), lambda i: (i, 0)),
            pl.BlockSpec((tm, LANES), lambda i: (i, 0)),
            pl.BlockSpec((tm, LANES), lambda i: (i, 0)),
        ],
        out_shape=[jax.ShapeDtypeStruct((T, D), F32), jax.ShapeDtypeStruct((T, LANES), jnp.int32),
                   jax.ShapeDtypeStruct((T, LANES), F32)],
        compiler_params=_cparams(("parallel",)),
        name="moe_router",
    )(x, g.reshape(1, D), wr)


def _expert_kernel(te_ref, tv_ref, tok_ref, hn_hbm, wg_ref, wu_ref, wd_ref, o_ref, xg_ref, xb_ref, acc_ref, sem, *, nf):
    i = pl.program_id(0)
    f = pl.program_id(1)
    tm = xb_ref.shape[0]
    valid = tv_ref[i] > 0
    prev_valid = tv_ref[jnp.maximum(i - 1, 0)] > 0
    slot = i % 2

    def issue_row(tile, dst_slot, r):
        tok = tok_ref[tile * tm + r]
        pltpu.make_async_copy(hn_hbm.at[pl.ds(tok, 1), :], xg_ref.at[dst_slot, pl.ds(r, 1), :], sem.at[dst_slot]).start()

    @pl.when(jnp.logical_and(valid, jnp.logical_and(i == 0, f == 0)))
    def _():
        def body(r, c):
            issue_row(0, 0, r)
            return c

        lax.fori_loop(0, tm, body, 0, unroll=8)

    @pl.when(jnp.logical_and(f == 0, jnp.logical_or(valid, jnp.logical_and(i > 0, prev_valid))))
    def _():
        pltpu.make_async_copy(hn_hbm.at[pl.ds(0, tm), :], xg_ref.at[slot], sem.at[slot]).wait()

    @pl.when(jnp.logical_and(valid, f == 0))
    def _():
        xb_ref[...] = xg_ref[slot].astype(BF16)

    @pl.when(valid)
    def _():
        per = tm // nf
        row0 = f * per
        xb = xb_ref[...]
        a = jnp.dot(xb, wg_ref[0], preferred_element_type=F32)
        for j in range(per // 2):
            issue_row(i + 1, 1 - slot, row0 + j)
        b = jnp.dot(xb, wu_ref[0], preferred_element_type=F32)
        for j in range(per // 2, per):
            issue_row(i + 1, 1 - slot, row0 + j)
        h = (a * jax.nn.sigmoid(a) * b).astype(BF16)
        part = jnp.dot(h, wd_ref[0], preferred_element_type=F32)

        @pl.when(f == 0)
        def _():
            acc_ref[...] = part

        @pl.when(f > 0)
        def _():
            acc_ref[...] += part

    @pl.when(f == nf - 1)
    def _():
        o_ref[...] = jnp.where(valid, acc_ref[...], 0.0)


def moe_experts(hn, tile_expert, tile_valid, row_tok, wg, wu, wd, *, tm, tf):
    T, D = hn.shape
    F = wg.shape[2]
    P = row_tok.shape[0] - tm
    nt = P // tm + 1
    nf = F // tf

    def w_in_map(i, f, te, tv, tok):
        return (te[i], 0, jnp.where(tv[i] > 0, f, nf - 1))

    def w_out_map(i, f, te, tv, tok):
        return (te[i], jnp.where(tv[i] > 0, f, nf - 1), 0)

    grid_spec = pltpu.PrefetchScalarGridSpec(
        num_scalar_prefetch=3,
        grid=(nt, nf),
        in_specs=[
            pl.BlockSpec(memory_space=pl.ANY),
            pl.BlockSpec((1, D, tf), w_in_map),
            pl.BlockSpec((1, D, tf), w_in_map),
            pl.BlockSpec((1, tf, D), w_out_map),
        ],
        out_specs=pl.BlockSpec((tm, D), lambda i, f, te, tv, tok: (i, 0)),
        scratch_shapes=[pltpu.VMEM((2, tm, D), F32), pltpu.VMEM((tm, D), BF16), pltpu.VMEM((tm, D), F32),
                        pltpu.SemaphoreType.DMA((2,))],
    )
    return pl.pallas_call(
        functools.partial(_expert_kernel, nf=nf),
        grid_spec=grid_spec,
        out_shape=jax.ShapeDtypeStruct((P + tm, D), F32),
        compiler_params=_cparams(("arbitrary", "arbitrary")),
        name="moe_experts",
    )(tile_expert, tile_valid, row_tok, hn, wg, wu, wd)


def _combine_kernel(dest_ref, x_ref, gate_ref, yb_hbm, o_ref, buf_ref, sem):
    i = pl.program_id(0)
    n = pl.num_programs(0)
    tc = x_ref.shape[0]
    slot = i % 2

    def issue(step, dst_slot):
        base = step * tc * TOP_K

        for r in range(tc):
            for k in range(TOP_K):
                pltpu.make_async_copy(yb_hbm.at[pl.ds(dest_ref[base + r * TOP_K + k], 1), :],
                                      buf_ref.at[dst_slot, k, pl.ds(r, 1), :], sem.at[dst_slot]).start()

    @pl.when(i == 0)
    def _():
        issue(0, 0)

    for k in range(TOP_K):
        pltpu.make_async_copy(yb_hbm.at[pl.ds(0, tc), :], buf_ref.at[slot, k], sem.at[slot]).wait()

    @pl.when(i + 1 < n)
    def _():
        issue(i + 1, 1 - slot)

    g = gate_ref[...]
    o_ref[...] = x_ref[...] + g[:, 0:1] * buf_ref[slot, 0] + g[:, 1:2] * buf_ref[slot, 1]


def moe_combine(x, gates, yb, dest, *, tc):
    T, D = x.shape
    grid_spec = pltpu.PrefetchScalarGridSpec(
        num_scalar_prefetch=1,
        grid=(T // tc,),
        in_specs=[
            pl.BlockSpec((tc, D), lambda i, d: (i, 0)),
            pl.BlockSpec((tc, LANES), lambda i, d: (i, 0)),
            pl.BlockSpec(memory_space=pl.ANY),
        ],
        out_specs=pl.BlockSpec((tc, D), lambda i, d: (i, 0)),
        scratch_shapes=[pltpu.VMEM((2, TOP_K, tc, D), F32), pltpu.SemaphoreType.DMA((2,))],
    )
    return pl.pallas_call(
        _combine_kernel,
        grid_spec=grid_spec,
        out_shape=jax.ShapeDtypeStruct((T, D), F32),
        compiler_params=_cparams(("arbitrary",)),
        name="moe_combine",
    )(dest, x, gates, yb)


def moe_dispatch_plan(idx, *, tm):
    T = idx.shape[0]
    A = T * TOP_K
    e_flat = idx[:, :TOP_K].reshape(A)
    onehot = (e_flat[:, None] == jnp.arange(N_EXPERTS, dtype=jnp.int32)[None, :]).astype(jnp.int32)
    rank = jnp.sum((jnp.cumsum(onehot, axis=0) - onehot) * onehot, axis=1)
    counts = jnp.sum(onehot, axis=0)
    padded = (counts + tm - 1) // tm * tm
    ends = jnp.cumsum(padded)
    pstart = ends - padded
    dest = pstart[e_flat] + rank
    P = (A // tm + N_EXPERTS + 1) * tm
    nt = P // tm
    tok_flat = jnp.arange(A, dtype=jnp.int32) // TOP_K
    row_tok = jnp.zeros((P,), jnp.int32).at[dest].set(tok_flat)
    tile_start = jnp.arange(nt, dtype=jnp.int32) * tm
    tile_expert = jnp.minimum(jnp.sum((tile_start[:, None] >= ends[None, :]).astype(jnp.int32), axis=1), N_EXPERTS - 1)
    tile_valid = (tile_start < ends[-1]).astype(jnp.int32)
    last_valid = jnp.maximum(jnp.sum(tile_valid) - 1, 0)
    tile_expert = jnp.where(tile_valid > 0, tile_expert, tile_expert[last_valid])
    return dest.astype(jnp.int32), row_tok, tile_expert, tile_valid


def moe_layer(x, norm2, router, wg, wu, wd, *, tm=MOE_ROWS, tf=MOE_HIDDEN, tr=ROUTER_ROWS, tc=COMBINE_ROWS):
    hn, idx, gates = moe_router(x, norm2, router, tm=tr)
    dest, row_tok, tile_expert, tile_valid = moe_dispatch_plan(idx, tm=tm)
    yb = moe_experts(hn, tile_expert, tile_valid, row_tok, wg, wu, wd, tm=tm, tf=tf)
    return moe_combine(x, gates, yb, dest, tc=tc)


def _even_layer(x, norm1, w_in, shift_mu, w0_f, w_up_f, w0_b, w_up_b, a0_f, a_up_f, a0_b, a_up_b, g_up,
                k_k, k_a, r_k, gn_w, gn_b, q_norm, k_norm, w_out, norm2, ffn_gate, ffn_up, ffn_down, cast):
    Bn, S, D = x.shape
    T = Bn * S
    xf = x.reshape(T, D)
    w_in = w_in.astype(BF16)
    p_att = norm_matmul(xf, norm1, w_in[:, RW_IN:], tm=IN_PROJ_ROWS, tn=ATT_IN).reshape(Bn, S, ATT_IN)
    y_a = rwkv7_mixer(x, norm1, w_in[:, :RW_IN], shift_mu, w0_f, w_up_f, w0_b, w_up_b, a0_f, a_up_f, a0_b, a_up_b,
                      g_up, k_k, k_a, r_k, gn_w, gn_b, out_dtype=BF16)
    y_b = gqa_attention(p_att, q_norm, k_norm, out_dtype=BF16)
    w_out = w_out.astype(BF16)
    xf = matmul_residual([y_a.reshape(T, RW_DIM), y_b.reshape(T, ATT_DIM)], [w_out[:RW_DIM], w_out[RW_DIM:]], xf,
                         tm=OUT_PROJ_ROWS, tn=D)
    return ffn_swiglu(xf, norm2, ffn_gate.astype(BF16), ffn_up.astype(BF16), ffn_down.astype(BF16), tm=FFN_ROWS,
                      tf=FFN_HIDDEN, cast=cast)


def _odd_layer(xf, Bn, S, norm1, w_in, gate_up_f, gate_bias_f, gate_up_b, gate_bias_b, out_norm, w_out,
               norm2, router, exp_gate, exp_up, exp_down):
    T, D = xf.shape
    o3 = 2 * GLA_DK + GLA_DV
    o4 = o3 + GLA_GATE_RANK
    w_main = jnp.concatenate([w_in[:, :o3], w_in[:, o4:]], axis=1).astype(BF16)
    w_gd = jnp.zeros((D, LANES), F32).at[:, :GLA_GATE_RANK].set(w_in[:, o3:o4])
    p_main = norm_matmul(xf, norm1, w_main, tm=IN_PROJ_ROWS, tn=D).reshape(Bn, S, -1)
    gd = norm_matmul(xf, norm1, w_gd, tm=OUT_PROJ_ROWS, tn=LANES)[:, :GLA_GATE_RANK].reshape(Bn, S, GLA_GATE_RANK)
    o = gla_mixer(p_main, gd, gate_up_f, gate_bias_f, gate_up_b, gate_bias_b, out_norm, out_dtype=BF16)
    xf = matmul_residual([o.reshape(T, GLA_DV)], [w_out.astype(BF16)], xf, tm=OUT_PROJ_ROWS, tn=D)
    return moe_layer(xf, norm2, router, exp_gate, exp_up, exp_down)


def kernel(x, e_norm1, e_w_in, e_shift_mu, e_w0_f, e_w_up_f, e_w0_b, e_w_up_b, e_a0_f, e_a_up_f, e_a0_b, e_a_up_b, e_g_up, e_k_k, e_k_a, e_r_k, e_gn_w, e_gn_b, e_q_norm, e_k_norm, e_w_out, e_norm2, e_ffn_gate, e_ffn_up, e_ffn_down, o_norm1, o_w_in, o_gate_up_f, o_gate_bias_f, o_gate_up_b, o_gate_bias_b, o_out_norm, o_w_out, o_norm2, o_router, o_exp_gate, o_exp_up, o_exp_down):
    Bn, S, D = x.shape
    xf, (exp_gate, exp_up, exp_down) = _even_layer(
        x, e_norm1[0], e_w_in[0], e_shift_mu[0], e_w0_f[0], e_w_up_f[0], e_w0_b[0],
        e_w_up_b[0], e_a0_f[0], e_a_up_f[0], e_a0_b[0], e_a_up_b[0], e_g_up[0], e_k_k[0],
        e_k_a[0], e_r_k[0], e_gn_w[0], e_gn_b[0], e_q_norm[0], e_k_norm[0], e_w_out[0],
        e_norm2[0], e_ffn_gate[0], e_ffn_up[0], e_ffn_down[0], (o_exp_gate[0], o_exp_up[0], o_exp_down[0]))
    xf = _odd_layer(xf, Bn, S, o_norm1[0], o_w_in[0], o_gate_up_f[0], o_gate_bias_f[0], o_gate_up_b[0],
                    o_gate_bias_b[0], o_out_norm[0], o_w_out[0], o_norm2[0], o_router[0],
                    exp_gate, exp_up, exp_down)
    return xf.reshape(Bn, S, D)
```

```python
import functools

import jax
import jax.numpy as jnp
from jax import lax
from jax.experimental import pallas as pl
from jax.experimental.pallas import tpu as pltpu

F32 = jnp.float32
BF16 = jnp.bfloat16
HI = lax.Precision.HIGHEST

D_MODEL = 1024
GRID_W = 64
HEAD_DIM = 64
NORM_EPS = 1e-6
RW_HEADS = 8
RW_DIM = 512
DECAY_RANK = 64
ICLR_RANK = 64
GATE_RANK = 128
RWKV_GN_EPS = 64e-5
RW_IN = 3 * RW_DIM + DECAY_RANK + ICLR_RANK + GATE_RANK
RW_CHUNK = 64
ATT_HEADS = 8
ATT_KV_HEADS = 2
ATT_DIM = 512
ATT_KV_DIM = 128
ATT_IN = ATT_DIM + 2 * ATT_KV_DIM
ROPE_THETA = 10000.0
ATT_VT_ROWS = HEAD_DIM + 16
LOG2_E = 1.4426950408889634
ATT_LOCKSTEP = 2
GLA_HEADS = 4
GLA_DK = 512
GLA_DV = 1024
GLA_DKH = 128
GLA_DVH = 256
GLA_GATE_RANK = 16
GLA_GATE_NORM = 16.0
GLA_CHUNK = 64
N_EXPERTS = 8
TOP_K = 2
LANES = 128

IN_PROJ_ROWS = 2048
OUT_PROJ_ROWS = 1024
FFN_ROWS, FFN_HIDDEN = 512, 1408
PREP_ROWS = 512
ATT_Q_ROWS = 1024
MOE_ROWS, MOE_HIDDEN = 512, 1792
ROUTER_ROWS = 1024
COMBINE_ROWS = 1024

VMEM_LIMIT = 48 * 1024 * 1024
VMEM_LIMIT_BIG = 56 * 1024 * 1024


def _cparams(sem, vmem=VMEM_LIMIT):
    return pltpu.CompilerParams(dimension_semantics=sem, vmem_limit_bytes=vmem)


def _iota2(shape, dim):
    return lax.broadcasted_iota(jnp.int32, shape, dim)


def _split(x, parts):
    out = []
    for _ in range(parts):
        t = x.astype(BF16)
        out.append(t)
        x = x - t.astype(F32)
    return out


def _dot_sel(x, m, parts=2):
    mb = m.astype(BF16)
    acc = None
    for t in _split(x, parts):
        d = jnp.dot(t, mb, preferred_element_type=F32)
        acc = d if acc is None else acc + d
    return acc


def _sel_dot(m, x, parts=3):
    mb = m.astype(BF16)
    acc = None
    for t in _split(x, parts):
        d = jnp.dot(mb, t, preferred_element_type=F32)
        acc = d if acc is None else acc + d
    return acc


def _dot3(a, b):
    ah, al = _split(a, 2)
    bh, bl = _split(b, 2)
    d = lambda u, v: jnp.dot(u, v, preferred_element_type=F32)
    return d(ah, bh) + (d(ah, bl) + d(al, bh))


def _bdot(a, b):
    return jnp.dot(a.astype(BF16), b.astype(BF16), preferred_element_type=F32)


def _bdot_nt(a, b):
    return lax.dot_general(a.astype(BF16), b.astype(BF16), (((1,), (1,)), ((), ())), preferred_element_type=F32)


def _bdot_tn(a, b):
    return lax.dot_general(a.astype(BF16), b.astype(BF16), (((0,), (0,)), ((), ())), preferred_element_type=F32)


def _norm_mm_kernel(x_ref, g_ref, w_ref, o_ref, xn_ref):
    @pl.when(pl.program_id(1) == 0)
    def _():
        x = x_ref[...]
        ms = jnp.mean(x * x, axis=-1, keepdims=True)
        xn_ref[...] = (x * lax.rsqrt(ms + NORM_EPS) * g_ref[...]).astype(xn_ref.dtype)

    if xn_ref.dtype == F32:
        o_ref[...] = _dot3(xn_ref[...], w_ref[...]).astype(o_ref.dtype)
    else:
        o_ref[...] = jnp.dot(xn_ref[...], w_ref[...], preferred_element_type=F32).astype(o_ref.dtype)


def norm_matmul(x, g, w, *, tm, tn, out_dtype=F32):
    M, K = x.shape
    N = w.shape[1]
    return pl.pallas_call(
        _norm_mm_kernel,
        grid=(M // tm, N // tn),
        in_specs=[
            pl.BlockSpec((tm, K), lambda i, j: (i, 0)),
            pl.BlockSpec((1, K), lambda i, j: (0, 0)),
            pl.BlockSpec((K, tn), lambda i, j: (0, j)),
        ],
        out_specs=pl.BlockSpec((tm, tn), lambda i, j: (i, j)),
        out_shape=jax.ShapeDtypeStruct((M, N), out_dtype),
        scratch_shapes=[pltpu.VMEM((tm, K), w.dtype)],
        compiler_params=_cparams(("parallel", "arbitrary")),
        name="norm_matmul",
    )(x, g.reshape(1, K), w)


def _mm_res_kernel(*refs):
    n = (len(refs) - 2) // 2
    r_ref, o_ref = refs[2 * n], refs[2 * n + 1]
    acc = r_ref[...]
    for y_ref, w_ref in zip(refs[:n], refs[n:2 * n]):
        acc = acc + jnp.dot(y_ref[...].astype(BF16), w_ref[...], preferred_element_type=F32)
    o_ref[...] = acc


def matmul_residual(ys, ws, r, *, tm, tn):
    M, N = r.shape
    y_specs = [pl.BlockSpec((tm, y.shape[1]), lambda i, j: (i, 0)) for y in ys]
    w_specs = [pl.BlockSpec((w.shape[0], tn), lambda i, j: (0, j)) for w in ws]
    return pl.pallas_call(
        _mm_res_kernel,
        grid=(M // tm, N // tn),
        in_specs=y_specs + w_specs + [pl.BlockSpec((tm, tn), lambda i, j: (i, j))],
        out_specs=pl.BlockSpec((tm, tn), lambda i, j: (i, j)),
        out_shape=jax.ShapeDtypeStruct((M, N), F32),
        compiler_params=_cparams(("parallel", "arbitrary")),
        name="matmul_residual",
    )(*ys, *ws, r)


def _ffn_kernel(*refs, n_cast):
    x_ref, g_ref, wg_ref, wu_ref, wd_ref = refs[:5]
    cast_in = refs[5:5 + n_cast]
    o_ref = refs[5 + n_cast]
    cast_out = refs[6 + n_cast:6 + 2 * n_cast]
    (xn_ref,) = refs[6 + 2 * n_cast:]
    f = pl.program_id(1)

    @pl.when(f == 0)
    def _():
        x = x_ref[...]
        ms = jnp.mean(x * x, axis=-1, keepdims=True)
        xn_ref[...] = (x * lax.rsqrt(ms + NORM_EPS) * g_ref[...]).astype(BF16)

    for src, dst in zip(cast_in, cast_out):
        dst[...] = src[...].astype(BF16)

    xn = xn_ref[...]
    a = jnp.dot(xn, wg_ref[...], preferred_element_type=F32)
    b = jnp.dot(xn, wu_ref[...], preferred_element_type=F32)
    h = (a * jax.nn.sigmoid(a) * b).astype(BF16)
    part = jnp.dot(h, wd_ref[...], preferred_element_type=F32)

    @pl.when(f == 0)
    def _():
        o_ref[...] = x_ref[...] + part

    @pl.when(f > 0)
    def _():
        o_ref[...] += part


def ffn_swiglu(x, g, wg, wu, wd, *, tm, tf, cast=()):
    M, D = x.shape
    F = wg.shape[1]
    nf = F // tf
    steps = (M // tm) * nf
    cast_specs, cast_shapes = [], []
    for w in cast:
        E, R, C = w.shape
        per = steps // E
        cast_specs.append(pl.BlockSpec((1, R // per, C), lambda i, f, per=per: ((i * nf + f) // per, (i * nf + f) % per, 0)))
        cast_shapes.append(jax.ShapeDtypeStruct(w.shape, BF16))
    outs = pl.pallas_call(
        functools.partial(_ffn_kernel, n_cast=len(cast)),
        grid=(M // tm, nf),
        in_specs=[
            pl.BlockSpec((tm, D), lambda i, f: (i, 0)),
            pl.BlockSpec((1, D), lambda i, f: (0, 0)),
            pl.BlockSpec((D, tf), lambda i, f: (0, f)),
            pl.BlockSpec((D, tf), lambda i, f: (0, f)),
            pl.BlockSpec((tf, D), lambda i, f: (f, 0)),
        ] + cast_specs,
        out_specs=[pl.BlockSpec((tm, D), lambda i, f: (i, 0))] + cast_specs,
        out_shape=[jax.ShapeDtypeStruct((M, D), F32)] + cast_shapes,
        scratch_shapes=[pltpu.VMEM((tm, D), BF16)],
        compiler_params=_cparams(("arbitrary", "arbitrary"), vmem=VMEM_LIMIT_BIG if cast else VMEM_LIMIT),
        name="ffn_swiglu",
    )(x, g.reshape(1, D), wg, wu, wd, *cast)
    return outs[0], tuple(outs[1:])


def _rwkv_prep_kernel(cur_ref, prev_ref, next_ref, n1_ref, win_ref, mu_ref, w0f_ref, wupf_ref, w0b_ref, wupb_ref,
                      a0f_ref, aupf_ref, a0b_ref, aupb_ref, gup_ref, kk_ref, hsum_ref,
                      r_ref, k_ref, v_ref, kkn_ref, g_ref, lwf_ref, lf_ref, asf_ref, lwb_ref, lb_ref, asb_ref):
    i = pl.program_id(1)
    nt = pl.num_programs(1)

    def project(xv):
        ms = jnp.mean(xv * xv, axis=-1, keepdims=True)
        xn = (xv * lax.rsqrt(ms + NORM_EPS) * n1_ref[...]).astype(BF16)
        return jnp.dot(xn, win_ref[...], preferred_element_type=F32)

    Tt = cur_ref.shape[1]
    pall = project(jnp.concatenate([cur_ref[0], prev_ref[0], next_ref[0]], axis=0))
    x = pall[:Tt]
    halo = pall[Tt:]
    row = _iota2(x.shape, 0)
    prev_row = jnp.where(i > 0, halo[7:8, :], 0.0)
    next_row = jnp.where(i < nt - 1, halo[8:9, :], 0.0)
    prev = jnp.where(row == 0, prev_row, pltpu.roll(x, 1, 0))
    nxt = jnp.where(row == Tt - 1, next_row, pltpu.roll(x, Tt - 1, 0))
    pm = x + (0.5 * (prev + nxt) - x) * mu_ref[...]
    o1 = 3 * RW_DIM
    o2 = o1 + DECAY_RANK
    o3 = o2 + ICLR_RANK
    r_ref[0] = pm[:, :RW_DIM]
    k = pm[:, RW_DIM:2 * RW_DIM]
    k_ref[0] = k
    v_ref[0] = pm[:, 2 * RW_DIM:o1]
    wd = jnp.tanh(pm[:, o1:o2])
    ad = pm[:, o2:o3]
    gin = jax.nn.sigmoid(pm[:, o3:])
    g_ref[0] = jnp.dot(gin.astype(BF16), gup_ref[...].astype(BF16), preferred_element_type=F32)
    kk = k * kk_ref[...]
    ss = _dot_sel(kk * kk, hsum_ref[...])
    kkn_ref[0] = kk / jnp.maximum(jnp.sqrt(ss), 1e-12)

    ci = _iota2((RW_CHUNK, RW_CHUNK), 0)
    cj = _iota2((RW_CHUNK, RW_CHUNK), 1)
    tril = (cj <= ci).astype(F32)
    triu = (cj >= ci).astype(F32)

    def direction(w0_ref, wup_ref, a0_ref, aup_ref, tri, lw_ref, l_ref, as_ref):
        z = w0_ref[...] + _dot3(wd, wup_ref[...])
        sp = jnp.maximum(-z, 0.0) + jnp.log(1.0 + jnp.exp(-jnp.abs(z)))
        lw = -jnp.exp(-sp - 0.5)
        lw_ref[0] = lw
        as_ref[0] = jax.nn.sigmoid(a0_ref[...] + _dot3(ad, aup_ref[...]))
        for c in range(Tt // RW_CHUNK):
            sl = slice(c * RW_CHUNK, (c + 1) * RW_CHUNK)
            l_ref[0, sl, :] = _sel_dot(tri, lw[sl, :])

    direction(w0f_ref, wupf_ref, a0f_ref, aupf_ref, tril, lwf_ref, lf_ref, asf_ref)
    direction(w0b_ref, wupb_ref, a0b_ref, aupb_ref, triu, lwb_ref, lb_ref, asb_ref)


def rwkv_prep(x, norm1, w_in, shift_mu, w0_f, w_up_f, w0_b, w_up_b, a0_f, a_up_f, a0_b, a_up_b, g_up, k_k, *, tt):
    B, S, D = x.shape
    nt = S // tt
    hsum = (jnp.arange(RW_DIM)[:, None] // HEAD_DIM == jnp.arange(RW_DIM)[None, :] // HEAD_DIM).astype(F32)
    row = lambda a: a.reshape(1, -1)
    full = lambda a: pl.BlockSpec(a.shape, lambda b, i: (0,) * a.ndim)
    params = [row(norm1), w_in, row(shift_mu), row(w0_f), w_up_f, row(w0_b), w_up_b, row(a0_f), a_up_f, row(a0_b), a_up_b,
              g_up, row(k_k), hsum]
    tb = tt // 8
    in_specs = [
        pl.BlockSpec((1, tt, D), lambda b, i: (b, i, 0)),
        pl.BlockSpec((1, 8, D), lambda b, i: (b, jnp.maximum(i * tb - 1, 0), 0)),
        pl.BlockSpec((1, 8, D), lambda b, i: (b, jnp.minimum((i + 1) * tb, S // 8 - 1), 0)),
    ] + [full(a) for a in params]
    out_spec = pl.BlockSpec((1, tt, RW_DIM), lambda b, i: (b, i, 0))
    out_sds = jax.ShapeDtypeStruct((B, S, RW_DIM), F32)
    return pl.pallas_call(
        _rwkv_prep_kernel,
        grid=(B, nt),
        in_specs=in_specs,
        out_specs=[out_spec] * 11,
        out_shape=[out_sds] * 11,
        compiler_params=_cparams(("parallel", "arbitrary")),
        name="rwkv_prep",
    )(x, x, x, *params)


def _bd(y, bd_mask):
    return jnp.where(bd_mask, jnp.concatenate([y, y], axis=0), jnp.zeros((), y.dtype))


def _chunk_terms(chains, out, bd_mask):
    C = chains[0]["r"].shape[0]
    n = len(chains)
    R = range(n)
    dot = lambda u, v: jnp.dot(u, v, preferred_element_type=F32)
    bd = lambda y: _bd(y, bd_mask)
    pre = []
    for ch in chains:
        r, k, v, kk, lw, L, asig, ka, rev = (ch[x] for x in ("r", "k", "v", "kk", "lw", "L", "asig", "ka", "rev"))
        a = -kk
        b = kk * asig
        kd = k * (1.0 + (asig - 1.0) * ka)
        Lp = L - lw
        Lr = Lp if rev else L
        Lend = L[0:1, :] if rev else L[C - 1:C, :]
        Lmid = L[C // 2:C // 2 + 1, :]
        einv = jnp.exp(Lmid - L)
        eend = jnp.exp(Lend - L)
        pre.append(dict(
            ar=jnp.concatenate([a * jnp.exp(Lp - Lmid), r * jnp.exp(Lr - Lmid)], axis=0).astype(BF16),
            bt=bd((b * einv).astype(BF16)), kt=bd((kd * einv).astype(BF16)),
            a0=bd((a * jnp.exp(Lp)).astype(BF16)), r0=r * jnp.exp(Lr),
            bh=(b * eend).astype(BF16), kh=(kd * eend).astype(BF16), vb=v.astype(BF16),
            dec=jnp.exp(Lend)))
    mk = [ch["masks"] for ch in chains]
    Ab = [_bdot_nt(pre[i]["ar"], pre[i]["bt"]) for i in R]
    Ak = [_bdot_nt(pre[i]["ar"], pre[i]["kt"]) for i in R]
    yield
    Aab = [jnp.where(mk[i][0], Ab[i][:C], 0.0) for i in R]
    Arb = [jnp.where(mk[i][1], Ab[i][C:], 0.0).astype(BF16) for i in R]
    AakArk = [jnp.concatenate([jnp.where(mk[i][0], Ak[i][:C], 0.0), jnp.where(mk[i][1], Ak[i][C:], 0.0)],
                              axis=0).astype(BF16) for i in R]
    AV = [dot(AakArk[i], bd(pre[i]["vb"])) for i in R]
    Xb = [jnp.where(mk[i][2], Aab[i], 0.0).astype(BF16) for i in R]
    X2b = [dot(Xb[i], bd(Xb[i])).astype(BF16) for i in R]
    yield
    T = [mk[i][6] + Xb[i].astype(F32) for i in R]
    T = [T[i] + dot(T[i].astype(BF16), bd(X2b[i])) for i in R]
    X4b = [dot(X2b[i], bd(X2b[i])).astype(BF16) for i in R]
    yield
    T = [T[i] + dot(T[i].astype(BF16), bd(X4b[i])) for i in R]
    yield
    for lvl in (3, 4, 5):
        Tb = [T[i].astype(BF16) for i in R]
        ET = [dot(jnp.where(mk[i][lvl], Aab[i], 0.0).astype(BF16), bd(Tb[i])).astype(BF16) for i in R]
        yield
        T = [T[i] + dot(Tb[i], bd(ET[i])) for i in R]
        yield
    Tb = [T[i].astype(BF16) for i in R]
    A0p = [dot(Tb[i], pre[i]["a0"]).astype(BF16) for i in R]
    Uv = [dot(Tb[i], bd(AV[i][:C].astype(BF16))).astype(BF16) for i in R]
    yield
    Rpp = [pre[i]["r0"] + dot(Arb[i], bd(A0p[i])) for i in R]
    Yv = [dot(Arb[i], bd(Uv[i])) + AV[i][C:] for i in R]
    eye2 = (_iota2(bd_mask.shape, 0) == _iota2(bd_mask.shape, 1)).astype(F32)
    P = [jnp.where(bd_mask, _bdot_tn(pre[i]["bh"], A0p[i]), 0.0) + eye2 * pre[i]["dec"] for i in R]
    Q = [jnp.where(bd_mask, _bdot_tn(pre[i]["bh"], Uv[i]) + _bdot_tn(pre[i]["kh"], pre[i]["vb"]), 0.0) for i in R]
    out.extend((Rpp[i], Yv[i], P[i], Q[i]) for i in R)


def _make_masks(C, N, rev):
    ii = _iota2((C, 2 * N), 0)
    jj = _iota2((C, 2 * N), 1) % N
    strict = (jj > ii) if rev else (jj < ii)
    rmask = strict if rev else (jj <= ii)
    blk = lambda s: (ii // s) == (jj // s)
    m8 = blk(8)
    e16 = blk(16) & jnp.logical_not(blk(8))
    e32 = blk(32) & jnp.logical_not(blk(16))
    e64 = jnp.logical_not(blk(32))
    eye = (ii == jj).astype(F32)
    return strict, rmask, m8, e16, e32, e64, eye


RW_UNROLL = 8


def _rwkv_scan_kernel(r_ref, k_ref, v_ref, kk_ref, g_ref, lwf_ref, lf_ref, asf_ref, lwb_ref, lb_ref, asb_ref,
                      ka_ref, rk_ref, gnw_ref, gnb_ref, havg_ref, o_ref,
                      yf_ref, yb_ref, h_ref, rpp_ref, yv_ref, p_ref, q_ref):
    S = r_ref.shape[1]
    C = RW_CHUNK
    N = HEAD_DIM
    W = 2 * N
    n = S // C
    U = RW_UNROLL
    nset = n // U
    masks_f = _make_masks(C, N, False)
    masks_b = _make_masks(C, N, True)
    bd_mask = (_iota2((W, W), 0) // N) == (_iota2((W, W), 1) // N)
    h_ref[...] = jnp.zeros_like(h_ref)
    dirs = ((False, lwf_ref, lf_ref, asf_ref, yf_ref, masks_f), (True, lwb_ref, lb_ref, asb_ref, yb_ref, masks_b))

    def chunk_rows(rev, st, u):
        c = st * U + u
        c = (n - 1 - c) if rev else c
        return pl.ds(pl.multiple_of(c * C, C), C)

    def state_step(st, u):
        slot = st % 2
        for di, (rev, lw_ref, l_ref, as_ref, y_ref, masks) in enumerate(dirs):
            Hb = h_ref[di].astype(BF16)
            y_ref[chunk_rows(rev, st, u), :] = (jnp.dot(rpp_ref[slot, di, u], Hb, preferred_element_type=F32)
                                                + yv_ref[slot, di, u])
            h_ref[di] = jnp.dot(p_ref[slot, di, u], Hb, preferred_element_type=F32) + q_ref[slot, di, u]

    def chunk_set(st, state_of):
        chains, slots = [], []
        for u in range(U):
            for di, (rev, lw_ref, l_ref, as_ref, y_ref, masks) in enumerate(dirs):
                rows = chunk_rows(rev, st, u)
                chains.append(dict(
                    r=r_ref[0, rows, :], k=k_ref[0, rows, :], v=v_ref[0, rows, :], kk=kk_ref[0, rows, :],
                    lw=lw_ref[0, rows, :], L=l_ref[0, rows, :], asig=as_ref[0, rows, :], ka=ka_ref[...],
                    rev=rev, masks=masks))
                slots.append((di, u))
        res = []
        pending = list(range(U)) if state_of is not None else []
        for stage, _ in enumerate(_chunk_terms(chains, res, bd_mask)):
            if pending and stage >= 1:
                state_step(state_of, pending.pop(0))
        for u in pending:
            state_step(state_of, u)
        slot = st % 2
        for (di, u), (Rpp, Yv, P, Q) in zip(slots, res):
            rpp_ref[slot, di, u] = Rpp.astype(BF16)
            yv_ref[slot, di, u] = Yv
            p_ref[slot, di, u] = P.astype(BF16)
            q_ref[slot, di, u] = Q

    chunk_set(0, None)

    def body(st, carry):
        chunk_set(st, st - 1)
        return carry

    lax.fori_loop(1, nset, body, 0)
    for u in range(U):
        state_step(nset - 1, u)

    RT = min(S, 1024)

    def fin(i, carry):
        rows = pl.ds(pl.multiple_of(i * RT, RT), RT)
        y = yf_ref[rows, :] + yb_ref[rows, :]
        mean = _dot_sel(y, havg_ref[...])
        d = y - mean
        var = _dot_sel(d * d, havg_ref[...])
        yn = d * lax.rsqrt(var + RWKV_GN_EPS) * gnw_ref[...] + gnb_ref[...]
        r = r_ref[0, rows, :]
        kf = k_ref[0, rows, :] * (1.0 + (asf_ref[0, rows, :] - 1.0) * ka_ref[...])
        bonus = _dot_sel(r * kf * rk_ref[...], havg_ref[...]) * float(N) * v_ref[0, rows, :]
        o_ref[0, rows, :] = ((yn + bonus) * g_ref[0, rows, :]).astype(o_ref.dtype)
        return carry

    lax.fori_loop(0, S // RT, fin, 0)


def rwkv_scan(r, k, v, kk, g, lw_f, l_f, as_f, lw_b, l_b, as_b, k_a, r_k, gn_w, gn_b, out_dtype=F32):
    B, S, _ = r.shape
    W = 2 * HEAD_DIM
    N = HEAD_DIM
    havg = (jnp.arange(W)[:, None] // HEAD_DIM == jnp.arange(W)[None, :] // HEAD_DIM).astype(F32) / HEAD_DIM
    seq = pl.BlockSpec((1, S, W), lambda b, h: (b, 0, h))
    par = pl.BlockSpec((1, W), lambda b, h: (0, h))
    row = lambda a: a.reshape(1, -1)
    U = RW_UNROLL
    return pl.pallas_call(
        _rwkv_scan_kernel,
        grid=(B, RW_DIM // W),
        in_specs=[seq] * 11 + [par] * 4 + [pl.BlockSpec((W, W), lambda b, h: (0, 0))],
        out_specs=seq,
        out_shape=jax.ShapeDtypeStruct((B, S, RW_DIM), out_dtype),
        scratch_shapes=[pltpu.VMEM((S, W), F32), pltpu.VMEM((S, W), F32), pltpu.VMEM((2, W, W), F32),
                        pltpu.VMEM((2, 2, U, RW_CHUNK, W), BF16), pltpu.VMEM((2, 2, U, RW_CHUNK, W), F32),
                        pltpu.VMEM((2, 2, U, W, W), BF16), pltpu.VMEM((2, 2, U, W, W), F32)],
        compiler_params=_cparams(("parallel", "parallel")),
        name="rwkv_scan",
    )(r, k, v, kk, g, lw_f, l_f, as_f, lw_b, l_b, as_b, row(k_a), row(r_k), row(gn_w), row(gn_b), havg)


def rwkv7_mixer(x, norm1, w_in, shift_mu, w0_f, w_up_f, w0_b, w_up_b, a0_f, a_up_f, a0_b, a_up_b, g_up, k_k, k_a,
                r_k, gn_w, gn_b, out_dtype=F32, tt=PREP_ROWS):
    outs = rwkv_prep(x, norm1, w_in, shift_mu, w0_f, w_up_f, w0_b, w_up_b, a0_f, a_up_f, a0_b, a_up_b, g_up, k_k, tt=tt)
    return rwkv_scan(*outs, k_a, r_k, gn_w, gn_b, out_dtype=out_dtype)


def _head_norm_rope(x, gain, cos, sin_signed, havg):
    ms = _dot_sel(x * x, havg)
    xn = x * lax.rsqrt(ms + NORM_EPS) * gain
    W = x.shape[1]
    even = (_iota2(x.shape, 1) % 2) == 0
    partner = jnp.where(even, pltpu.roll(xn, W - 1, 1), pltpu.roll(xn, 1, 1))
    return xn * cos + partner * sin_signed


def _attn_kernel(q_ref, k_ref, v_ref, cosq_ref, sinq_ref, cosk_ref, sink_ref, qg_ref, kg_ref, hq_ref, hk_ref,
                 o_ref, ks_ref, vt_ref):
    i = pl.program_id(1)
    G = ATT_HEADS // ATT_KV_HEADS
    D = HEAD_DIM

    @pl.when(i == 0)
    def _():
        kr = _head_norm_rope(k_ref[0], kg_ref[...], cosk_ref[...], sink_ref[...], hk_ref[...])
        vt = jnp.transpose(v_ref[0])
        ones = jnp.ones((ATT_VT_ROWS - D, vt.shape[1]), BF16)
        for kv in range(ATT_KV_HEADS):
            ks_ref[kv] = kr[:, kv * D:(kv + 1) * D].astype(BF16)
            vt_ref[kv] = jnp.concatenate([vt[kv * D:(kv + 1) * D, :].astype(BF16), ones], axis=0)

    q = _head_norm_rope(q_ref[0], qg_ref[...], cosq_ref[...], sinq_ref[...], hq_ref[...]) * (D ** -0.5 * LOG2_E)
    qb = q.astype(BF16)
    outs = []
    for h0 in range(0, ATT_HEADS, ATT_LOCKSTEP):
        hs = range(h0, h0 + ATT_LOCKSTEP)
        st = [lax.dot_general(ks_ref[h // G], qb[:, h * D:(h + 1) * D], (((1,), (1,)), ((), ())),
                              preferred_element_type=F32) for h in hs]
        m = [jnp.max(x, axis=0, keepdims=True) for x in st]
        p = [jnp.exp2(x - mm).astype(BF16) for x, mm in zip(st, m)]
        ot = [jnp.dot(vt_ref[h // G], x, preferred_element_type=F32) for h, x in zip(hs, p)]
        outs.extend(o[:D] / o[D:D + 1] for o in ot)
    pairs = [jnp.transpose(jnp.concatenate(outs[2 * j:2 * j + 2], axis=0)) for j in range(ATT_HEADS // 2)]
    o_ref[0] = jnp.concatenate(pairs, axis=1).astype(o_ref.dtype)


def _rope_tables(S):
    rows = S // GRID_W
    row = jnp.repeat(jnp.arange(rows), GRID_W).astype(F32)
    col = jnp.tile(jnp.arange(GRID_W), rows).astype(F32)
    half = HEAD_DIM // 2
    freq = ROPE_THETA ** (-jnp.arange(0, half, 2, dtype=F32) / half)
    ang = jnp.concatenate([row[:, None] * freq, col[:, None] * freq], axis=-1)
    cos = jnp.repeat(jnp.cos(ang), 2, axis=-1)
    sin = jnp.repeat(jnp.sin(ang), 2, axis=-1)
    sign = jnp.where(jnp.arange(HEAD_DIM) % 2 == 0, -1.0, 1.0).astype(F32)
    return cos, sin * sign


def gqa_attention(p, q_norm, k_norm, *, tq=ATT_Q_ROWS, out_dtype=F32):
    B, S, _ = p.shape
    cos, sin = _rope_tables(S)
    tile = lambda t, n: jnp.tile(t, (1, n))
    hq = (jnp.arange(ATT_DIM)[:, None] // HEAD_DIM == jnp.arange(ATT_DIM)[None, :] // HEAD_DIM).astype(F32) / HEAD_DIM
    hk = hq[:ATT_KV_DIM, :ATT_KV_DIM]
    qg = jnp.tile(q_norm, ATT_HEADS).reshape(1, ATT_DIM)
    kg = jnp.tile(k_norm, ATT_KV_HEADS).reshape(1, ATT_KV_DIM)
    nq = ATT_DIM // ATT_KV_DIM
    const = lambda a: pl.BlockSpec(a.shape, lambda b, i: (0, 0))
    return pl.pallas_call(
        _attn_kernel,
        grid=(B, S // tq),
        in_specs=[
            pl.BlockSpec((1, tq, ATT_DIM), lambda b, i: (b, i, 0)),
            pl.BlockSpec((1, S, ATT_KV_DIM), lambda b, i: (b, 0, nq)),
            pl.BlockSpec((1, S, ATT_KV_DIM), lambda b, i: (b, 0, nq + 1)),
            pl.BlockSpec((tq, ATT_DIM), lambda b, i: (i, 0)),
            pl.BlockSpec((tq, ATT_DIM), lambda b, i: (i, 0)),
            pl.BlockSpec((S, ATT_KV_DIM), lambda b, i: (0, 0)),
            pl.BlockSpec((S, ATT_KV_DIM), lambda b, i: (0, 0)),
            const(qg), const(kg), const(hq), const(hk),
        ],
        out_specs=pl.BlockSpec((1, tq, ATT_DIM), lambda b, i: (b, i, 0)),
        out_shape=jax.ShapeDtypeStruct((B, S, ATT_DIM), out_dtype),
        scratch_shapes=[pltpu.VMEM((ATT_KV_HEADS, S, HEAD_DIM), BF16), pltpu.VMEM((ATT_KV_HEADS, ATT_VT_ROWS, S), BF16)],
        compiler_params=_cparams(("parallel", "arbitrary")),
        name="gqa_attention",
    )(p, p, p, tile(cos, ATT_HEADS), tile(sin, ATT_HEADS), tile(cos, ATT_KV_HEADS), tile(sin, ATT_KV_HEADS),
      qg, kg, hq, hk)


GLA_UNROLL = 16


def _gla_kernel(q_ref, k_ref, v_ref, gd_ref, og_ref, upf_ref, bf_ref, upb_ref, bb_ref, on_ref, o_ref,
                yf_ref, yb_ref, st_ref, kv_ref, qe_ref, dec_ref):
    S = q_ref.shape[1]
    C = GLA_CHUNK
    n = S // C
    U = GLA_UNROLL
    nset = n // U
    ii = _iota2((C, C), 0)
    jj = _iota2((C, C), 1)
    dirs = (
        (False, (jj <= ii).astype(F32), jj <= ii, upf_ref, bf_ref, yf_ref),
        (True, (jj >= ii).astype(F32), jj > ii, upb_ref, bb_ref, yb_ref),
    )
    scale = GLA_DKH ** -0.5
    st_ref[...] = jnp.zeros_like(st_ref)

    def chunk_rows(rev, st, u):
        c = st * U + u
        c = (n - 1 - c) if rev else c
        return pl.ds(pl.multiple_of(c * C, C), C)

    def state_step(st, u):
        slot = st % 2
        for di, (rev, tri, mask, up_ref, b_ref, y_ref) in enumerate(dirs):
            state = st_ref[di]
            rows = chunk_rows(rev, st, u)
            y_ref[rows, :] += lax.dot_general(qe_ref[slot, di, u], state.astype(BF16), (((1,), (1,)), ((), ())),
                                              preferred_element_type=F32)
            st_ref[di] = state * dec_ref[slot, di, u] + kv_ref[slot, di, u]

    def chunk_set(st, state_of):
        pending = list(range(U)) if state_of is not None else []
        items = []
        for u in range(U):
            for di, (rev, tri, mask, up_ref, b_ref, y_ref) in enumerate(dirs):
                rows = chunk_rows(rev, st, u)
                z = _dot3(gd_ref[0, rows, :], up_ref[...]) + b_ref[...]
                g = (jnp.minimum(z, 0.0) - jnp.log(1.0 + jnp.exp(-jnp.abs(z)))) * (1.0 / GLA_GATE_NORM)
                items.append(dict(u=u, di=di, rows=rows, rev=rev, mask=mask, y_ref=y_ref, g=g, tri=tri))
        for x in items:
            x["b"] = _sel_dot(x["tri"], x["g"])
        if pending:
            state_step(state_of, pending.pop(0))
        for x in items:
            b = x["b"]
            q = q_ref[0, x["rows"], :] * scale
            k = k_ref[0, x["rows"], :]
            b_mid = b[C // 2:C // 2 + 1, :]
            b_last = b[0:1, :] if x["rev"] else b[C - 1:C, :]
            x["vb"] = v_ref[0, x["rows"], :].astype(BF16)
            x["qm"] = (q * jnp.exp(b - b_mid)).astype(BF16)
            x["km"] = (k * jnp.exp(b_mid - b)).astype(BF16)
            x["ke"] = (k * jnp.exp(b_last - b)).astype(BF16)
            x["qe"] = (q * jnp.exp(b)).astype(BF16)
            x["dec"] = jnp.exp(b_last)
        att = [lax.dot_general(x["qm"], x["km"], (((1,), (1,)), ((), ())), preferred_element_type=F32) for x in items]
        if pending:
            state_step(state_of, pending.pop(0))
        att = [jnp.where(x["mask"], a, 0.0).astype(BF16) for x, a in zip(items, att)]
        slot = st % 2
        for idx, (x, a) in enumerate(zip(items, att)):
            x["y_ref"][x["rows"], :] = jnp.dot(a, x["vb"], preferred_element_type=F32)
            kv_ref[slot, x["di"], x["u"]] = lax.dot_general(x["vb"], x["ke"], (((0,), (0,)), ((), ())),
                                                            preferred_element_type=F32)
            qe_ref[slot, x["di"], x["u"]] = x["qe"]
            dec_ref[slot, x["di"], x["u"]] = x["dec"]
            if pending and idx % 2 == 1:
                state_step(state_of, pending.pop(0))
        for u in pending:
            state_step(state_of, u)

    chunk_set(0, None)

    def body(st, carry):
        chunk_set(st, st - 1)
        return carry

    lax.fori_loop(1, nset, body, 0)
    for u in range(U):
        state_step(nset - 1, u)

    RT = min(S, 1024)

    def fin(i, carry):
        rows = pl.ds(pl.multiple_of(i * RT, RT), RT)
        o = yf_ref[rows, :] + yb_ref[rows, :]
        ms = jnp.mean(o * o, axis=-1, keepdims=True)
        on = o * lax.rsqrt(ms + NORM_EPS) * on_ref[...]
        og = og_ref[0, rows, :]
        o_ref[0, rows, :] = (on * (og * jax.nn.sigmoid(og))).astype(o_ref.dtype)
        return carry

    lax.fori_loop(0, S // RT, fin, 0)


def gla_mixer(pm, gd, gate_up_f, gate_bias_f, gate_up_b, gate_bias_b, out_norm, out_dtype=F32):
    B, S, _ = pm.shape
    U = GLA_UNROLL
    H = GLA_HEADS
    kb = GLA_DK // GLA_DKH
    vb0 = 2 * GLA_DK // GLA_DVH
    ob0 = vb0 + GLA_DV // GLA_DVH
    return pl.pallas_call(
        _gla_kernel,
        grid=(B, H),
        in_specs=[
            pl.BlockSpec((1, S, GLA_DKH), lambda b, h: (b, 0, h)),
            pl.BlockSpec((1, S, GLA_DKH), lambda b, h: (b, 0, kb + h)),
            pl.BlockSpec((1, S, GLA_DVH), lambda b, h: (b, 0, vb0 + h)),
            pl.BlockSpec((1, S, GLA_GATE_RANK), lambda b, h: (b, 0, 0)),
            pl.BlockSpec((1, S, GLA_DVH), lambda b, h: (b, 0, ob0 + h)),
            pl.BlockSpec((GLA_GATE_RANK, GLA_DKH), lambda b, h: (0, h)),
            pl.BlockSpec((1, GLA_DKH), lambda b, h: (0, h)),
            pl.BlockSpec((GLA_GATE_RANK, GLA_DKH), lambda b, h: (0, h)),
            pl.BlockSpec((1, GLA_DKH), lambda b, h: (0, h)),
            pl.BlockSpec((1, GLA_DVH), lambda b, h: (0, 0)),
        ],
        out_specs=pl.BlockSpec((1, S, GLA_DVH), lambda b, h: (b, 0, h)),
        out_shape=jax.ShapeDtypeStruct((B, S, GLA_DV), out_dtype),
        scratch_shapes=[pltpu.VMEM((S, GLA_DVH), F32), pltpu.VMEM((S, GLA_DVH), F32),
                        pltpu.VMEM((2, GLA_DVH, GLA_DKH), F32),
                        pltpu.VMEM((2, 2, U, GLA_DVH, GLA_DKH), F32), pltpu.VMEM((2, 2, U, GLA_CHUNK, GLA_DKH), BF16),
                        pltpu.VMEM((2, 2, U, 1, GLA_DKH), F32)],
        compiler_params=_cparams(("parallel", "parallel")),
        name="gla_mixer",
    )(pm, pm, pm, gd, pm, gate_up_f, gate_bias_f.reshape(1, -1), gate_up_b, gate_bias_b.reshape(1, -1),
      out_norm.reshape(1, -1))


def _router_kernel(x_ref, g_ref, wr_ref, hn_ref, idx_ref, gate_ref):
    x = x_ref[...]
    ms = jnp.mean(x * x, axis=-1, keepdims=True)
    hn = x * lax.rsqrt(ms + NORM_EPS) * g_ref[...]
    hn_ref[...] = hn
    logits = _dot3(hn, wr_ref[...])
    lane = _iota2(logits.shape, 1)
    neg = jnp.float32(-jnp.inf)
    logits = jnp.where(lane < N_EXPERTS, logits, neg)
    m1 = jnp.max(logits, axis=-1, keepdims=True)
    i1 = jnp.min(jnp.where(logits == m1, lane, LANES), axis=-1, keepdims=True)
    rest = jnp.where(lane == i1, neg, logits)
    m2 = jnp.max(rest, axis=-1, keepdims=True)
    i2 = jnp.min(jnp.where(rest == m2, lane, LANES), axis=-1, keepdims=True)
    e2 = jnp.exp(m2 - m1)
    g1 = 1.0 / (1.0 + e2)
    g2 = e2 / (1.0 + e2)
    idx_ref[...] = jnp.where(lane == 0, i1, jnp.where(lane == 1, i2, 0))
    gate_ref[...] = jnp.where(lane == 0, g1, jnp.where(lane == 1, g2, 0.0))


def moe_router(x, g, router, *, tm):
    T, D = x.shape
    wr = jnp.zeros((D, LANES), F32).at[:, :N_EXPERTS].set(router)
    return pl.pallas_call(
        _router_kernel,
        grid=(T // tm,),
        in_specs=[
            pl.BlockSpec((tm, D), lambda i: (i, 0)),
            pl.BlockSpec((1, D), lambda i: (0, 0)),
            pl.BlockSpec((D, LANES), lambda i: (0, 0)),
        ],
        out_specs=[
            pl.BlockSpec((tm, D), lambda i: (i, 0)),
            pl.BlockSpec((tm, LANES), lambda i: (i, 0)),
            pl.BlockSpec((tm, LANES), lambda i: (i, 0)),
        ],
        out_shape=[jax.ShapeDtypeStruct((T, D), F32), jax.ShapeDtypeStruct((T, LANES), jnp.int32),
                   jax.ShapeDtypeStruct((T, LANES), F32)],
        compiler_params=_cparams(("parallel",)),
        name="moe_router",
    )(x, g.reshape(1, D), wr)


def _expert_kernel(te_ref, tv_ref, tok_ref, hn_hbm, wg_ref, wu_ref, wd_ref, o_ref, xg_ref, xb_ref, acc_ref, sem, *, nf):
    i = pl.program_id(0)
    f = pl.program_id(1)
    tm = xb_ref.shape[0]
    valid = tv_ref[i] > 0
    prev_valid = tv_ref[jnp.maximum(i - 1, 0)] > 0
    slot = i % 2

    def issue_row(tile, dst_slot, r):
        tok = tok_ref[tile * tm + r]
        pltpu.make_async_copy(hn_hbm.at[pl.ds(tok, 1), :], xg_ref.at[dst_slot, pl.ds(r, 1), :], sem.at[dst_slot]).start()

    @pl.when(jnp.logical_and(valid, jnp.logical_and(i == 0, f == 0)))
    def _():
        def body(r, c):
            issue_row(0, 0, r)
            return c

        lax.fori_loop(0, tm, body, 0, unroll=8)

    @pl.when(jnp.logical_and(f == 0, jnp.logical_or(valid, jnp.logical_and(i > 0, prev_valid))))
    def _():
        pltpu.make_async_copy(hn_hbm.at[pl.ds(0, tm), :], xg_ref.at[slot], sem.at[slot]).wait()

    @pl.when(jnp.logical_and(valid, f == 0))
    def _():
        xb_ref[...] = xg_ref[slot].astype(BF16)

    @pl.when(valid)
    def _():
        per = tm // nf
        row0 = f * per
        xb = xb_ref[...]
        a = jnp.dot(xb, wg_ref[0], preferred_element_type=F32)
        for j in range(per // 2):
            issue_row(i + 1, 1 - slot, row0 + j)
        b = jnp.dot(xb, wu_ref[0], preferred_element_type=F32)
        for j in range(per // 2, per):
            issue_row(i + 1, 1 - slot, row0 + j)
        h = (a * jax.nn.sigmoid(a) * b).astype(BF16)
        part = jnp.dot(h, wd_ref[0], preferred_element_type=F32)

        @pl.when(f == 0)
        def _():
            acc_ref[...] = part

        @pl.when(f > 0)
        def _():
            acc_ref[...] += part

    @pl.when(f == nf - 1)
    def _():
        o_ref[...] = jnp.where(valid, acc_ref[...], 0.0)


def moe_experts(hn, tile_expert, tile_valid, row_tok, wg, wu, wd, *, tm, tf):
    T, D = hn.shape
    F = wg.shape[2]
    P = row_tok.shape[0] - tm
    nt = P // tm + 1
    nf = F // tf

    def w_in_map(i, f, te, tv, tok):
        return (te[i], 0, jnp.where(tv[i] > 0, f, nf - 1))

    def w_out_map(i, f, te, tv, tok):
        return (te[i], jnp.where(tv[i] > 0, f, nf - 1), 0)

    grid_spec = pltpu.PrefetchScalarGridSpec(
        num_scalar_prefetch=3,
        grid=(nt, nf),
        in_specs=[
            pl.BlockSpec(memory_space=pl.ANY),
            pl.BlockSpec((1, D, tf), w_in_map),
            pl.BlockSpec((1, D, tf), w_in_map),
            pl.BlockSpec((1, tf, D), w_out_map),
        ],
        out_specs=pl.BlockSpec((tm, D), lambda i, f, te, tv, tok: (i, 0)),
        scratch_shapes=[pltpu.VMEM((2, tm, D), F32), pltpu.VMEM((tm, D), BF16), pltpu.VMEM((tm, D), F32),
                        pltpu.SemaphoreType.DMA((2,))],
    )
    return pl.pallas_call(
        functools.partial(_expert_kernel, nf=nf),
        grid_spec=grid_spec,
        out_shape=jax.ShapeDtypeStruct((P + tm, D), F32),
        compiler_params=_cparams(("arbitrary", "arbitrary")),
        name="moe_experts",
    )(tile_expert, tile_valid, row_tok, hn, wg, wu, wd)


def _combine_kernel(dest_ref, x_ref, gate_ref, yb_hbm, o_ref, buf_ref, sem):
    i = pl.program_id(0)
    n = pl.num_programs(0)
    tc = x_ref.shape[0]
    slot = i % 2

    def issue(step, dst_slot):
        base = step * tc * TOP_K

        for r in range(tc):
            for k in range(TOP_K):
                pltpu.make_async_copy(yb_hbm.at[pl.ds(dest_ref[base + r * TOP_K + k], 1), :],
                                      buf_ref.at[dst_slot, k, pl.ds(r, 1), :], sem.at[dst_slot]).start(priority=k)

    @pl.when(i == 0)
    def _():
        issue(0, 0)

    for k in range(TOP_K):
        pltpu.make_async_copy(yb_hbm.at[pl.ds(0, tc), :], buf_ref.at[slot, k], sem.at[slot]).wait()

    @pl.when(i + 1 < n)
    def _():
        issue(i + 1, 1 - slot)

    g = gate_ref[...]
    o_ref[...] = x_ref[...] + g[:, 0:1] * buf_ref[slot, 0] + g[:, 1:2] * buf_ref[slot, 1]


def moe_combine(x, gates, yb, dest, *, tc):
    T, D = x.shape
    grid_spec = pltpu.PrefetchScalarGridSpec(
        num_scalar_prefetch=1,
        grid=(T // tc,),
        in_specs=[
            pl.BlockSpec((tc, D), lambda i, d: (i, 0)),
            pl.BlockSpec((tc, LANES), lambda i, d: (i, 0)),
            pl.BlockSpec(memory_space=pl.ANY),
        ],
        out_specs=pl.BlockSpec((tc, D), lambda i, d: (i, 0)),
        scratch_shapes=[pltpu.VMEM((2, TOP_K, tc, D), F32), pltpu.SemaphoreType.DMA((2,))],
    )
    return pl.pallas_call(
        _combine_kernel,
        grid_spec=grid_spec,
        out_shape=jax.ShapeDtypeStruct((T, D), F32),
        compiler_params=_cparams(("arbitrary",)),
        name="moe_combine",
    )(dest, x, gates, yb)


def moe_dispatch_plan(idx, *, tm):
    T = idx.shape[0]
    A = T * TOP_K
    e_flat = idx[:, :TOP_K].reshape(A)
    onehot = (e_flat[:, None] == jnp.arange(N_EXPERTS, dtype=jnp.int32)[None, :]).astype(jnp.int32)
    rank = jnp.sum((jnp.cumsum(onehot, axis=0) - onehot) * onehot, axis=1)
    counts = jnp.sum(onehot, axis=0)
    padded = (counts + tm - 1) // tm * tm
    ends = jnp.cumsum(padded)
    pstart = ends - padded
    dest = pstart[e_flat] + rank
    P = (A // tm + N_EXPERTS + 1) * tm
    nt = P // tm
    tok_flat = jnp.arange(A, dtype=jnp.int32) // TOP_K
    row_tok = jnp.zeros((P,), jnp.int32).at[dest].set(tok_flat)
    tile_start = jnp.arange(nt, dtype=jnp.int32) * tm
    tile_expert = jnp.minimum(jnp.sum((tile_start[:, None] >= ends[None, :]).astype(jnp.int32), axis=1), N_EXPERTS - 1)
    tile_valid = (tile_start < ends[-1]).astype(jnp.int32)
    last_valid = jnp.maximum(jnp.sum(tile_valid) - 1, 0)
    tile_expert = jnp.where(tile_valid > 0, tile_expert, tile_expert[last_valid])
    return dest.astype(jnp.int32), row_tok, tile_expert, tile_valid


def moe_layer(x, norm2, router, wg, wu, wd, *, tm=MOE_ROWS, tf=MOE_HIDDEN, tr=ROUTER_ROWS, tc=COMBINE_ROWS):
    hn, idx, gates = moe_router(x, norm2, router, tm=tr)
    dest, row_tok, tile_expert, tile_valid = moe_dispatch_plan(idx, tm=tm)
    yb = moe_experts(hn, tile_expert, tile_valid, row_tok, wg, wu, wd, tm=tm, tf=tf)
    return moe_combine(x, gates, yb, dest, tc=tc)


def _even_layer(x, norm1, w_in, shift_mu, w0_f, w_up_f, w0_b, w_up_b, a0_f, a_up_f, a0_b, a_up_b, g_up,
                k_k, k_a, r_k, gn_w, gn_b, q_norm, k_norm, w_out, norm2, ffn_gate, ffn_up, ffn_down, cast):
    Bn, S, D = x.shape
    T = Bn * S
    xf = x.reshape(T, D)
    w_in = w_in.astype(BF16)
    p_att = norm_matmul(xf, norm1, w_in[:, RW_IN:], tm=IN_PROJ_ROWS, tn=ATT_IN).reshape(Bn, S, ATT_IN)
    y_a = rwkv7_mixer(x, norm1, w_in[:, :RW_IN], shift_mu, w0_f, w_up_f, w0_b, w_up_b, a0_f, a_up_f, a0_b, a_up_b,
                      g_up, k_k, k_a, r_k, gn_w, gn_b, out_dtype=BF16)
    y_b = gqa_attention(p_att, q_norm, k_norm, out_dtype=BF16)
    w_out = w_out.astype(BF16)
    xf = matmul_residual([y_a.reshape(T, RW_DIM), y_b.reshape(T, ATT_DIM)], [w_out[:RW_DIM], w_out[RW_DIM:]], xf,
                         tm=OUT_PROJ_ROWS, tn=D)
    return ffn_swiglu(xf, norm2, ffn_gate.astype(BF16), ffn_up.astype(BF16), ffn_down.astype(BF16), tm=FFN_ROWS,
                      tf=FFN_HIDDEN, cast=cast)


def _odd_layer(xf, Bn, S, norm1, w_in, gate_up_f, gate_bias_f, gate_up_b, gate_bias_b, out_norm, w_out,
               norm2, router, exp_gate, exp_up, exp_down):
    T, D = xf.shape
    o3 = 2 * GLA_DK + GLA_DV
    o4 = o3 + GLA_GATE_RANK
    w_main = jnp.concatenate([w_in[:, :o3], w_in[:, o4:]], axis=1).astype(BF16)
    w_gd = jnp.zeros((D, LANES), F32).at[:, :GLA_GATE_RANK].set(w_in[:, o3:o4])
    p_main = norm_matmul(xf, norm1, w_main, tm=IN_PROJ_ROWS, tn=D).reshape(Bn, S, -1)
    gd = norm_matmul(xf, norm1, w_gd, tm=OUT_PROJ_ROWS, tn=LANES)[:, :GLA_GATE_RANK].reshape(Bn, S, GLA_GATE_RANK)
    o = gla_mixer(p_main, gd, gate_up_f, gate_bias_f, gate_up_b, gate_bias_b, out_norm, out_dtype=BF16)
    xf = matmul_residual([o.reshape(T, GLA_DV)], [w_out.astype(BF16)], xf, tm=OUT_PROJ_ROWS, tn=D)
    return moe_layer(xf, norm2, router, exp_gate, exp_up, exp_down)


def kernel(x, e_norm1, e_w_in, e_shift_mu, e_w0_f, e_w_up_f, e_w0_b, e_w_up_b, e_a0_f, e_a_up_f, e_a0_b, e_a_up_b, e_g_up, e_k_k, e_k_a, e_r_k, e_gn_w, e_gn_b, e_q_norm, e_k_norm, e_w_out, e_norm2, e_ffn_gate, e_ffn_up, e_ffn_down, o_norm1, o_w_in, o_gate_up_f, o_gate_bias_f, o_gate_up_b, o_gate_bias_b, o_out_norm, o_w_out, o_norm2, o_router, o_exp_gate, o_exp_up, o_exp_down):
    Bn, S, D = x.shape
    xf, (exp_gate, exp_up, exp_down) = _even_layer(
        x, e_norm1[0], e_w_in[0], e_shift_mu[0], e_w0_f[0], e_w_up_f[0], e_w0_b[0],
        e_w_up_b[0], e_a0_f[0], e_a_up_f[0], e_a0_b[0], e_a_up_b[0], e_g_up[0], e_k_k[0],
        e_k_a[0], e_r_k[0], e_gn_w[0], e_gn_b[0], e_q_norm[0], e_k_norm[0], e_w_out[0],
        e_norm2[0], e_ffn_gate[0], e_ffn_up[0], e_ffn_down[0], (o_exp_gate[0], o_exp_up[0], o_exp_down[0]))
    xf = _odd_layer(xf, Bn, S, o_norm1[0], o_w_in[0], o_gate_up_f[0], o_gate_bias_f[0], o_gate_up_b[0],
                    o_gate_bias_b[0], o_out_norm[0], o_w_out[0], o_norm2[0], o_router[0],
                    exp_gate, exp_up, exp_down)
    return xf.reshape(Bn, S, D)
```
